```python
import math
import jax, jax.numpy as jnp
from jax import lax
import numpy as np

D_MODEL = 1024
BATCH = 8
SEQ = 2048
DEPTH = 1
DEC_BATCH = 128
DEC_SEQ = 4
PAST_LEN = 16384
PAGE_SIZE = 128

MIX_W = D_MODEL
GLA_W = MIX_W // 2
GLA_HEADS = 4
GLA_DV = GLA_W // GLA_HEADS
GLA_DK = GLA_DV // 2
GLA_K_W = GLA_HEADS * GLA_DK
GLA_GATE_RANK = 16
GLA_GATE_NORM = 16.0
GLA_CHUNK = 64
RWKV_W = MIX_W - GLA_W
RWKV_HEAD = 64
RWKV_HEADS = RWKV_W // RWKV_HEAD
RWKV_DECAY_RANK = 64
RWKV_A_RANK = 64
RWKV_G_RANK = 128
GLA_SIZES = [GLA_K_W, GLA_K_W, GLA_W, GLA_W, GLA_GATE_RANK]
RWKV_SIZES = [RWKV_W, RWKV_W, RWKV_W, RWKV_DECAY_RANK, RWKV_A_RANK, RWKV_G_RANK]
GLA_PROJ = sum(GLA_SIZES)
RWKV_PROJ = sum(RWKV_SIZES)
IN_W = GLA_PROJ + RWKV_PROJ
D_FF = ((-(-8 * D_MODEL // 3) + 255) // 256) * 256
PLE_DIM = 256
EPS = 1e-6
RWKV_GN_EPS = 64e-5

kernel_name = "hymba_gla_rwkv7_ple_step"


def _rmsnorm(x, g):
    x32 = x.astype(jnp.float32)
    return x32 * lax.rsqrt(jnp.mean(x32 * x32, axis=-1, keepdims=True) + EPS) * g


def _split(t, sizes):
    idx = np.cumsum(sizes)[:-1].tolist()
    return jnp.split(t, idx, axis=-1)


def _gla_chunked(q, k, v, log_a, s0, chunk):
    B, T, H, _ = q.shape
    n = T // chunk
    rs = lambda t: t.reshape(B, n, chunk, H, t.shape[-1])
    q, k, v, log_a = rs(q), rs(k), rs(v), rs(log_a)
    b = jnp.cumsum(log_a, axis=2)
    b_last = b[:, :, -1:]
    q_i = q * jnp.exp(b)
    k_i = k * jnp.exp(-b)
    k_e = k * jnp.exp(b_last - b)
    mask = jnp.tril(jnp.ones((chunk, chunk), dtype=bool))
    scores = jnp.einsum('bnthd,bnshd->bnhts', q_i, k_i)
    scores = jnp.where(mask, scores, 0.0)
    o_intra = jnp.einsum('bnhts,bnshv->bnthv', scores, v)
    d_state = jnp.einsum('bnshd,bnshv->bnhdv', k_e, v)
    decay_last = jnp.exp(b_last[:, :, 0])

    def step(s, inp):
        dec, ds = inp
        return dec[..., None] * s + ds, s

    s_final, s_starts = lax.scan(step, s0, (jnp.moveaxis(decay_last, 1, 0), jnp.moveaxis(d_state, 1, 0)))
    s_starts = jnp.moveaxis(s_starts, 0, 1)
    o_inter = jnp.einsum('bnthd,bnhdv->bnthv', q_i, s_starts)
    o = (o_intra + o_inter).reshape(B, T, H, v.shape[-1])
    return o, s_final


def _rwkv7_scan(r, decay, k, v, a_vec, b_vec, s0):
    def step(s, inp):
        r_t, d_t, k_t, v_t, a_t, b_t = inp
        sa = jnp.einsum('bhvk,bhk->bhv', s, a_t)
        s = s * d_t[:, :, None, :] + sa[..., None] * b_t[:, :, None, :] + v_t[..., None] * k_t[:, :, None, :]
        y = jnp.einsum('bhvk,bhk->bhv', s, r_t)
        return s, y

    xs = tuple(jnp.moveaxis(t, 1, 0) for t in (r, decay, k, v, a_vec, b_vec))
    s_final, ys = lax.scan(step, s0, xs)
    return jnp.moveaxis(ys, 0, 1), s_final


def _layer(x, p, s_gla, s_rwkv, s_shift, lw):
    f32 = jnp.float32
    B, T, _ = x.shape
    h = _rmsnorm(x, lw['norm_mix'])
    u = h @ lw['w_in'].astype(f32)
    u_gla, u_rw = u[..., :GLA_PROJ], u[..., GLA_PROJ:]

    gq, gk, gv, gg, gkd = _split(u_gla, GLA_SIZES)
    q = gq.reshape(B, T, GLA_HEADS, GLA_DK) * (GLA_DK ** -0.5)
    kg = gk.reshape(B, T, GLA_HEADS, GLA_DK)
    vg = gv.reshape(B, T, GLA_HEADS, GLA_DV)
    log_a = jax.nn.log_sigmoid(gkd @ lw['gla_gk_up'] + lw['gla_gk_bias']) / GLA_GATE_NORM
    log_a = log_a.reshape(B, T, GLA_HEADS, GLA_DK)
    chunk = math.gcd(T, GLA_CHUNK)
    o_g, s_gla_new = _gla_chunked(q, kg, vg, log_a, s_gla.astype(f32), chunk)
    o_g = _rmsnorm(o_g, lw['gla_norm']) * jax.nn.silu(gg.reshape(B, T, GLA_HEADS, GLA_DV))
    o_gla = o_g.reshape(B, T, GLA_W)

    prev = jnp.concatenate([s_shift.astype(f32)[:, None, :], u_rw[:, :-1]], axis=1)
    xr = u_rw + lw['rwkv_mu'] * (prev - u_rw)
    shift_new = u_rw[:, -1]
    r, kr, vr, wd, ad, gd = _split(xr, RWKV_SIZES)
    w = -jax.nn.softplus(-(lw['rwkv_w0'] + jnp.tanh(wd) @ lw['rwkv_w2'])) - 0.5
    decay = jnp.exp(-jnp.exp(w))
    a = jax.nn.sigmoid(lw['rwkv_a0'] + ad @ lw['rwkv_a2'])
    g = jax.nn.sigmoid(gd) @ lw['rwkv_g2']
    heads = lambda t: t.reshape(B, T, RWKV_HEADS, RWKV_HEAD)
    kk = heads(kr * lw['rwkv_k_k'])
    kk = kk / jnp.maximum(jnp.sqrt(jnp.sum(kk * kk, axis=-1, keepdims=True)), 1e-12)
    kr = kr * (1.0 + (a - 1.0) * lw['rwkv_k_a'])
    rh, kh, vh, ah = heads(r), heads(kr), heads(vr), heads(a)
    y, s_rwkv_new = _rwkv7_scan(rh, heads(decay), kh, vh, -kk, kk * ah, s_rwkv.astype(f32))
    mean = jnp.mean(y, axis=-1, keepdims=True)
    var = jnp.mean(jnp.square(y - mean), axis=-1, keepdims=True)
    y = ((y - mean) * lax.rsqrt(var + RWKV_GN_EPS)).reshape(B, T, RWKV_W) * lw['rwkv_ln_w'] + lw['rwkv_ln_b']
    bonus = jnp.sum(rh * kh * lw['rwkv_r_k'], axis=-1, keepdims=True) * vh
    o_rw = (y + bonus.reshape(B, T, RWKV_W)) * g

    x = x + jnp.concatenate([o_gla, o_rw], axis=-1) @ lw['w_out']

    h2 = _rmsnorm(x, lw['norm_ffn'])
    x = x + (jax.nn.silu(h2 @ lw['w_gate']) * (h2 @ lw['w_up'])) @ lw['w_down']

    h3 = _rmsnorm(x, lw['norm_ple'])
    x = x + jax.nn.sigmoid(h3 @ lw['w_ple_gate']) * (p.astype(f32) @ lw['w_ple_proj'])
    return x, s_gla_new, s_rwkv_new, shift_new


def setup_inputs(seed: int = 0) -> dict:
    key = jax.random.key(seed)
    ks = jax.random.split(key, 40)
    f32 = jnp.float32
    nrm = lambda kk, shape, scale: jax.random.normal(kk, shape, f32) * scale
    L = DEPTH
    return {
        'x_prompt': nrm(ks[0], (BATCH, SEQ, D_MODEL), 1.0),
        'x_sample': nrm(ks[1], (DEC_BATCH, DEC_SEQ, D_MODEL), 1.0),
        'state_gla': nrm(ks[2], (L, DEC_BATCH, GLA_HEADS, GLA_DK, GLA_DV), 0.5),
        'state_rwkv': nrm(ks[3], (L, DEC_BATCH, RWKV_HEADS, RWKV_HEAD, RWKV_HEAD), 0.5),
        'state_shift': nrm(ks[4], (L, DEC_BATCH, RWKV_PROJ), 1.0),
        'p_prompt': nrm(ks[5], (L, BATCH, SEQ, PLE_DIM), 1.0),
        'p_sample': nrm(ks[6], (L, DEC_BATCH, DEC_SEQ, PLE_DIM), 1.0),
        'norm_mix': 1.0 + nrm(ks[7], (L, D_MODEL), 0.02),
        'w_in': nrm(ks[8], (L, D_MODEL, IN_W), D_MODEL ** -0.5),
        'gla_gk_up': nrm(ks[9], (L, GLA_GATE_RANK, GLA_K_W), GLA_GATE_RANK ** -0.5),
        'gla_gk_bias': nrm(ks[10], (L, GLA_K_W), 0.1),
        'gla_norm': 1.0 + nrm(ks[11], (L, GLA_DV), 0.02),
        'rwkv_mu': jax.random.uniform(ks[12], (L, RWKV_PROJ), f32),
        'rwkv_w0': jax.random.uniform(ks[13], (L, RWKV_W), f32, -2.5, 0.5),
        'rwkv_w2': nrm(ks[14], (L, RWKV_DECAY_RANK, RWKV_W), 0.1),
        'rwkv_a0': nrm(ks[15], (L, RWKV_W), 0.1),
        'rwkv_a2': nrm(ks[16], (L, RWKV_A_RANK, RWKV_W), RWKV_A_RANK ** -0.5),
        'rwkv_g2': nrm(ks[17], (L, RWKV_G_RANK, RWKV_W), RWKV_G_RANK ** -0.5),
        'rwkv_k_k': 0.85 + nrm(ks[18], (L, RWKV_W), 0.02),
        'rwkv_k_a': 1.0 + nrm(ks[19], (L, RWKV_W), 0.02),
        'rwkv_r_k': nrm(ks[20], (L, RWKV_HEADS, RWKV_HEAD), 0.1),
        'rwkv_ln_w': 1.0 + nrm(ks[21], (L, RWKV_W), 0.02),
        'rwkv_ln_b': nrm(ks[22], (L, RWKV_W), 0.01),
        'w_out': nrm(ks[23], (L, MIX_W, D_MODEL), MIX_W ** -0.5),
        'norm_ffn': 1.0 + nrm(ks[24], (L, D_MODEL), 0.02),
        'w_gate': nrm(ks[25], (L, D_MODEL, D_FF), D_MODEL ** -0.5),
        'w_up': nrm(ks[26], (L, D_MODEL, D_FF), D_MODEL ** -0.5),
        'w_down': nrm(ks[27], (L, D_FF, D_MODEL), D_FF ** -0.5),
        'norm_ple': 1.0 + nrm(ks[28], (L, D_MODEL), 0.02),
        'w_ple_gate': nrm(ks[29], (L, D_MODEL, D_MODEL), D_MODEL ** -0.5),
        'w_ple_proj': nrm(ks[30], (L, PLE_DIM, D_MODEL), PLE_DIM ** -0.5),
        'norm_final': 1.0 + nrm(ks[31], (D_MODEL,), 0.02),
    }


def reference(x_prompt, x_sample, state_gla, state_rwkv, state_shift, p_prompt, p_sample,
              norm_mix, w_in, gla_gk_up, gla_gk_bias, gla_norm, rwkv_mu, rwkv_w0, rwkv_w2,
              rwkv_a0, rwkv_a2, rwkv_g2, rwkv_k_k, rwkv_k_a, rwkv_r_k, rwkv_ln_w, rwkv_ln_b,
              w_out, norm_ffn, w_gate, w_up, w_down, norm_ple, w_ple_gate, w_ple_proj, norm_final):
    f32 = jnp.float32
    xp = x_prompt.astype(f32)
    xs = x_sample.astype(f32)
    zero_gla = jnp.zeros((BATCH, GLA_HEADS, GLA_DK, GLA_DV), f32)
    zero_rwkv = jnp.zeros((BATCH, RWKV_HEADS, RWKV_HEAD, RWKV_HEAD), f32)
    zero_shift = jnp.zeros((BATCH, RWKV_PROJ), f32)
    gla_p, rwkv_p, shift_p, gla_s, rwkv_s, shift_s = [], [], [], [], [], []
    for i in range(DEPTH):
        lw = {
            'norm_mix': norm_mix[i].astype(f32), 'w_in': w_in[i].astype(f32),
            'gla_gk_up': gla_gk_up[i].astype(f32), 'gla_gk_bias': gla_gk_bias[i].astype(f32),
            'gla_norm': gla_norm[i].astype(f32), 'rwkv_mu': rwkv_mu[i].astype(f32),
            'rwkv_w0': rwkv_w0[i].astype(f32), 'rwkv_w2': rwkv_w2[i].astype(f32),
            'rwkv_a0': rwkv_a0[i].astype(f32), 'rwkv_a2': rwkv_a2[i].astype(f32),
            'rwkv_g2': rwkv_g2[i].astype(f32), 'rwkv_k_k': rwkv_k_k[i].astype(f32),
            'rwkv_k_a': rwkv_k_a[i].astype(f32), 'rwkv_r_k': rwkv_r_k[i].astype(f32),
            'rwkv_ln_w': rwkv_ln_w[i].astype(f32), 'rwkv_ln_b': rwkv_ln_b[i].astype(f32),
            'w_out': w_out[i].astype(f32), 'norm_ffn': norm_ffn[i].astype(f32),
            'w_gate': w_gate[i].astype(f32), 'w_up': w_up[i].astype(f32), 'w_down': w_down[i].astype(f32),
            'norm_ple': norm_ple[i].astype(f32), 'w_ple_gate': w_ple_gate[i].astype(f32),
            'w_ple_proj': w_ple_proj[i].astype(f32),
        }
        xp, g1, r1, s1 = _layer(xp, p_prompt[i], zero_gla, zero_rwkv, zero_shift, lw)
        xs, g2, r2, s2 = _layer(xs, p_sample[i], state_gla[i], state_rwkv[i], state_shift[i], lw)
        gla_p.append(g1); rwkv_p.append(r1); shift_p.append(s1)
        gla_s.append(g2); rwkv_s.append(r2); shift_s.append(s2)
    nf = norm_final.astype(f32)
    y_prompt = _rmsnorm(xp, nf).astype(x_prompt.dtype)
    y_sample = _rmsnorm(xs, nf).astype(x_sample.dtype)
    gla_prompt = jnp.stack(gla_p).astype(state_gla.dtype)
    rwkv_prompt = jnp.stack(rwkv_p).astype(state_rwkv.dtype)
    shift_prompt = jnp.stack(shift_p).astype(state_shift.dtype)
    gla_sample = jnp.stack(gla_s).astype(state_gla.dtype)
    rwkv_sample = jnp.stack(rwkv_s).astype(state_rwkv.dtype)
    shift_sample = jnp.stack(shift_s).astype(state_shift.dtype)
    return (y_prompt, y_sample, gla_prompt, rwkv_prompt, shift_prompt, gla_sample, rwkv_sample, shift_sample)
```

```python
import functools

import jax
import jax.numpy as jnp
from jax import lax
from jax.experimental import pallas as pl
from jax.experimental.pallas import tpu as pltpu

F32 = jnp.float32
BF16 = jnp.bfloat16

D_MODEL = 1024
GLA_HEADS = 4
GLA_DK = 64
GLA_DV = 128
GLA_K_W = GLA_HEADS * GLA_DK
GLA_W = GLA_HEADS * GLA_DV
GLA_GATE_RANK = 16
GLA_GATE_NORM = 16.0
GLA_MAIN = 2 * GLA_K_W + 2 * GLA_W
GLA_PROJ = GLA_MAIN + GLA_GATE_RANK
LANES = 128
GLA_COLS = GLA_MAIN + LANES
RWKV_HEAD = 64
RWKV_HEADS = 8
RWKV_W = RWKV_HEADS * RWKV_HEAD
RWKV_PAIRS = RWKV_W // LANES
RWKV_PROJ = 3 * RWKV_W + 64 + 64 + 128
D_FF = 2816
PLE_DIM = 256
EPS = 1e-6
RWKV_GN_EPS = 64e-5

VMEM_LIMIT = 56 * 1024 * 1024

NN = ((1,), (0,))
NT = ((1,), (1,))


def _split(x, n):
    parts = []
    r = x
    for i in range(n):
        p = r.astype(BF16)
        parts.append(p)
        if i + 1 < n:
            r = r - p.astype(F32)
    return parts


def _mm(a, b, dims=NN, pa=1, pb=1):
    pieces_a = _split(a, pa)
    pieces_b = _split(b, pb)
    n = max(pa, pb)
    acc = None
    for i, ai in enumerate(pieces_a):
        for j, bj in enumerate(pieces_b):
            if i + j < n:
                t = lax.dot_general(ai, bj, (dims, ((), ())), preferred_element_type=F32)
                acc = t if acc is None else acc + t
    return acc


def _mm_tn(a, b, pa=1, pb=1):
    rows = a.shape[0]
    pad = (-rows) % LANES
    if pad:
        a = jnp.concatenate([a, jnp.zeros((pad, a.shape[1]), a.dtype)], axis=0)
        b = jnp.concatenate([b, jnp.zeros((pad, b.shape[1]), b.dtype)], axis=0)
    return _mm(a.T, b, NN, pa, pb)


def _iota(shape, dim):
    return lax.broadcasted_iota(jnp.int32, shape, dim)


def _softplus(z):
    return jnp.maximum(z, 0.0) + jnp.log(1.0 + jnp.exp(-jnp.abs(z)))


def _sigmoid(z):
    return 1.0 / (1.0 + jnp.exp(-z))


def _rms(x, g):
    return x * lax.rsqrt(jnp.mean(x * x, axis=-1, keepdims=True) + EPS) * g


def _load_chunk(u_ref, rows):
    u = u_ref[0]
    if rows > u.shape[0]:
        u = jnp.concatenate([u, jnp.zeros((rows - u.shape[0], u.shape[1]), u.dtype)], axis=0)
    return u


def _proj_kernel(x_ref, g_ref, wg_ref, wr_ref, ug_ref, ur_ref):
    h = _rms(x_ref[...], g_ref[...]).astype(BF16)
    ug_ref[...] = jnp.dot(h, wg_ref[...], preferred_element_type=F32)
    ur_ref[...] = jnp.dot(h, wr_ref[...], preferred_element_type=F32)


def _const_spec(shape):
    return pl.BlockSpec(shape, lambda *_: (0,) * len(shape), pipeline_mode=pl.Buffered(1))


def _proj(x2d, g, w_gla, w_rw, tm):
    n = x2d.shape[0]
    return pl.pallas_call(
        _proj_kernel,
        grid=(n // tm,),
        in_specs=[
            pl.BlockSpec((tm, D_MODEL), lambda i: (i, 0)),
            _const_spec((1, D_MODEL)),
            _const_spec((D_MODEL, GLA_COLS)),
            _const_spec((D_MODEL, RWKV_PROJ)),
        ],
        out_specs=[
            pl.BlockSpec((tm, GLA_COLS), lambda i: (i, 0)),
            pl.BlockSpec((tm, RWKV_PROJ), lambda i: (i, 0)),
        ],
        out_shape=[
            jax.ShapeDtypeStruct((n, GLA_COLS), F32),
            jax.ShapeDtypeStruct((n, RWKV_PROJ), F32),
        ],
        compiler_params=pltpu.CompilerParams(
            dimension_semantics=("arbitrary",), vmem_limit_bytes=VMEM_LIMIT),
        name="proj",
    )(x2d, g, w_gla, w_rw)


def _gla_kernel(u_ref, s0_ref, gkup_ref, gkb_ref, gn_ref, o_ref, sout_ref, s_scr,
                *, rows, t_valid, prec):
    c = pl.program_id(1)
    rows_in = o_ref.shape[1]

    @pl.when(c == 0)
    def _():
        s_scr[...] = s0_ref[0]

    u = _load_chunk(u_ref, rows)
    q = u[:, 0:GLA_K_W] * (GLA_DK ** -0.5)
    k = u[:, GLA_K_W:2 * GLA_K_W]
    v = u[:, 2 * GLA_K_W:2 * GLA_K_W + GLA_W]
    gate = u[:, 2 * GLA_K_W + GLA_W:GLA_MAIN]
    gkd = u[:, GLA_MAIN:GLA_COLS]

    z = _mm(gkd, gkup_ref[...], NN, prec, prec) + gkb_ref[...]
    log_a = -_softplus(-z) * (1.0 / GLA_GATE_NORM)
    row = _iota((rows, 1), 0)
    if t_valid < rows:
        valid = row < t_valid
        log_a = jnp.where(valid, log_a, 0.0)
        k = jnp.where(valid, k, 0.0)

    tri_incl = _iota((rows, rows), 0) >= _iota((rows, rows), 1)
    ltri = jnp.where(tri_incl, 1.0, 0.0).astype(BF16)
    cum = _mm(ltri, log_a, NN, 1, 3)
    cum_last = cum[rows - 1:rows]
    q_i = q * jnp.exp(cum)
    k_i = k * jnp.exp(-cum)
    k_e = k * jnp.exp(cum_last - cum)

    s2 = s_scr[...]
    lane_k = _iota((1, GLA_K_W), 1)
    gnorm = gn_ref[...]
    for h in range(GLA_HEADS):
        in_head = (lane_k >= h * GLA_DK) & (lane_k < (h + 1) * GLA_DK)
        q_h = jnp.where(in_head, q_i, 0.0)
        scores = jnp.where(tri_incl, _mm(q_h, k_i, NT, prec, prec), 0.0)
        v_h = v[:, h * GLA_DV:(h + 1) * GLA_DV]
        o_h = _mm(scores, v_h, NN, prec, prec) + _mm(q_h, s2, NN, prec, prec)
        g_h = gate[:, h * GLA_DV:(h + 1) * GLA_DV]
        o_h = _rms(o_h, gnorm) * (g_h * _sigmoid(g_h))
        o_ref[0, :, h * GLA_DV:(h + 1) * GLA_DV] = o_h[:rows_in]

    kv = _mm_tn(k_e, v, prec, prec)
    d_state = jnp.concatenate(
        [kv[h * GLA_DK:(h + 1) * GLA_DK, h * GLA_DV:(h + 1) * GLA_DV] for h in range(GLA_HEADS)],
        axis=0)
    dec = jnp.exp(jnp.broadcast_to(cum_last, (LANES, GLA_K_W))).T
    s_new = dec * s2 + d_state
    s_scr[...] = s_new

    @pl.when(c == pl.num_programs(1) - 1)
    def _():
        sout_ref[0] = s_new


def _gla(u3, s0, gkup, gkb, gnorm, *, rows, t_valid, prec):
    b, t, _ = u3.shape
    rows_in = min(rows, t)
    nc = t // rows_in
    kern = functools.partial(_gla_kernel, rows=rows, t_valid=t_valid, prec=prec)
    sdim = GLA_HEADS * GLA_DK
    return pl.pallas_call(
        kern,
        grid=(b, nc),
        in_specs=[
            pl.BlockSpec((1, rows_in, GLA_COLS), lambda i, j: (i, j, 0)),
            pl.BlockSpec((1, sdim, GLA_DV), lambda i, j: (i, 0, 0)),
            _const_spec((LANES, GLA_K_W)),
            _const_spec((1, GLA_K_W)),
            _const_spec((1, GLA_DV)),
        ],
        out_specs=[
            pl.BlockSpec((1, rows_in, GLA_W), lambda i, j: (i, j, 0)),
            pl.BlockSpec((1, sdim, GLA_DV), lambda i, j: (i, 0, 0)),
        ],
        out_shape=[
            jax.ShapeDtypeStruct((b, t, GLA_W), F32),
            jax.ShapeDtypeStruct((b, sdim, GLA_DV), F32),
        ],
        scratch_shapes=[pltpu.VMEM((sdim, GLA_DV), F32)],
        compiler_params=pltpu.CompilerParams(
            dimension_semantics=("arbitrary", "arbitrary"), vmem_limit_bytes=VMEM_LIMIT),
        name="gla",
    )(u3, s0, gkup, gkb, gnorm)


def _rwkv_kernel(u_ref, shift0_ref, s0_ref, mu_ref, w0_ref, w2_ref, a0_ref, a2_ref, g2_ref,
                 kk_ref, ka_ref, rk_ref, lnw_ref, lnb_ref, seg_ref,
                 o_ref, sout_ref, shift_ref, s_scr, prev_scr, *, rows, t_valid, prec):
    c = pl.program_id(1)
    rows_in = o_ref.shape[1]
    hd = RWKV_HEAD

    @pl.when(c == 0)
    def _():
        zero = jnp.zeros((hd, hd), F32)
        for p in range(RWKV_PAIRS):
            top = jnp.concatenate([s0_ref[0, 2 * p], zero], axis=1)
            bot = jnp.concatenate([zero, s0_ref[0, 2 * p + 1]], axis=1)
            s_scr[p] = jnp.concatenate([top, bot], axis=0)
        prev_scr[...] = shift0_ref[0]

    u = _load_chunk(u_ref, rows)
    row = _iota((rows, 1), 0)
    prev = jnp.where(row == 0, prev_scr[...], pltpu.roll(u, 1, axis=0))
    prev_scr[...] = u[rows - 1:rows]
    xr = u + mu_ref[...] * (prev - u)

    r = xr[:, 0:RWKV_W]
    kr = xr[:, RWKV_W:2 * RWKV_W]
    vr = xr[:, 2 * RWKV_W:3 * RWKV_W]
    wa = xr[:, 3 * RWKV_W:3 * RWKV_W + LANES]
    gd = xr[:, 3 * RWKV_W + LANES:RWKV_PROJ]

    w = -_softplus(-(w0_ref[...] + _mm(jnp.tanh(wa), w2_ref[...], NN, prec, prec))) - 0.5
    lw = -jnp.exp(w)
    a_sig = _sigmoid(a0_ref[...] + _mm(wa, a2_ref[...], NN, prec, prec))
    g = _mm(_sigmoid(gd), g2_ref[...], NN, prec, prec)
    seg = seg_ref[...]
    kk = kr * kk_ref[...]
    kk = kk / jnp.maximum(jnp.sqrt(_mm(kk * kk, seg, NN, 3, 1)), 1e-12)
    kr = kr * (1.0 + (a_sig - 1.0) * ka_ref[...])
    a_vec = -kk
    b_vec = kk * a_sig
    if t_valid < rows:
        valid = row < t_valid
        lw = jnp.where(valid, lw, 0.0)
        kr = jnp.where(valid, kr, 0.0)
        vr = jnp.where(valid, vr, 0.0)
        a_vec = jnp.where(valid, a_vec, 0.0)
        b_vec = jnp.where(valid, b_vec, 0.0)

    ri = _iota((rows, rows), 0)
    ci = _iota((rows, rows), 1)
    tri_incl = ri >= ci
    tri_strict = ri > ci
    eye = jnp.where(ri == ci, 1.0, 0.0)
    ltri = jnp.where(tri_incl, 1.0, 0.0).astype(BF16)
    cum = _mm(ltri, lw, NN, 1, 3)
    cum_last = cum[rows - 1:rows]
    e_neg = jnp.exp(-cum)
    e_end = jnp.exp(cum_last - cum)
    a_t = a_vec * jnp.exp(cum - lw)
    r_t = r * jnp.exp(cum)
    k_t = kr * e_neg
    b_t = b_vec * e_neg
    k_e = kr * e_end
    b_e = b_vec * e_end
    g_end = jnp.exp(cum_last)

    lane = _iota((1, LANES), 1)
    first = lane < hd
    blk = (_iota((LANES, LANES), 0) // hd) == (_iota((LANES, LANES), 1) // hd)
    n_double = rows.bit_length() - 2

    for p in range(RWKV_PAIRS):
        sl = slice(p * LANES, (p + 1) * LANES)
        at, rt, kt, bt, ke, be, vp = a_t[:, sl], r_t[:, sl], k_t[:, sl], b_t[:, sl], k_e[:, sl], b_e[:, sl], vr[:, sl]
        per_head = []
        for e in range(2):
            mine = first if e == 0 else jnp.logical_not(first)
            lhs = jnp.concatenate([jnp.where(mine, at, 0.0), jnp.where(mine, rt, 0.0)], axis=0)
            gb = _mm(lhs, bt, NT, prec, prec)
            gk = _mm(lhs, kt, NT, prec, prec)
            a_ab = jnp.where(tri_strict, gb[:rows], 0.0)
            a_ak = jnp.where(tri_strict, gk[:rows], 0.0)
            a_rb = jnp.where(tri_incl, gb[rows:], 0.0)
            a_rk = jnp.where(tri_incl, gk[rows:], 0.0)
            tinv = eye + a_ab
            apow = a_ab
            for _ in range(n_double):
                apow = _mm(apow, apow, NN, prec, prec)
                tinv = tinv + _mm(apow, tinv, NN, prec, prec)
            av = _mm(a_ak, vp, NN, prec, prec)
            wu = _mm(tinv, jnp.concatenate([at, av], axis=1), NN, prec, prec)
            z = _mm(a_rb, wu, NN, prec, prec)
            zr = z[:, :LANES]
            y0 = z[:, LANES:] + _mm(a_rk, vp, NN, prec, prec)
            per_head.append((wu[:, :LANES], wu[:, LANES:], zr, y0))
        w_m, u0, zr, y0 = (jnp.where(first, x0, x1) for x0, x1 in zip(*per_head))
        r_m = rt + zr
        s2 = s_scr[p]
        uy = _mm(jnp.concatenate([w_m, r_m], axis=0), s2, NT, prec, prec)
        u_m = uy[:rows] + u0
        y = uy[rows:] + y0
        upd = _mm_tn(jnp.concatenate([u_m, vp], axis=0), jnp.concatenate([be, ke], axis=0), prec, prec)
        s_new = s2 * g_end[:, sl] + jnp.where(blk, upd, 0.0)
        s_scr[p] = s_new

        segp = seg[sl, sl]
        mean = _mm(y, segp, NN, 3, 1) * (1.0 / hd)
        yc = y - mean
        var = _mm(yc * yc, segp, NN, 3, 1) * (1.0 / hd)
        yn = yc * lax.rsqrt(var + RWKV_GN_EPS) * lnw_ref[:, sl] + lnb_ref[:, sl]
        bonus = _mm(r[:, sl] * kr[:, sl] * rk_ref[:, sl], segp, NN, 3, 1) * vp
        o_ref[0, :, sl] = ((yn + bonus) * g[:, sl])[:rows_in]

    @pl.when(c == pl.num_programs(1) - 1)
    def _():
        for p in range(RWKV_PAIRS):
            s2 = s_scr[p]
            sout_ref[0, 2 * p] = s2[:hd, :hd]
            sout_ref[0, 2 * p + 1] = s2[hd:, hd:]
        shift_ref[0] = u[t_valid - 1:t_valid]


def _rwkv(u3, shift0, s0, params, *, rows, t_valid, prec):
    b, t, _ = u3.shape
    rows_in = min(rows, t)
    nc = t // rows_in
    kern = functools.partial(_rwkv_kernel, rows=rows, t_valid=t_valid, prec=prec)
    vec = lambda n: _const_spec((1, n))
    return pl.pallas_call(
        kern,
        grid=(b, nc),
        in_specs=[
            pl.BlockSpec((1, rows_in, RWKV_PROJ), lambda i, j: (i, j, 0)),
            pl.BlockSpec((1, 1, RWKV_PROJ), lambda i, j: (i, 0, 0)),
            pl.BlockSpec((1, RWKV_HEADS, RWKV_HEAD, RWKV_HEAD), lambda i, j: (i, 0, 0, 0)),
            vec(RWKV_PROJ),
            vec(RWKV_W),
            _const_spec((LANES, RWKV_W)),
            vec(RWKV_W),
            _const_spec((LANES, RWKV_W)),
            _const_spec((LANES, RWKV_W)),
            vec(RWKV_W), vec(RWKV_W), vec(RWKV_W), vec(RWKV_W), vec(RWKV_W),
            _const_spec((RWKV_W, RWKV_W)),
        ],
        out_specs=[
            pl.BlockSpec((1, rows_in, RWKV_W), lambda i, j: (i, j, 0)),
            pl.BlockSpec((1, RWKV_HEADS, RWKV_HEAD, RWKV_HEAD), lambda i, j: (i, 0, 0, 0)),
            pl.BlockSpec((1, 1, RWKV_PROJ), lambda i, j: (i, 0, 0)),
        ],
        out_shape=[
            jax.ShapeDtypeStruct((b, t, RWKV_W), F32),
            jax.ShapeDtypeStruct((b, RWKV_HEADS, RWKV_HEAD, RWKV_HEAD), F32),
            jax.ShapeDtypeStruct((b, 1, RWKV_PROJ), F32),
        ],
        scratch_shapes=[
            pltpu.VMEM((RWKV_PAIRS, LANES, LANES), F32),
            pltpu.VMEM((1, RWKV_PROJ), F32),
        ],
        compiler_params=pltpu.CompilerParams(
            dimension_semantics=("arbitrary", "arbitrary"), vmem_limit_bytes=VMEM_LIMIT),
        name="rwkv",
    )(u3, shift0, s0, *params)


def _post_kernel(x_ref, og_ref, or_ref, p_ref, wo_ref, nffn_ref, wg_ref, wu_ref, wd_ref,
                 nple_ref, wpg_ref, wpp_ref, nf_ref, y_ref):
    o = jnp.concatenate([og_ref[...], or_ref[...]], axis=1).astype(BF16)
    x = x_ref[...] + jnp.dot(o, wo_ref[...], preferred_element_type=F32)
    h2 = _rms(x, nffn_ref[...]).astype(BF16)
    gate = jnp.dot(h2, wg_ref[...], preferred_element_type=F32)
    up = jnp.dot(h2, wu_ref[...], preferred_element_type=F32)
    act = (gate * _sigmoid(gate) * up).astype(BF16)
    x = x + jnp.dot(act, wd_ref[...], preferred_element_type=F32)
    h3 = _rms(x, nple_ref[...]).astype(BF16)
    pg = _sigmoid(jnp.dot(h3, wpg_ref[...], preferred_element_type=F32))
    pp = jnp.dot(p_ref[...].astype(BF16), wpp_ref[...], preferred_element_type=F32)
    x = x + pg * pp
    y_ref[...] = _rms(x, nf_ref[...])


def _post(x2d, og, orw, p2d, weights, tm):
    n = x2d.shape[0]
    wo, nffn, wg, wu, wd, nple, wpg, wpp, nf = weights
    tok = lambda w: pl.BlockSpec((tm, w), lambda i: (i, 0))
    return pl.pallas_call(
        _post_kernel,
        grid=(n // tm,),
        in_specs=[
            tok(D_MODEL), tok(GLA_W), tok(RWKV_W), tok(PLE_DIM),
            _const_spec((D_MODEL, D_MODEL)), _const_spec((1, D_MODEL)),
            _const_spec((D_MODEL, D_FF)), _const_spec((D_MODEL, D_FF)), _const_spec((D_FF, D_MODEL)),
            _const_spec((1, D_MODEL)), _const_spec((D_MODEL, D_MODEL)), _const_spec((PLE_DIM, D_MODEL)),
            _const_spec((1, D_MODEL)),
        ],
        out_specs=tok(D_MODEL),
        out_shape=jax.ShapeDtypeStruct((n, D_MODEL), F32),
        compiler_params=pltpu.CompilerParams(
            dimension_semantics=("arbitrary",), vmem_limit_bytes=VMEM_LIMIT),
        name="post",
    )(x2d, og, orw, p2d, wo, nffn, wg, wu, wd, nple, wpg, wpp, nf)


PROMPT_CHUNK = 64
SAMPLE_ROWS = 8
SAMPLE_CHUNK = 16
TOKEN_TILE = 256
REC_PREC = 2


def _run_path(x, p, s_gla, s_rwkv, s_shift, wts, *, rows, t_valid):
    b, t, _ = x.shape
    ug, ur = _proj(x.reshape(b * t, D_MODEL), wts["norm_mix"], wts["w_gla"], wts["w_rw"], TOKEN_TILE)
    og, gla_new = _gla(ug.reshape(b, t, GLA_COLS), s_gla.reshape(b, GLA_HEADS * GLA_DK, GLA_DV),
                       wts["gk_up"], wts["gk_bias"], wts["gla_norm"],
                       rows=rows, t_valid=t_valid, prec=REC_PREC)
    orw, rwkv_new, shift_new = _rwkv(ur.reshape(b, t, RWKV_PROJ), s_shift.reshape(b, 1, RWKV_PROJ),
                                     s_rwkv, wts["rwkv"], rows=rows, t_valid=t_valid, prec=REC_PREC)
    y = _post(x.reshape(b * t, D_MODEL), og.reshape(b * t, GLA_W), orw.reshape(b * t, RWKV_W),
              p.reshape(b * t, PLE_DIM), wts["post"], TOKEN_TILE)
    return (y.reshape(b, t, D_MODEL), gla_new.reshape(b, GLA_HEADS, GLA_DK, GLA_DV), rwkv_new,
            shift_new.reshape(b, RWKV_PROJ))


def kernel(x_prompt, x_sample, state_gla, state_rwkv, state_shift, p_prompt, p_sample, norm_mix, w_in, gla_gk_up, gla_gk_bias, gla_norm, rwkv_mu, rwkv_w0, rwkv_w2, rwkv_a0, rwkv_a2, rwkv_g2, rwkv_k_k, rwkv_k_a, rwkv_r_k, rwkv_ln_w, rwkv_ln_b, w_out, norm_ffn, w_gate, w_up, w_down, norm_ple, w_ple_gate, w_ple_proj, norm_final):
    depth = w_in.shape[0]
    assert depth == 1
    i = 0
    rowv = lambda a: a.astype(F32).reshape(1, -1)
    w_in_i = w_in[i]
    zeros = lambda r, c: jnp.zeros((r, c), F32)
    seg = (jnp.arange(RWKV_W)[:, None] // RWKV_HEAD == jnp.arange(RWKV_W)[None, :] // RWKV_HEAD)
    wts = {
        "norm_mix": rowv(norm_mix[i]),
        "w_gla": jnp.concatenate(
            [w_in_i[:, :GLA_PROJ], zeros(D_MODEL, GLA_COLS - GLA_PROJ)], axis=1).astype(BF16),
        "w_rw": w_in_i[:, GLA_PROJ:].astype(BF16),
        "gk_up": jnp.concatenate(
            [gla_gk_up[i].astype(F32), zeros(LANES - GLA_GATE_RANK, GLA_K_W)], axis=0),
        "gk_bias": rowv(gla_gk_bias[i]),
        "gla_norm": rowv(gla_norm[i]),
        "rwkv": (
            rowv(rwkv_mu[i]), rowv(rwkv_w0[i]),
            jnp.concatenate([rwkv_w2[i].astype(F32), zeros(64, RWKV_W)], axis=0),
            rowv(rwkv_a0[i]),
            jnp.concatenate([zeros(64, RWKV_W), rwkv_a2[i].astype(F32)], axis=0),
            rwkv_g2[i].astype(F32),
            rowv(rwkv_k_k[i]), rowv(rwkv_k_a[i]), rowv(rwkv_r_k[i]),
            rowv(rwkv_ln_w[i]), rowv(rwkv_ln_b[i]),
            seg.astype(BF16),
        ),
        "post": (
            w_out[i].astype(BF16), rowv(norm_ffn[i]), w_gate[i].astype(BF16), w_up[i].astype(BF16),
            w_down[i].astype(BF16), rowv(norm_ple[i]), w_ple_gate[i].astype(BF16),
            w_ple_proj[i].astype(BF16), rowv(norm_final),
        ),
    }

    bp, tp, _ = x_prompt.shape
    yp, gla_p, rwkv_p, shift_p = _run_path(
        x_prompt.astype(F32), p_prompt[i],
        jnp.zeros((bp, GLA_HEADS, GLA_DK, GLA_DV), F32),
        jnp.zeros((bp, RWKV_HEADS, RWKV_HEAD, RWKV_HEAD), F32),
        jnp.zeros((bp, RWKV_PROJ), F32), wts, rows=PROMPT_CHUNK, t_valid=PROMPT_CHUNK)

    bs, ts, _ = x_sample.shape
    pad = ((0, 0), (0, SAMPLE_ROWS - ts), (0, 0))
    ys, gla_s, rwkv_s, shift_s = _run_path(
        jnp.pad(x_sample.astype(F32), pad), jnp.pad(p_sample[i], pad),
        state_gla[i], state_rwkv[i], state_shift[i], wts, rows=SAMPLE_CHUNK, t_valid=ts)

    return (yp.astype(x_prompt.dtype), ys[:, :ts].astype(x_sample.dtype),
            gla_p[None].astype(state_gla.dtype), rwkv_p[None].astype(state_rwkv.dtype),
            shift_p[None].astype(state_shift.dtype),
            gla_s[None].astype(state_gla.dtype), rwkv_s[None].astype(state_rwkv.dtype),
            shift_s[None].astype(state_shift.dtype))
```

```python
import functools

import jax
import jax.numpy as jnp
from jax import lax
from jax.experimental import pallas as pl
from jax.experimental.pallas import tpu as pltpu

F32 = jnp.float32
BF16 = jnp.bfloat16

D_MODEL = 1024
GLA_HEADS = 4
GLA_DK = 64
GLA_DV = 128
GLA_K_W = GLA_HEADS * GLA_DK
GLA_W = GLA_HEADS * GLA_DV
GLA_GATE_RANK = 16
GLA_GATE_NORM = 16.0
GLA_MAIN = 2 * GLA_K_W + 2 * GLA_W
GLA_PROJ = GLA_MAIN + GLA_GATE_RANK
LANES = 128
GLA_COLS = GLA_MAIN + LANES
RWKV_HEAD = 64
RWKV_HEADS = 8
RWKV_W = RWKV_HEADS * RWKV_HEAD
RWKV_PROJ = 3 * RWKV_W + 64 + 64 + 128
D_FF = 2816
PLE_DIM = 256
EPS = 1e-6
RWKV_GN_EPS = 64e-5

VMEM_LIMIT = 56 * 1024 * 1024

NN = ((1,), (0,))
NT = ((1,), (1,))


def _split(x, n):
    parts = []
    r = x
    for i in range(n):
        p = r.astype(BF16)
        parts.append(p)
        if i + 1 < n:
            r = r - p.astype(F32)
    return parts


def _mm(a, b, dims=NN, pa=1, pb=1):
    pieces_a = _split(a, pa)
    pieces_b = _split(b, pb)
    n = max(pa, pb)
    acc = None
    for i, ai in enumerate(pieces_a):
        for j, bj in enumerate(pieces_b):
            if i + j < n:
                t = lax.dot_general(ai, bj, (dims, ((), ())), preferred_element_type=F32)
                acc = t if acc is None else acc + t
    return acc


def _mm_tn(a, b, pa=1, pb=1):
    rows = a.shape[0]
    pad = (-rows) % LANES
    if pad:
        a = jnp.concatenate([a, jnp.zeros((pad, a.shape[1]), a.dtype)], axis=0)
        b = jnp.concatenate([b, jnp.zeros((pad, b.shape[1]), b.dtype)], axis=0)
    return _mm(a.T, b, NN, pa, pb)


def _iota(shape, dim):
    return lax.broadcasted_iota(jnp.int32, shape, dim)


def _softplus(z):
    return jnp.maximum(z, 0.0) + jnp.log(1.0 + jnp.exp(-jnp.abs(z)))


def _sigmoid(z):
    return 1.0 / (1.0 + jnp.exp(-z))


def _rms(x, g):
    return x * lax.rsqrt(jnp.mean(x * x, axis=-1, keepdims=True) + EPS) * g


def _load_chunk(u_ref, rows):
    u = u_ref[0]
    if rows > u.shape[0]:
        u = jnp.concatenate([u, jnp.zeros((rows - u.shape[0], u.shape[1]), u.dtype)], axis=0)
    return u


def _proj_kernel(x_ref, g_ref, wg_ref, wr_ref, ug_ref, ur_ref):
    h = _rms(x_ref[...], g_ref[...]).astype(BF16)
    ug_ref[...] = jnp.dot(h, wg_ref[...], preferred_element_type=F32)
    ur_ref[...] = jnp.dot(h, wr_ref[...], preferred_element_type=F32)


def _const_spec(shape):
    return pl.BlockSpec(shape, lambda *_: (0,) * len(shape), pipeline_mode=pl.Buffered(1))


def _proj(x2d, g, w_gla, w_rw, tm):
    n = x2d.shape[0]
    return pl.pallas_call(
        _proj_kernel,
        grid=(n // tm,),
        in_specs=[
            pl.BlockSpec((tm, D_MODEL), lambda i: (i, 0)),
            _const_spec((1, D_MODEL)),
            _const_spec((D_MODEL, GLA_COLS)),
            _const_spec((D_MODEL, RWKV_PROJ)),
        ],
        out_specs=[
            pl.BlockSpec((tm, GLA_COLS), lambda i: (i, 0)),
            pl.BlockSpec((tm, RWKV_PROJ), lambda i: (i, 0)),
        ],
        out_shape=[
            jax.ShapeDtypeStruct((n, GLA_COLS), F32),
            jax.ShapeDtypeStruct((n, RWKV_PROJ), F32),
        ],
        compiler_params=pltpu.CompilerParams(
            dimension_semantics=("arbitrary",), vmem_limit_bytes=VMEM_LIMIT),
        name="proj",
    )(x2d, g, w_gla, w_rw)


def _gla_kernel(u_ref, s0_ref, gkup_ref, gkb_ref, gn_ref, o_ref, sout_ref, s_scr,
                *, rows, t_valid, prec):
    c = pl.program_id(1)
    rows_in = o_ref.shape[1]

    @pl.when(c == 0)
    def _():
        s_scr[...] = s0_ref[0]

    u = _load_chunk(u_ref, rows)
    q = u[:, 0:GLA_K_W] * (GLA_DK ** -0.5)
    k = u[:, GLA_K_W:2 * GLA_K_W]
    v = u[:, 2 * GLA_K_W:2 * GLA_K_W + GLA_W]
    gate = u[:, 2 * GLA_K_W + GLA_W:GLA_MAIN]
    gkd = u[:, GLA_MAIN:GLA_COLS]

    z = _mm(gkd, gkup_ref[...], NN, prec, prec) + gkb_ref[...]
    log_a = -_softplus(-z) * (1.0 / GLA_GATE_NORM)
    row = _iota((rows, 1), 0)
    if t_valid < rows:
        valid = row < t_valid
        log_a = jnp.where(valid, log_a, 0.0)
        k = jnp.where(valid, k, 0.0)

    tri_incl = _iota((rows, rows), 0) >= _iota((rows, rows), 1)
    ltri = jnp.where(tri_incl, 1.0, 0.0).astype(BF16)
    cum = _mm(ltri, log_a, NN, 1, 3)
    cum_last = cum[rows - 1:rows]
    q_i = q * jnp.exp(cum)
    k_i = k * jnp.exp(-cum)
    k_e = k * jnp.exp(cum_last - cum)

    s2 = s_scr[...]
    lane_k = _iota((1, GLA_K_W), 1)
    gnorm = gn_ref[...]
    for h in range(GLA_HEADS):
        in_head = (lane_k >= h * GLA_DK) & (lane_k < (h + 1) * GLA_DK)
        q_h = jnp.where(in_head, q_i, 0.0)
        scores = jnp.where(tri_incl, _mm(q_h, k_i, NT, prec, prec), 0.0)
        v_h = v[:, h * GLA_DV:(h + 1) * GLA_DV]
        o_h = _mm(scores, v_h, NN, prec, prec) + _mm(q_h, s2, NN, prec, prec)
        g_h = gate[:, h * GLA_DV:(h + 1) * GLA_DV]
        o_h = _rms(o_h, gnorm) * (g_h * _sigmoid(g_h))
        o_ref[0, :, h * GLA_DV:(h + 1) * GLA_DV] = o_h[:rows_in]

    kv = _mm_tn(k_e, v, prec, prec)
    d_state = jnp.concatenate(
        [kv[h * GLA_DK:(h + 1) * GLA_DK, h * GLA_DV:(h + 1) * GLA_DV] for h in range(GLA_HEADS)],
        axis=0)
    dec = jnp.exp(jnp.broadcast_to(cum_last, (LANES, GLA_K_W))).T
    s_new = dec * s2 + d_state
    s_scr[...] = s_new

    @pl.when(c == pl.num_programs(1) - 1)
    def _():
        sout_ref[0] = s_new


def _gla(u3, s0, gkup, gkb, gnorm, *, rows, t_valid, prec):
    b, t, _ = u3.shape
    rows_in = min(rows, t)
    nc = t // rows_in
    kern = functools.partial(_gla_kernel, rows=rows, t_valid=t_valid, prec=prec)
    sdim = GLA_HEADS * GLA_DK
    return pl.pallas_call(
        kern,
        grid=(b, nc),
        in_specs=[
            pl.BlockSpec((1, rows_in, GLA_COLS), lambda i, j: (i, j, 0)),
            pl.BlockSpec((1, sdim, GLA_DV), lambda i, j: (i, 0, 0)),
            _const_spec((LANES, GLA_K_W)),
            _const_spec((1, GLA_K_W)),
            _const_spec((1, GLA_DV)),
        ],
        out_specs=[
            pl.BlockSpec((1, rows_in, GLA_W), lambda i, j: (i, j, 0)),
            pl.BlockSpec((1, sdim, GLA_DV), lambda i, j: (i, 0, 0)),
        ],
        out_shape=[
            jax.ShapeDtypeStruct((b, t, GLA_W), F32),
            jax.ShapeDtypeStruct((b, sdim, GLA_DV), F32),
        ],
        scratch_shapes=[pltpu.VMEM((sdim, GLA_DV), F32)],
        compiler_params=pltpu.CompilerParams(
            dimension_semantics=("arbitrary", "arbitrary"), vmem_limit_bytes=VMEM_LIMIT),
        name="gla",
    )(u3, s0, gkup, gkb, gnorm)


GROUP_HEADS = 4
GROUP_W = GROUP_HEADS * RWKV_HEAD
RWKV_GROUPS = RWKV_W // GROUP_W


def _head_masks(rows):
    lane = _iota((rows, GROUP_W), 1)
    return [jnp.where(lane // RWKV_HEAD == h, 1.0, 0.0).astype(BF16) for h in range(GROUP_HEADS)]


def _stack_heads(x, hmasks):
    xb = x.astype(BF16)
    return jnp.concatenate([xb * m for m in hmasks], axis=0)


def _block_diag(x_cat, blk):
    return jnp.concatenate([x_cat.astype(BF16)] * GROUP_HEADS, axis=0) * blk


def _chunk_masks(rows, seq_len):
    t = _iota((rows, GROUP_HEADS * rows), 0)
    s = _iota((rows, GROUP_HEADS * rows), 1) % rows
    strict, incl = t > s, t >= s
    t2 = _iota((2 * rows, rows), 0)
    s2 = _iota((2 * rows, rows), 1)
    cum_rows = (t2 < rows) & (t2 >= s2)
    tot_rows = t2 >= rows
    if seq_len < rows:
        same = (t // seq_len) == (s // seq_len)
        strict, incl = strict & same, incl & same
        same2 = ((t2 % rows) // seq_len) == (s2 // seq_len)
        cum_rows, tot_rows = cum_rows & same2, tot_rows & same2
    big = GROUP_HEADS * rows
    blk = jnp.where(_iota((big, big), 0) // rows == _iota((big, big), 1) // rows, 1.0, 0.0)
    return dict(strict=strict, incl=incl, eye=jnp.where(t == s, 1.0, 0.0),
                sums=jnp.where(cum_rows | tot_rows, 1.0, 0.0).astype(BF16), blk=blk.astype(BF16))


def _seg_sum(x, seg, pa=1):
    return jnp.concatenate(
        [_mm(x[:, g * GROUP_W:(g + 1) * GROUP_W], seg, NN, pa, 1) for g in range(RWKV_GROUPS)], axis=1)


def _rwkv_prep(u, prev, mu, w0, w2, a0, a2, g2, k_k, k_a, seg):
    xr = u + mu * (prev - u)
    r = xr[:, 0:RWKV_W]
    kr = xr[:, RWKV_W:2 * RWKV_W]
    vr = xr[:, 2 * RWKV_W:3 * RWKV_W]
    wa = xr[:, 3 * RWKV_W:3 * RWKV_W + LANES]
    gd = xr[:, 3 * RWKV_W + LANES:RWKV_PROJ]
    w = -_softplus(-(w0 + _mm(jnp.tanh(wa), w2))) - 0.5
    lw = -jnp.exp(w)
    a_sig = _sigmoid(a0 + _mm(wa, a2))
    gate = _mm(_sigmoid(gd), g2)
    kk = kr * k_k
    kk = kk / jnp.maximum(jnp.sqrt(_seg_sum(kk * kk, seg)), 1e-12)
    kr = kr * (1.0 + (a_sig - 1.0) * k_a)
    return r, kr, vr, lw, -kk, kk * a_sig, gate


def _rwkv_scaled(r, kr, lw, a_vec, b_vec, sums):
    rows = r.shape[0]
    cums = _mm(sums, lw, NN, 1, 2)
    cum, cum_end = cums[:rows], cums[rows:]
    e_neg = jnp.exp(-cum)
    e_end = jnp.exp(cum_end - cum)
    return (a_vec * jnp.exp(cum - lw), r * jnp.exp(cum), kr * e_neg, b_vec * e_neg,
            kr * e_end, b_vec * e_end, jnp.exp(cum_end))


def _rwkv_intra(at, rt, kt, bt, vp, m, hmasks, n_double):
    rows = at.shape[0]
    lhs = jnp.concatenate([at, rt], axis=0)
    gb = _mm(lhs, _stack_heads(bt, hmasks), NT)
    gk = _mm(lhs, _stack_heads(kt, hmasks), NT)
    a_ab = jnp.where(m["strict"], gb[:rows], 0.0)
    a_rb = jnp.where(m["incl"], gb[rows:], 0.0)
    a_ak = jnp.where(m["strict"], gk[:rows], 0.0)
    a_rk = jnp.where(m["incl"], gk[rows:], 0.0)
    tinv = m["eye"] + a_ab
    apow = a_ab
    for _ in range(n_double):
        apow = _mm(apow, _block_diag(apow, m["blk"]))
        tinv = tinv + _mm(apow, _block_diag(tinv, m["blk"]))
    v_s = _stack_heads(vp, hmasks)
    av = _mm(a_ak, v_s)
    wu = _mm(tinv, jnp.concatenate([_stack_heads(at, hmasks), _stack_heads(av, hmasks)], axis=1))
    w_m, u0 = wu[:, :GROUP_W], wu[:, GROUP_W:]
    z = _mm(a_rb, jnp.concatenate([_stack_heads(w_m, hmasks), _stack_heads(u0, hmasks)], axis=1))
    return w_m, u0, rt + z[:, :GROUP_W], z[:, GROUP_W:] + _mm(a_rk, v_s)


def _rwkv_out(y, r, kr, vp, gate, lnw, lnb, rk, seg):
    inv = 1.0 / RWKV_HEAD
    yc = y - _mm(y, seg, NN, 2, 1) * inv
    var = _mm(yc * yc, seg) * inv
    yn = yc * lax.rsqrt(var + RWKV_GN_EPS) * lnw + lnb
    bonus = _mm(r * kr * rk, seg) * vp
    return (yn + bonus) * gate


def _rwkv_state_in(s_ref, idx, g):
    zero = jnp.zeros((RWKV_HEAD, RWKV_HEAD), F32)
    blocks = []
    for h in range(GROUP_HEADS):
        parts = [zero] * GROUP_HEADS
        parts[h] = s_ref[idx, g * GROUP_HEADS + h]
        blocks.append(jnp.concatenate(parts, axis=1))
    return jnp.concatenate(blocks, axis=0)


def _rwkv_state_out(s_ref, idx, g, s2):
    for h in range(GROUP_HEADS):
        sl = slice(h * RWKV_HEAD, (h + 1) * RWKV_HEAD)
        s_ref[idx, g * GROUP_HEADS + h] = s2[sl, sl]


def _head_block_mask():
    return (_iota((GROUP_W, GROUP_W), 0) // RWKV_HEAD) == (_iota((GROUP_W, GROUP_W), 1) // RWKV_HEAD)


def _rwkv_seq_kernel(u_ref, shift0_ref, s0_ref, mu_ref, w0_ref, w2_ref, a0_ref, a2_ref, g2_ref,
                     kk_ref, ka_ref, rk_ref, lnw_ref, lnb_ref, seg_ref,
                     o_ref, sout_ref, s_scr, prev_scr):
    c = pl.program_id(1)
    rows = u_ref.shape[1]

    @pl.when(c == 0)
    def _():
        for g in range(RWKV_GROUPS):
            s_scr[g] = _rwkv_state_in(s0_ref, 0, g)
        prev_scr[...] = shift0_ref[0]

    u = u_ref[0]
    row = _iota((rows, 1), 0)
    prev = jnp.where(row == 0, prev_scr[...], pltpu.roll(u, 1, axis=0))
    prev_scr[...] = u[rows - 1:rows]
    seg = seg_ref[...]
    r, kr, vr, lw, a_vec, b_vec, gate = _rwkv_prep(
        u, prev, mu_ref[...], w0_ref[...], w2_ref[...], a0_ref[...], a2_ref[...], g2_ref[...],
        kk_ref[...], ka_ref[...], seg)
    m = _chunk_masks(rows, rows)
    at, rt, kt, bt, ke, be, g_end = _rwkv_scaled(r, kr, lw, a_vec, b_vec, m["sums"])
    hmasks = _head_masks(rows)
    head_blk = _head_block_mask()
    n_double = rows.bit_length() - 2

    for g in range(RWKV_GROUPS):
        gl = slice(g * GROUP_W, (g + 1) * GROUP_W)
        w_m, u0, r_m, y0 = _rwkv_intra(at[:, gl], rt[:, gl], kt[:, gl], bt[:, gl], vr[:, gl],
                                       m, hmasks, n_double)
        s2 = s_scr[g]
        uy = _mm(jnp.concatenate([w_m, r_m], axis=0), s2, NT)
        u_m = uy[:rows] + u0
        y = uy[rows:] + y0
        upd = _mm_tn(jnp.concatenate([u_m, vr[:, gl]], axis=0),
                     jnp.concatenate([be[:, gl], ke[:, gl]], axis=0))
        s_scr[g] = s2 * g_end[0:1, gl] + jnp.where(head_blk, upd, 0.0)
        o_ref[0, :, gl] = _rwkv_out(y, r[:, gl], kr[:, gl], vr[:, gl], gate[:, gl],
                                    lnw_ref[:, gl], lnb_ref[:, gl], rk_ref[:, gl], seg)

    @pl.when(c == pl.num_programs(1) - 1)
    def _():
        for g in range(RWKV_GROUPS):
            _rwkv_state_out(sout_ref, 0, g, s_scr[g])


def _rwkv_dec_kernel(u_ref, shiftx_ref, s0_ref, mu_ref, w0_ref, w2_ref, a0_ref, a2_ref, g2_ref,
                     kk_ref, ka_ref, rk_ref, lnw_ref, lnb_ref, seg_ref,
                     o_ref, sout_ref, *, seq_len):
    rows = u_ref.shape[0]
    n_seq = rows // seq_len
    u = u_ref[...]
    row = _iota((rows, 1), 0)
    prev = jnp.where(row % seq_len == 0, shiftx_ref[...], pltpu.roll(u, 1, axis=0))
    seg = seg_ref[...]
    r, kr, vr, lw, a_vec, b_vec, gate = _rwkv_prep(
        u, prev, mu_ref[...], w0_ref[...], w2_ref[...], a0_ref[...], a2_ref[...], g2_ref[...],
        kk_ref[...], ka_ref[...], seg)
    m = _chunk_masks(rows, seq_len)
    at, rt, kt, bt, ke, be, g_end = _rwkv_scaled(r, kr, lw, a_vec, b_vec, m["sums"])
    hmasks = _head_masks(rows)
    head_blk = _head_block_mask()
    n_double = seq_len.bit_length() - 2
    lane2 = _iota((1, 2 * rows), 1)
    u_half = lane2 < rows

    for g in range(RWKV_GROUPS):
        gl = slice(g * GROUP_W, (g + 1) * GROUP_W)
        w_m, u0, r_m, y0 = _rwkv_intra(at[:, gl], rt[:, gl], kt[:, gl], bt[:, gl], vr[:, gl],
                                       m, hmasks, n_double)
        wr = jnp.concatenate([w_m, r_m], axis=0).astype(BF16)
        bk = jnp.concatenate([be[:, gl], ke[:, gl]], axis=0).astype(BF16)
        base = jnp.concatenate([u0, vr[:, gl]], axis=0).T
        acc = jnp.zeros((GROUP_W, 2 * rows), F32)
        for j in range(n_seq):
            mine = (lane2 % rows) // seq_len == j
            s2 = _rwkv_state_in(s0_ref, j, g)
            uyt = _mm(s2, wr, NT)
            lhs = jnp.where(mine, jnp.where(u_half, uyt, 0.0) + base, 0.0)
            upd = _mm(lhs, bk)
            s_new = s2 * g_end[j * seq_len:j * seq_len + 1, gl] + jnp.where(head_blk, upd, 0.0)
            _rwkv_state_out(sout_ref, j, g, s_new)
            acc = jnp.where(mine, uyt, acc)
        y = acc.T[rows:] + y0
        o_ref[:, gl] = _rwkv_out(y, r[:, gl], kr[:, gl], vr[:, gl], gate[:, gl],
                                 lnw_ref[:, gl], lnb_ref[:, gl], rk_ref[:, gl], seg)


def _rwkv_param_specs():
    vec = lambda n: _const_spec((1, n))
    return [
        vec(RWKV_PROJ),
        vec(RWKV_W),
        _const_spec((LANES, RWKV_W)),
        vec(RWKV_W),
        _const_spec((LANES, RWKV_W)),
        _const_spec((LANES, RWKV_W)),
        vec(RWKV_W), vec(RWKV_W), vec(RWKV_W), vec(RWKV_W), vec(RWKV_W),
        _const_spec((GROUP_W, GROUP_W)),
    ]


def _rwkv_seq(u3, shift0, s0, params, rows):
    b, t, _ = u3.shape
    state_spec = pl.BlockSpec((1, RWKV_HEADS, RWKV_HEAD, RWKV_HEAD), lambda i, j: (i, 0, 0, 0))
    return pl.pallas_call(
        _rwkv_seq_kernel,
        grid=(b, t // rows),
        in_specs=[
            pl.BlockSpec((1, rows, RWKV_PROJ), lambda i, j: (i, j, 0)),
            pl.BlockSpec((1, 1, RWKV_PROJ), lambda i, j: (i, 0, 0)),
            state_spec,
        ] + _rwkv_param_specs(),
        out_specs=[pl.BlockSpec((1, rows, RWKV_W), lambda i, j: (i, j, 0)), state_spec],
        out_shape=[
            jax.ShapeDtypeStruct((b, t, RWKV_W), F32),
            jax.ShapeDtypeStruct((b, RWKV_HEADS, RWKV_HEAD, RWKV_HEAD), F32),
        ],
        scratch_shapes=[
            pltpu.VMEM((RWKV_GROUPS, GROUP_W, GROUP_W), F32),
            pltpu.VMEM((1, RWKV_PROJ), F32),
        ],
        compiler_params=pltpu.CompilerParams(
            dimension_semantics=("arbitrary", "arbitrary"), vmem_limit_bytes=VMEM_LIMIT),
        name="rwkv_seq",
    )(u3, shift0, s0, *params)


def _rwkv_dec(u2, shiftx, s0, params, seq_len, n_seq):
    n = u2.shape[0]
    rows = n_seq * seq_len
    state_spec = pl.BlockSpec((n_seq, RWKV_HEADS, RWKV_HEAD, RWKV_HEAD), lambda i: (i, 0, 0, 0))
    return pl.pallas_call(
        functools.partial(_rwkv_dec_kernel, seq_len=seq_len),
        grid=(n // rows,),
        in_specs=[
            pl.BlockSpec((rows, RWKV_PROJ), lambda i: (i, 0)),
            pl.BlockSpec((rows, RWKV_PROJ), lambda i: (i, 0)),
            state_spec,
        ] + _rwkv_param_specs(),
        out_specs=[pl.BlockSpec((rows, RWKV_W), lambda i: (i, 0)), state_spec],
        out_shape=[
            jax.ShapeDtypeStruct((n, RWKV_W), F32),
            jax.ShapeDtypeStruct(s0.shape, F32),
        ],
        compiler_params=pltpu.CompilerParams(
            dimension_semantics=("arbitrary",), vmem_limit_bytes=VMEM_LIMIT),
        name="rwkv_dec",
    )(u2, shiftx, s0, *params)


def _post_kernel(x_ref, og_ref, or_ref, p_ref, wo_ref, nffn_ref, wg_ref, wu_ref, wd_ref,
                 nple_ref, wpg_ref, wpp_ref, nf_ref, y_ref):
    o = jnp.concatenate([og_ref[...], or_ref[...]], axis=1).astype(BF16)
    x = x_ref[...] + jnp.dot(o, wo_ref[...], preferred_element_type=F32)
    h2 = _rms(x, nffn_ref[...]).astype(BF16)
    gate = jnp.dot(h2, wg_ref[...], preferred_element_type=F32)
    up = jnp.dot(h2, wu_ref[...], preferred_element_type=F32)
    act = (gate * _sigmoid(gate) * up).astype(BF16)
    x = x + jnp.dot(act, wd_ref[...], preferred_element_type=F32)
    h3 = _rms(x, nple_ref[...]).astype(BF16)
    pg = _sigmoid(jnp.dot(h3, wpg_ref[...], preferred_element_type=F32))
    pp = jnp.dot(p_ref[...].astype(BF16), wpp_ref[...], preferred_element_type=F32)
    x = x + pg * pp
    y_ref[...] = _rms(x, nf_ref[...])


def _post(x2d, og, orw, p2d, weights, tm):
    n = x2d.shape[0]
    wo, nffn, wg, wu, wd, nple, wpg, wpp, nf = weights
    tok = lambda w: pl.BlockSpec((tm, w), lambda i: (i, 0))
    return pl.pallas_call(
        _post_kernel,
        grid=(n // tm,),
        in_specs=[
            tok(D_MODEL), tok(GLA_W), tok(RWKV_W), tok(PLE_DIM),
            _const_spec((D_MODEL, D_MODEL)), _const_spec((1, D_MODEL)),
            _const_spec((D_MODEL, D_FF)), _const_spec((D_MODEL, D_FF)), _const_spec((D_FF, D_MODEL)),
            _const_spec((1, D_MODEL)), _const_spec((D_MODEL, D_MODEL)), _const_spec((PLE_DIM, D_MODEL)),
            _const_spec((1, D_MODEL)),
        ],
        out_specs=tok(D_MODEL),
        out_shape=jax.ShapeDtypeStruct((n, D_MODEL), F32),
        compiler_params=pltpu.CompilerParams(
            dimension_semantics=("arbitrary",), vmem_limit_bytes=VMEM_LIMIT),
        name="post",
    )(x2d, og, orw, p2d, wo, nffn, wg, wu, wd, nple, wpg, wpp, nf)


PROMPT_CHUNK = 64
SAMPLE_ROWS = 8
SAMPLE_CHUNK = 16
DEC_TILE_SEQS = 16
TOKEN_TILE = 256


def kernel(x_prompt, x_sample, state_gla, state_rwkv, state_shift, p_prompt, p_sample, norm_mix, w_in, gla_gk_up, gla_gk_bias, gla_norm, rwkv_mu, rwkv_w0, rwkv_w2, rwkv_a0, rwkv_a2, rwkv_g2, rwkv_k_k, rwkv_k_a, rwkv_r_k, rwkv_ln_w, rwkv_ln_b, w_out, norm_ffn, w_gate, w_up, w_down, norm_ple, w_ple_gate, w_ple_proj, norm_final):
    assert w_in.shape[0] == 1
    i = 0
    rowv = lambda a: a.astype(F32).reshape(1, -1)
    zeros = lambda r, c: jnp.zeros((r, c), F32)
    w_in_i = w_in[i]
    w_gla = jnp.concatenate(
        [w_in_i[:, :GLA_PROJ], zeros(D_MODEL, GLA_COLS - GLA_PROJ)], axis=1).astype(BF16)
    w_rw = w_in_i[:, GLA_PROJ:].astype(BF16)
    gk_up = jnp.concatenate([gla_gk_up[i].astype(F32), zeros(LANES - GLA_GATE_RANK, GLA_K_W)], axis=0)
    seg = jnp.arange(GROUP_W)[:, None] // RWKV_HEAD == jnp.arange(GROUP_W)[None, :] // RWKV_HEAD
    rwkv_params = (
        rowv(rwkv_mu[i]), rowv(rwkv_w0[i]),
        jnp.concatenate([rwkv_w2[i].astype(F32), zeros(64, RWKV_W)], axis=0).astype(BF16),
        rowv(rwkv_a0[i]),
        jnp.concatenate([zeros(64, RWKV_W), rwkv_a2[i].astype(F32)], axis=0).astype(BF16),
        rwkv_g2[i].astype(BF16),
        rowv(rwkv_k_k[i]), rowv(rwkv_k_a[i]), rowv(rwkv_r_k[i]), rowv(rwkv_ln_w[i]), rowv(rwkv_ln_b[i]),
        seg.astype(BF16),
    )
    post_w = (
        w_out[i].astype(BF16), rowv(norm_ffn[i]), w_gate[i].astype(BF16), w_up[i].astype(BF16),
        w_down[i].astype(BF16), rowv(norm_ple[i]), w_ple_gate[i].astype(BF16),
        w_ple_proj[i].astype(BF16), rowv(norm_final),
    )
    gla_w = (gk_up, rowv(gla_gk_bias[i]), rowv(gla_norm[i]))
    g_mix = rowv(norm_mix[i])

    bp, tp, _ = x_prompt.shape
    xp = x_prompt.astype(F32).reshape(bp * tp, D_MODEL)
    ug, ur = _proj(xp, g_mix, w_gla, w_rw, TOKEN_TILE)
    og, gla_p = _gla(ug.reshape(bp, tp, GLA_COLS), jnp.zeros((bp, GLA_HEADS * GLA_DK, GLA_DV), F32),
                     *gla_w, rows=PROMPT_CHUNK, t_valid=PROMPT_CHUNK, prec=1)
    ur3 = ur.reshape(bp, tp, RWKV_PROJ)
    orw, rwkv_p = _rwkv_seq(ur3, jnp.zeros((bp, 1, RWKV_PROJ), F32),
                            jnp.zeros((bp, RWKV_HEADS, RWKV_HEAD, RWKV_HEAD), F32), rwkv_params,
                            PROMPT_CHUNK)
    shift_p = ur3[:, tp - 1]
    yp = _post(xp, og.reshape(bp * tp, GLA_W), orw.reshape(bp * tp, RWKV_W),
               p_prompt[i].reshape(bp * tp, PLE_DIM), post_w, TOKEN_TILE)

    bs, ts, _ = x_sample.shape
    pad = ((0, 0), (0, SAMPLE_ROWS - ts), (0, 0))
    xs = jnp.pad(x_sample.astype(F32), pad).reshape(bs * SAMPLE_ROWS, D_MODEL)
    ug, ur = _proj(xs, g_mix, w_gla, w_rw, TOKEN_TILE)
    og, gla_s = _gla(ug.reshape(bs, SAMPLE_ROWS, GLA_COLS),
                     state_gla[i].astype(F32).reshape(bs, GLA_HEADS * GLA_DK, GLA_DV),
                     *gla_w, rows=SAMPLE_CHUNK, t_valid=ts, prec=1)
    ur3 = ur.reshape(bs, SAMPLE_ROWS, RWKV_PROJ)
    shiftx = jnp.pad(state_shift[i].astype(F32)[:, None, :], ((0, 0), (0, ts - 1), (0, 0)))
    orw, rwkv_s = _rwkv_dec(ur3[:, :ts].reshape(bs * ts, RWKV_PROJ), shiftx.reshape(bs * ts, RWKV_PROJ),
                            state_rwkv[i].astype(F32), rwkv_params, ts, DEC_TILE_SEQS)
    shift_s = ur3[:, ts - 1]
    orw = jnp.pad(orw.reshape(bs, ts, RWKV_W), pad).reshape(bs * SAMPLE_ROWS, RWKV_W)
    ys = _post(xs, og.reshape(bs * SAMPLE_ROWS, GLA_W), orw,
               jnp.pad(p_sample[i], pad).reshape(bs * SAMPLE_ROWS, PLE_DIM), post_w, TOKEN_TILE)
    ys = ys.reshape(bs, SAMPLE_ROWS, D_MODEL)[:, :ts]

    gla_shape = (1, -1, GLA_HEADS, GLA_DK, GLA_DV)
    return (yp.reshape(bp, tp, D_MODEL).astype(x_prompt.dtype), ys.astype(x_sample.dtype),
            gla_p.reshape(gla_shape).astype(state_gla.dtype), rwkv_p[None].astype(state_rwkv.dtype),
            shift_p[None].astype(state_shift.dtype),
            gla_s.reshape(gla_shape).astype(state_gla.dtype), rwkv_s[None].astype(state_rwkv.dtype),
            shift_s[None].astype(state_shift.dtype))
```

```python
import functools

import jax
import jax.numpy as jnp
from jax import lax
from jax.experimental import pallas as pl
from jax.experimental.pallas import tpu as pltpu

F32 = jnp.float32
BF16 = jnp.bfloat16

D_MODEL = 1024
GLA_HEADS = 4
GLA_DK = 64
GLA_DV = 128
GLA_K_W = GLA_HEADS * GLA_DK
GLA_W = GLA_HEADS * GLA_DV
GLA_GATE_RANK = 16
GLA_GATE_NORM = 16.0
GLA_MAIN = 2 * GLA_K_W + 2 * GLA_W
GLA_PROJ = GLA_MAIN + GLA_GATE_RANK
LANES = 128
GLA_COLS = GLA_MAIN + LANES
RWKV_HEAD = 64
RWKV_HEADS = 8
RWKV_W = RWKV_HEADS * RWKV_HEAD
RWKV_PROJ = 3 * RWKV_W + 64 + 64 + 128
D_FF = 2816
PLE_DIM = 256
EPS = 1e-6
RWKV_GN_EPS = 64e-5

VMEM_LIMIT = 56 * 1024 * 1024

NN = ((1,), (0,))
NT = ((1,), (1,))


def _split(x, n):
    parts = []
    r = x
    for i in range(n):
        p = r.astype(BF16)
        parts.append(p)
        if i + 1 < n:
            r = r - p.astype(F32)
    return parts


def _mm(a, b, dims=NN, pa=1, pb=1):
    pieces_a = _split(a, pa)
    pieces_b = _split(b, pb)
    n = max(pa, pb)
    acc = None
    for i, ai in enumerate(pieces_a):
        for j, bj in enumerate(pieces_b):
            if i + j < n:
                t = lax.dot_general(ai, bj, (dims, ((), ())), preferred_element_type=F32)
                acc = t if acc is None else acc + t
    return acc


def _mm_tn(a, b, pa=1, pb=1):
    rows = a.shape[0]
    pad = (-rows) % LANES
    if pad:
        a = jnp.concatenate([a, jnp.zeros((pad, a.shape[1]), a.dtype)], axis=0)
        b = jnp.concatenate([b, jnp.zeros((pad, b.shape[1]), b.dtype)], axis=0)
    return _mm(a.T, b, NN, pa, pb)


def _iota(shape, dim):
    return lax.broadcasted_iota(jnp.int32, shape, dim)


def _softplus(z):
    return jnp.maximum(z, 0.0) + jnp.log(1.0 + jnp.exp(-jnp.abs(z)))


def _sigmoid(z):
    return 1.0 / (1.0 + jnp.exp(-z))


def _rms(x, g):
    return x * lax.rsqrt(jnp.mean(x * x, axis=-1, keepdims=True) + EPS) * g


GROUP_HEADS = 4
GROUP_W = GROUP_HEADS * RWKV_HEAD
RWKV_GROUPS = RWKV_W // GROUP_W


def _head_masks(rows, head_w):
    lane = _iota((rows, GROUP_HEADS * head_w), 1)
    return [jnp.where(lane // head_w == h, 1.0, 0.0).astype(BF16) for h in range(GROUP_HEADS)]


def _stack_heads(x, hmasks):
    xb = x.astype(BF16)
    return jnp.concatenate([xb * m for m in hmasks], axis=0)


def _block_diag(x_cat, blk):
    return jnp.concatenate([x_cat.astype(BF16)] * GROUP_HEADS, axis=0) * blk


def _chunk_masks(rows, seq_len):
    t = _iota((rows, GROUP_HEADS * rows), 0)
    s = _iota((rows, GROUP_HEADS * rows), 1) % rows
    strict, incl = t > s, t >= s
    t2 = _iota((2 * rows, rows), 0)
    s2 = _iota((2 * rows, rows), 1)
    cum_rows = (t2 < rows) & (t2 >= s2)
    tot_rows = t2 >= rows
    if seq_len < rows:
        same = (t // seq_len) == (s // seq_len)
        strict, incl = strict & same, incl & same
        same2 = ((t2 % rows) // seq_len) == (s2 // seq_len)
        cum_rows, tot_rows = cum_rows & same2, tot_rows & same2
    big = GROUP_HEADS * rows
    blk = jnp.where(_iota((big, big), 0) // rows == _iota((big, big), 1) // rows, 1.0, 0.0)
    return dict(strict=strict, incl=incl, eye=jnp.where(t == s, 1.0, 0.0),
                sums=jnp.where(cum_rows | tot_rows, 1.0, 0.0).astype(BF16), blk=blk.astype(BF16))


def _proj_kernel(x_ref, g_ref, wg_ref, wr_ref, ug_ref, ur_ref):
    h = _rms(x_ref[...], g_ref[...]).astype(BF16)
    ug_ref[...] = jnp.dot(h, wg_ref[...], preferred_element_type=F32)
    ur_ref[...] = jnp.dot(h, wr_ref[...], preferred_element_type=F32)


def _const_spec(shape):
    return pl.BlockSpec(shape, lambda *_: (0,) * len(shape), pipeline_mode=pl.Buffered(1))


def _proj(x2d, g, w_gla, w_rw, tm):
    n = x2d.shape[0]
    return pl.pallas_call(
        _proj_kernel,
        grid=(n // tm,),
        in_specs=[
            pl.BlockSpec((tm, D_MODEL), lambda i: (i, 0)),
            _const_spec((1, D_MODEL)),
            _const_spec((D_MODEL, GLA_COLS)),
            _const_spec((D_MODEL, RWKV_PROJ)),
        ],
        out_specs=[
            pl.BlockSpec((tm, GLA_COLS), lambda i: (i, 0)),
            pl.BlockSpec((tm, RWKV_PROJ), lambda i: (i, 0)),
        ],
        out_shape=[
            jax.ShapeDtypeStruct((n, GLA_COLS), F32),
            jax.ShapeDtypeStruct((n, RWKV_PROJ), F32),
        ],
        compiler_params=pltpu.CompilerParams(
            dimension_semantics=("arbitrary",), vmem_limit_bytes=VMEM_LIMIT),
        name="proj",
    )(x2d, g, w_gla, w_rw)


def _gla_prep(u, gkup, gkb, sums):
    rows = u.shape[0]
    q = u[:, 0:GLA_K_W] * (GLA_DK ** -0.5)
    k = u[:, GLA_K_W:2 * GLA_K_W]
    v = u[:, 2 * GLA_K_W:2 * GLA_K_W + GLA_W]
    gate = u[:, 2 * GLA_K_W + GLA_W:GLA_MAIN]
    log_a = -_softplus(-(_mm(u[:, GLA_MAIN:GLA_COLS], gkup) + gkb)) * (1.0 / GLA_GATE_NORM)
    cums = _mm(sums, log_a, NN, 1, 2)
    cum, cum_end = cums[:rows], cums[rows:]
    return (q * jnp.exp(cum), k * jnp.exp(-cum), k * jnp.exp(cum_end - cum), v, gate,
            jnp.exp(cum_end))


def _gla_intra(q_i, k_i, v, m, kmasks, vmasks):
    scores = jnp.where(m["incl"], _mm(q_i, _stack_heads(k_i, kmasks), NT), 0.0)
    return _mm(scores, _stack_heads(v, vmasks))


def _gla_out(o, gate, gnorm):
    heads = [slice(h * GLA_DV, (h + 1) * GLA_DV) for h in range(GLA_HEADS)]
    return jnp.concatenate(
        [_rms(o[:, hl], gnorm) * (gate[:, hl] * _sigmoid(gate[:, hl])) for hl in heads], axis=1)


def _gla_block_mask():
    return (_iota((GLA_K_W, GLA_W), 0) // GLA_DK) == (_iota((GLA_K_W, GLA_W), 1) // GLA_DV)


def _gla_state_in(s, blk):
    return jnp.where(blk, jnp.concatenate([s] * GLA_HEADS, axis=1), 0.0)


def _gla_state_out(s_bd):
    heads = [s_bd[:, h * GLA_DV:(h + 1) * GLA_DV] for h in range(GLA_HEADS)]
    return (heads[0] + heads[1]) + (heads[2] + heads[3])


def _lane_tiled_t(x):
    pad = LANES - x.shape[0]
    if pad:
        x = jnp.concatenate([x, jnp.zeros((pad, x.shape[1]), x.dtype)], axis=0)
    return x.T


def _gla_seq_kernel(u_ref, s0_ref, gkup_ref, gkb_ref, gn_ref, o_ref, sout_ref, s_scr):
    c = pl.program_id(1)
    n_seq, rows = u_ref.shape[0], u_ref.shape[1]
    blk = _gla_block_mask()

    @pl.when(c == 0)
    def _():
        for b in range(n_seq):
            s_scr[b] = _gla_state_in(s0_ref[b], blk)

    m = _chunk_masks(rows, rows)
    kmasks = _head_masks(rows, GLA_DK)
    vmasks = _head_masks(rows, GLA_DV)
    tok = [_gla_prep(u_ref[b], gkup_ref[...], gkb_ref[...], m["sums"]) for b in range(n_seq)]
    s_old = [s_scr[b] for b in range(n_seq)]
    intra = [_gla_intra(q_i, k_i, v, m, kmasks, vmasks) for q_i, k_i, _, v, _, _ in tok]
    inter = [_mm(t[0], s) for t, s in zip(tok, s_old)]
    kv = [_mm_tn(k_e, v) for _, _, k_e, v, _, _ in tok]
    for b in range(n_seq):
        gate, g_end = tok[b][4], tok[b][5]
        o_ref[b] = _gla_out(intra[b] + inter[b], gate, gn_ref[...])
        dec = jnp.concatenate([_lane_tiled_t(jnp.broadcast_to(g_end[0:1], (LANES, GLA_K_W)))] * GLA_HEADS,
                              axis=1)
        s_scr[b] = dec * s_old[b] + jnp.where(blk, kv[b], 0.0)

    @pl.when(c == pl.num_programs(1) - 1)
    def _():
        for b in range(n_seq):
            sout_ref[b] = _gla_state_out(s_scr[b])


def _gla_dec_kernel(u_ref, s0_ref, gkup_ref, gkb_ref, gn_ref, o_ref, sout_ref, *, seq_len):
    rows = u_ref.shape[0]
    n_seq = rows // seq_len
    blk = _gla_block_mask()
    m = _chunk_masks(rows, seq_len)
    q_i, k_i, k_e, v, gate, g_end = _gla_prep(u_ref[...], gkup_ref[...], gkb_ref[...], m["sums"])
    o = _gla_intra(q_i, k_i, v, m, _head_masks(rows, GLA_DK), _head_masks(rows, GLA_DV))
    k_et = _lane_tiled_t(k_e).astype(BF16)
    dec_t = _lane_tiled_t(g_end)
    v_pad = jnp.concatenate([v, jnp.zeros((LANES - rows, GLA_W), F32)], axis=0)
    row = _iota((rows, 1), 0)
    row_pad = _iota((LANES, 1), 0)
    for j in range(n_seq):
        s_bd = _gla_state_in(s0_ref[j], blk)
        o = o + _mm(jnp.where(row // seq_len == j, q_i, 0.0), s_bd)
        kv = _mm(k_et, jnp.where(row_pad // seq_len == j, v_pad, 0.0))
        first = j * seq_len
        sout_ref[j] = _gla_state_out(dec_t[:, first:first + 1] * s_bd + jnp.where(blk, kv, 0.0))
    o_ref[...] = _gla_out(o, gate, gn_ref[...])


def _gla_param_specs():
    return [_const_spec((LANES, GLA_K_W)), _const_spec((1, GLA_K_W)), _const_spec((1, GLA_DV))]


def _gla_seq(u3, s0, params, rows, n_seq):
    b, t, _ = u3.shape
    sdim = GLA_HEADS * GLA_DK
    state_spec = pl.BlockSpec((n_seq, sdim, GLA_DV), lambda i, j: (i, 0, 0))
    return pl.pallas_call(
        _gla_seq_kernel,
        grid=(b // n_seq, t // rows),
        in_specs=[pl.BlockSpec((n_seq, rows, GLA_COLS), lambda i, j: (i, j, 0)), state_spec]
        + _gla_param_specs(),
        out_specs=[pl.BlockSpec((n_seq, rows, GLA_W), lambda i, j: (i, j, 0)), state_spec],
        out_shape=[
            jax.ShapeDtypeStruct((b, t, GLA_W), F32),
            jax.ShapeDtypeStruct((b, sdim, GLA_DV), F32),
        ],
        scratch_shapes=[pltpu.VMEM((n_seq, sdim, GLA_W), F32)],
        compiler_params=pltpu.CompilerParams(
            dimension_semantics=("arbitrary", "arbitrary"), vmem_limit_bytes=VMEM_LIMIT),
        name="gla_seq",
    )(u3, s0, *params)


def _gla_dec(u2, s0, params, seq_len, n_seq):
    n = u2.shape[0]
    rows = n_seq * seq_len
    sdim = GLA_HEADS * GLA_DK
    state_spec = pl.BlockSpec((n_seq, sdim, GLA_DV), lambda i: (i, 0, 0))
    return pl.pallas_call(
        functools.partial(_gla_dec_kernel, seq_len=seq_len),
        grid=(n // rows,),
        in_specs=[pl.BlockSpec((rows, GLA_COLS), lambda i: (i, 0)), state_spec] + _gla_param_specs(),
        out_specs=[pl.BlockSpec((rows, GLA_W), lambda i: (i, 0)), state_spec],
        out_shape=[
            jax.ShapeDtypeStruct((n, GLA_W), F32),
            jax.ShapeDtypeStruct(s0.shape, F32),
        ],
        compiler_params=pltpu.CompilerParams(
            dimension_semantics=("arbitrary",), vmem_limit_bytes=VMEM_LIMIT),
        name="gla_dec",
    )(u2, s0, *params)


def _seg_sum(x, seg, pa=1):
    return jnp.concatenate(
        [_mm(x[:, g * GROUP_W:(g + 1) * GROUP_W], seg, NN, pa, 1) for g in range(RWKV_GROUPS)], axis=1)


def _rwkv_prep(u, prev, mu, w0, w2, a0, a2, g2, k_k, k_a, seg):
    xr = u + mu * (prev - u)
    r = xr[:, 0:RWKV_W]
    kr = xr[:, RWKV_W:2 * RWKV_W]
    vr = xr[:, 2 * RWKV_W:3 * RWKV_W]
    wa = xr[:, 3 * RWKV_W:3 * RWKV_W + LANES]
    gd = xr[:, 3 * RWKV_W + LANES:RWKV_PROJ]
    w = -_softplus(-(w0 + _mm(jnp.tanh(wa), w2))) - 0.5
    lw = -jnp.exp(w)
    a_sig = _sigmoid(a0 + _mm(wa, a2))
    gate = _mm(_sigmoid(gd), g2)
    kk = kr * k_k
    kk = kk / jnp.maximum(jnp.sqrt(_seg_sum(kk * kk, seg)), 1e-12)
    kr = kr * (1.0 + (a_sig - 1.0) * k_a)
    return r, kr, vr, lw, -kk, kk * a_sig, gate


def _rwkv_scaled(r, kr, lw, a_vec, b_vec, sums):
    rows = r.shape[0]
    cums = _mm(sums, lw, NN, 1, 2)
    cum, cum_end = cums[:rows], cums[rows:]
    e_neg = jnp.exp(-cum)
    e_end = jnp.exp(cum_end - cum)
    return (a_vec * jnp.exp(cum - lw), r * jnp.exp(cum), kr * e_neg, b_vec * e_neg,
            kr * e_end, b_vec * e_end, jnp.exp(cum_end))


def _rwkv_intra(units, m, hmasks, n_double):
    rows = units[0][0].shape[0]
    stack = lambda x: _stack_heads(x, hmasks)
    lhs = [jnp.concatenate([at, rt], axis=0).astype(BF16) for at, rt, _, _, _ in units]
    gb = [_mm(l, stack(bt), NT) for l, (_, _, _, bt, _) in zip(lhs, units)]
    gk = [_mm(l, stack(kt), NT) for l, (_, _, kt, _, _) in zip(lhs, units)]
    a_ab = [jnp.where(m["strict"], x[:rows], 0.0) for x in gb]
    a_rb = [jnp.where(m["incl"], x[rows:], 0.0) for x in gb]
    a_ak = [jnp.where(m["strict"], x[:rows], 0.0) for x in gk]
    a_rk = [jnp.where(m["incl"], x[rows:], 0.0) for x in gk]
    tinv = [m["eye"] + a for a in a_ab]
    apow = a_ab
    for _ in range(n_double):
        apow = [_mm(a, _block_diag(a, m["blk"])) for a in apow]
        tinv = [t + _mm(a, _block_diag(t, m["blk"])) for a, t in zip(apow, tinv)]
    v_s = [stack(vp) for _, _, _, _, vp in units]
    av = [_mm(a, v) for a, v in zip(a_ak, v_s)]
    wu = [_mm(t, jnp.concatenate([stack(at), stack(x)], axis=1))
          for t, (at, _, _, _, _), x in zip(tinv, units, av)]
    z = [_mm(a, jnp.concatenate([stack(x[:, :GROUP_W]), stack(x[:, GROUP_W:])], axis=1))
         for a, x in zip(a_rb, wu)]
    y0 = [zz[:, GROUP_W:] + _mm(a, v) for zz, a, v in zip(z, a_rk, v_s)]
    return [(x[:, :GROUP_W], x[:, GROUP_W:], rt + zz[:, :GROUP_W], yy)
            for x, (_, rt, _, _, _), zz, yy in zip(wu, units, z, y0)]


def _rwkv_out(y, r, kr, vp, gate, lnw, lnb, rk, seg):
    inv = 1.0 / RWKV_HEAD
    yc = y - _mm(y, seg, NN, 2, 1) * inv
    var = _mm(yc * yc, seg) * inv
    yn = yc * lax.rsqrt(var + RWKV_GN_EPS) * lnw + lnb
    bonus = _mm(r * kr * rk, seg) * vp
    return (yn + bonus) * gate


def _rwkv_state_in(s_ref, idx, g):
    zero = jnp.zeros((RWKV_HEAD, RWKV_HEAD), F32)
    blocks = []
    for h in range(GROUP_HEADS):
        parts = [zero] * GROUP_HEADS
        parts[h] = s_ref[idx, g * GROUP_HEADS + h]
        blocks.append(jnp.concatenate(parts, axis=1))
    return jnp.concatenate(blocks, axis=0)


def _rwkv_state_out(s_ref, idx, g, s2):
    for h in range(GROUP_HEADS):
        sl = slice(h * RWKV_HEAD, (h + 1) * RWKV_HEAD)
        s_ref[idx, g * GROUP_HEADS + h] = s2[sl, sl]


def _head_block_mask():
    return (_iota((GROUP_W, GROUP_W), 0) // RWKV_HEAD) == (_iota((GROUP_W, GROUP_W), 1) // RWKV_HEAD)


def _rwkv_seq_kernel(u_ref, shift0_ref, s0_ref, mu_ref, w0_ref, w2_ref, a0_ref, a2_ref, g2_ref,
                     kk_ref, ka_ref, rk_ref, lnw_ref, lnb_ref, seg_ref,
                     o_ref, sout_ref, s_scr, prev_scr):
    c = pl.program_id(1)
    n_seq, rows = u_ref.shape[0], u_ref.shape[1]
    groups = [slice(g * GROUP_W, (g + 1) * GROUP_W) for g in range(RWKV_GROUPS)]

    @pl.when(c == 0)
    def _():
        for b in range(n_seq):
            for g in range(RWKV_GROUPS):
                s_scr[b, g] = _rwkv_state_in(s0_ref, b, g)
            prev_scr[b] = shift0_ref[b]

    seg = seg_ref[...]
    m = _chunk_masks(rows, rows)
    hmasks = _head_masks(rows, RWKV_HEAD)
    head_blk = _head_block_mask()
    n_double = rows.bit_length() - 2
    row = _iota((rows, 1), 0)

    tok, units = [], []
    for b in range(n_seq):
        u = u_ref[b]
        prev = jnp.where(row == 0, prev_scr[b], pltpu.roll(u, 1, axis=0))
        prev_scr[b] = u[rows - 1:rows]
        r, kr, vr, lw, a_vec, b_vec, gate = _rwkv_prep(
            u, prev, mu_ref[...], w0_ref[...], w2_ref[...], a0_ref[...], a2_ref[...], g2_ref[...],
            kk_ref[...], ka_ref[...], seg)
        at, rt, kt, bt, ke, be, g_end = _rwkv_scaled(r, kr, lw, a_vec, b_vec, m["sums"])
        tok.append((r, kr, vr, gate, ke, be, g_end))
        units += [(at[:, gl], rt[:, gl], kt[:, gl], bt[:, gl], vr[:, gl]) for gl in groups]

    intra = _rwkv_intra(units, m, hmasks, n_double)
    ids = [(b, g) for b in range(n_seq) for g in range(RWKV_GROUPS)]
    s_old = [s_scr[b, g] for b, g in ids]
    uy = [_mm(jnp.concatenate([w_m, r_m], axis=0), s2, NT) for (w_m, _, r_m, _), s2 in zip(intra, s_old)]
    upd = []
    for (b, g), (_, u0, _, _), x in zip(ids, intra, uy):
        _, _, vr, _, ke, be, _ = tok[b]
        gl = groups[g]
        upd.append(_mm_tn(jnp.concatenate([x[:rows] + u0, vr[:, gl]], axis=0),
                          jnp.concatenate([be[:, gl], ke[:, gl]], axis=0)))
    for (b, g), (_, _, _, y0), x, s2, d in zip(ids, intra, uy, s_old, upd):
        r, kr, vr, gate, _, _, g_end = tok[b]
        gl = groups[g]
        s_scr[b, g] = s2 * g_end[0:1, gl] + jnp.where(head_blk, d, 0.0)
        o_ref[b, :, gl] = _rwkv_out(x[rows:] + y0, r[:, gl], kr[:, gl], vr[:, gl], gate[:, gl],
                                    lnw_ref[:, gl], lnb_ref[:, gl], rk_ref[:, gl], seg)

    @pl.when(c == pl.num_programs(1) - 1)
    def _():
        for b, g in ids:
            _rwkv_state_out(sout_ref, b, g, s_scr[b, g])


def _rwkv_dec_kernel(u_ref, shiftx_ref, s0_ref, mu_ref, w0_ref, w2_ref, a0_ref, a2_ref, g2_ref,
                     kk_ref, ka_ref, rk_ref, lnw_ref, lnb_ref, seg_ref,
                     o_ref, sout_ref, *, seq_len):
    rows = u_ref.shape[0]
    n_seq = rows // seq_len
    u = u_ref[...]
    row = _iota((rows, 1), 0)
    prev = jnp.where(row % seq_len == 0, shiftx_ref[...], pltpu.roll(u, 1, axis=0))
    seg = seg_ref[...]
    r, kr, vr, lw, a_vec, b_vec, gate = _rwkv_prep(
        u, prev, mu_ref[...], w0_ref[...], w2_ref[...], a0_ref[...], a2_ref[...], g2_ref[...],
        kk_ref[...], ka_ref[...], seg)
    m = _chunk_masks(rows, seq_len)
    at, rt, kt, bt, ke, be, g_end = _rwkv_scaled(r, kr, lw, a_vec, b_vec, m["sums"])
    hmasks = _head_masks(rows, RWKV_HEAD)
    head_blk = _head_block_mask()
    n_double = seq_len.bit_length() - 2
    lane2 = _iota((1, 2 * rows), 1)
    u_half = lane2 < rows

    groups = [slice(g * GROUP_W, (g + 1) * GROUP_W) for g in range(RWKV_GROUPS)]
    intra = _rwkv_intra([(at[:, gl], rt[:, gl], kt[:, gl], bt[:, gl], vr[:, gl]) for gl in groups],
                        m, hmasks, n_double)
    for g, gl in enumerate(groups):
        w_m, u0, r_m, y0 = intra[g]
        wr = jnp.concatenate([w_m, r_m], axis=0).astype(BF16)
        bk = jnp.concatenate([be[:, gl], ke[:, gl]], axis=0).astype(BF16)
        base = jnp.concatenate([u0, vr[:, gl]], axis=0).T
        acc = jnp.zeros((GROUP_W, 2 * rows), F32)
        for j in range(n_seq):
            mine = (lane2 % rows) // seq_len == j
            s2 = _rwkv_state_in(s0_ref, j, g)
            uyt = _mm(s2, wr, NT)
            lhs = jnp.where(mine, jnp.where(u_half, uyt, 0.0) + base, 0.0)
            upd = _mm(lhs, bk)
            s_new = s2 * g_end[j * seq_len:j * seq_len + 1, gl] + jnp.where(head_blk, upd, 0.0)
            _rwkv_state_out(sout_ref, j, g, s_new)
            acc = jnp.where(mine, uyt, acc)
        y = acc.T[rows:] + y0
        o_ref[:, gl] = _rwkv_out(y, r[:, gl], kr[:, gl], vr[:, gl], gate[:, gl],
                                 lnw_ref[:, gl], lnb_ref[:, gl], rk_ref[:, gl], seg)


def _rwkv_param_specs():
    vec = lambda n: _const_spec((1, n))
    return [
        vec(RWKV_PROJ),
        vec(RWKV_W),
        _const_spec((LANES, RWKV_W)),
        vec(RWKV_W),
        _const_spec((LANES, RWKV_W)),
        _const_spec((LANES, RWKV_W)),
        vec(RWKV_W), vec(RWKV_W), vec(RWKV_W), vec(RWKV_W), vec(RWKV_W),
        _const_spec((GROUP_W, GROUP_W)),
    ]


def _rwkv_seq(u3, shift0, s0, params, rows, n_seq):
    b, t, _ = u3.shape
    state_spec = pl.BlockSpec((n_seq, RWKV_HEADS, RWKV_HEAD, RWKV_HEAD), lambda i, j: (i, 0, 0, 0))
    return pl.pallas_call(
        _rwkv_seq_kernel,
        grid=(b // n_seq, t // rows),
        in_specs=[
            pl.BlockSpec((n_seq, rows, RWKV_PROJ), lambda i, j: (i, j, 0)),
            pl.BlockSpec((n_seq, 1, RWKV_PROJ), lambda i, j: (i, 0, 0)),
            state_spec,
        ] + _rwkv_param_specs(),
        out_specs=[pl.BlockSpec((n_seq, rows, RWKV_W), lambda i, j: (i, j, 0)), state_spec],
        out_shape=[
            jax.ShapeDtypeStruct((b, t, RWKV_W), F32),
            jax.ShapeDtypeStruct((b, RWKV_HEADS, RWKV_HEAD, RWKV_HEAD), F32),
        ],
        scratch_shapes=[
            pltpu.VMEM((n_seq, RWKV_GROUPS, GROUP_W, GROUP_W), F32),
            pltpu.VMEM((n_seq, 1, RWKV_PROJ), F32),
        ],
        compiler_params=pltpu.CompilerParams(
            dimension_semantics=("arbitrary", "arbitrary"), vmem_limit_bytes=VMEM_LIMIT),
        name="rwkv_seq",
    )(u3, shift0, s0, *params)


def _rwkv_dec(u2, shiftx, s0, params, seq_len, n_seq):
    n = u2.shape[0]
    rows = n_seq * seq_len
    state_spec = pl.BlockSpec((n_seq, RWKV_HEADS, RWKV_HEAD, RWKV_HEAD), lambda i: (i, 0, 0, 0))
    return pl.pallas_call(
        functools.partial(_rwkv_dec_kernel, seq_len=seq_len),
        grid=(n // rows,),
        in_specs=[
            pl.BlockSpec((rows, RWKV_PROJ), lambda i: (i, 0)),
            pl.BlockSpec((rows, RWKV_PROJ), lambda i: (i, 0)),
            state_spec,
        ] + _rwkv_param_specs(),
        out_specs=[pl.BlockSpec((rows, RWKV_W), lambda i: (i, 0)), state_spec],
        out_shape=[
            jax.ShapeDtypeStruct((n, RWKV_W), F32),
            jax.ShapeDtypeStruct(s0.shape, F32),
        ],
        compiler_params=pltpu.CompilerParams(
            dimension_semantics=("arbitrary",), vmem_limit_bytes=VMEM_LIMIT),
        name="rwkv_dec",
    )(u2, shiftx, s0, *params)


def _post_kernel(x_ref, og_ref, or_ref, p_ref, wo_ref, nffn_ref, wg_ref, wu_ref, wd_ref,
                 nple_ref, wpg_ref, wpp_ref, nf_ref, y_ref):
    o = jnp.concatenate([og_ref[...], or_ref[...]], axis=1).astype(BF16)
    x = x_ref[...] + jnp.dot(o, wo_ref[...], preferred_element_type=F32)
    h2 = _rms(x, nffn_ref[...]).astype(BF16)
    gate = jnp.dot(h2, wg_ref[...], preferred_element_type=F32)
    up = jnp.dot(h2, wu_ref[...], preferred_element_type=F32)
    act = (gate * _sigmoid(gate) * up).astype(BF16)
    x = x + jnp.dot(act, wd_ref[...], preferred_element_type=F32)
    h3 = _rms(x, nple_ref[...]).astype(BF16)
    pg = _sigmoid(jnp.dot(h3, wpg_ref[...], preferred_element_type=F32))
    pp = jnp.dot(p_ref[...].astype(BF16), wpp_ref[...], preferred_element_type=F32)
    x = x + pg * pp
    y_ref[...] = _rms(x, nf_ref[...])


def _post(x2d, og, orw, p2d, weights, tm):
    n = x2d.shape[0]
    wo, nffn, wg, wu, wd, nple, wpg, wpp, nf = weights
    tok = lambda w: pl.BlockSpec((tm, w), lambda i: (i, 0))
    return pl.pallas_call(
        _post_kernel,
        grid=(n // tm,),
        in_specs=[
            tok(D_MODEL), tok(GLA_W), tok(RWKV_W), tok(PLE_DIM),
            _const_spec((D_MODEL, D_MODEL)), _const_spec((1, D_MODEL)),
            _const_spec((D_MODEL, D_FF)), _const_spec((D_MODEL, D_FF)), _const_spec((D_FF, D_MODEL)),
            _const_spec((1, D_MODEL)), _const_spec((D_MODEL, D_MODEL)), _const_spec((PLE_DIM, D_MODEL)),
            _const_spec((1, D_MODEL)),
        ],
        out_specs=tok(D_MODEL),
        out_shape=jax.ShapeDtypeStruct((n, D_MODEL), F32),
        compiler_params=pltpu.CompilerParams(
            dimension_semantics=("arbitrary",), vmem_limit_bytes=VMEM_LIMIT),
        name="post",
    )(x2d, og, orw, p2d, wo, nffn, wg, wu, wd, nple, wpg, wpp, nf)


PROMPT_CHUNK = 64
PROMPT_SEQS_PER_STEP = 4
DEC_TILE_SEQS = 16
TOKEN_TILE = 256


def kernel(x_prompt, x_sample, state_gla, state_rwkv, state_shift, p_prompt, p_sample, norm_mix, w_in, gla_gk_up, gla_gk_bias, gla_norm, rwkv_mu, rwkv_w0, rwkv_w2, rwkv_a0, rwkv_a2, rwkv_g2, rwkv_k_k, rwkv_k_a, rwkv_r_k, rwkv_ln_w, rwkv_ln_b, w_out, norm_ffn, w_gate, w_up, w_down, norm_ple, w_ple_gate, w_ple_proj, norm_final):
    assert w_in.shape[0] == 1
    i = 0
    rowv = lambda a: a.astype(F32).reshape(1, -1)
    zeros = lambda r, c: jnp.zeros((r, c), F32)
    w_in_i = w_in[i]
    w_gla = jnp.concatenate(
        [w_in_i[:, :GLA_PROJ], zeros(D_MODEL, GLA_COLS - GLA_PROJ)], axis=1).astype(BF16)
    w_rw = w_in_i[:, GLA_PROJ:].astype(BF16)
    gk_up = jnp.concatenate(
        [gla_gk_up[i].astype(F32), zeros(LANES - GLA_GATE_RANK, GLA_K_W)], axis=0).astype(BF16)
    seg = jnp.arange(GROUP_W)[:, None] // RWKV_HEAD == jnp.arange(GROUP_W)[None, :] // RWKV_HEAD
    rwkv_params = (
        rowv(rwkv_mu[i]), rowv(rwkv_w0[i]),
        jnp.concatenate([rwkv_w2[i].astype(F32), zeros(64, RWKV_W)], axis=0).astype(BF16),
        rowv(rwkv_a0[i]),
        jnp.concatenate([zeros(64, RWKV_W), rwkv_a2[i].astype(F32)], axis=0).astype(BF16),
        rwkv_g2[i].astype(BF16),
        rowv(rwkv_k_k[i]), rowv(rwkv_k_a[i]), rowv(rwkv_r_k[i]), rowv(rwkv_ln_w[i]), rowv(rwkv_ln_b[i]),
        seg.astype(BF16),
    )
    post_w = (
        w_out[i].astype(BF16), rowv(norm_ffn[i]), w_gate[i].astype(BF16), w_up[i].astype(BF16),
        w_down[i].astype(BF16), rowv(norm_ple[i]), w_ple_gate[i].astype(BF16),
        w_ple_proj[i].astype(BF16), rowv(norm_final),
    )
    gla_w = (gk_up, rowv(gla_gk_bias[i]), rowv(gla_norm[i]))
    g_mix = rowv(norm_mix[i])

    bp, tp, _ = x_prompt.shape
    xp = x_prompt.astype(F32).reshape(bp * tp, D_MODEL)
    ug, ur = _proj(xp, g_mix, w_gla, w_rw, TOKEN_TILE)
    og, gla_p = _gla_seq(ug.reshape(bp, tp, GLA_COLS),
                         jnp.zeros((bp, GLA_HEADS * GLA_DK, GLA_DV), F32), gla_w,
                         PROMPT_CHUNK, PROMPT_SEQS_PER_STEP)
    ur3 = ur.reshape(bp, tp, RWKV_PROJ)
    orw, rwkv_p = _rwkv_seq(ur3, jnp.zeros((bp, 1, RWKV_PROJ), F32),
                            jnp.zeros((bp, RWKV_HEADS, RWKV_HEAD, RWKV_HEAD), F32), rwkv_params,
                            PROMPT_CHUNK, PROMPT_SEQS_PER_STEP)
    shift_p = ur3[:, tp - 1]
    yp = _post(xp, og.reshape(bp * tp, GLA_W), orw.reshape(bp * tp, RWKV_W),
               p_prompt[i].reshape(bp * tp, PLE_DIM), post_w, TOKEN_TILE)

    bs, ts, _ = x_sample.shape
    xs = x_sample.astype(F32).reshape(bs * ts, D_MODEL)
    ug, ur = _proj(xs, g_mix, w_gla, w_rw, TOKEN_TILE)
    og, gla_s = _gla_dec(ug, state_gla[i].astype(F32).reshape(bs, GLA_HEADS * GLA_DK, GLA_DV),
                         gla_w, ts, DEC_TILE_SEQS)
    shiftx = jnp.pad(state_shift[i].astype(F32)[:, None, :], ((0, 0), (0, ts - 1), (0, 0)))
    orw, rwkv_s = _rwkv_dec(ur, shiftx.reshape(bs * ts, RWKV_PROJ), state_rwkv[i].astype(F32),
                            rwkv_params, ts, DEC_TILE_SEQS)
    shift_s = ur.reshape(bs, ts, RWKV_PROJ)[:, ts - 1]
    ys = _post(xs, og, orw, p_sample[i].reshape(bs * ts, PLE_DIM), post_w, TOKEN_TILE)

    gla_shape = (1, -1, GLA_HEADS, GLA_DK, GLA_DV)
    return (yp.reshape(bp, tp, D_MODEL).astype(x_prompt.dtype),
            ys.reshape(bs, ts, D_MODEL).astype(x_sample.dtype),
            gla_p.reshape(gla_shape).astype(state_gla.dtype), rwkv_p[None].astype(state_rwkv.dtype),
            shift_p[None].astype(state_shift.dtype),
            gla_s.reshape(gla_shape).astype(state_gla.dtype), rwkv_s[None].astype(state_rwkv.dtype),
            shift_s[None].astype(state_shift.dtype))
```

```python
import functools

import jax
import jax.numpy as jnp
from jax import lax
from jax.experimental import pallas as pl
from jax.experimental.pallas import tpu as pltpu

F32 = jnp.float32
BF16 = jnp.bfloat16

D_MODEL = 1024
GLA_HEADS = 4
GLA_DK = 64
GLA_DV = 128
GLA_K_W = GLA_HEADS * GLA_DK
GLA_W = GLA_HEADS * GLA_DV
GLA_GATE_RANK = 16
GLA_GATE_NORM = 16.0
GLA_MAIN = 2 * GLA_K_W + 2 * GLA_W
GLA_PROJ = GLA_MAIN + GLA_GATE_RANK
LANES = 128
GLA_COLS = GLA_MAIN + LANES
RWKV_HEAD = 64
RWKV_HEADS = 8
RWKV_W = RWKV_HEADS * RWKV_HEAD
RWKV_PROJ = 3 * RWKV_W + 64 + 64 + 128
D_FF = 2816
PLE_DIM = 256
EPS = 1e-6
RWKV_GN_EPS = 64e-5

VMEM_LIMIT = 56 * 1024 * 1024

NN = ((1,), (0,))
NT = ((1,), (1,))


def _split(x, n):
    parts = []
    r = x
    for i in range(n):
        p = r.astype(BF16)
        parts.append(p)
        if i + 1 < n:
            r = r - p.astype(F32)
    return parts


def _mm(a, b, dims=NN, pa=1, pb=1):
    pieces_a = _split(a, pa)
    pieces_b = _split(b, pb)
    n = max(pa, pb)
    acc = None
    for i, ai in enumerate(pieces_a):
        for j, bj in enumerate(pieces_b):
            if i + j < n:
                t = lax.dot_general(ai, bj, (dims, ((), ())), preferred_element_type=F32)
                acc = t if acc is None else acc + t
    return acc


def _mm_tn(a, b, pa=1, pb=1):
    rows = a.shape[0]
    pad = (-rows) % LANES
    if pad:
        a = jnp.concatenate([a, jnp.zeros((pad, a.shape[1]), a.dtype)], axis=0)
        b = jnp.concatenate([b, jnp.zeros((pad, b.shape[1]), b.dtype)], axis=0)
    return _mm(a.T, b, NN, pa, pb)


def _iota(shape, dim):
    return lax.broadcasted_iota(jnp.int32, shape, dim)


def _softplus(z):
    return jnp.maximum(z, 0.0) + jnp.log(1.0 + jnp.exp(-jnp.abs(z)))


def _sigmoid(z):
    return 1.0 / (1.0 + jnp.exp(-z))


def _rms(x, g):
    return x * lax.rsqrt(jnp.mean(x * x, axis=-1, keepdims=True) + EPS) * g


GROUP_HEADS = 2
GROUP_W = GROUP_HEADS * RWKV_HEAD
RWKV_GROUPS = RWKV_W // GROUP_W
SEG_W = 2 * LANES


def _head_masks(rows, head_w, heads):
    lane = _iota((rows, heads * head_w), 1)
    return [jnp.where(lane // head_w == h, 1.0, 0.0).astype(BF16) for h in range(heads)]


def _stack_heads(x, hmasks):
    xb = x.astype(BF16)
    return jnp.concatenate([xb * m for m in hmasks], axis=0)


def _block_diag(x_cat, blk):
    heads = blk.shape[0] // x_cat.shape[0]
    return jnp.concatenate([x_cat.astype(BF16)] * heads, axis=0) * blk


def _chunk_masks(rows, seq_len, heads):
    t = _iota((rows, heads * rows), 0)
    s = _iota((rows, heads * rows), 1) % rows
    strict, incl = t > s, t >= s
    t2 = _iota((2 * rows, rows), 0)
    s2 = _iota((2 * rows, rows), 1)
    cum_rows = (t2 < rows) & (t2 >= s2)
    tot_rows = t2 >= rows
    if seq_len < rows:
        same = (t // seq_len) == (s // seq_len)
        strict, incl = strict & same, incl & same
        same2 = ((t2 % rows) // seq_len) == (s2 // seq_len)
        cum_rows, tot_rows = cum_rows & same2, tot_rows & same2
    big = heads * rows
    blk = jnp.where(_iota((big, big), 0) // rows == _iota((big, big), 1) // rows, 1.0, 0.0)
    return dict(strict=strict, incl=incl, eye=jnp.where(t == s, 1.0, 0.0),
                sums=jnp.where(cum_rows | tot_rows, 1.0, 0.0).astype(BF16), blk=blk.astype(BF16))


def _proj_kernel(x_ref, g_ref, wg_ref, wr_ref, ug_ref, ur_ref):
    h = _rms(x_ref[...], g_ref[...]).astype(BF16)
    ug_ref[...] = jnp.dot(h, wg_ref[...], preferred_element_type=F32)
    ur_ref[...] = jnp.dot(h, wr_ref[...], preferred_element_type=F32)


def _const_spec(shape):
    return pl.BlockSpec(shape, lambda *_: (0,) * len(shape), pipeline_mode=pl.Buffered(1))


def _proj(x2d, g, w_gla, w_rw, tm):
    n = x2d.shape[0]
    return pl.pallas_call(
        _proj_kernel,
        grid=(n // tm,),
        in_specs=[
            pl.BlockSpec((tm, D_MODEL), lambda i: (i, 0)),
            _const_spec((1, D_MODEL)),
            _const_spec((D_MODEL, GLA_COLS)),
            _const_spec((D_MODEL, RWKV_PROJ)),
        ],
        out_specs=[
            pl.BlockSpec((tm, GLA_COLS), lambda i: (i, 0)),
            pl.BlockSpec((tm, RWKV_PROJ), lambda i: (i, 0)),
        ],
        out_shape=[
            jax.ShapeDtypeStruct((n, GLA_COLS), F32),
            jax.ShapeDtypeStruct((n, RWKV_PROJ), F32),
        ],
        compiler_params=pltpu.CompilerParams(
            dimension_semantics=("arbitrary",), vmem_limit_bytes=VMEM_LIMIT),
        name="proj",
    )(x2d, g, w_gla, w_rw)


def _gla_prep(u, gkup, gkb, sums):
    rows = u.shape[0]
    q = u[:, 0:GLA_K_W] * (GLA_DK ** -0.5)
    k = u[:, GLA_K_W:2 * GLA_K_W]
    v = u[:, 2 * GLA_K_W:2 * GLA_K_W + GLA_W]
    gate = u[:, 2 * GLA_K_W + GLA_W:GLA_MAIN]
    log_a = -_softplus(-(_mm(u[:, GLA_MAIN:GLA_COLS], gkup) + gkb)) * (1.0 / GLA_GATE_NORM)
    cums = _mm(sums, log_a, NN, 1, 2)
    cum, cum_end = cums[:rows], cums[rows:]
    return (q * jnp.exp(cum), k * jnp.exp(-cum), k * jnp.exp(cum_end - cum), v, gate,
            jnp.exp(cum_end))


def _gla_intra(q_i, k_i, v, m, kmasks, vmasks):
    scores = jnp.where(m["incl"], _mm(q_i, _stack_heads(k_i, kmasks), NT), 0.0)
    return _mm(scores, _stack_heads(v, vmasks))


def _gla_out(o, gate, gnorm):
    heads = [slice(h * GLA_DV, (h + 1) * GLA_DV) for h in range(GLA_HEADS)]
    return jnp.concatenate(
        [_rms(o[:, hl], gnorm) * (gate[:, hl] * _sigmoid(gate[:, hl])) for hl in heads], axis=1)


def _gla_block_mask():
    return (_iota((GLA_K_W, GLA_W), 0) // GLA_DK) == (_iota((GLA_K_W, GLA_W), 1) // GLA_DV)


def _gla_state_in(s, blk):
    return jnp.where(blk, jnp.concatenate([s] * GLA_HEADS, axis=1), 0.0)


def _gla_state_out(s_bd):
    heads = [s_bd[:, h * GLA_DV:(h + 1) * GLA_DV] for h in range(GLA_HEADS)]
    return (heads[0] + heads[1]) + (heads[2] + heads[3])


def _lane_tiled_t(x):
    pad = LANES - x.shape[0]
    if pad:
        x = jnp.concatenate([x, jnp.zeros((pad, x.shape[1]), x.dtype)], axis=0)
    return x.T


def _gla_seq_kernel(u_ref, s0_ref, gkup_ref, gkb_ref, gn_ref, o_ref, sout_ref, s_scr):
    c = pl.program_id(1)
    n_seq, rows = u_ref.shape[0], u_ref.shape[1]
    blk = _gla_block_mask()

    @pl.when(c == 0)
    def _():
        for b in range(n_seq):
            s_scr[b] = _gla_state_in(s0_ref[b], blk)

    m = _chunk_masks(rows, rows, GLA_HEADS)
    kmasks = _head_masks(rows, GLA_DK, GLA_HEADS)
    vmasks = _head_masks(rows, GLA_DV, GLA_HEADS)
    tok = [_gla_prep(u_ref[b], gkup_ref[...], gkb_ref[...], m["sums"]) for b in range(n_seq)]
    s_old = [s_scr[b] for b in range(n_seq)]
    intra = [_gla_intra(q_i, k_i, v, m, kmasks, vmasks) for q_i, k_i, _, v, _, _ in tok]
    inter = [_mm(t[0], s) for t, s in zip(tok, s_old)]
    kv = [_mm_tn(k_e, v) for _, _, k_e, v, _, _ in tok]
    for b in range(n_seq):
        gate, g_end = tok[b][4], tok[b][5]
        o_ref[b] = _gla_out(intra[b] + inter[b], gate, gn_ref[...])
        dec = jnp.concatenate([_lane_tiled_t(jnp.broadcast_to(g_end[0:1], (LANES, GLA_K_W)))] * GLA_HEADS,
                              axis=1)
        s_scr[b] = dec * s_old[b] + jnp.where(blk, kv[b], 0.0)

    @pl.when(c == pl.num_programs(1) - 1)
    def _():
        for b in range(n_seq):
            sout_ref[b] = _gla_state_out(s_scr[b])


def _gla_dec_kernel(u_ref, s0_ref, gkup_ref, gkb_ref, gn_ref, o_ref, sout_ref, *, seq_len):
    rows = u_ref.shape[0]
    n_seq = rows // seq_len
    blk = _gla_block_mask()
    m = _chunk_masks(rows, seq_len, GLA_HEADS)
    q_i, k_i, k_e, v, gate, g_end = _gla_prep(u_ref[...], gkup_ref[...], gkb_ref[...], m["sums"])
    o = _gla_intra(q_i, k_i, v, m, _head_masks(rows, GLA_DK, GLA_HEADS),
                   _head_masks(rows, GLA_DV, GLA_HEADS))
    k_et = _lane_tiled_t(k_e).astype(BF16)
    dec_t = _lane_tiled_t(g_end)
    v_pad = jnp.concatenate([v, jnp.zeros((LANES - rows, GLA_W), F32)], axis=0)
    row = _iota((rows, 1), 0)
    row_pad = _iota((LANES, 1), 0)
    for j in range(n_seq):
        s_bd = _gla_state_in(s0_ref[j], blk)
        o = o + _mm(jnp.where(row // seq_len == j, q_i, 0.0), s_bd)
        kv = _mm(k_et, jnp.where(row_pad // seq_len == j, v_pad, 0.0))
        first = j * seq_len
        sout_ref[j] = _gla_state_out(dec_t[:, first:first + 1] * s_bd + jnp.where(blk, kv, 0.0))
    o_ref[...] = _gla_out(o, gate, gn_ref[...])


def _gla_param_specs():
    return [_const_spec((LANES, GLA_K_W)), _const_spec((1, GLA_K_W)), _const_spec((1, GLA_DV))]


def _gla_seq(u3, s0, params, rows, n_seq):
    b, t, _ = u3.shape
    sdim = GLA_HEADS * GLA_DK
    state_spec = pl.BlockSpec((n_seq, sdim, GLA_DV), lambda i, j: (i, 0, 0))
    return pl.pallas_call(
        _gla_seq_kernel,
        grid=(b // n_seq, t // rows),
        in_specs=[pl.BlockSpec((n_seq, rows, GLA_COLS), lambda i, j: (i, j, 0)), state_spec]
        + _gla_param_specs(),
        out_specs=[pl.BlockSpec((n_seq, rows, GLA_W), lambda i, j: (i, j, 0)), state_spec],
        out_shape=[
            jax.ShapeDtypeStruct((b, t, GLA_W), F32),
            jax.ShapeDtypeStruct((b, sdim, GLA_DV), F32),
        ],
        scratch_shapes=[pltpu.VMEM((n_seq, sdim, GLA_W), F32)],
        compiler_params=pltpu.CompilerParams(
            dimension_semantics=("arbitrary", "arbitrary"), vmem_limit_bytes=VMEM_LIMIT),
        name="gla_seq",
    )(u3, s0, *params)


def _gla_dec(u2, s0, params, seq_len, n_seq):
    n = u2.shape[0]
    rows = n_seq * seq_len
    sdim = GLA_HEADS * GLA_DK
    state_spec = pl.BlockSpec((n_seq, sdim, GLA_DV), lambda i: (i, 0, 0))
    return pl.pallas_call(
        functools.partial(_gla_dec_kernel, seq_len=seq_len),
        grid=(n // rows,),
        in_specs=[pl.BlockSpec((rows, GLA_COLS), lambda i: (i, 0)), state_spec] + _gla_param_specs(),
        out_specs=[pl.BlockSpec((rows, GLA_W), lambda i: (i, 0)), state_spec],
        out_shape=[
            jax.ShapeDtypeStruct((n, GLA_W), F32),
            jax.ShapeDtypeStruct(s0.shape, F32),
        ],
        compiler_params=pltpu.CompilerParams(
            dimension_semantics=("arbitrary",), vmem_limit_bytes=VMEM_LIMIT),
        name="gla_dec",
    )(u2, s0, *params)


def _seg_sum(x, seg, pa=1):
    return jnp.concatenate(
        [_mm(x[:, i:i + SEG_W], seg, NN, pa, 1) for i in range(0, RWKV_W, SEG_W)], axis=1)


def _rwkv_prep(u, prev, mu, w0, w2, a0, a2, g2, k_k, k_a, seg):
    xr = u + mu * (prev - u)
    r = xr[:, 0:RWKV_W]
    kr = xr[:, RWKV_W:2 * RWKV_W]
    vr = xr[:, 2 * RWKV_W:3 * RWKV_W]
    wa = xr[:, 3 * RWKV_W:3 * RWKV_W + LANES]
    gd = xr[:, 3 * RWKV_W + LANES:RWKV_PROJ]
    w = -_softplus(-(w0 + _mm(jnp.tanh(wa), w2))) - 0.5
    lw = -jnp.exp(w)
    a_sig = _sigmoid(a0 + _mm(wa, a2))
    gate = _mm(_sigmoid(gd), g2)
    kk = kr * k_k
    kk = kk / jnp.maximum(jnp.sqrt(_seg_sum(kk * kk, seg)), 1e-12)
    kr = kr * (1.0 + (a_sig - 1.0) * k_a)
    return r, kr, vr, lw, -kk, kk * a_sig, gate


def _rwkv_scaled(r, kr, lw, a_vec, b_vec, sums):
    rows = r.shape[0]
    cums = _mm(sums, lw, NN, 1, 2)
    cum, cum_end = cums[:rows], cums[rows:]
    e_neg = jnp.exp(-cum)
    e_end = jnp.exp(cum_end - cum)
    return (a_vec * jnp.exp(cum - lw), r * jnp.exp(cum), kr * e_neg, b_vec * e_neg,
            kr * e_end, b_vec * e_end, jnp.exp(cum_end))


def _rwkv_intra(units, m, hmasks, n_double):
    rows = units[0][0].shape[0]
    stack = lambda x: _stack_heads(x, hmasks)
    lhs = [jnp.concatenate([at, rt], axis=0).astype(BF16) for at, rt, _, _, _ in units]
    gb = [_mm(l, stack(bt), NT) for l, (_, _, _, bt, _) in zip(lhs, units)]
    gk = [_mm(l, stack(kt), NT) for l, (_, _, kt, _, _) in zip(lhs, units)]
    a_ab = [jnp.where(m["strict"], x[:rows], 0.0) for x in gb]
    a_rb = [jnp.where(m["incl"], x[rows:], 0.0) for x in gb]
    a_ak = [jnp.where(m["strict"], x[:rows], 0.0) for x in gk]
    a_rk = [jnp.where(m["incl"], x[rows:], 0.0) for x in gk]
    tinv = [m["eye"] + a for a in a_ab]
    apow = a_ab
    for _ in range(n_double):
        apow = [_mm(a, _block_diag(a, m["blk"])) for a in apow]
        tinv = [t + _mm(a, _block_diag(t, m["blk"])) for a, t in zip(apow, tinv)]
    v_s = [stack(vp) for _, _, _, _, vp in units]
    av = [_mm(a, v) for a, v in zip(a_ak, v_s)]
    wu = [_mm(t, jnp.concatenate([stack(at), stack(x)], axis=1))
          for t, (at, _, _, _, _), x in zip(tinv, units, av)]
    z = [_mm(a, jnp.concatenate([stack(x[:, :GROUP_W]), stack(x[:, GROUP_W:])], axis=1))
         for a, x in zip(a_rb, wu)]
    y0 = [zz[:, GROUP_W:] + _mm(a, v) for zz, a, v in zip(z, a_rk, v_s)]
    return [(x[:, :GROUP_W], x[:, GROUP_W:], rt + zz[:, :GROUP_W], yy)
            for x, (_, rt, _, _, _), zz, yy in zip(wu, units, z, y0)]


def _rwkv_out(y, r, kr, vr, gate, lnw, lnb, rk, seg):
    inv = 1.0 / RWKV_HEAD
    yc = y - _seg_sum(y, seg, 2) * inv
    var = _seg_sum(yc * yc, seg) * inv
    yn = yc * lax.rsqrt(var + RWKV_GN_EPS) * lnw + lnb
    bonus = _seg_sum(r * kr * rk, seg) * vr
    return (yn + bonus) * gate


def _rwkv_state_in(s_ref, idx, g):
    zero = jnp.zeros((RWKV_HEAD, RWKV_HEAD), F32)
    blocks = []
    for h in range(GROUP_HEADS):
        parts = [zero] * GROUP_HEADS
        parts[h] = s_ref[idx, g * GROUP_HEADS + h]
        blocks.append(jnp.concatenate(parts, axis=1))
    return jnp.concatenate(blocks, axis=0)


def _rwkv_state_out(s_ref, idx, g, s2):
    for h in range(GROUP_HEADS):
        sl = slice(h * RWKV_HEAD, (h + 1) * RWKV_HEAD)
        s_ref[idx, g * GROUP_HEADS + h] = s2[sl, sl]


def _head_block_mask():
    return (_iota((GROUP_W, GROUP_W), 0) // RWKV_HEAD) == (_iota((GROUP_W, GROUP_W), 1) // RWKV_HEAD)


def _rwkv_seq_kernel(u_ref, shift0_ref, s0_ref, mu_ref, w0_ref, w2_ref, a0_ref, a2_ref, g2_ref,
                     kk_ref, ka_ref, rk_ref, lnw_ref, lnb_ref, seg_ref,
                     o_ref, sout_ref, s_scr, prev_scr):
    c = pl.program_id(1)
    n_seq, rows = u_ref.shape[0], u_ref.shape[1]
    groups = [slice(g * GROUP_W, (g + 1) * GROUP_W) for g in range(RWKV_GROUPS)]

    @pl.when(c == 0)
    def _():
        for b in range(n_seq):
            for g in range(RWKV_GROUPS):
                s_scr[b, g] = _rwkv_state_in(s0_ref, b, g)
            prev_scr[b] = shift0_ref[b]

    seg = seg_ref[...]
    m = _chunk_masks(rows, rows, GROUP_HEADS)
    hmasks = _head_masks(rows, RWKV_HEAD, GROUP_HEADS)
    head_blk = _head_block_mask()
    n_double = rows.bit_length() - 2
    row = _iota((rows, 1), 0)

    tok, units = [], []
    for b in range(n_seq):
        u = u_ref[b]
        prev = jnp.where(row == 0, prev_scr[b], pltpu.roll(u, 1, axis=0))
        prev_scr[b] = u[rows - 1:rows]
        r, kr, vr, lw, a_vec, b_vec, gate = _rwkv_prep(
            u, prev, mu_ref[...], w0_ref[...], w2_ref[...], a0_ref[...], a2_ref[...], g2_ref[...],
            kk_ref[...], ka_ref[...], seg)
        at, rt, kt, bt, ke, be, g_end = _rwkv_scaled(r, kr, lw, a_vec, b_vec, m["sums"])
        tok.append((r, kr, vr, gate, ke, be, g_end))
        units += [(at[:, gl], rt[:, gl], kt[:, gl], bt[:, gl], vr[:, gl]) for gl in groups]

    intra = _rwkv_intra(units, m, hmasks, n_double)
    ids = [(b, g) for b in range(n_seq) for g in range(RWKV_GROUPS)]
    s_old = [s_scr[b, g] for b, g in ids]
    uy = [_mm(jnp.concatenate([w_m, r_m], axis=0), s2, NT) for (w_m, _, r_m, _), s2 in zip(intra, s_old)]
    upd = []
    for (b, g), (_, u0, _, _), x in zip(ids, intra, uy):
        _, _, vr, _, ke, be, _ = tok[b]
        gl = groups[g]
        upd.append(_mm_tn(jnp.concatenate([x[:rows] + u0, vr[:, gl]], axis=0),
                          jnp.concatenate([be[:, gl], ke[:, gl]], axis=0)))
    for (b, g), s2, d in zip(ids, s_old, upd):
        s_scr[b, g] = s2 * tok[b][6][0:1, groups[g]] + jnp.where(head_blk, d, 0.0)
    for b in range(n_seq):
        r, kr, vr, gate, _, _, _ = tok[b]
        y = jnp.concatenate([uy[i][rows:] + intra[i][3] for i, (bb, _) in enumerate(ids) if bb == b],
                            axis=1)
        o_ref[b] = _rwkv_out(y, r, kr, vr, gate, lnw_ref[...], lnb_ref[...], rk_ref[...], seg)

    @pl.when(c == pl.num_programs(1) - 1)
    def _():
        for b, g in ids:
            _rwkv_state_out(sout_ref, b, g, s_scr[b, g])


def _rwkv_dec_kernel(u_ref, shiftx_ref, s0_ref, mu_ref, w0_ref, w2_ref, a0_ref, a2_ref, g2_ref,
                     kk_ref, ka_ref, rk_ref, lnw_ref, lnb_ref, seg_ref,
                     o_ref, sout_ref, *, seq_len):
    rows = u_ref.shape[0]
    n_seq = rows // seq_len
    u = u_ref[...]
    row = _iota((rows, 1), 0)
    prev = jnp.where(row % seq_len == 0, shiftx_ref[...], pltpu.roll(u, 1, axis=0))
    seg = seg_ref[...]
    r, kr, vr, lw, a_vec, b_vec, gate = _rwkv_prep(
        u, prev, mu_ref[...], w0_ref[...], w2_ref[...], a0_ref[...], a2_ref[...], g2_ref[...],
        kk_ref[...], ka_ref[...], seg)
    m = _chunk_masks(rows, seq_len, GROUP_HEADS)
    at, rt, kt, bt, ke, be, g_end = _rwkv_scaled(r, kr, lw, a_vec, b_vec, m["sums"])
    hmasks = _head_masks(rows, RWKV_HEAD, GROUP_HEADS)
    head_blk = _head_block_mask()
    n_double = seq_len.bit_length() - 2
    lane2 = _iota((1, 2 * rows), 1)
    u_half = lane2 < rows

    groups = [slice(g * GROUP_W, (g + 1) * GROUP_W) for g in range(RWKV_GROUPS)]
    intra = _rwkv_intra([(at[:, gl], rt[:, gl], kt[:, gl], bt[:, gl], vr[:, gl]) for gl in groups],
                        m, hmasks, n_double)
    wr = [jnp.concatenate([w_m, r_m], axis=0).astype(BF16) for w_m, _, r_m, _ in intra]
    bk = [jnp.concatenate([be[:, gl], ke[:, gl]], axis=0).astype(BF16) for gl in groups]
    base = [jnp.concatenate([x[1], vr[:, gl]], axis=0).T for x, gl in zip(intra, groups)]
    mine = [(lane2 % rows) // seq_len == j for j in range(n_seq)]
    ids = [(j, g) for j in range(n_seq) for g in range(RWKV_GROUPS)]
    s_old = [_rwkv_state_in(s0_ref, j, g) for j, g in ids]
    uyt = [_mm(s2, wr[g], NT) for (j, g), s2 in zip(ids, s_old)]
    upd = [_mm(jnp.where(mine[j], jnp.where(u_half, x, 0.0) + base[g], 0.0), bk[g])
           for (j, g), x in zip(ids, uyt)]
    for (j, g), s2, d in zip(ids, s_old, upd):
        first = j * seq_len
        _rwkv_state_out(sout_ref, j, g,
                        s2 * g_end[first:first + 1, groups[g]] + jnp.where(head_blk, d, 0.0))
    ys = []
    for g in range(RWKV_GROUPS):
        acc = jnp.zeros((GROUP_W, 2 * rows), F32)
        for (j, gg), x in zip(ids, uyt):
            if gg == g:
                acc = jnp.where(mine[j], x, acc)
        ys.append(acc.T[rows:] + intra[g][3])
    o_ref[...] = _rwkv_out(jnp.concatenate(ys, axis=1), r, kr, vr, gate,
                           lnw_ref[...], lnb_ref[...], rk_ref[...], seg)


def _rwkv_param_specs():
    vec = lambda n: _const_spec((1, n))
    return [
        vec(RWKV_PROJ),
        vec(RWKV_W),
        _const_spec((LANES, RWKV_W)),
        vec(RWKV_W),
        _const_spec((LANES, RWKV_W)),
        _const_spec((LANES, RWKV_W)),
        vec(RWKV_W), vec(RWKV_W), vec(RWKV_W), vec(RWKV_W), vec(RWKV_W),
        _const_spec((SEG_W, SEG_W)),
    ]


def _rwkv_seq(u3, shift0, s0, params, rows, n_seq):
    b, t, _ = u3.shape
    state_spec = pl.BlockSpec((n_seq, RWKV_HEADS, RWKV_HEAD, RWKV_HEAD), lambda i, j: (i, 0, 0, 0))
    return pl.pallas_call(
        _rwkv_seq_kernel,
        grid=(b // n_seq, t // rows),
        in_specs=[
            pl.BlockSpec((n_seq, rows, RWKV_PROJ), lambda i, j: (i, j, 0)),
            pl.BlockSpec((n_seq, 1, RWKV_PROJ), lambda i, j: (i, 0, 0)),
            state_spec,
        ] + _rwkv_param_specs(),
        out_specs=[pl.BlockSpec((n_seq, rows, RWKV_W), lambda i, j: (i, j, 0)), state_spec],
        out_shape=[
            jax.ShapeDtypeStruct((b, t, RWKV_W), F32),
            jax.ShapeDtypeStruct((b, RWKV_HEADS, RWKV_HEAD, RWKV_HEAD), F32),
        ],
        scratch_shapes=[
            pltpu.VMEM((n_seq, RWKV_GROUPS, GROUP_W, GROUP_W), F32),
            pltpu.VMEM((n_seq, 1, RWKV_PROJ), F32),
        ],
        compiler_params=pltpu.CompilerParams(
            dimension_semantics=("arbitrary", "arbitrary"), vmem_limit_bytes=VMEM_LIMIT),
        name="rwkv_seq",
    )(u3, shift0, s0, *params)


def _rwkv_dec(u2, shiftx, s0, params, seq_len, n_seq):
    n = u2.shape[0]
    rows = n_seq * seq_len
    state_spec = pl.BlockSpec((n_seq, RWKV_HEADS, RWKV_HEAD, RWKV_HEAD), lambda i: (i, 0, 0, 0))
    return pl.pallas_call(
        functools.partial(_rwkv_dec_kernel, seq_len=seq_len),
        grid=(n // rows,),
        in_specs=[
            pl.BlockSpec((rows, RWKV_PROJ), lambda i: (i, 0)),
            pl.BlockSpec((rows, RWKV_PROJ), lambda i: (i, 0)),
            state_spec,
        ] + _rwkv_param_specs(),
        out_specs=[pl.BlockSpec((rows, RWKV_W), lambda i: (i, 0)), state_spec],
        out_shape=[
            jax.ShapeDtypeStruct((n, RWKV_W), F32),
            jax.ShapeDtypeStruct(s0.shape, F32),
        ],
        compiler_params=pltpu.CompilerParams(
            dimension_semantics=("arbitrary",), vmem_limit_bytes=VMEM_LIMIT),
        name="rwkv_dec",
    )(u2, shiftx, s0, *params)


FF_CHUNK = D_FF // 2


def _post_kernel(x_ref, og_ref, or_ref, p_ref, wo_ref, nffn_ref, wg_ref, wu_ref, wd_ref,
                 nple_ref, wpg_ref, wpp_ref, nf_ref, y_ref):
    half = x_ref.shape[0] // 2
    parts = [slice(0, half), slice(half, 2 * half)]
    dot = lambda a, b: jnp.dot(a, b, preferred_element_type=F32)
    o = [jnp.concatenate([og_ref[p, :], or_ref[p, :]], axis=1).astype(BF16) for p in parts]
    x = [x_ref[p, :] + dot(oo, wo_ref[...]) for p, oo in zip(parts, o)]
    h2 = [_rms(xx, nffn_ref[...]).astype(BF16) for xx in x]
    for i in range(0, D_FF, FF_CHUNK):
        gate = [dot(h, wg_ref[:, i:i + FF_CHUNK]) for h in h2]
        up = [dot(h, wu_ref[:, i:i + FF_CHUNK]) for h in h2]
        act = [(g * _sigmoid(g) * u).astype(BF16) for g, u in zip(gate, up)]
        x = [xx + dot(a, wd_ref[i:i + FF_CHUNK, :]) for xx, a in zip(x, act)]
    h3 = [_rms(xx, nple_ref[...]).astype(BF16) for xx in x]
    pg = [_sigmoid(dot(h, wpg_ref[...])) for h in h3]
    pp = [dot(p_ref[p, :].astype(BF16), wpp_ref[...]) for p in parts]
    for p, xx, a, b in zip(parts, x, pg, pp):
        y_ref[p, :] = _rms(xx + a * b, nf_ref[...])


def _post(x2d, og, orw, p2d, weights, tm):
    n = x2d.shape[0]
    wo, nffn, wg, wu, wd, nple, wpg, wpp, nf = weights
    tok = lambda w: pl.BlockSpec((tm, w), lambda i: (i, 0))
    return pl.pallas_call(
        _post_kernel,
        grid=(n // tm,),
        in_specs=[
            tok(D_MODEL), tok(GLA_W), tok(RWKV_W), tok(PLE_DIM),
            _const_spec((D_MODEL, D_MODEL)), _const_spec((1, D_MODEL)),
            _const_spec((D_MODEL, D_FF)), _const_spec((D_MODEL, D_FF)), _const_spec((D_FF, D_MODEL)),
            _const_spec((1, D_MODEL)), _const_spec((D_MODEL, D_MODEL)), _const_spec((PLE_DIM, D_MODEL)),
            _const_spec((1, D_MODEL)),
        ],
        out_specs=tok(D_MODEL),
        out_shape=jax.ShapeDtypeStruct((n, D_MODEL), F32),
        compiler_params=pltpu.CompilerParams(
            dimension_semantics=("arbitrary",), vmem_limit_bytes=VMEM_LIMIT),
        name="post",
    )(x2d, og, orw, p2d, wo, nffn, wg, wu, wd, nple, wpg, wpp, nf)


PROMPT_CHUNK = 64
PROMPT_SEQS_PER_STEP = 8
DEC_TILE_SEQS = 16
TOKEN_TILE = 512


def kernel(x_prompt, x_sample, state_gla, state_rwkv, state_shift, p_prompt, p_sample, norm_mix, w_in, gla_gk_up, gla_gk_bias, gla_norm, rwkv_mu, rwkv_w0, rwkv_w2, rwkv_a0, rwkv_a2, rwkv_g2, rwkv_k_k, rwkv_k_a, rwkv_r_k, rwkv_ln_w, rwkv_ln_b, w_out, norm_ffn, w_gate, w_up, w_down, norm_ple, w_ple_gate, w_ple_proj, norm_final):
    assert w_in.shape[0] == 1
    i = 0
    rowv = lambda a: a.astype(F32).reshape(1, -1)
    zeros = lambda r, c: jnp.zeros((r, c), F32)
    w_in_i = w_in[i]
    w_gla = jnp.concatenate(
        [w_in_i[:, :GLA_PROJ], zeros(D_MODEL, GLA_COLS - GLA_PROJ)], axis=1).astype(BF16)
    w_rw = w_in_i[:, GLA_PROJ:].astype(BF16)
    gk_up = jnp.concatenate(
        [gla_gk_up[i].astype(F32), zeros(LANES - GLA_GATE_RANK, GLA_K_W)], axis=0).astype(BF16)
    seg = jnp.arange(SEG_W)[:, None] // RWKV_HEAD == jnp.arange(SEG_W)[None, :] // RWKV_HEAD
    rwkv_params = (
        rowv(rwkv_mu[i]), rowv(rwkv_w0[i]),
        jnp.concatenate([rwkv_w2[i].astype(F32), zeros(64, RWKV_W)], axis=0).astype(BF16),
        rowv(rwkv_a0[i]),
        jnp.concatenate([zeros(64, RWKV_W), rwkv_a2[i].astype(F32)], axis=0).astype(BF16),
        rwkv_g2[i].astype(BF16),
        rowv(rwkv_k_k[i]), rowv(rwkv_k_a[i]), rowv(rwkv_r_k[i]), rowv(rwkv_ln_w[i]), rowv(rwkv_ln_b[i]),
        seg.astype(BF16),
    )
    post_w = (
        w_out[i].astype(BF16), rowv(norm_ffn[i]), w_gate[i].astype(BF16), w_up[i].astype(BF16),
        w_down[i].astype(BF16), rowv(norm_ple[i]), w_ple_gate[i].astype(BF16),
        w_ple_proj[i].astype(BF16), rowv(norm_final),
    )
    gla_w = (gk_up, rowv(gla_gk_bias[i]), rowv(gla_norm[i]))
    g_mix = rowv(norm_mix[i])

    bp, tp, _ = x_prompt.shape
    xp = x_prompt.astype(F32).reshape(bp * tp, D_MODEL)
    ug, ur = _proj(xp, g_mix, w_gla, w_rw, TOKEN_TILE)
    og, gla_p = _gla_seq(ug.reshape(bp, tp, GLA_COLS),
                         jnp.zeros((bp, GLA_HEADS * GLA_DK, GLA_DV), F32), gla_w,
                         PROMPT_CHUNK, PROMPT_SEQS_PER_STEP)
    ur3 = ur.reshape(bp, tp, RWKV_PROJ)
    orw, rwkv_p = _rwkv_seq(ur3, jnp.zeros((bp, 1, RWKV_PROJ), F32),
                            jnp.zeros((bp, RWKV_HEADS, RWKV_HEAD, RWKV_HEAD), F32), rwkv_params,
                            PROMPT_CHUNK, PROMPT_SEQS_PER_STEP)
    shift_p = ur3[:, tp - 1]
    yp = _post(xp, og.reshape(bp * tp, GLA_W), orw.reshape(bp * tp, RWKV_W),
               p_prompt[i].reshape(bp * tp, PLE_DIM), post_w, TOKEN_TILE)

    bs, ts, _ = x_sample.shape
    xs = x_sample.astype(F32).reshape(bs * ts, D_MODEL)
    ug, ur = _proj(xs, g_mix, w_gla, w_rw, TOKEN_TILE)
    og, gla_s = _gla_dec(ug, state_gla[i].astype(F32).reshape(bs, GLA_HEADS * GLA_DK, GLA_DV),
                         gla_w, ts, DEC_TILE_SEQS)
    shiftx = jnp.pad(state_shift[i].astype(F32)[:, None, :], ((0, 0), (0, ts - 1), (0, 0)))
    orw, rwkv_s = _rwkv_dec(ur, shiftx.reshape(bs * ts, RWKV_PROJ), state_rwkv[i].astype(F32),
                            rwkv_params, ts, DEC_TILE_SEQS)
    shift_s = ur.reshape(bs, ts, RWKV_PROJ)[:, ts - 1]
    ys = _post(xs, og, orw, p_sample[i].reshape(bs * ts, PLE_DIM), post_w, TOKEN_TILE)

    gla_shape = (1, -1, GLA_HEADS, GLA_DK, GLA_DV)
    return (yp.reshape(bp, tp, D_MODEL).astype(x_prompt.dtype),
            ys.reshape(bs, ts, D_MODEL).astype(x_sample.dtype),
            gla_p.reshape(gla_shape).astype(state_gla.dtype), rwkv_p[None].astype(state_rwkv.dtype),
            shift_p[None].astype(state_shift.dtype),
            gla_s.reshape(gla_shape).astype(state_gla.dtype), rwkv_s[None].astype(state_rwkv.dtype),
            shift_s[None].astype(state_shift.dtype))
```

```python
import functools

import jax
import jax.numpy as jnp
from jax import lax
from jax.experimental import pallas as pl
from jax.experimental.pallas import tpu as pltpu

F32 = jnp.float32
BF16 = jnp.bfloat16

D_MODEL = 1024
GLA_HEADS = 4
GLA_DK = 64
GLA_DV = 128
GLA_K_W = GLA_HEADS * GLA_DK
GLA_W = GLA_HEADS * GLA_DV
GLA_GATE_RANK = 16
GLA_GATE_NORM = 16.0
GLA_MAIN = 2 * GLA_K_W + 2 * GLA_W
GLA_PROJ = GLA_MAIN + GLA_GATE_RANK
LANES = 128
GLA_COLS = GLA_MAIN + LANES
RWKV_HEAD = 64
RWKV_HEADS = 8
RWKV_W = RWKV_HEADS * RWKV_HEAD
RWKV_PROJ = 3 * RWKV_W + 64 + 64 + 128
D_FF = 2816
PLE_DIM = 256
EPS = 1e-6
RWKV_GN_EPS = 64e-5

VMEM_LIMIT = 56 * 1024 * 1024

NN = ((1,), (0,))
NT = ((1,), (1,))


def _split(x, n):
    parts = []
    r = x
    for i in range(n):
        p = r.astype(BF16)
        parts.append(p)
        if i + 1 < n:
            r = r - p.astype(F32)
    return parts


def _mm(a, b, dims=NN, pa=1, pb=1):
    pieces_a = _split(a, pa)
    pieces_b = _split(b, pb)
    n = max(pa, pb)
    acc = None
    for i, ai in enumerate(pieces_a):
        for j, bj in enumerate(pieces_b):
            if i + j < n:
                t = lax.dot_general(ai, bj, (dims, ((), ())), preferred_element_type=F32)
                acc = t if acc is None else acc + t
    return acc


def _mm_tn(a, b, pa=1, pb=1):
    rows = a.shape[0]
    pad = (-rows) % LANES
    if pad:
        a = jnp.concatenate([a, jnp.zeros((pad, a.shape[1]), a.dtype)], axis=0)
        b = jnp.concatenate([b, jnp.zeros((pad, b.shape[1]), b.dtype)], axis=0)
    return _mm(a.T, b, NN, pa, pb)


def _iota(shape, dim):
    return lax.broadcasted_iota(jnp.int32, shape, dim)


def _softplus(z):
    return jnp.maximum(z, 0.0) + jnp.log(1.0 + jnp.exp(-jnp.abs(z)))


def _sigmoid(z):
    return 1.0 / (1.0 + jnp.exp(-z))


def _tanh(z):
    return 2.0 * _sigmoid(2.0 * z) - 1.0


def _rms(x, g):
    return x * lax.rsqrt(jnp.mean(x * x, axis=-1, keepdims=True) + EPS) * g


GROUP_HEADS = 2
GROUP_W = GROUP_HEADS * RWKV_HEAD
RWKV_GROUPS = RWKV_W // GROUP_W
SEG_W = 2 * LANES


def _head_masks(rows, head_w, heads):
    lane = _iota((rows, heads * head_w), 1)
    return [jnp.where(lane // head_w == h, 1.0, 0.0).astype(BF16) for h in range(heads)]


def _stack_heads(x, hmasks):
    xb = x.astype(BF16)
    return jnp.concatenate([xb * m for m in hmasks], axis=0)


def _block_diag(x_cat, blk):
    heads = blk.shape[0] // x_cat.shape[0]
    return jnp.concatenate([x_cat.astype(BF16)] * heads, axis=0) * blk


def _chunk_masks(rows, seq_len, heads):
    t = _iota((rows, heads * rows), 0)
    s = _iota((rows, heads * rows), 1) % rows
    strict, incl = t > s, t >= s
    t2 = _iota((2 * rows, rows), 0)
    s2 = _iota((2 * rows, rows), 1)
    cum_rows = (t2 < rows) & (t2 >= s2)
    tot_rows = t2 >= rows
    if seq_len < rows:
        same = (t // seq_len) == (s // seq_len)
        strict, incl = strict & same, incl & same
        same2 = ((t2 % rows) // seq_len) == (s2 // seq_len)
        cum_rows, tot_rows = cum_rows & same2, tot_rows & same2
    big = heads * rows
    blk = jnp.where(_iota((big, big), 0) // rows == _iota((big, big), 1) // rows, 1.0, 0.0)
    return dict(strict=strict, incl=incl, eye=jnp.where(t == s, 1.0, 0.0),
                sums=jnp.where(cum_rows | tot_rows, 1.0, 0.0).astype(BF16), blk=blk.astype(BF16))


def _proj_kernel(x_ref, g_ref, wg_ref, wr_ref, ug_ref, ur_ref):
    h = _rms(x_ref[...], g_ref[...]).astype(BF16)
    ug_ref[...] = jnp.dot(h, wg_ref[...], preferred_element_type=F32)
    ur_ref[...] = jnp.dot(h, wr_ref[...], preferred_element_type=F32)


def _const_spec(shape):
    return pl.BlockSpec(shape, lambda *_: (0,) * len(shape), pipeline_mode=pl.Buffered(1))


def _proj(x2d, g, w_gla, w_rw, tm):
    n = x2d.shape[0]
    return pl.pallas_call(
        _proj_kernel,
        grid=(n // tm,),
        in_specs=[
            pl.BlockSpec((tm, D_MODEL), lambda i: (i, 0)),
            _const_spec((1, D_MODEL)),
            _const_spec((D_MODEL, GLA_COLS)),
            _const_spec((D_MODEL, RWKV_PROJ)),
        ],
        out_specs=[
            pl.BlockSpec((tm, GLA_COLS), lambda i: (i, 0)),
            pl.BlockSpec((tm, RWKV_PROJ), lambda i: (i, 0)),
        ],
        out_shape=[
            jax.ShapeDtypeStruct((n, GLA_COLS), F32),
            jax.ShapeDtypeStruct((n, RWKV_PROJ), F32),
        ],
        compiler_params=pltpu.CompilerParams(
            dimension_semantics=("arbitrary",), vmem_limit_bytes=VMEM_LIMIT),
        name="proj",
    )(x2d, g, w_gla, w_rw)


def _gla_prep(u, gkup, gkb, sums):
    rows = u.shape[0]
    q = u[:, 0:GLA_K_W] * (GLA_DK ** -0.5)
    k = u[:, GLA_K_W:2 * GLA_K_W]
    v = u[:, 2 * GLA_K_W:2 * GLA_K_W + GLA_W]
    gate = u[:, 2 * GLA_K_W + GLA_W:GLA_MAIN]
    log_a = -_softplus(-(_mm(u[:, GLA_MAIN:GLA_COLS], gkup) + gkb)) * (1.0 / GLA_GATE_NORM)
    cums = _mm(sums, log_a, NN, 1, 2)
    cum, cum_end = cums[:rows], cums[rows:]
    return (q * jnp.exp(cum), k * jnp.exp(-cum), k * jnp.exp(cum_end - cum), v, gate,
            jnp.exp(cum_end))


def _gla_intra(q_i, k_i, v, m, kmasks, vmasks):
    scores = jnp.where(m["incl"], _mm(q_i, _stack_heads(k_i, kmasks), NT), 0.0)
    return _mm(scores, _stack_heads(v, vmasks))


def _gla_out(o, gate, gnorm):
    heads = [slice(h * GLA_DV, (h + 1) * GLA_DV) for h in range(GLA_HEADS)]
    return jnp.concatenate(
        [_rms(o[:, hl], gnorm) * (gate[:, hl] * _sigmoid(gate[:, hl])) for hl in heads], axis=1)


def _gla_block_mask():
    return (_iota((GLA_K_W, GLA_W), 0) // GLA_DK) == (_iota((GLA_K_W, GLA_W), 1) // GLA_DV)


def _gla_state_in(s, blk):
    return jnp.where(blk, jnp.concatenate([s] * GLA_HEADS, axis=1), 0.0)


def _gla_state_out(s_bd):
    heads = [s_bd[:, h * GLA_DV:(h + 1) * GLA_DV] for h in range(GLA_HEADS)]
    return (heads[0] + heads[1]) + (heads[2] + heads[3])


def _lane_tiled_t(x):
    pad = LANES - x.shape[0]
    if pad:
        x = jnp.concatenate([x, jnp.zeros((pad, x.shape[1]), x.dtype)], axis=0)
    return x.T


def _gla_seq_kernel(u_ref, s0_ref, gkup_ref, gkb_ref, gn_ref, o_ref, sout_ref, s_scr):
    c = pl.program_id(1)
    n_seq, rows = u_ref.shape[0], u_ref.shape[1]
    blk = _gla_block_mask()

    @pl.when(c == 0)
    def _():
        for b in range(n_seq):
            s_scr[b] = _gla_state_in(s0_ref[b], blk)

    m = _chunk_masks(rows, rows, GLA_HEADS)
    kmasks = _head_masks(rows, GLA_DK, GLA_HEADS)
    vmasks = _head_masks(rows, GLA_DV, GLA_HEADS)
    tok = [_gla_prep(u_ref[b], gkup_ref[...], gkb_ref[...], m["sums"]) for b in range(n_seq)]
    s_old = [s_scr[b] for b in range(n_seq)]
    intra = [_gla_intra(q_i, k_i, v, m, kmasks, vmasks) for q_i, k_i, _, v, _, _ in tok]
    inter = [_mm(t[0], s) for t, s in zip(tok, s_old)]
    kv = [_mm_tn(k_e, v) for _, _, k_e, v, _, _ in tok]
    for b in range(n_seq):
        gate, g_end = tok[b][4], tok[b][5]
        o_ref[b] = _gla_out(intra[b] + inter[b], gate, gn_ref[...])
        dec = jnp.concatenate([_lane_tiled_t(jnp.broadcast_to(g_end[0:1], (LANES, GLA_K_W)))] * GLA_HEADS,
                              axis=1)
        s_scr[b] = dec * s_old[b] + jnp.where(blk, kv[b], 0.0)

    @pl.when(c == pl.num_programs(1) - 1)
    def _():
        for b in range(n_seq):
            sout_ref[b] = _gla_state_out(s_scr[b])


def _gla_dec_kernel(u_ref, s0_ref, gkup_ref, gkb_ref, gn_ref, o_ref, sout_ref, *, seq_len):
    rows = u_ref.shape[0]
    n_seq = rows // seq_len
    blk = _gla_block_mask()
    m = _chunk_masks(rows, seq_len, GLA_HEADS)
    q_i, k_i, k_e, v, gate, g_end = _gla_prep(u_ref[...], gkup_ref[...], gkb_ref[...], m["sums"])
    o = _gla_intra(q_i, k_i, v, m, _head_masks(rows, GLA_DK, GLA_HEADS),
                   _head_masks(rows, GLA_DV, GLA_HEADS))
    k_et = _lane_tiled_t(k_e).astype(BF16)
    dec_t = _lane_tiled_t(g_end)
    v_pad = jnp.concatenate([v, jnp.zeros((LANES - rows, GLA_W), F32)], axis=0)
    row = _iota((rows, 1), 0)
    row_pad = _iota((LANES, 1), 0)
    for j in range(n_seq):
        s_bd = _gla_state_in(s0_ref[j], blk)
        o = o + _mm(jnp.where(row // seq_len == j, q_i, 0.0), s_bd)
        kv = _mm(k_et, jnp.where(row_pad // seq_len == j, v_pad, 0.0))
        first = j * seq_len
        sout_ref[j] = _gla_state_out(dec_t[:, first:first + 1] * s_bd + jnp.where(blk, kv, 0.0))
    o_ref[...] = _gla_out(o, gate, gn_ref[...])


def _gla_param_specs():
    return [_const_spec((LANES, GLA_K_W)), _const_spec((1, GLA_K_W)), _const_spec((1, GLA_DV))]


def _gla_seq(u3, s0, params, rows, n_seq):
    b, t, _ = u3.shape
    sdim = GLA_HEADS * GLA_DK
    state_spec = pl.BlockSpec((n_seq, sdim, GLA_DV), lambda i, j: (i, 0, 0))
    return pl.pallas_call(
        _gla_seq_kernel,
        grid=(b // n_seq, t // rows),
        in_specs=[pl.BlockSpec((n_seq, rows, GLA_COLS), lambda i, j: (i, j, 0)), state_spec]
        + _gla_param_specs(),
        out_specs=[pl.BlockSpec((n_seq, rows, GLA_W), lambda i, j: (i, j, 0)), state_spec],
        out_shape=[
            jax.ShapeDtypeStruct((b, t, GLA_W), F32),
            jax.ShapeDtypeStruct((b, sdim, GLA_DV), F32),
        ],
        scratch_shapes=[pltpu.VMEM((n_seq, sdim, GLA_W), F32)],
        compiler_params=pltpu.CompilerParams(
            dimension_semantics=("arbitrary", "arbitrary"), vmem_limit_bytes=VMEM_LIMIT),
        name="gla_seq",
    )(u3, s0, *params)


def _gla_dec(u2, s0, params, seq_len, n_seq):
    n = u2.shape[0]
    rows = n_seq * seq_len
    sdim = GLA_HEADS * GLA_DK
    state_spec = pl.BlockSpec((n_seq, sdim, GLA_DV), lambda i: (i, 0, 0))
    return pl.pallas_call(
        functools.partial(_gla_dec_kernel, seq_len=seq_len),
        grid=(n // rows,),
        in_specs=[pl.BlockSpec((rows, GLA_COLS), lambda i: (i, 0)), state_spec] + _gla_param_specs(),
        out_specs=[pl.BlockSpec((rows, GLA_W), lambda i: (i, 0)), state_spec],
        out_shape=[
            jax.ShapeDtypeStruct((n, GLA_W), F32),
            jax.ShapeDtypeStruct(s0.shape, F32),
        ],
        compiler_params=pltpu.CompilerParams(
            dimension_semantics=("arbitrary",), vmem_limit_bytes=VMEM_LIMIT),
        name="gla_dec",
    )(u2, s0, *params)


def _seg_sum(x, seg, pa=1):
    return jnp.concatenate(
        [_mm(x[:, i:i + SEG_W], seg, NN, pa, 1) for i in range(0, RWKV_W, SEG_W)], axis=1)


def _rwkv_prep(u, prev, mu, w0, w2, a0, a2, g2, k_k, k_a, rk, seg, sums):
    rows = u.shape[0]
    xr = u + mu * (prev - u)
    r = xr[:, 0:RWKV_W]
    kr = xr[:, RWKV_W:2 * RWKV_W]
    vr = xr[:, 2 * RWKV_W:3 * RWKV_W]
    wa = xr[:, 3 * RWKV_W:3 * RWKV_W + LANES]
    gd = xr[:, 3 * RWKV_W + LANES:RWKV_PROJ]
    w = -_softplus(-(w0 + _mm(_tanh(wa), w2))) - 0.5
    lw = -jnp.exp(w)
    a_sig = _sigmoid(a0 + _mm(wa, a2))
    gate = _mm(_sigmoid(gd), g2)
    kk = kr * k_k
    kk = kk * lax.rsqrt(jnp.maximum(_seg_sum(kk * kk, seg), 1e-24))
    kr = kr * (1.0 + (a_sig - 1.0) * k_a)
    bonus = _seg_sum(r * kr * rk, seg) * vr
    b_vec = kk * a_sig
    block = sums.shape[1]
    cums = [_mm(sums, lw[i:i + block], NN, 1, 2) for i in range(0, rows, block)]
    cum = jnp.concatenate([x[:block] for x in cums], axis=0)
    cum_end = jnp.concatenate([x[block:] for x in cums], axis=0)
    e_neg = jnp.exp(-cum)
    e_end = jnp.exp(cum_end - cum)
    bf = lambda x: x.astype(BF16)
    return (bf(-kk * jnp.exp(cum - lw)), bf(r * jnp.exp(cum)), bf(kr * e_neg), bf(b_vec * e_neg),
            bf(kr * e_end), bf(b_vec * e_end), bf(vr), gate, bonus, jnp.exp(cum_end))


def _rwkv_intra(units, m, hmasks, n_double):
    rows = units[0][0].shape[0]
    stack = lambda x: _stack_heads(x, hmasks)
    lhs = [jnp.concatenate([at, rt], axis=0) for at, rt, _, _, _ in units]
    gb = [_mm(l, stack(bt), NT) for l, (_, _, _, bt, _) in zip(lhs, units)]
    gk = [_mm(l, stack(kt), NT) for l, (_, _, kt, _, _) in zip(lhs, units)]
    a_ab = [jnp.where(m["strict"], x[:rows], 0.0) for x in gb]
    a_rb = [jnp.where(m["incl"], x[rows:], 0.0) for x in gb]
    a_ak = [jnp.where(m["strict"], x[:rows], 0.0) for x in gk]
    a_rk = [jnp.where(m["incl"], x[rows:], 0.0) for x in gk]
    tinv = [m["eye"] + a for a in a_ab]
    apow = a_ab
    for _ in range(n_double):
        apow = [_mm(a, _block_diag(a, m["blk"])) for a in apow]
        tinv = [t + _mm(a, _block_diag(t, m["blk"])) for a, t in zip(apow, tinv)]
    v_s = [stack(vp) for _, _, _, _, vp in units]
    av = [_mm(a, v) for a, v in zip(a_ak, v_s)]
    wu = [_mm(t, jnp.concatenate([stack(at), stack(x)], axis=1))
          for t, (at, _, _, _, _), x in zip(tinv, units, av)]
    z = [_mm(a, jnp.concatenate([stack(x[:, :GROUP_W]), stack(x[:, GROUP_W:])], axis=1))
         for a, x in zip(a_rb, wu)]
    y0 = [zz[:, GROUP_W:] + _mm(a, v) for zz, a, v in zip(z, a_rk, v_s)]
    return [(x[:, :GROUP_W], x[:, GROUP_W:], rt + zz[:, :GROUP_W], yy)
            for x, (_, rt, _, _, _), zz, yy in zip(wu, units, z, y0)]


def _rwkv_out(ys, gates, bonuses, lnw, lnb, seg):
    inv = 1.0 / RWKV_HEAD
    yc = [y - _seg_sum(y, seg, 2) * inv for y in ys]
    var = [_seg_sum(c * c, seg) * inv for c in yc]
    return [(c * lax.rsqrt(v + RWKV_GN_EPS) * lnw + lnb + bonus) * gate
            for c, v, bonus, gate in zip(yc, var, bonuses, gates)]


def _rwkv_state_in(s_ref, idx, g):
    zero = jnp.zeros((RWKV_HEAD, RWKV_HEAD), F32)
    blocks = []
    for h in range(GROUP_HEADS):
        parts = [zero] * GROUP_HEADS
        parts[h] = s_ref[idx, g * GROUP_HEADS + h]
        blocks.append(jnp.concatenate(parts, axis=1))
    return jnp.concatenate(blocks, axis=0)


def _rwkv_state_out(s_ref, idx, g, s2):
    for h in range(GROUP_HEADS):
        sl = slice(h * RWKV_HEAD, (h + 1) * RWKV_HEAD)
        s_ref[idx, g * GROUP_HEADS + h] = s2[sl, sl]


def _head_block_mask():
    return (_iota((GROUP_W, GROUP_W), 0) // RWKV_HEAD) == (_iota((GROUP_W, GROUP_W), 1) // RWKV_HEAD)


def _rwkv_seq_kernel(u_ref, shift0_ref, s0_ref, mu_ref, w0_ref, w2_ref, a0_ref, a2_ref,
                     g2_ref, kk_ref, ka_ref, rk_ref, lnw_ref, lnb_ref, seg_ref,
                     o_ref, sout_ref, s_scr, prev_scr):
    c = pl.program_id(1)
    n_seq, rows = u_ref.shape[0], u_ref.shape[1]
    groups = [slice(g * GROUP_W, (g + 1) * GROUP_W) for g in range(RWKV_GROUPS)]
    ids = [(b, g) for b in range(n_seq) for g in range(RWKV_GROUPS)]

    @pl.when(c == 0)
    def _():
        for b in range(n_seq):
            for g in range(RWKV_GROUPS):
                s_scr[b, g] = _rwkv_state_in(s0_ref, b, g)
            prev_scr[b] = shift0_ref[b]

    seg = seg_ref[...]
    m = _chunk_masks(rows, rows, GROUP_HEADS)
    hmasks = _head_masks(rows, RWKV_HEAD, GROUP_HEADS)
    head_blk = _head_block_mask()
    n_double = rows.bit_length() - 2
    row = _iota((rows, 1), 0)

    tok, units = [], []
    for b in range(n_seq):
        u = u_ref[b]
        prev = jnp.where(row == 0, prev_scr[b], pltpu.roll(u, 1, axis=0))
        prev_scr[b] = u[rows - 1:rows]
        at, rt, kt, bt, ke, be, vb, gate, bonus, g_end = _rwkv_prep(
            u, prev, mu_ref[...], w0_ref[...], w2_ref[...], a0_ref[...], a2_ref[...], g2_ref[...],
            kk_ref[...], ka_ref[...], rk_ref[...], seg, m["sums"])
        tok.append((vb, gate, bonus, ke, be, g_end[0:1]))
        units += [(at[:, gl], rt[:, gl], kt[:, gl], bt[:, gl], vb[:, gl]) for gl in groups]

    intra = _rwkv_intra(units, m, hmasks, n_double)
    s_old = [s_scr[b, g] for b, g in ids]
    uy = [_mm(jnp.concatenate([w_m, r_m], axis=0), s2, NT) for (w_m, _, r_m, _), s2 in zip(intra, s_old)]
    upd = []
    for (b, g), (_, u0, _, _), x in zip(ids, intra, uy):
        vb, _, _, ke, be, _ = tok[b]
        gl = groups[g]
        upd.append(_mm_tn(jnp.concatenate([x[:rows] + u0, vb[:, gl].astype(F32)], axis=0),
                          jnp.concatenate([be[:, gl], ke[:, gl]], axis=0)))
    for (b, g), s2, d in zip(ids, s_old, upd):
        s_scr[b, g] = s2 * tok[b][5][:, groups[g]] + jnp.where(head_blk, d, 0.0)
    ys = [jnp.concatenate([uy[i][rows:] + intra[i][3] for i, (bb, _) in enumerate(ids) if bb == b],
                          axis=1) for b in range(n_seq)]
    outs = _rwkv_out(ys, [t[1] for t in tok], [t[2] for t in tok], lnw_ref[...], lnb_ref[...], seg)
    for b in range(n_seq):
        o_ref[b] = outs[b]

    @pl.when(c == pl.num_programs(1) - 1)
    def _():
        for b, g in ids:
            _rwkv_state_out(sout_ref, b, g, s_scr[b, g])


def _rwkv_dec_kernel(u_ref, shiftx_ref, s0_ref, mu_ref, w0_ref, w2_ref, a0_ref, a2_ref, g2_ref,
                     kk_ref, ka_ref, rk_ref, lnw_ref, lnb_ref, seg_ref,
                     o_ref, sout_ref, *, seq_len):
    rows = u_ref.shape[0]
    n_seq = rows // seq_len
    u = u_ref[...]
    row = _iota((rows, 1), 0)
    prev = jnp.where(row % seq_len == 0, shiftx_ref[...], pltpu.roll(u, 1, axis=0))
    seg = seg_ref[...]
    m = _chunk_masks(rows, seq_len, GROUP_HEADS)
    at, rt, kt, bt, ke, be, vb, gate, bonus, g_end = _rwkv_prep(
        u, prev, mu_ref[...], w0_ref[...], w2_ref[...], a0_ref[...], a2_ref[...], g2_ref[...],
        kk_ref[...], ka_ref[...], rk_ref[...], seg, m["sums"])
    hmasks = _head_masks(rows, RWKV_HEAD, GROUP_HEADS)
    head_blk = _head_block_mask()
    n_double = seq_len.bit_length() - 2
    lane2 = _iota((1, 2 * rows), 1)
    u_half = lane2 < rows

    groups = [slice(g * GROUP_W, (g + 1) * GROUP_W) for g in range(RWKV_GROUPS)]
    intra = _rwkv_intra([tuple(x[:, gl] for x in (at, rt, kt, bt, vb)) for gl in groups],
                        m, hmasks, n_double)
    wr = [jnp.concatenate([w_m, r_m], axis=0).astype(BF16) for w_m, _, r_m, _ in intra]
    bk = [jnp.concatenate([be[:, gl], ke[:, gl]], axis=0) for gl in groups]
    base = [jnp.concatenate([x[1], vb[:, gl].astype(F32)], axis=0).T for x, gl in zip(intra, groups)]
    mine = [(lane2 % rows) // seq_len == j for j in range(n_seq)]
    ids = [(j, g) for j in range(n_seq) for g in range(RWKV_GROUPS)]
    s_old = [_rwkv_state_in(s0_ref, j, g) for j, g in ids]
    uyt = [_mm(s2, wr[g], NT) for (j, g), s2 in zip(ids, s_old)]
    upd = [_mm(jnp.where(mine[j], jnp.where(u_half, x, 0.0) + base[g], 0.0), bk[g])
           for (j, g), x in zip(ids, uyt)]
    for (j, g), s2, d in zip(ids, s_old, upd):
        first = j * seq_len
        _rwkv_state_out(sout_ref, j, g,
                        s2 * g_end[first:first + 1, groups[g]] + jnp.where(head_blk, d, 0.0))
    ys = []
    for g in range(RWKV_GROUPS):
        acc = jnp.zeros((GROUP_W, 2 * rows), F32)
        for (j, gg), x in zip(ids, uyt):
            if gg == g:
                acc = jnp.where(mine[j], x, acc)
        ys.append(acc.T[rows:] + intra[g][3])
    o_ref[...] = _rwkv_out([jnp.concatenate(ys, axis=1)], [gate], [bonus],
                           lnw_ref[...], lnb_ref[...], seg)[0]


def _rwkv_param_specs():
    vec = lambda n: _const_spec((1, n))
    return [
        vec(RWKV_PROJ),
        vec(RWKV_W),
        _const_spec((LANES, RWKV_W)),
        vec(RWKV_W),
        _const_spec((LANES, RWKV_W)),
        _const_spec((LANES, RWKV_W)),
        vec(RWKV_W), vec(RWKV_W), vec(RWKV_W), vec(RWKV_W), vec(RWKV_W),
        _const_spec((SEG_W, SEG_W)),
    ]


def _rwkv_seq(u3, shift0, s0, params, rows, n_seq):
    b, t, _ = u3.shape
    state_spec = pl.BlockSpec((None, n_seq, RWKV_HEADS, RWKV_HEAD, RWKV_HEAD),
                              lambda i, j: (0, i, 0, 0, 0))
    return pl.pallas_call(
        _rwkv_seq_kernel,
        grid=(b // n_seq, t // rows),
        in_specs=[
            pl.BlockSpec((n_seq, rows, RWKV_PROJ), lambda i, j: (i, j, 0)),
            pl.BlockSpec((n_seq, 1, RWKV_PROJ), lambda i, j: (i, 0, 0)),
            state_spec,
        ] + _rwkv_param_specs(),
        out_specs=[pl.BlockSpec((n_seq, rows, RWKV_W), lambda i, j: (i, j, 0)), state_spec],
        out_shape=[
            jax.ShapeDtypeStruct((b, t, RWKV_W), F32),
            jax.ShapeDtypeStruct(s0.shape, F32),
        ],
        scratch_shapes=[
            pltpu.VMEM((n_seq, RWKV_GROUPS, GROUP_W, GROUP_W), F32),
            pltpu.VMEM((n_seq, 1, RWKV_PROJ), F32),
        ],
        compiler_params=pltpu.CompilerParams(
            dimension_semantics=("arbitrary", "arbitrary"), vmem_limit_bytes=VMEM_LIMIT),
        name="rwkv_seq",
    )(u3, shift0, s0, *params)


def _rwkv_dec(u2, shiftx, s0, params, seq_len, n_seq):
    n = u2.shape[0]
    rows = n_seq * seq_len
    state_spec = pl.BlockSpec((None, n_seq, RWKV_HEADS, RWKV_HEAD, RWKV_HEAD),
                              lambda i: (0, i, 0, 0, 0))
    return pl.pallas_call(
        functools.partial(_rwkv_dec_kernel, seq_len=seq_len),
        grid=(n // rows,),
        in_specs=[
            pl.BlockSpec((rows, RWKV_PROJ), lambda i: (i, 0)),
            pl.BlockSpec((rows, RWKV_PROJ), lambda i: (i, 0)),
            state_spec,
        ] + _rwkv_param_specs(),
        out_specs=[pl.BlockSpec((rows, RWKV_W), lambda i: (i, 0)), state_spec],
        out_shape=[
            jax.ShapeDtypeStruct((n, RWKV_W), F32),
            jax.ShapeDtypeStruct(s0.shape, F32),
        ],
        compiler_params=pltpu.CompilerParams(
            dimension_semantics=("arbitrary",), vmem_limit_bytes=VMEM_LIMIT),
        name="rwkv_dec",
    )(u2, shiftx, s0, *params)


FF_CHUNK = D_FF // 2


def _post_kernel(x_ref, og_ref, or_ref, p_ref, wo_ref, nffn_ref, wg_ref, wu_ref, wd_ref,
                 nple_ref, wpg_ref, wpp_ref, nf_ref, y_ref):
    half = x_ref.shape[0] // 2
    parts = [slice(0, half), slice(half, 2 * half)]
    dot = lambda a, b: jnp.dot(a, b, preferred_element_type=F32)
    o = [jnp.concatenate([og_ref[p, :], or_ref[p, :]], axis=1).astype(BF16) for p in parts]
    x = [x_ref[p, :] + dot(oo, wo_ref[...]) for p, oo in zip(parts, o)]
    h2 = [_rms(xx, nffn_ref[...]).astype(BF16) for xx in x]
    for i in range(0, D_FF, FF_CHUNK):
        gate = [dot(h, wg_ref[:, i:i + FF_CHUNK]) for h in h2]
        up = [dot(h, wu_ref[:, i:i + FF_CHUNK]) for h in h2]
        act = [(g * _sigmoid(g) * u).astype(BF16) for g, u in zip(gate, up)]
        x = [xx + dot(a, wd_ref[i:i + FF_CHUNK, :]) for xx, a in zip(x, act)]
    h3 = [_rms(xx, nple_ref[...]).astype(BF16) for xx in x]
    pg = [_sigmoid(dot(h, wpg_ref[...])) for h in h3]
    pp = [dot(p_ref[p, :].astype(BF16), wpp_ref[...]) for p in parts]
    for p, xx, a, b in zip(parts, x, pg, pp):
        y_ref[p, :] = _rms(xx + a * b, nf_ref[...])


def _post(x2d, og, orw, p2d, weights, tm):
    n = x2d.shape[0]
    wo, nffn, wg, wu, wd, nple, wpg, wpp, nf = weights
    tok = lambda w: pl.BlockSpec((tm, w), lambda i: (i, 0))
    return pl.pallas_call(
        _post_kernel,
        grid=(n // tm,),
        in_specs=[
            tok(D_MODEL), tok(GLA_W), tok(RWKV_W), tok(PLE_DIM),
            _const_spec((D_MODEL, D_MODEL)), _const_spec((1, D_MODEL)),
            _const_spec((D_MODEL, D_FF)), _const_spec((D_MODEL, D_FF)), _const_spec((D_FF, D_MODEL)),
            _const_spec((1, D_MODEL)), _const_spec((D_MODEL, D_MODEL)), _const_spec((PLE_DIM, D_MODEL)),
            _const_spec((1, D_MODEL)),
        ],
        out_specs=tok(D_MODEL),
        out_shape=jax.ShapeDtypeStruct((n, D_MODEL), F32),
        compiler_params=pltpu.CompilerParams(
            dimension_semantics=("arbitrary",), vmem_limit_bytes=VMEM_LIMIT),
        name="post",
    )(x2d, og, orw, p2d, wo, nffn, wg, wu, wd, nple, wpg, wpp, nf)


PROMPT_CHUNK = 64
PROMPT_SEQS_PER_STEP = 8
DEC_TILE_SEQS = 16
TOKEN_TILE = 512


def kernel(x_prompt, x_sample, state_gla, state_rwkv, state_shift, p_prompt, p_sample, norm_mix, w_in, gla_gk_up, gla_gk_bias, gla_norm, rwkv_mu, rwkv_w0, rwkv_w2, rwkv_a0, rwkv_a2, rwkv_g2, rwkv_k_k, rwkv_k_a, rwkv_r_k, rwkv_ln_w, rwkv_ln_b, w_out, norm_ffn, w_gate, w_up, w_down, norm_ple, w_ple_gate, w_ple_proj, norm_final):
    assert w_in.shape[0] == 1
    i = 0
    rowv = lambda a: a.astype(F32).reshape(1, -1)
    zeros = lambda r, c: jnp.zeros((r, c), F32)
    w_in_i = w_in[i]
    w_gla = jnp.concatenate(
        [w_in_i[:, :GLA_PROJ], zeros(D_MODEL, GLA_COLS - GLA_PROJ)], axis=1).astype(BF16)
    w_rw = w_in_i[:, GLA_PROJ:].astype(BF16)
    gk_up = jnp.concatenate(
        [gla_gk_up[i].astype(F32), zeros(LANES - GLA_GATE_RANK, GLA_K_W)], axis=0).astype(BF16)
    seg = jnp.arange(SEG_W)[:, None] // RWKV_HEAD == jnp.arange(SEG_W)[None, :] // RWKV_HEAD
    rwkv_params = (
        rowv(rwkv_mu[i]), rowv(rwkv_w0[i]),
        jnp.concatenate([rwkv_w2[i].astype(F32), zeros(64, RWKV_W)], axis=0).astype(BF16),
        rowv(rwkv_a0[i]),
        jnp.concatenate([zeros(64, RWKV_W), rwkv_a2[i].astype(F32)], axis=0).astype(BF16),
        rwkv_g2[i].astype(BF16),
        rowv(rwkv_k_k[i]), rowv(rwkv_k_a[i]), rowv(rwkv_r_k[i]), rowv(rwkv_ln_w[i]), rowv(rwkv_ln_b[i]),
        seg.astype(BF16),
    )
    post_w = (
        w_out[i].astype(BF16), rowv(norm_ffn[i]), w_gate[i].astype(BF16), w_up[i].astype(BF16),
        w_down[i].astype(BF16), rowv(norm_ple[i]), w_ple_gate[i].astype(BF16),
        w_ple_proj[i].astype(BF16), rowv(norm_final),
    )
    gla_w = (gk_up, rowv(gla_gk_bias[i]), rowv(gla_norm[i]))
    g_mix = rowv(norm_mix[i])

    bp, tp, _ = x_prompt.shape
    xp = x_prompt.astype(F32).reshape(bp * tp, D_MODEL)
    ug, ur = _proj(xp, g_mix, w_gla, w_rw, TOKEN_TILE)
    og, gla_p = _gla_seq(ug.reshape(bp, tp, GLA_COLS),
                         jnp.zeros((bp, GLA_HEADS * GLA_DK, GLA_DV), F32), gla_w,
                         PROMPT_CHUNK, PROMPT_SEQS_PER_STEP)
    ur3 = ur.reshape(bp, tp, RWKV_PROJ)
    orw, rwkv_p = _rwkv_seq(ur3, jnp.zeros((bp, 1, RWKV_PROJ), F32),
                            jnp.zeros((1, bp, RWKV_HEADS, RWKV_HEAD, RWKV_HEAD), F32), rwkv_params,
                            PROMPT_CHUNK, PROMPT_SEQS_PER_STEP)
    shift_p = ur3[:, tp - 1]
    yp = _post(xp, og.reshape(bp * tp, GLA_W), orw.reshape(bp * tp, RWKV_W),
               p_prompt[i].reshape(bp * tp, PLE_DIM), post_w, TOKEN_TILE)

    bs, ts, _ = x_sample.shape
    xs = x_sample.astype(F32).reshape(bs * ts, D_MODEL)
    ug, ur = _proj(xs, g_mix, w_gla, w_rw, TOKEN_TILE)
    og, gla_s = _gla_dec(ug, state_gla[i].astype(F32).reshape(bs, GLA_HEADS * GLA_DK, GLA_DV),
                         gla_w, ts, DEC_TILE_SEQS)
    shiftx = jnp.pad(state_shift[i].astype(F32)[:, None, :], ((0, 0), (0, ts - 1), (0, 0)))
    orw, rwkv_s = _rwkv_dec(ur, shiftx.reshape(bs * ts, RWKV_PROJ), state_rwkv.astype(F32),
                            rwkv_params, ts, DEC_TILE_SEQS)
    shift_s = ur.reshape(bs, ts, RWKV_PROJ)[:, ts - 1]
    ys = _post(xs, og, orw, p_sample[i].reshape(bs * ts, PLE_DIM), post_w, TOKEN_TILE)

    gla_shape = (1, -1, GLA_HEADS, GLA_DK, GLA_DV)
    return (yp.reshape(bp, tp, D_MODEL).astype(x_prompt.dtype),
            ys.reshape(bs, ts, D_MODEL).astype(x_sample.dtype),
            gla_p.reshape(gla_shape).astype(state_gla.dtype), rwkv_p.astype(state_rwkv.dtype),
            shift_p[None].astype(state_shift.dtype),
            gla_s.reshape(gla_shape).astype(state_gla.dtype), rwkv_s.astype(state_rwkv.dtype),
            shift_s[None].astype(state_shift.dtype))
```

```python
import functools

import jax
import jax.numpy as jnp
from jax import lax
from jax.experimental import pallas as pl
from jax.experimental.pallas import tpu as pltpu

F32 = jnp.float32
BF16 = jnp.bfloat16

D_MODEL = 1024
GLA_HEADS = 4
GLA_DK = 64
GLA_DV = 128
GLA_K_W = GLA_HEADS * GLA_DK
GLA_W = GLA_HEADS * GLA_DV
GLA_GATE_RANK = 16
GLA_GATE_NORM = 16.0
GLA_MAIN = 2 * GLA_K_W + 2 * GLA_W
GLA_PROJ = GLA_MAIN + GLA_GATE_RANK
LANES = 128
GLA_COLS = GLA_MAIN + LANES
RWKV_HEAD = 64
RWKV_HEADS = 8
RWKV_W = RWKV_HEADS * RWKV_HEAD
RWKV_PROJ = 3 * RWKV_W + 64 + 64 + 128
D_FF = 2816
PLE_DIM = 256
EPS = 1e-6
RWKV_GN_EPS = 64e-5

VMEM_LIMIT = 56 * 1024 * 1024

NN = ((1,), (0,))
NT = ((1,), (1,))


def _split(x, n):
    parts = []
    r = x
    for i in range(n):
        p = r.astype(BF16)
        parts.append(p)
        if i + 1 < n:
            r = r - p.astype(F32)
    return parts


def _mm(a, b, dims=NN, pa=1, pb=1):
    pieces_a = _split(a, pa)
    pieces_b = _split(b, pb)
    n = max(pa, pb)
    acc = None
    for i, ai in enumerate(pieces_a):
        for j, bj in enumerate(pieces_b):
            if i + j < n:
                t = lax.dot_general(ai, bj, (dims, ((), ())), preferred_element_type=F32)
                acc = t if acc is None else acc + t
    return acc


def _mm_tn(a, b, pa=1, pb=1):
    rows = a.shape[0]
    pad = (-rows) % LANES
    if pad:
        a = jnp.concatenate([a, jnp.zeros((pad, a.shape[1]), a.dtype)], axis=0)
        b = jnp.concatenate([b, jnp.zeros((pad, b.shape[1]), b.dtype)], axis=0)
    return _mm(a.T, b, NN, pa, pb)


def _iota(shape, dim):
    return lax.broadcasted_iota(jnp.int32, shape, dim)


def _softplus(z):
    return jnp.maximum(z, 0.0) + jnp.log(1.0 + jnp.exp(-jnp.abs(z)))


def _sigmoid(z):
    return 1.0 / (1.0 + jnp.exp(-z))


def _tanh(z):
    return 2.0 * _sigmoid(2.0 * z) - 1.0


def _rms(x, g):
    return x * lax.rsqrt(jnp.mean(x * x, axis=-1, keepdims=True) + EPS) * g


GROUP_HEADS = 2
GROUP_W = GROUP_HEADS * RWKV_HEAD
RWKV_GROUPS = RWKV_W // GROUP_W
SEG_W = 2 * LANES


def _head_masks(rows, head_w, heads):
    lane = _iota((rows, heads * head_w), 1)
    return [jnp.where(lane // head_w == h, 1.0, 0.0).astype(BF16) for h in range(heads)]


def _stack_heads(x, hmasks):
    xb = x.astype(BF16)
    return jnp.concatenate([xb * m for m in hmasks], axis=0)


def _block_diag(x_cat, blk):
    heads = blk.shape[0] // x_cat.shape[0]
    return jnp.concatenate([x_cat.astype(BF16)] * heads, axis=0) * blk


def _chunk_masks(rows, seq_len, heads):
    t = _iota((rows, heads * rows), 0)
    s = _iota((rows, heads * rows), 1) % rows
    strict, incl = t > s, t >= s
    t2 = _iota((2 * rows, rows), 0)
    s2 = _iota((2 * rows, rows), 1)
    cum_rows = (t2 < rows) & (t2 >= s2)
    tot_rows = t2 >= rows
    if seq_len < rows:
        same = (t // seq_len) == (s // seq_len)
        strict, incl = strict & same, incl & same
        same2 = ((t2 % rows) // seq_len) == (s2 // seq_len)
        cum_rows, tot_rows = cum_rows & same2, tot_rows & same2
    big = heads * rows
    blk = jnp.where(_iota((big, big), 0) // rows == _iota((big, big), 1) // rows, 1.0, 0.0)
    return dict(strict=strict, incl=incl, eye=jnp.where(t == s, 1.0, 0.0),
                sums=jnp.where(cum_rows | tot_rows, 1.0, 0.0).astype(BF16), blk=blk.astype(BF16))


def _proj_kernel(x_ref, g_ref, wg_ref, wr_ref, ug_ref, ur_ref):
    h = _rms(x_ref[...], g_ref[...]).astype(BF16)
    ug_ref[...] = jnp.dot(h, wg_ref[...], preferred_element_type=F32)
    ur_ref[...] = jnp.dot(h, wr_ref[...], preferred_element_type=F32)


def _const_spec(shape):
    return pl.BlockSpec(shape, lambda *_: (0,) * len(shape), pipeline_mode=pl.Buffered(1))


def _proj(x2d, g, w_gla, w_rw, tm):
    n = x2d.shape[0]
    return pl.pallas_call(
        _proj_kernel,
        grid=(n // tm,),
        in_specs=[
            pl.BlockSpec((tm, D_MODEL), lambda i: (i, 0)),
            _const_spec((1, D_MODEL)),
            _const_spec((D_MODEL, GLA_COLS)),
            _const_spec((D_MODEL, RWKV_PROJ)),
        ],
        out_specs=[
            pl.BlockSpec((tm, GLA_COLS), lambda i: (i, 0)),
            pl.BlockSpec((tm, RWKV_PROJ), lambda i: (i, 0)),
        ],
        out_shape=[
            jax.ShapeDtypeStruct((n, GLA_COLS), F32),
            jax.ShapeDtypeStruct((n, RWKV_PROJ), F32),
        ],
        compiler_params=pltpu.CompilerParams(
            dimension_semantics=("arbitrary",), vmem_limit_bytes=VMEM_LIMIT),
        name="proj",
    )(x2d, g, w_gla, w_rw)


def _gla_prep(us, gkup, gkb, sums):
    rows = us[0].shape[0]
    z = [_mm(u[:, GLA_MAIN:GLA_COLS], gkup) for u in us]
    log_a = [-_softplus(-(x + gkb)) * (1.0 / GLA_GATE_NORM) for x in z]
    cums = [_mm(sums, x, NN, 1, 2) for x in log_a]
    out = []
    for u, x in zip(us, cums):
        cum, cum_end = x[:rows], x[rows:]
        q = u[:, 0:GLA_K_W] * (GLA_DK ** -0.5)
        k = u[:, GLA_K_W:2 * GLA_K_W]
        out.append((q * jnp.exp(cum), k * jnp.exp(-cum), k * jnp.exp(cum_end - cum),
                    u[:, 2 * GLA_K_W:2 * GLA_K_W + GLA_W], u[:, 2 * GLA_K_W + GLA_W:GLA_MAIN],
                    jnp.exp(cum_end)))
    return out


def _gla_intra(q_i, k_i, v, m, kmasks, vmasks):
    scores = jnp.where(m["incl"], _mm(q_i, _stack_heads(k_i, kmasks), NT), 0.0)
    return _mm(scores, _stack_heads(v, vmasks))


def _gla_out(o, gate, gnorm):
    heads = [slice(h * GLA_DV, (h + 1) * GLA_DV) for h in range(GLA_HEADS)]
    return jnp.concatenate(
        [_rms(o[:, hl], gnorm) * (gate[:, hl] * _sigmoid(gate[:, hl])) for hl in heads], axis=1)


def _gla_block_mask():
    return (_iota((GLA_K_W, GLA_W), 0) // GLA_DK) == (_iota((GLA_K_W, GLA_W), 1) // GLA_DV)


def _gla_state_in(s, blk):
    return jnp.where(blk, jnp.concatenate([s] * GLA_HEADS, axis=1), 0.0)


def _gla_state_out(s_bd):
    heads = [s_bd[:, h * GLA_DV:(h + 1) * GLA_DV] for h in range(GLA_HEADS)]
    return (heads[0] + heads[1]) + (heads[2] + heads[3])


def _lane_tiled_t(x):
    pad = LANES - x.shape[0]
    if pad:
        x = jnp.concatenate([x, jnp.zeros((pad, x.shape[1]), x.dtype)], axis=0)
    return x.T


def _gla_seq_kernel(u_ref, s0_ref, gkup_ref, gkb_ref, gn_ref, o_ref, sout_ref, s_scr):
    c = pl.program_id(1)
    n_seq, rows = u_ref.shape[0], u_ref.shape[1]
    blk = _gla_block_mask()

    @pl.when(c == 0)
    def _():
        for b in range(n_seq):
            s_scr[b] = _gla_state_in(s0_ref[b], blk)

    m = _chunk_masks(rows, rows, GLA_HEADS)
    kmasks = _head_masks(rows, GLA_DK, GLA_HEADS)
    vmasks = _head_masks(rows, GLA_DV, GLA_HEADS)
    tok = _gla_prep([u_ref[b] for b in range(n_seq)], gkup_ref[...], gkb_ref[...], m["sums"])
    s_old = [s_scr[b] for b in range(n_seq)]
    intra = [_gla_intra(q_i, k_i, v, m, kmasks, vmasks) for q_i, k_i, _, v, _, _ in tok]
    inter = [_mm(t[0], s) for t, s in zip(tok, s_old)]
    kv = [_mm_tn(k_e, v) for _, _, k_e, v, _, _ in tok]
    for b in range(n_seq):
        gate, g_end = tok[b][4], tok[b][5]
        o_ref[b] = _gla_out(intra[b] + inter[b], gate, gn_ref[...])
        dec = jnp.concatenate([_lane_tiled_t(jnp.broadcast_to(g_end[0:1], (LANES, GLA_K_W)))] * GLA_HEADS,
                              axis=1)
        s_scr[b] = dec * s_old[b] + jnp.where(blk, kv[b], 0.0)

    @pl.when(c == pl.num_programs(1) - 1)
    def _():
        for b in range(n_seq):
            sout_ref[b] = _gla_state_out(s_scr[b])


def _gla_dec_kernel(u_ref, s0_ref, gkup_ref, gkb_ref, gn_ref, o_ref, sout_ref, *, seq_len):
    rows = u_ref.shape[0]
    n_seq = rows // seq_len
    blk = _gla_block_mask()
    m = _chunk_masks(rows, seq_len, GLA_HEADS)
    q_i, k_i, k_e, v, gate, g_end = _gla_prep([u_ref[...]], gkup_ref[...], gkb_ref[...], m["sums"])[0]
    o = _gla_intra(q_i, k_i, v, m, _head_masks(rows, GLA_DK, GLA_HEADS),
                   _head_masks(rows, GLA_DV, GLA_HEADS))
    k_et = _lane_tiled_t(k_e).astype(BF16)
    dec_t = _lane_tiled_t(g_end)
    v_pad = jnp.concatenate([v, jnp.zeros((LANES - rows, GLA_W), F32)], axis=0)
    row = _iota((rows, 1), 0)
    row_pad = _iota((LANES, 1), 0)
    for j in range(n_seq):
        s_bd = _gla_state_in(s0_ref[j], blk)
        o = o + _mm(jnp.where(row // seq_len == j, q_i, 0.0), s_bd)
        kv = _mm(k_et, jnp.where(row_pad // seq_len == j, v_pad, 0.0))
        first = j * seq_len
        sout_ref[j] = _gla_state_out(dec_t[:, first:first + 1] * s_bd + jnp.where(blk, kv, 0.0))
    o_ref[...] = _gla_out(o, gate, gn_ref[...])


def _gla_param_specs():
    return [_const_spec((LANES, GLA_K_W)), _const_spec((1, GLA_K_W)), _const_spec((1, GLA_DV))]


def _gla_seq(u3, s0, params, rows, n_seq):
    b, t, _ = u3.shape
    sdim = GLA_HEADS * GLA_DK
    state_spec = pl.BlockSpec((n_seq, sdim, GLA_DV), lambda i, j: (i, 0, 0))
    return pl.pallas_call(
        _gla_seq_kernel,
        grid=(b // n_seq, t // rows),
        in_specs=[pl.BlockSpec((n_seq, rows, GLA_COLS), lambda i, j: (i, j, 0)), state_spec]
        + _gla_param_specs(),
        out_specs=[pl.BlockSpec((n_seq, rows, GLA_W), lambda i, j: (i, j, 0)), state_spec],
        out_shape=[
            jax.ShapeDtypeStruct((b, t, GLA_W), F32),
            jax.ShapeDtypeStruct((b, sdim, GLA_DV), F32),
        ],
        scratch_shapes=[pltpu.VMEM((n_seq, sdim, GLA_W), F32)],
        compiler_params=pltpu.CompilerParams(
            dimension_semantics=("arbitrary", "arbitrary"), vmem_limit_bytes=VMEM_LIMIT),
        name="gla_seq",
    )(u3, s0, *params)


def _gla_dec(u2, s0, params, seq_len, n_seq):
    n = u2.shape[0]
    rows = n_seq * seq_len
    sdim = GLA_HEADS * GLA_DK
    state_spec = pl.BlockSpec((n_seq, sdim, GLA_DV), lambda i: (i, 0, 0))
    return pl.pallas_call(
        functools.partial(_gla_dec_kernel, seq_len=seq_len),
        grid=(n // rows,),
        in_specs=[pl.BlockSpec((rows, GLA_COLS), lambda i: (i, 0)), state_spec] + _gla_param_specs(),
        out_specs=[pl.BlockSpec((rows, GLA_W), lambda i: (i, 0)), state_spec],
        out_shape=[
            jax.ShapeDtypeStruct((n, GLA_W), F32),
            jax.ShapeDtypeStruct(s0.shape, F32),
        ],
        compiler_params=pltpu.CompilerParams(
            dimension_semantics=("arbitrary",), vmem_limit_bytes=VMEM_LIMIT),
        name="gla_dec",
    )(u2, s0, *params)


def _seg_sum(x, seg, pa=1):
    return jnp.concatenate(
        [_mm(x[:, i:i + SEG_W], seg, NN, pa, 1) for i in range(0, RWKV_W, SEG_W)], axis=1)


def _rwkv_prep(u, prev, mu, w0, w2, a0, a2, g2, k_k, k_a, rk, seg, sums):
    rows = u.shape[0]
    xr = u + mu * (prev - u)
    r = xr[:, 0:RWKV_W]
    kr = xr[:, RWKV_W:2 * RWKV_W]
    vr = xr[:, 2 * RWKV_W:3 * RWKV_W]
    wa = xr[:, 3 * RWKV_W:3 * RWKV_W + LANES]
    gd = xr[:, 3 * RWKV_W + LANES:RWKV_PROJ]
    w = -_softplus(-(w0 + _mm(_tanh(wa), w2))) - 0.5
    lw = -jnp.exp(w)
    a_sig = _sigmoid(a0 + _mm(wa, a2))
    gate = _mm(_sigmoid(gd), g2)
    kk = kr * k_k
    kk = kk * lax.rsqrt(jnp.maximum(_seg_sum(kk * kk, seg), 1e-24))
    kr = kr * (1.0 + (a_sig - 1.0) * k_a)
    bonus = _seg_sum(r * kr * rk, seg) * vr
    b_vec = kk * a_sig
    block = sums.shape[1]
    cums = [_mm(sums, lw[i:i + block], NN, 1, 2) for i in range(0, rows, block)]
    cum = jnp.concatenate([x[:block] for x in cums], axis=0)
    cum_end = jnp.concatenate([x[block:] for x in cums], axis=0)
    e_neg = jnp.exp(-cum)
    e_end = jnp.exp(cum_end - cum)
    bf = lambda x: x.astype(BF16)
    return (bf(-kk * jnp.exp(cum - lw)), bf(r * jnp.exp(cum)), bf(kr * e_neg), bf(b_vec * e_neg),
            bf(kr * e_end), bf(b_vec * e_end), bf(vr), gate, bonus, jnp.exp(cum_end))


def _rwkv_intra(units, m, hmasks, n_double):
    rows = units[0][0].shape[0]
    stack = lambda x: _stack_heads(x, hmasks)
    cat_w = m["strict"].shape[1]
    g = [_mm(jnp.concatenate([at, rt], axis=0), jnp.concatenate([stack(bt), stack(kt)], axis=0), NT)
         for at, rt, kt, bt, _ in units]
    a_ab = [jnp.where(m["strict"], x[:rows, :cat_w], 0.0) for x in g]
    a_rb = [jnp.where(m["incl"], x[rows:, :cat_w], 0.0) for x in g]
    a_ak = [jnp.where(m["strict"], x[:rows, cat_w:], 0.0) for x in g]
    a_rk = [jnp.where(m["incl"], x[rows:, cat_w:], 0.0) for x in g]
    tinv = [m["eye"] + a for a in a_ab]
    apow = [_mm(a, _block_diag(a, m["blk"])) for a in a_ab]
    for _ in range(n_double - 1):
        both = [_mm(jnp.concatenate([t, a], axis=0), _block_diag(a, m["blk"])) for t, a in zip(tinv, apow)]
        tinv = [t + x[:rows] for t, x in zip(tinv, both)]
        apow = [x[rows:] for x in both]
    tinv = [t + _mm(t, _block_diag(a, m["blk"])) for t, a in zip(tinv, apow)]
    v_s = [stack(vp) for _, _, _, _, vp in units]
    akv = [_mm(jnp.concatenate([a, b], axis=0), v) for a, b, v in zip(a_ak, a_rk, v_s)]
    wu = [_mm(t, jnp.concatenate([stack(at), stack(x[:rows])], axis=1))
          for t, (at, _, _, _, _), x in zip(tinv, units, akv)]
    z = [_mm(a, jnp.concatenate([stack(x[:, :GROUP_W]), stack(x[:, GROUP_W:])], axis=1))
         for a, x in zip(a_rb, wu)]
    return [(x[:, :GROUP_W], x[:, GROUP_W:], rt + zz[:, :GROUP_W], zz[:, GROUP_W:] + kv[rows:])
            for x, (_, rt, _, _, _), zz, kv in zip(wu, units, z, akv)]


def _rwkv_out(ys, gates, bonuses, lnw, lnb, seg):
    inv = 1.0 / RWKV_HEAD
    yc = [y - _seg_sum(y, seg, 2) * inv for y in ys]
    var = [_seg_sum(c * c, seg) * inv for c in yc]
    return [(c * lax.rsqrt(v + RWKV_GN_EPS) * lnw + lnb + bonus) * gate
            for c, v, bonus, gate in zip(yc, var, bonuses, gates)]


def _rwkv_state_in(s_ref, idx, g):
    zero = jnp.zeros((RWKV_HEAD, RWKV_HEAD), F32)
    blocks = []
    for h in range(GROUP_HEADS):
        parts = [zero] * GROUP_HEADS
        parts[h] = s_ref[idx, g * GROUP_HEADS + h]
        blocks.append(jnp.concatenate(parts, axis=1))
    return jnp.concatenate(blocks, axis=0)


def _rwkv_state_out(s_ref, idx, g, s2):
    for h in range(GROUP_HEADS):
        sl = slice(h * RWKV_HEAD, (h + 1) * RWKV_HEAD)
        s_ref[idx, g * GROUP_HEADS + h] = s2[sl, sl]


def _head_block_mask():
    return (_iota((GROUP_W, GROUP_W), 0) // RWKV_HEAD) == (_iota((GROUP_W, GROUP_W), 1) // RWKV_HEAD)


def _rwkv_seq_kernel(u_ref, shift0_ref, s0_ref, mu_ref, w0_ref, w2_ref, a0_ref, a2_ref,
                     g2_ref, kk_ref, ka_ref, rk_ref, lnw_ref, lnb_ref, seg_ref,
                     o_ref, sout_ref, s_scr, prev_scr):
    c = pl.program_id(1)
    n_seq, rows = u_ref.shape[0], u_ref.shape[1]
    groups = [slice(g * GROUP_W, (g + 1) * GROUP_W) for g in range(RWKV_GROUPS)]
    ids = [(b, g) for b in range(n_seq) for g in range(RWKV_GROUPS)]

    @pl.when(c == 0)
    def _():
        for b in range(n_seq):
            for g in range(RWKV_GROUPS):
                s_scr[b, g] = _rwkv_state_in(s0_ref, b, g)
            prev_scr[b] = shift0_ref[b]

    seg = seg_ref[...]
    m = _chunk_masks(rows, rows, GROUP_HEADS)
    hmasks = _head_masks(rows, RWKV_HEAD, GROUP_HEADS)
    head_blk = _head_block_mask()
    n_double = rows.bit_length() - 2
    row = _iota((rows, 1), 0)

    us = [u_ref[b] for b in range(n_seq)]
    prev = jnp.concatenate(
        [jnp.where(row == 0, prev_scr[b], pltpu.roll(u, 1, axis=0)) for b, u in enumerate(us)], axis=0)
    for b, u in enumerate(us):
        prev_scr[b] = u[rows - 1:rows]
    at, rt, kt, bt, ke, be, vb, gate, bonus, g_end = _rwkv_prep(
        jnp.concatenate(us, axis=0), prev, mu_ref[...], w0_ref[...], w2_ref[...], a0_ref[...],
        a2_ref[...], g2_ref[...], kk_ref[...], ka_ref[...], rk_ref[...], seg, m["sums"])
    tok, units = [], []
    for b in range(n_seq):
        sl = slice(b * rows, (b + 1) * rows)
        tok.append((vb[sl], gate[sl], bonus[sl], ke[sl], be[sl], g_end[b * rows:b * rows + 1]))
        units += [(at[sl, gl], rt[sl, gl], kt[sl, gl], bt[sl, gl], vb[sl, gl]) for gl in groups]

    intra = _rwkv_intra(units, m, hmasks, n_double)
    s_old = [s_scr[b, g] for b, g in ids]
    uy = [_mm(jnp.concatenate([w_m, r_m], axis=0), s2, NT) for (w_m, _, r_m, _), s2 in zip(intra, s_old)]
    upd = []
    for (b, g), (_, u0, _, _), x in zip(ids, intra, uy):
        vb, _, _, ke, be, _ = tok[b]
        gl = groups[g]
        upd.append(_mm_tn(jnp.concatenate([x[:rows] + u0, vb[:, gl].astype(F32)], axis=0),
                          jnp.concatenate([be[:, gl], ke[:, gl]], axis=0)))
    for (b, g), s2, d in zip(ids, s_old, upd):
        s_scr[b, g] = s2 * tok[b][5][:, groups[g]] + jnp.where(head_blk, d, 0.0)
    ys = [jnp.concatenate([uy[i][rows:] + intra[i][3] for i, (bb, _) in enumerate(ids) if bb == b],
                          axis=1) for b in range(n_seq)]
    out = _rwkv_out([jnp.concatenate(ys, axis=0)], [gate], [bonus], lnw_ref[...], lnb_ref[...], seg)[0]
    for b in range(n_seq):
        o_ref[b] = out[b * rows:(b + 1) * rows]

    @pl.when(c == pl.num_programs(1) - 1)
    def _():
        for b, g in ids:
            _rwkv_state_out(sout_ref, b, g, s_scr[b, g])


def _rwkv_dec_kernel(u_ref, shiftx_ref, s0_ref, mu_ref, w0_ref, w2_ref, a0_ref, a2_ref, g2_ref,
                     kk_ref, ka_ref, rk_ref, lnw_ref, lnb_ref, seg_ref,
                     o_ref, sout_ref, *, seq_len):
    rows = u_ref.shape[0]
    n_seq = rows // seq_len
    u = u_ref[...]
    row = _iota((rows, 1), 0)
    prev = jnp.where(row % seq_len == 0, shiftx_ref[...], pltpu.roll(u, 1, axis=0))
    seg = seg_ref[...]
    m = _chunk_masks(rows, seq_len, GROUP_HEADS)
    at, rt, kt, bt, ke, be, vb, gate, bonus, g_end = _rwkv_prep(
        u, prev, mu_ref[...], w0_ref[...], w2_ref[...], a0_ref[...], a2_ref[...], g2_ref[...],
        kk_ref[...], ka_ref[...], rk_ref[...], seg, m["sums"])
    hmasks = _head_masks(rows, RWKV_HEAD, GROUP_HEADS)
    head_blk = _head_block_mask()
    n_double = seq_len.bit_length() - 2
    lane2 = _iota((1, 2 * rows), 1)
    u_half = lane2 < rows

    groups = [slice(g * GROUP_W, (g + 1) * GROUP_W) for g in range(RWKV_GROUPS)]
    intra = _rwkv_intra([tuple(x[:, gl] for x in (at, rt, kt, bt, vb)) for gl in groups],
                        m, hmasks, n_double)
    wr = [jnp.concatenate([w_m, r_m], axis=0).astype(BF16) for w_m, _, r_m, _ in intra]
    bk = [jnp.concatenate([be[:, gl], ke[:, gl]], axis=0) for gl in groups]
    base = [jnp.concatenate([x[1], vb[:, gl].astype(F32)], axis=0).T for x, gl in zip(intra, groups)]
    mine = [(lane2 % rows) // seq_len == j for j in range(n_seq)]
    ids = [(j, g) for j in range(n_seq) for g in range(RWKV_GROUPS)]
    s_old = [_rwkv_state_in(s0_ref, j, g) for j, g in ids]
    uyt = [_mm(s2, wr[g], NT) for (j, g), s2 in zip(ids, s_old)]
    upd = [_mm(jnp.where(mine[j], jnp.where(u_half, x, 0.0) + base[g], 0.0), bk[g])
           for (j, g), x in zip(ids, uyt)]
    for (j, g), s2, d in zip(ids, s_old, upd):
        first = j * seq_len
        _rwkv_state_out(sout_ref, j, g,
                        s2 * g_end[first:first + 1, groups[g]] + jnp.where(head_blk, d, 0.0))
    ys = []
    for g in range(RWKV_GROUPS):
        acc = jnp.zeros((GROUP_W, 2 * rows), F32)
        for (j, gg), x in zip(ids, uyt):
            if gg == g:
                acc = jnp.where(mine[j], x, acc)
        ys.append(acc.T[rows:] + intra[g][3])
    o_ref[...] = _rwkv_out([jnp.concatenate(ys, axis=1)], [gate], [bonus],
                           lnw_ref[...], lnb_ref[...], seg)[0]


def _rwkv_param_specs():
    vec = lambda n: _const_spec((1, n))
    return [
        vec(RWKV_PROJ),
        vec(RWKV_W),
        _const_spec((LANES, RWKV_W)),
        vec(RWKV_W),
        _const_spec((LANES, RWKV_W)),
        _const_spec((LANES, RWKV_W)),
        vec(RWKV_W), vec(RWKV_W), vec(RWKV_W), vec(RWKV_W), vec(RWKV_W),
        _const_spec((SEG_W, SEG_W)),
    ]


def _rwkv_seq(u3, shift0, s0, params, rows, n_seq):
    b, t, _ = u3.shape
    state_spec = pl.BlockSpec((None, n_seq, RWKV_HEADS, RWKV_HEAD, RWKV_HEAD),
                              lambda i, j: (0, i, 0, 0, 0))
    return pl.pallas_call(
        _rwkv_seq_kernel,
        grid=(b // n_seq, t // rows),
        in_specs=[
            pl.BlockSpec((n_seq, rows, RWKV_PROJ), lambda i, j: (i, j, 0)),
            pl.BlockSpec((n_seq, 1, RWKV_PROJ), lambda i, j: (i, 0, 0)),
            state_spec,
        ] + _rwkv_param_specs(),
        out_specs=[pl.BlockSpec((n_seq, rows, RWKV_W), lambda i, j: (i, j, 0)), state_spec],
        out_shape=[
            jax.ShapeDtypeStruct((b, t, RWKV_W), F32),
            jax.ShapeDtypeStruct(s0.shape, F32),
        ],
        scratch_shapes=[
            pltpu.VMEM((n_seq, RWKV_GROUPS, GROUP_W, GROUP_W), F32),
            pltpu.VMEM((n_seq, 1, RWKV_PROJ), F32),
        ],
        compiler_params=pltpu.CompilerParams(
            dimension_semantics=("arbitrary", "arbitrary"), vmem_limit_bytes=VMEM_LIMIT),
        name="rwkv_seq",
    )(u3, shift0, s0, *params)


def _rwkv_dec(u2, shiftx, s0, params, seq_len, n_seq):
    n = u2.shape[0]
    rows = n_seq * seq_len
    state_spec = pl.BlockSpec((None, n_seq, RWKV_HEADS, RWKV_HEAD, RWKV_HEAD),
                              lambda i: (0, i, 0, 0, 0))
    return pl.pallas_call(
        functools.partial(_rwkv_dec_kernel, seq_len=seq_len),
        grid=(n // rows,),
        in_specs=[
            pl.BlockSpec((rows, RWKV_PROJ), lambda i: (i, 0)),
            pl.BlockSpec((rows, RWKV_PROJ), lambda i: (i, 0)),
            state_spec,
        ] + _rwkv_param_specs(),
        out_specs=[pl.BlockSpec((rows, RWKV_W), lambda i: (i, 0)), state_spec],
        out_shape=[
            jax.ShapeDtypeStruct((n, RWKV_W), F32),
            jax.ShapeDtypeStruct(s0.shape, F32),
        ],
        compiler_params=pltpu.CompilerParams(
            dimension_semantics=("arbitrary",), vmem_limit_bytes=VMEM_LIMIT),
        name="rwkv_dec",
    )(u2, shiftx, s0, *params)


FF_CHUNK = D_FF // 2


def _post_kernel(x_ref, og_ref, or_ref, p_ref, wo_ref, nffn_ref, wg_ref, wu_ref, wd_ref,
                 nple_ref, wpg_ref, wpp_ref, nf_ref, y_ref):
    half = x_ref.shape[0] // 2
    parts = [slice(0, half), slice(half, 2 * half)]
    dot = lambda a, b: jnp.dot(a, b, preferred_element_type=F32)
    o = [jnp.concatenate([og_ref[p, :], or_ref[p, :]], axis=1).astype(BF16) for p in parts]
    x = [x_ref[p, :] + dot(oo, wo_ref[...]) for p, oo in zip(parts, o)]
    h2 = [_rms(xx, nffn_ref[...]).astype(BF16) for xx in x]
    for i in range(0, D_FF, FF_CHUNK):
        gate = [dot(h, wg_ref[:, i:i + FF_CHUNK]) for h in h2]
        up = [dot(h, wu_ref[:, i:i + FF_CHUNK]) for h in h2]
        act = [(g * _sigmoid(g) * u).astype(BF16) for g, u in zip(gate, up)]
        x = [xx + dot(a, wd_ref[i:i + FF_CHUNK, :]) for xx, a in zip(x, act)]
    h3 = [_rms(xx, nple_ref[...]).astype(BF16) for xx in x]
    pg = [_sigmoid(dot(h, wpg_ref[...])) for h in h3]
    pp = [dot(p_ref[p, :].astype(BF16), wpp_ref[...]) for p in parts]
    for p, xx, a, b in zip(parts, x, pg, pp):
        y_ref[p, :] = _rms(xx + a * b, nf_ref[...])


def _post(x2d, og, orw, p2d, weights, tm):
    n = x2d.shape[0]
    wo, nffn, wg, wu, wd, nple, wpg, wpp, nf = weights
    tok = lambda w: pl.BlockSpec((tm, w), lambda i: (i, 0))
    return pl.pallas_call(
        _post_kernel,
        grid=(n // tm,),
        in_specs=[
            tok(D_MODEL), tok(GLA_W), tok(RWKV_W), tok(PLE_DIM),
            _const_spec((D_MODEL, D_MODEL)), _const_spec((1, D_MODEL)),
            _const_spec((D_MODEL, D_FF)), _const_spec((D_MODEL, D_FF)), _const_spec((D_FF, D_MODEL)),
            _const_spec((1, D_MODEL)), _const_spec((D_MODEL, D_MODEL)), _const_spec((PLE_DIM, D_MODEL)),
            _const_spec((1, D_MODEL)),
        ],
        out_specs=tok(D_MODEL),
        out_shape=jax.ShapeDtypeStruct((n, D_MODEL), F32),
        compiler_params=pltpu.CompilerParams(
            dimension_semantics=("arbitrary",), vmem_limit_bytes=VMEM_LIMIT),
        name="post",
    )(x2d, og, orw, p2d, wo, nffn, wg, wu, wd, nple, wpg, wpp, nf)


PROMPT_CHUNK = 64
PROMPT_SEQS_PER_STEP = 8
DEC_TILE_SEQS = 16
TOKEN_TILE = 512


def kernel(x_prompt, x_sample, state_gla, state_rwkv, state_shift, p_prompt, p_sample, norm_mix, w_in, gla_gk_up, gla_gk_bias, gla_norm, rwkv_mu, rwkv_w0, rwkv_w2, rwkv_a0, rwkv_a2, rwkv_g2, rwkv_k_k, rwkv_k_a, rwkv_r_k, rwkv_ln_w, rwkv_ln_b, w_out, norm_ffn, w_gate, w_up, w_down, norm_ple, w_ple_gate, w_ple_proj, norm_final):
    assert w_in.shape[0] == 1
    i = 0
    rowv = lambda a: a.astype(F32).reshape(1, -1)
    zeros = lambda r, c: jnp.zeros((r, c), F32)
    w_in_i = w_in[i]
    w_gla = jnp.concatenate(
        [w_in_i[:, :GLA_PROJ], zeros(D_MODEL, GLA_COLS - GLA_PROJ)], axis=1).astype(BF16)
    w_rw = w_in_i[:, GLA_PROJ:].astype(BF16)
    gk_up = jnp.concatenate(
        [gla_gk_up[i].astype(F32), zeros(LANES - GLA_GATE_RANK, GLA_K_W)], axis=0).astype(BF16)
    seg = jnp.arange(SEG_W)[:, None] // RWKV_HEAD == jnp.arange(SEG_W)[None, :] // RWKV_HEAD
    rwkv_params = (
        rowv(rwkv_mu[i]), rowv(rwkv_w0[i]),
        jnp.concatenate([rwkv_w2[i].astype(F32), zeros(64, RWKV_W)], axis=0).astype(BF16),
        rowv(rwkv_a0[i]),
        jnp.concatenate([zeros(64, RWKV_W), rwkv_a2[i].astype(F32)], axis=0).astype(BF16),
        rwkv_g2[i].astype(BF16),
        rowv(rwkv_k_k[i]), rowv(rwkv_k_a[i]), rowv(rwkv_r_k[i]), rowv(rwkv_ln_w[i]), rowv(rwkv_ln_b[i]),
        seg.astype(BF16),
    )
    post_w = (
        w_out[i].astype(BF16), rowv(norm_ffn[i]), w_gate[i].astype(BF16), w_up[i].astype(BF16),
        w_down[i].astype(BF16), rowv(norm_ple[i]), w_ple_gate[i].astype(BF16),
        w_ple_proj[i].astype(BF16), rowv(norm_final),
    )
    gla_w = (gk_up, rowv(gla_gk_bias[i]), rowv(gla_norm[i]))
    g_mix = rowv(norm_mix[i])

    bp, tp, _ = x_prompt.shape
    xp = x_prompt.astype(F32).reshape(bp * tp, D_MODEL)
    ug, ur = _proj(xp, g_mix, w_gla, w_rw, TOKEN_TILE)
    og, gla_p = _gla_seq(ug.reshape(bp, tp, GLA_COLS),
                         jnp.zeros((bp, GLA_HEADS * GLA_DK, GLA_DV), F32), gla_w,
                         PROMPT_CHUNK, PROMPT_SEQS_PER_STEP)
    ur3 = ur.reshape(bp, tp, RWKV_PROJ)
    orw, rwkv_p = _rwkv_seq(ur3, jnp.zeros((bp, 1, RWKV_PROJ), F32),
                            jnp.zeros((1, bp, RWKV_HEADS, RWKV_HEAD, RWKV_HEAD), F32), rwkv_params,
                            PROMPT_CHUNK, PROMPT_SEQS_PER_STEP)
    shift_p = ur3[:, tp - 1]
    yp = _post(xp, og.reshape(bp * tp, GLA_W), orw.reshape(bp * tp, RWKV_W),
               p_prompt[i].reshape(bp * tp, PLE_DIM), post_w, TOKEN_TILE)

    bs, ts, _ = x_sample.shape
    xs = x_sample.astype(F32).reshape(bs * ts, D_MODEL)
    ug, ur = _proj(xs, g_mix, w_gla, w_rw, TOKEN_TILE)
    og, gla_s = _gla_dec(ug, state_gla[i].astype(F32).reshape(bs, GLA_HEADS * GLA_DK, GLA_DV),
                         gla_w, ts, DEC_TILE_SEQS)
    shiftx = jnp.pad(state_shift[i].astype(F32)[:, None, :], ((0, 0), (0, ts - 1), (0, 0)))
    orw, rwkv_s = _rwkv_dec(ur, shiftx.reshape(bs * ts, RWKV_PROJ), state_rwkv.astype(F32),
                            rwkv_params, ts, DEC_TILE_SEQS)
    shift_s = ur.reshape(bs, ts, RWKV_PROJ)[:, ts - 1]
    ys = _post(xs, og, orw, p_sample[i].reshape(bs * ts, PLE_DIM), post_w, TOKEN_TILE)

    gla_shape = (1, -1, GLA_HEADS, GLA_DK, GLA_DV)
    return (yp.reshape(bp, tp, D_MODEL).astype(x_prompt.dtype),
            ys.reshape(bs, ts, D_MODEL).astype(x_sample.dtype),
            gla_p.reshape(gla_shape).astype(state_gla.dtype), rwkv_p.astype(state_rwkv.dtype),
            shift_p[None].astype(state_shift.dtype),
            gla_s.reshape(gla_shape).astype(state_gla.dtype), rwkv_s.astype(state_rwkv.dtype),
            shift_s[None].astype(state_shift.dtype))
```

```python
import functools

import jax
import jax.numpy as jnp
from jax import lax
from jax.experimental import pallas as pl
from jax.experimental.pallas import tpu as pltpu

F32 = jnp.float32
BF16 = jnp.bfloat16

D_MODEL = 1024
GLA_HEADS = 4
GLA_DK = 64
GLA_DV = 128
GLA_K_W = GLA_HEADS * GLA_DK
GLA_W = GLA_HEADS * GLA_DV
GLA_GATE_RANK = 16
GLA_GATE_NORM = 16.0
GLA_MAIN = 2 * GLA_K_W + 2 * GLA_W
GLA_PROJ = GLA_MAIN + GLA_GATE_RANK
LANES = 128
GLA_COLS = GLA_MAIN + LANES
RWKV_HEAD = 64
RWKV_HEADS = 8
RWKV_W = RWKV_HEADS * RWKV_HEAD
RWKV_PROJ = 3 * RWKV_W + 64 + 64 + 128
D_FF = 2816
PLE_DIM = 256
EPS = 1e-6
RWKV_GN_EPS = 64e-5

VMEM_LIMIT = 56 * 1024 * 1024

NN = ((1,), (0,))
NT = ((1,), (1,))


def _split(x, n):
    parts = []
    r = x
    for i in range(n):
        p = r.astype(BF16)
        parts.append(p)
        if i + 1 < n:
            r = r - p.astype(F32)
    return parts


def _mm(a, b, dims=NN, pa=1, pb=1):
    pieces_a = _split(a, pa)
    pieces_b = _split(b, pb)
    n = max(pa, pb)
    acc = None
    for i, ai in enumerate(pieces_a):
        for j, bj in enumerate(pieces_b):
            if i + j < n:
                t = lax.dot_general(ai, bj, (dims, ((), ())), preferred_element_type=F32)
                acc = t if acc is None else acc + t
    return acc


def _mm_tn(a, b, pa=1, pb=1):
    rows = a.shape[0]
    pad = (-rows) % LANES
    if pad:
        a = jnp.concatenate([a, jnp.zeros((pad, a.shape[1]), a.dtype)], axis=0)
        b = jnp.concatenate([b, jnp.zeros((pad, b.shape[1]), b.dtype)], axis=0)
    return _mm(a.T, b, NN, pa, pb)


def _iota(shape, dim):
    return lax.broadcasted_iota(jnp.int32, shape, dim)


def _log_sigmoid(z):
    return jnp.minimum(z, 0.0) - jnp.log(1.0 + jnp.exp(-jnp.abs(z)))


SUBLANES = 8


def _shift_rows(u, first):
    r = pltpu.roll(u, 1, axis=0)
    head = jnp.where(_iota((SUBLANES, 1), 0) == 0, first, r[:SUBLANES])
    return jnp.concatenate([head, r[SUBLANES:]], axis=0)


def _sigmoid(z):
    return 1.0 / (1.0 + jnp.exp(-z))


def _tanh(z):
    return 2.0 * _sigmoid(2.0 * z) - 1.0


def _rms(x, g):
    return x * lax.rsqrt(jnp.mean(x * x, axis=-1, keepdims=True) + EPS) * g


GROUP_HEADS = 2
GROUP_W = GROUP_HEADS * RWKV_HEAD
RWKV_GROUPS = RWKV_W // GROUP_W
SEG_W = 2 * LANES


def _head_masks(rows, head_w, heads):
    lane = _iota((rows, heads * head_w), 1)
    return [jnp.where(lane // head_w == h, 1.0, 0.0).astype(BF16) for h in range(heads)]


def _stack_heads(x, hmasks):
    xb = x.astype(BF16)
    return jnp.concatenate([xb * m for m in hmasks], axis=0)


def _block_diag(x_cat, blk):
    heads = blk.shape[0] // x_cat.shape[0]
    return jnp.concatenate([x_cat.astype(BF16)] * heads, axis=0) * blk


def _chunk_masks(rows, seq_len, heads):
    t = _iota((rows, heads * rows), 0)
    s = _iota((rows, heads * rows), 1) % rows
    strict, incl = t > s, t >= s
    t2 = _iota((2 * rows, rows), 0)
    s2 = _iota((2 * rows, rows), 1)
    cum_rows = (t2 < rows) & (t2 >= s2)
    tot_rows = t2 >= rows
    if seq_len < rows:
        same = (t // seq_len) == (s // seq_len)
        strict, incl = strict & same, incl & same
        same2 = ((t2 % rows) // seq_len) == (s2 // seq_len)
        cum_rows, tot_rows = cum_rows & same2, tot_rows & same2
    big = heads * rows
    blk = jnp.where(_iota((big, big), 0) // rows == _iota((big, big), 1) // rows, 1.0, 0.0)
    return dict(strict=strict, incl=incl, eye=jnp.where(t == s, 1.0, 0.0),
                sums=jnp.where(cum_rows | tot_rows, 1.0, 0.0).astype(BF16), blk=blk.astype(BF16))


def _proj_kernel(x_ref, g_ref, wg_ref, wr_ref, ug_ref, ur_ref):
    h = _rms(x_ref[...], g_ref[...]).astype(BF16)
    ug_ref[...] = jnp.dot(h, wg_ref[...], preferred_element_type=F32)
    ur_ref[...] = jnp.dot(h, wr_ref[...], preferred_element_type=F32)


def _const_spec(shape):
    return pl.BlockSpec(shape, lambda *_: (0,) * len(shape), pipeline_mode=pl.Buffered(1))


def _proj(x2d, g, w_gla, w_rw, tm):
    n = x2d.shape[0]
    return pl.pallas_call(
        _proj_kernel,
        grid=(n // tm,),
        in_specs=[
            pl.BlockSpec((tm, D_MODEL), lambda i: (i, 0)),
            _const_spec((1, D_MODEL)),
            _const_spec((D_MODEL, GLA_COLS)),
            _const_spec((D_MODEL, RWKV_PROJ)),
        ],
        out_specs=[
            pl.BlockSpec((tm, GLA_COLS), lambda i: (i, 0)),
            pl.BlockSpec((tm, RWKV_PROJ), lambda i: (i, 0)),
        ],
        out_shape=[
            jax.ShapeDtypeStruct((n, GLA_COLS), F32),
            jax.ShapeDtypeStruct((n, RWKV_PROJ), F32),
        ],
        compiler_params=pltpu.CompilerParams(
            dimension_semantics=("arbitrary",), vmem_limit_bytes=VMEM_LIMIT),
        name="proj",
    )(x2d, g, w_gla, w_rw)


def _gla_prep(us, gkup, gkb, sums):
    rows = us[0].shape[0]
    z = [_mm(u[:, GLA_MAIN:GLA_COLS], gkup) for u in us]
    log_a = [_log_sigmoid(x + gkb) * (1.0 / GLA_GATE_NORM) for x in z]
    cums = [_mm(sums, x, NN, 1, 2) for x in log_a]
    out = []
    for u, x in zip(us, cums):
        cum, cum_end = x[:rows], x[rows:]
        q = u[:, 0:GLA_K_W] * (GLA_DK ** -0.5)
        k = u[:, GLA_K_W:2 * GLA_K_W]
        out.append((q * jnp.exp(cum), k * jnp.exp(-cum), k * jnp.exp(cum_end - cum),
                    u[:, 2 * GLA_K_W:2 * GLA_K_W + GLA_W], u[:, 2 * GLA_K_W + GLA_W:GLA_MAIN],
                    jnp.exp(cum_end)))
    return out


def _gla_intra(q_i, k_i, v, m, kmasks, vmasks):
    scores = jnp.where(m["incl"], _mm(q_i, _stack_heads(k_i, kmasks), NT), 0.0)
    return _mm(scores, _stack_heads(v, vmasks))


def _gla_out(o, gate, gnorm):
    heads = [slice(h * GLA_DV, (h + 1) * GLA_DV) for h in range(GLA_HEADS)]
    return jnp.concatenate(
        [_rms(o[:, hl], gnorm) * (gate[:, hl] * _sigmoid(gate[:, hl])) for hl in heads], axis=1)


def _gla_block_mask():
    return (_iota((GLA_K_W, GLA_W), 0) // GLA_DK) == (_iota((GLA_K_W, GLA_W), 1) // GLA_DV)


def _gla_state_in(s, blk):
    return jnp.where(blk, jnp.concatenate([s] * GLA_HEADS, axis=1), 0.0)


def _gla_state_out(s_bd):
    heads = [s_bd[:, h * GLA_DV:(h + 1) * GLA_DV] for h in range(GLA_HEADS)]
    return (heads[0] + heads[1]) + (heads[2] + heads[3])


def _lane_tiled_t(x):
    pad = LANES - x.shape[0]
    if pad:
        x = jnp.concatenate([x, jnp.zeros((pad, x.shape[1]), x.dtype)], axis=0)
    return x.T


def _gla_seq_kernel(u_ref, s0_ref, gkup_ref, gkb_ref, gn_ref, o_ref, sout_ref, s_scr):
    c = pl.program_id(1)
    n_seq, rows = u_ref.shape[0], u_ref.shape[1]
    blk = _gla_block_mask()

    @pl.when(c == 0)
    def _():
        for b in range(n_seq):
            s_scr[b] = _gla_state_in(s0_ref[b], blk)

    m = _chunk_masks(rows, rows, GLA_HEADS)
    kmasks = _head_masks(rows, GLA_DK, GLA_HEADS)
    vmasks = _head_masks(rows, GLA_DV, GLA_HEADS)
    tok = _gla_prep([u_ref[b] for b in range(n_seq)], gkup_ref[...], gkb_ref[...], m["sums"])
    s_old = [s_scr[b] for b in range(n_seq)]
    intra = [_gla_intra(q_i, k_i, v, m, kmasks, vmasks) for q_i, k_i, _, v, _, _ in tok]
    inter = [_mm(t[0], s) for t, s in zip(tok, s_old)]
    kv = [_mm_tn(k_e, v) for _, _, k_e, v, _, _ in tok]
    for b in range(n_seq):
        gate, g_end = tok[b][4], tok[b][5]
        o_ref[b] = _gla_out(intra[b] + inter[b], gate, gn_ref[...])
        dec = jnp.concatenate([_lane_tiled_t(jnp.broadcast_to(g_end[0:1], (LANES, GLA_K_W)))] * GLA_HEADS,
                              axis=1)
        s_scr[b] = dec * s_old[b] + jnp.where(blk, kv[b], 0.0)

    @pl.when(c == pl.num_programs(1) - 1)
    def _():
        for b in range(n_seq):
            sout_ref[b] = _gla_state_out(s_scr[b])


def _gla_dec_kernel(u_ref, s0_ref, gkup_ref, gkb_ref, gn_ref, o_ref, sout_ref, *, seq_len):
    rows = u_ref.shape[0]
    n_seq = rows // seq_len
    blk = _gla_block_mask()
    m = _chunk_masks(rows, seq_len, GLA_HEADS)
    q_i, k_i, k_e, v, gate, g_end = _gla_prep([u_ref[...]], gkup_ref[...], gkb_ref[...], m["sums"])[0]
    o = _gla_intra(q_i, k_i, v, m, _head_masks(rows, GLA_DK, GLA_HEADS),
                   _head_masks(rows, GLA_DV, GLA_HEADS))
    k_et = _lane_tiled_t(k_e).astype(BF16)
    dec_t = _lane_tiled_t(g_end)
    v_pad = jnp.concatenate([v, jnp.zeros((LANES - rows, GLA_W), F32)], axis=0)
    row = _iota((rows, 1), 0)
    row_pad = _iota((LANES, 1), 0)
    for j in range(n_seq):
        s_bd = _gla_state_in(s0_ref[j], blk)
        o = o + _mm(jnp.where(row // seq_len == j, q_i, 0.0), s_bd)
        kv = _mm(k_et, jnp.where(row_pad // seq_len == j, v_pad, 0.0))
        first = j * seq_len
        sout_ref[j] = _gla_state_out(dec_t[:, first:first + 1] * s_bd + jnp.where(blk, kv, 0.0))
    o_ref[...] = _gla_out(o, gate, gn_ref[...])


def _gla_param_specs():
    return [_const_spec((LANES, GLA_K_W)), _const_spec((1, GLA_K_W)), _const_spec((1, GLA_DV))]


def _gla_seq(u3, s0, params, rows, n_seq):
    b, t, _ = u3.shape
    sdim = GLA_HEADS * GLA_DK
    state_spec = pl.BlockSpec((n_seq, sdim, GLA_DV), lambda i, j: (i, 0, 0))
    return pl.pallas_call(
        _gla_seq_kernel,
        grid=(b // n_seq, t // rows),
        in_specs=[pl.BlockSpec((n_seq, rows, GLA_COLS), lambda i, j: (i, j, 0)), state_spec]
        + _gla_param_specs(),
        out_specs=[pl.BlockSpec((n_seq, rows, GLA_W), lambda i, j: (i, j, 0)), state_spec],
        out_shape=[
            jax.ShapeDtypeStruct((b, t, GLA_W), F32),
            jax.ShapeDtypeStruct((b, sdim, GLA_DV), F32),
        ],
        scratch_shapes=[pltpu.VMEM((n_seq, sdim, GLA_W), F32)],
        compiler_params=pltpu.CompilerParams(
            dimension_semantics=("arbitrary", "arbitrary"), vmem_limit_bytes=VMEM_LIMIT),
        name="gla_seq",
    )(u3, s0, *params)


def _gla_dec(u2, s0, params, seq_len, n_seq):
    n = u2.shape[0]
    rows = n_seq * seq_len
    sdim = GLA_HEADS * GLA_DK
    state_spec = pl.BlockSpec((n_seq, sdim, GLA_DV), lambda i: (i, 0, 0))
    return pl.pallas_call(
        functools.partial(_gla_dec_kernel, seq_len=seq_len),
        grid=(n // rows,),
        in_specs=[pl.BlockSpec((rows, GLA_COLS), lambda i: (i, 0)), state_spec] + _gla_param_specs(),
        out_specs=[pl.BlockSpec((rows, GLA_W), lambda i: (i, 0)), state_spec],
        out_shape=[
            jax.ShapeDtypeStruct((n, GLA_W), F32),
            jax.ShapeDtypeStruct(s0.shape, F32),
        ],
        compiler_params=pltpu.CompilerParams(
            dimension_semantics=("arbitrary",), vmem_limit_bytes=VMEM_LIMIT),
        name="gla_dec",
    )(u2, s0, *params)


def _seg_sum(x, seg, pa=1):
    return jnp.concatenate(
        [_mm(x[:, i:i + SEG_W], seg, NN, pa, 1) for i in range(0, RWKV_W, SEG_W)], axis=1)


def _rwkv_tokens(u, prev, mu, w0, w2, a0, a2, g2, k_k, k_a, rk, seg):
    xr = u + mu * (prev - u)
    r = xr[:, 0:RWKV_W]
    kr = xr[:, RWKV_W:2 * RWKV_W]
    vr = xr[:, 2 * RWKV_W:3 * RWKV_W]
    wa = xr[:, 3 * RWKV_W:3 * RWKV_W + LANES]
    gd = xr[:, 3 * RWKV_W + LANES:RWKV_PROJ]
    w = _log_sigmoid(w0 + _mm(_tanh(wa), w2)) - 0.5
    lw = -jnp.exp(w)
    a_sig = _sigmoid(a0 + _mm(wa, a2))
    gate = _mm(_sigmoid(gd), g2)
    kk = kr * k_k
    kk = kk * lax.rsqrt(jnp.maximum(_seg_sum(kk * kk, seg), 1e-24))
    kr = kr * (1.0 + (a_sig - 1.0) * k_a)
    bonus = _seg_sum(r * kr * rk, seg) * vr
    return r, kr, vr, lw, -kk, kk * a_sig, gate, bonus


def _rwkv_prep(u, prev, mu, w0, w2, a0, a2, g2, k_k, k_a, rk, seg, sums):
    rows = u.shape[0]
    r, kr, vr, lw, a_vec, b_vec, gate, bonus = _rwkv_tokens(
        u, prev, mu, w0, w2, a0, a2, g2, k_k, k_a, rk, seg)
    block = sums.shape[1]
    cums = [_mm(sums, lw[i:i + block], NN, 1, 2) for i in range(0, rows, block)]
    cum = jnp.concatenate([x[:block] for x in cums], axis=0)
    cum_end = jnp.concatenate([x[block:] for x in cums], axis=0)
    e_neg = jnp.exp(-cum)
    e_end = jnp.exp(cum_end - cum)
    bf = lambda x: x.astype(BF16)
    return (bf(a_vec * jnp.exp(cum - lw)), bf(r * jnp.exp(cum)), bf(kr * e_neg), bf(b_vec * e_neg),
            bf(kr * e_end), bf(b_vec * e_end), bf(vr), gate, bonus, jnp.exp(cum_end))


def _rwkv_intra(units, m, hmasks, n_double):
    rows = units[0][0].shape[0]
    stack = lambda x: _stack_heads(x, hmasks)
    cat_w = m["strict"].shape[1]
    g = [_mm(jnp.concatenate([at, rt], axis=0), jnp.concatenate([stack(bt), stack(kt)], axis=0), NT)
         for at, rt, kt, bt, _ in units]
    a_ab = [jnp.where(m["strict"], x[:rows, :cat_w], 0.0) for x in g]
    a_rb = [jnp.where(m["incl"], x[rows:, :cat_w], 0.0) for x in g]
    a_ak = [jnp.where(m["strict"], x[:rows, cat_w:], 0.0) for x in g]
    a_rk = [jnp.where(m["incl"], x[rows:, cat_w:], 0.0) for x in g]
    tinv = [m["eye"] + a for a in a_ab]
    apow = [_mm(a, _block_diag(a, m["blk"])) for a in a_ab]
    for _ in range(n_double - 1):
        both = [_mm(jnp.concatenate([t, a], axis=0), _block_diag(a, m["blk"])) for t, a in zip(tinv, apow)]
        tinv = [t + x[:rows] for t, x in zip(tinv, both)]
        apow = [x[rows:] for x in both]
    tinv = [t + _mm(t, _block_diag(a, m["blk"])) for t, a in zip(tinv, apow)]
    v_s = [stack(vp) for _, _, _, _, vp in units]
    akv = [_mm(jnp.concatenate([a, b], axis=0), v) for a, b, v in zip(a_ak, a_rk, v_s)]
    wu = [_mm(t, jnp.concatenate([stack(at), stack(x[:rows])], axis=1))
          for t, (at, _, _, _, _), x in zip(tinv, units, akv)]
    z = [_mm(a, jnp.concatenate([stack(x[:, :GROUP_W]), stack(x[:, GROUP_W:])], axis=1))
         for a, x in zip(a_rb, wu)]
    return [(x[:, :GROUP_W], x[:, GROUP_W:], rt + zz[:, :GROUP_W], zz[:, GROUP_W:] + kv[rows:])
            for x, (_, rt, _, _, _), zz, kv in zip(wu, units, z, akv)]


def _rwkv_out(ys, gates, bonuses, lnw, lnb, seg):
    inv = 1.0 / RWKV_HEAD
    yc = [y - _seg_sum(y, seg, 2) * inv for y in ys]
    var = [_seg_sum(c * c, seg) * inv for c in yc]
    return [(c * lax.rsqrt(v + RWKV_GN_EPS) * lnw + lnb + bonus) * gate
            for c, v, bonus, gate in zip(yc, var, bonuses, gates)]


def _rwkv_state_in(s_ref, idx, g):
    zero = jnp.zeros((RWKV_HEAD, RWKV_HEAD), F32)
    blocks = []
    for h in range(GROUP_HEADS):
        parts = [zero] * GROUP_HEADS
        parts[h] = s_ref[idx, g * GROUP_HEADS + h]
        blocks.append(jnp.concatenate(parts, axis=1))
    return jnp.concatenate(blocks, axis=0)


def _rwkv_state_out(s_ref, idx, g, s2):
    for h in range(GROUP_HEADS):
        sl = slice(h * RWKV_HEAD, (h + 1) * RWKV_HEAD)
        s_ref[idx, g * GROUP_HEADS + h] = s2[sl, sl]


def _head_block_mask():
    return (_iota((GROUP_W, GROUP_W), 0) // RWKV_HEAD) == (_iota((GROUP_W, GROUP_W), 1) // RWKV_HEAD)


def _rwkv_seq_kernel(u_ref, shift0_ref, s0_ref, mu_ref, w0_ref, w2_ref, a0_ref, a2_ref,
                     g2_ref, kk_ref, ka_ref, rk_ref, lnw_ref, lnb_ref, seg_ref,
                     o_ref, sout_ref, s_scr, prev_scr):
    c = pl.program_id(1)
    n_seq, rows = u_ref.shape[0], u_ref.shape[1]
    groups = [slice(g * GROUP_W, (g + 1) * GROUP_W) for g in range(RWKV_GROUPS)]
    ids = [(b, g) for b in range(n_seq) for g in range(RWKV_GROUPS)]

    @pl.when(c == 0)
    def _():
        for b in range(n_seq):
            for g in range(RWKV_GROUPS):
                s_scr[b, g] = _rwkv_state_in(s0_ref, b, g)
            prev_scr[b] = shift0_ref[b]

    seg = seg_ref[...]
    m = _chunk_masks(rows, rows, GROUP_HEADS)
    hmasks = _head_masks(rows, RWKV_HEAD, GROUP_HEADS)
    head_blk = _head_block_mask()
    n_double = rows.bit_length() - 2

    us = [u_ref[b] for b in range(n_seq)]
    prev = jnp.concatenate([_shift_rows(u, prev_scr[b]) for b, u in enumerate(us)], axis=0)
    for b, u in enumerate(us):
        prev_scr[b] = u[rows - 1:rows]
    at, rt, kt, bt, ke, be, vb, gate, bonus, g_end = _rwkv_prep(
        jnp.concatenate(us, axis=0), prev, mu_ref[...], w0_ref[...], w2_ref[...], a0_ref[...],
        a2_ref[...], g2_ref[...], kk_ref[...], ka_ref[...], rk_ref[...], seg, m["sums"])
    tok, units = [], []
    for b in range(n_seq):
        sl = slice(b * rows, (b + 1) * rows)
        tok.append((vb[sl], gate[sl], bonus[sl], ke[sl], be[sl], g_end[b * rows:b * rows + 1]))
        units += [(at[sl, gl], rt[sl, gl], kt[sl, gl], bt[sl, gl], vb[sl, gl]) for gl in groups]

    intra = _rwkv_intra(units, m, hmasks, n_double)
    s_old = [s_scr[b, g] for b, g in ids]
    uy = [_mm(jnp.concatenate([w_m, r_m], axis=0), s2, NT) for (w_m, _, r_m, _), s2 in zip(intra, s_old)]
    upd = []
    for (b, g), (_, u0, _, _), x in zip(ids, intra, uy):
        vb, _, _, ke, be, _ = tok[b]
        gl = groups[g]
        upd.append(_mm_tn(jnp.concatenate([x[:rows] + u0, vb[:, gl].astype(F32)], axis=0),
                          jnp.concatenate([be[:, gl], ke[:, gl]], axis=0)))
    for (b, g), s2, d in zip(ids, s_old, upd):
        s_scr[b, g] = s2 * tok[b][5][:, groups[g]] + jnp.where(head_blk, d, 0.0)
    ys = [jnp.concatenate([uy[i][rows:] + intra[i][3] for i, (bb, _) in enumerate(ids) if bb == b],
                          axis=1) for b in range(n_seq)]
    out = _rwkv_out([jnp.concatenate(ys, axis=0)], [gate], [bonus], lnw_ref[...], lnb_ref[...], seg)[0]
    for b in range(n_seq):
        o_ref[b] = out[b * rows:(b + 1) * rows]

    @pl.when(c == pl.num_programs(1) - 1)
    def _():
        for b, g in ids:
            _rwkv_state_out(sout_ref, b, g, s_scr[b, g])


def _rwkv_dec_kernel(u_ref, shiftx_ref, s0_ref, mu_ref, w0_ref, w2_ref, a0_ref, a2_ref, g2_ref,
                     kk_ref, ka_ref, rk_ref, lnw_ref, lnb_ref, seg_ref,
                     o_ref, sout_ref, *, seq_len):
    rows = u_ref.shape[0]
    n_seq = rows // seq_len
    u = u_ref[...]
    row = _iota((rows, 1), 0)
    prev = jnp.where(row % seq_len == 0, shiftx_ref[...], pltpu.roll(u, 1, axis=0))
    seg = seg_ref[...]
    m = _chunk_masks(rows, seq_len, GROUP_HEADS)
    at, rt, kt, bt, ke, be, vb, gate, bonus, g_end = _rwkv_prep(
        u, prev, mu_ref[...], w0_ref[...], w2_ref[...], a0_ref[...], a2_ref[...], g2_ref[...],
        kk_ref[...], ka_ref[...], rk_ref[...], seg, m["sums"])
    hmasks = _head_masks(rows, RWKV_HEAD, GROUP_HEADS)
    n_double = seq_len.bit_length() - 2
    lane2 = _iota((1, 2 * rows), 1)
    u_half = lane2 < rows

    groups = [slice(g * GROUP_W, (g + 1) * GROUP_W) for g in range(RWKV_GROUPS)]
    intra = _rwkv_intra([tuple(x[:, gl] for x in (at, rt, kt, bt, vb)) for gl in groups],
                        m, hmasks, n_double)
    wr = [jnp.concatenate([w_m, r_m], axis=0).astype(BF16) for w_m, _, r_m, _ in intra]
    bk = [jnp.concatenate([be[:, gl], ke[:, gl]], axis=0) for gl in groups]
    base = [jnp.concatenate([x[1], vb[:, gl].astype(F32)], axis=0).T for x, gl in zip(intra, groups)]
    stacked = n_seq * GROUP_W
    seq_of_row = _iota((stacked, 1), 0) // GROUP_W
    mine = seq_of_row == (lane2 % rows) // seq_len
    blk_all = ((_iota((stacked, GROUP_W), 0) % GROUP_W) // RWKV_HEAD
               == _iota((stacked, GROUP_W), 1) // RWKV_HEAD)
    ys = []
    for g, gl in enumerate(groups):
        s_all = jnp.concatenate([_rwkv_state_in(s0_ref, j, g) for j in range(n_seq)], axis=0)
        uyt = _mm(s_all, wr[g], NT)
        lhs = jnp.where(mine, jnp.where(u_half, uyt, 0.0) + jnp.concatenate([base[g]] * n_seq, axis=0),
                        0.0)
        upd = _mm(lhs, bk[g])
        dec = jnp.concatenate(
            [jnp.broadcast_to(g_end[j * seq_len:j * seq_len + 1, gl], (GROUP_W, GROUP_W))
             for j in range(n_seq)], axis=0)
        s_new = s_all * dec + jnp.where(blk_all, upd, 0.0)
        for j in range(n_seq):
            _rwkv_state_out(sout_ref, j, g, s_new[j * GROUP_W:(j + 1) * GROUP_W])
        own = jnp.where(mine, uyt, 0.0)
        parts = [own[j * GROUP_W:(j + 1) * GROUP_W] for j in range(n_seq)]
        while len(parts) > 1:
            parts = [a + b for a, b in zip(parts[0::2], parts[1::2])]
        ys.append(parts[0].T[rows:] + intra[g][3])
    o_ref[...] = _rwkv_out([jnp.concatenate(ys, axis=1)], [gate], [bonus],
                           lnw_ref[...], lnb_ref[...], seg)[0]


STEP_HEADS = 2
STEP_ROWS = 2
N_STEP_OPERANDS = 6


def _sum_keys(x):
    t = x[0:SUBLANES]
    for i in range(SUBLANES, x.shape[0], SUBLANES):
        t = t + x[i:i + SUBLANES]
    for shift in (4, 2, 1):
        t = t + pltpu.roll(t, shift, axis=0)
    return t


def _rwkv_step_kernel(u_ref, shiftx_ref, s0_ref, mu_ref, w0_ref, w2_ref, a0_ref, a2_ref, g2_ref,
                      kk_ref, ka_ref, rk_ref, lnw_ref, lnb_ref, seg_ref,
                      o_ref, sout_ref, ops_scr, y_scr, tmp_scr, aux_scr, *, seq_len):
    p = pl.program_id(0)
    n_tok = u_ref.shape[0]
    n_seq = n_tok // seq_len
    seg = seg_ref[...]
    cols = [slice(c, c + LANES) for c in range(0, RWKV_W, LANES)]

    @pl.when(p == 0)
    def _():
        u = u_ref[...]
        row = _iota((n_tok, 1), 0)
        prev = jnp.where(row % seq_len == 0, shiftx_ref[...], pltpu.roll(u, 1, axis=0))
        r, kr, vr, lw, a_vec, b_vec, gate, bonus = _rwkv_tokens(
            u, prev, mu_ref[...], w0_ref[...], w2_ref[...], a0_ref[...], a2_ref[...], g2_ref[...],
            kk_ref[...], ka_ref[...], rk_ref[...], seg)
        aux_scr[0] = gate
        aux_scr[1] = bonus
        for i, x in enumerate((jnp.exp(lw), a_vec, b_vec, kr, r, vr)):
            for c, cl in enumerate(cols):
                tmp_scr[c] = x[:, cl]
            for t in range(seq_len):
                for c, cl in enumerate(cols):
                    ops_scr[i, t, cl, :] = tmp_scr[c, pl.ds(t, n_seq, stride=seq_len), :].T

    def rows_step(i, carry):
        for hh in range(STEP_HEADS):
            chan = pl.multiple_of((p * STEP_HEADS + hh) * RWKV_HEAD, RWKV_HEAD)
            for j in range(STEP_ROWS):
                v = i * STEP_ROWS + j
                s = s0_ref[hh, v]
                for t in range(seq_len):
                    w, a, b, k, r = (ops_scr[n, t, pl.ds(chan, RWKV_HEAD), :] for n in range(5))
                    v_t = ops_scr[5, t, pl.ds(chan + v, 1), :]
                    sa = jnp.concatenate([_sum_keys(s * a)] * (RWKV_HEAD // SUBLANES), axis=0)
                    s = s * w + sa * b + v_t * k
                    y_scr[t, pl.ds(chan + v, 1), :] = _sum_keys(s * r)[0:1]
                sout_ref[hh, v] = s
        return carry

    lax.fori_loop(0, RWKV_HEAD // STEP_ROWS, rows_step, 0)

    @pl.when(p == pl.num_programs(0) - 1)
    def _():
        for t in range(seq_len):
            for c, cl in enumerate(cols):
                tmp_scr[c, pl.ds(t, n_seq, stride=seq_len), :] = y_scr[t, cl, :].T
        y = jnp.concatenate([tmp_scr[c] for c in range(len(cols))], axis=1)
        o_ref[...] = _rwkv_out([y], [aux_scr[0]], [aux_scr[1]], lnw_ref[...], lnb_ref[...], seg)[0]


def _rwkv_step(u2, shiftx, s0_t, params, seq_len):
    n = u2.shape[0]
    n_seq = n // seq_len
    state_spec = pl.BlockSpec((STEP_HEADS, RWKV_HEAD, RWKV_HEAD, n_seq), lambda i: (i, 0, 0, 0))
    whole = lambda w: pl.BlockSpec((n, w), lambda i: (0, 0), pipeline_mode=pl.Buffered(1))
    return pl.pallas_call(
        functools.partial(_rwkv_step_kernel, seq_len=seq_len),
        grid=(RWKV_HEADS // STEP_HEADS,),
        in_specs=[whole(RWKV_PROJ), whole(RWKV_PROJ), state_spec] + _rwkv_param_specs(),
        out_specs=[pl.BlockSpec((n, RWKV_W), lambda i: (0, 0)), state_spec],
        out_shape=[
            jax.ShapeDtypeStruct((n, RWKV_W), F32),
            jax.ShapeDtypeStruct(s0_t.shape, F32),
        ],
        scratch_shapes=[
            pltpu.VMEM((N_STEP_OPERANDS, seq_len, RWKV_W, n_seq), F32),
            pltpu.VMEM((seq_len, RWKV_W, n_seq), F32),
            pltpu.VMEM((RWKV_W // LANES, n, LANES), F32),
            pltpu.VMEM((2, n, RWKV_W), F32),
        ],
        compiler_params=pltpu.CompilerParams(
            dimension_semantics=("arbitrary",), vmem_limit_bytes=VMEM_LIMIT),
        name="rwkv_step",
    )(u2, shiftx, s0_t, *params)


def _rwkv_param_specs():
    vec = lambda n: _const_spec((1, n))
    return [
        vec(RWKV_PROJ),
        vec(RWKV_W),
        _const_spec((LANES, RWKV_W)),
        vec(RWKV_W),
        _const_spec((LANES, RWKV_W)),
        _const_spec((LANES, RWKV_W)),
        vec(RWKV_W), vec(RWKV_W), vec(RWKV_W), vec(RWKV_W), vec(RWKV_W),
        _const_spec((SEG_W, SEG_W)),
    ]


def _rwkv_seq(u3, shift0, s0, params, rows, n_seq):
    b, t, _ = u3.shape
    state_spec = pl.BlockSpec((None, n_seq, RWKV_HEADS, RWKV_HEAD, RWKV_HEAD),
                              lambda i, j: (0, i, 0, 0, 0))
    return pl.pallas_call(
        _rwkv_seq_kernel,
        grid=(b // n_seq, t // rows),
        in_specs=[
            pl.BlockSpec((n_seq, rows, RWKV_PROJ), lambda i, j: (i, j, 0)),
            pl.BlockSpec((n_seq, 1, RWKV_PROJ), lambda i, j: (i, 0, 0)),
            state_spec,
        ] + _rwkv_param_specs(),
        out_specs=[pl.BlockSpec((n_seq, rows, RWKV_W), lambda i, j: (i, j, 0)), state_spec],
        out_shape=[
            jax.ShapeDtypeStruct((b, t, RWKV_W), F32),
            jax.ShapeDtypeStruct(s0.shape, F32),
        ],
        scratch_shapes=[
            pltpu.VMEM((n_seq, RWKV_GROUPS, GROUP_W, GROUP_W), F32),
            pltpu.VMEM((n_seq, 1, RWKV_PROJ), F32),
        ],
        compiler_params=pltpu.CompilerParams(
            dimension_semantics=("arbitrary", "arbitrary"), vmem_limit_bytes=VMEM_LIMIT),
        name="rwkv_seq",
    )(u3, shift0, s0, *params)


def _rwkv_dec(u2, shiftx, s0, params, seq_len, n_seq):
    n = u2.shape[0]
    rows = n_seq * seq_len
    state_spec = pl.BlockSpec((None, n_seq, RWKV_HEADS, RWKV_HEAD, RWKV_HEAD),
                              lambda i: (0, i, 0, 0, 0))
    return pl.pallas_call(
        functools.partial(_rwkv_dec_kernel, seq_len=seq_len),
        grid=(n // rows,),
        in_specs=[
            pl.BlockSpec((rows, RWKV_PROJ), lambda i: (i, 0)),
            pl.BlockSpec((rows, RWKV_PROJ), lambda i: (i, 0)),
            state_spec,
        ] + _rwkv_param_specs(),
        out_specs=[pl.BlockSpec((rows, RWKV_W), lambda i: (i, 0)), state_spec],
        out_shape=[
            jax.ShapeDtypeStruct((n, RWKV_W), F32),
            jax.ShapeDtypeStruct(s0.shape, F32),
        ],
        compiler_params=pltpu.CompilerParams(
            dimension_semantics=("arbitrary",), vmem_limit_bytes=VMEM_LIMIT),
        name="rwkv_dec",
    )(u2, shiftx, s0, *params)


FF_CHUNK = D_FF // 2


def _post_kernel(x_ref, og_ref, or_ref, p_ref, wo_ref, nffn_ref, wg_ref, wu_ref, wd_ref,
                 nple_ref, wpg_ref, wpp_ref, nf_ref, y_ref):
    half = x_ref.shape[0] // 2
    parts = [slice(0, half), slice(half, 2 * half)]
    dot = lambda a, b: jnp.dot(a, b, preferred_element_type=F32)
    o = [jnp.concatenate([og_ref[p, :], or_ref[p, :]], axis=1).astype(BF16) for p in parts]
    x = [x_ref[p, :] + dot(oo, wo_ref[...]) for p, oo in zip(parts, o)]
    h2 = [_rms(xx, nffn_ref[...]).astype(BF16) for xx in x]
    for i in range(0, D_FF, FF_CHUNK):
        gate = [dot(h, wg_ref[:, i:i + FF_CHUNK]) for h in h2]
        up = [dot(h, wu_ref[:, i:i + FF_CHUNK]) for h in h2]
        act = [(g * _sigmoid(g) * u).astype(BF16) for g, u in zip(gate, up)]
        x = [xx + dot(a, wd_ref[i:i + FF_CHUNK, :]) for xx, a in zip(x, act)]
    h3 = [_rms(xx, nple_ref[...]).astype(BF16) for xx in x]
    pg = [_sigmoid(dot(h, wpg_ref[...])) for h in h3]
    pp = [dot(p_ref[p, :].astype(BF16), wpp_ref[...]) for p in parts]
    for p, xx, a, b in zip(parts, x, pg, pp):
        y_ref[p, :] = _rms(xx + a * b, nf_ref[...])


def _post(x2d, og, orw, p2d, weights, tm):
    n = x2d.shape[0]
    wo, nffn, wg, wu, wd, nple, wpg, wpp, nf = weights
    tok = lambda w: pl.BlockSpec((tm, w), lambda i: (i, 0))
    return pl.pallas_call(
        _post_kernel,
        grid=(n // tm,),
        in_specs=[
            tok(D_MODEL), tok(GLA_W), tok(RWKV_W), tok(PLE_DIM),
            _const_spec((D_MODEL, D_MODEL)), _const_spec((1, D_MODEL)),
            _const_spec((D_MODEL, D_FF)), _const_spec((D_MODEL, D_FF)), _const_spec((D_FF, D_MODEL)),
            _const_spec((1, D_MODEL)), _const_spec((D_MODEL, D_MODEL)), _const_spec((PLE_DIM, D_MODEL)),
            _const_spec((1, D_MODEL)),
        ],
        out_specs=tok(D_MODEL),
        out_shape=jax.ShapeDtypeStruct((n, D_MODEL), F32),
        compiler_params=pltpu.CompilerParams(
            dimension_semantics=("arbitrary",), vmem_limit_bytes=VMEM_LIMIT),
        name="post",
    )(x2d, og, orw, p2d, wo, nffn, wg, wu, wd, nple, wpg, wpp, nf)


PROMPT_CHUNK = 64
PROMPT_SEQS_PER_STEP = 8
DEC_TILE_SEQS = 16
TOKEN_TILE = 512


def kernel(x_prompt, x_sample, state_gla, state_rwkv, state_shift, p_prompt, p_sample, norm_mix, w_in, gla_gk_up, gla_gk_bias, gla_norm, rwkv_mu, rwkv_w0, rwkv_w2, rwkv_a0, rwkv_a2, rwkv_g2, rwkv_k_k, rwkv_k_a, rwkv_r_k, rwkv_ln_w, rwkv_ln_b, w_out, norm_ffn, w_gate, w_up, w_down, norm_ple, w_ple_gate, w_ple_proj, norm_final):
    assert w_in.shape[0] == 1
    i = 0
    rowv = lambda a: a.astype(F32).reshape(1, -1)
    zeros = lambda r, c: jnp.zeros((r, c), F32)
    w_in_i = w_in[i]
    w_gla = jnp.concatenate(
        [w_in_i[:, :GLA_PROJ], zeros(D_MODEL, GLA_COLS - GLA_PROJ)], axis=1).astype(BF16)
    w_rw = w_in_i[:, GLA_PROJ:].astype(BF16)
    gk_up = jnp.concatenate(
        [gla_gk_up[i].astype(F32), zeros(LANES - GLA_GATE_RANK, GLA_K_W)], axis=0).astype(BF16)
    seg = jnp.arange(SEG_W)[:, None] // RWKV_HEAD == jnp.arange(SEG_W)[None, :] // RWKV_HEAD
    rwkv_params = (
        rowv(rwkv_mu[i]), rowv(rwkv_w0[i]),
        jnp.concatenate([rwkv_w2[i].astype(F32), zeros(64, RWKV_W)], axis=0).astype(BF16),
        rowv(rwkv_a0[i]),
        jnp.concatenate([zeros(64, RWKV_W), rwkv_a2[i].astype(F32)], axis=0).astype(BF16),
        rwkv_g2[i].astype(BF16),
        rowv(rwkv_k_k[i]), rowv(rwkv_k_a[i]), rowv(rwkv_r_k[i]), rowv(rwkv_ln_w[i]), rowv(rwkv_ln_b[i]),
        seg.astype(BF16),
    )
    post_w = (
        w_out[i].astype(BF16), rowv(norm_ffn[i]), w_gate[i].astype(BF16), w_up[i].astype(BF16),
        w_down[i].astype(BF16), rowv(norm_ple[i]), w_ple_gate[i].astype(BF16),
        w_ple_proj[i].astype(BF16), rowv(norm_final),
    )
    gla_w = (gk_up, rowv(gla_gk_bias[i]), rowv(gla_norm[i]))
    g_mix = rowv(norm_mix[i])

    bp, tp, _ = x_prompt.shape
    xp = x_prompt.astype(F32).reshape(bp * tp, D_MODEL)
    ug, ur = _proj(xp, g_mix, w_gla, w_rw, TOKEN_TILE)
    og, gla_p = _gla_seq(ug.reshape(bp, tp, GLA_COLS),
                         jnp.zeros((bp, GLA_HEADS * GLA_DK, GLA_DV), F32), gla_w,
                         PROMPT_CHUNK, PROMPT_SEQS_PER_STEP)
    ur3 = ur.reshape(bp, tp, RWKV_PROJ)
    orw, rwkv_p = _rwkv_seq(ur3, jnp.zeros((bp, 1, RWKV_PROJ), F32),
                            jnp.zeros((1, bp, RWKV_HEADS, RWKV_HEAD, RWKV_HEAD), F32), rwkv_params,
                            PROMPT_CHUNK, PROMPT_SEQS_PER_STEP)
    shift_p = ur3[:, tp - 1]
    yp = _post(xp, og.reshape(bp * tp, GLA_W), orw.reshape(bp * tp, RWKV_W),
               p_prompt[i].reshape(bp * tp, PLE_DIM), post_w, TOKEN_TILE)

    bs, ts, _ = x_sample.shape
    xs = x_sample.astype(F32).reshape(bs * ts, D_MODEL)
    ug, ur = _proj(xs, g_mix, w_gla, w_rw, TOKEN_TILE)
    og, gla_s = _gla_dec(ug, state_gla[i].astype(F32).reshape(bs, GLA_HEADS * GLA_DK, GLA_DV),
                         gla_w, ts, DEC_TILE_SEQS)
    shiftx = jnp.pad(state_shift[i].astype(F32)[:, None, :], ((0, 0), (0, ts - 1), (0, 0)))
    orw, rwkv_s = _rwkv_step(ur, shiftx.reshape(bs * ts, RWKV_PROJ),
                             jnp.transpose(state_rwkv[i].astype(F32), (1, 2, 3, 0)), rwkv_params, ts)
    rwkv_s = jnp.transpose(rwkv_s, (3, 0, 1, 2))[None]
    shift_s = ur.reshape(bs, ts, RWKV_PROJ)[:, ts - 1]
    ys = _post(xs, og, orw, p_sample[i].reshape(bs * ts, PLE_DIM), post_w, TOKEN_TILE)

    gla_shape = (1, -1, GLA_HEADS, GLA_DK, GLA_DV)
    return (yp.reshape(bp, tp, D_MODEL).astype(x_prompt.dtype),
            ys.reshape(bs, ts, D_MODEL).astype(x_sample.dtype),
            gla_p.reshape(gla_shape).astype(state_gla.dtype), rwkv_p.astype(state_rwkv.dtype),
            shift_p[None].astype(state_shift.dtype),
            gla_s.reshape(gla_shape).astype(state_gla.dtype), rwkv_s.astype(state_rwkv.dtype),
            shift_s[None].astype(state_shift.dtype))
```

```python
import functools

import jax
import jax.numpy as jnp
from jax import lax
from jax.experimental import pallas as pl
from jax.experimental.pallas import tpu as pltpu

F32 = jnp.float32
BF16 = jnp.bfloat16

D_MODEL = 1024
GLA_HEADS = 4
GLA_DK = 64
GLA_DV = 128
GLA_K_W = GLA_HEADS * GLA_DK
GLA_W = GLA_HEADS * GLA_DV
GLA_GATE_RANK = 16
GLA_GATE_NORM = 16.0
GLA_MAIN = 2 * GLA_K_W + 2 * GLA_W
GLA_PROJ = GLA_MAIN + GLA_GATE_RANK
LANES = 128
GLA_COLS = GLA_MAIN + LANES
RWKV_HEAD = 64
RWKV_HEADS = 8
RWKV_W = RWKV_HEADS * RWKV_HEAD
RWKV_PROJ = 3 * RWKV_W + 64 + 64 + 128
D_FF = 2816
PLE_DIM = 256
EPS = 1e-6
RWKV_GN_EPS = 64e-5

VMEM_LIMIT = 56 * 1024 * 1024

NN = ((1,), (0,))
NT = ((1,), (1,))


def _split(x, n):
    parts = []
    r = x
    for i in range(n):
        p = r.astype(BF16)
        parts.append(p)
        if i + 1 < n:
            r = r - p.astype(F32)
    return parts


def _mm(a, b, dims=NN, pa=1, pb=1):
    pieces_a = _split(a, pa)
    pieces_b = _split(b, pb)
    n = max(pa, pb)
    acc = None
    for i, ai in enumerate(pieces_a):
        for j, bj in enumerate(pieces_b):
            if i + j < n:
                t = lax.dot_general(ai, bj, (dims, ((), ())), preferred_element_type=F32)
                acc = t if acc is None else acc + t
    return acc


def _mm_tn(a, b, pa=1, pb=1):
    rows = a.shape[0]
    pad = (-rows) % LANES
    if pad:
        a = jnp.concatenate([a, jnp.zeros((pad, a.shape[1]), a.dtype)], axis=0)
        b = jnp.concatenate([b, jnp.zeros((pad, b.shape[1]), b.dtype)], axis=0)
    return _mm(a.T, b, NN, pa, pb)


def _iota(shape, dim):
    return lax.broadcasted_iota(jnp.int32, shape, dim)


def _log_sigmoid(z):
    return jnp.minimum(z, 0.0) - jnp.log(1.0 + jnp.exp(-jnp.abs(z)))


SUBLANES = 8


def _shift_rows(u, first):
    r = pltpu.roll(u, 1, axis=0)
    head = jnp.where(_iota((SUBLANES, 1), 0) == 0, first, r[:SUBLANES])
    return jnp.concatenate([head, r[SUBLANES:]], axis=0)


def _sigmoid(z):
    return 1.0 / (1.0 + jnp.exp(-z))


def _tanh(z):
    return 2.0 * _sigmoid(2.0 * z) - 1.0


def _rms(x, g):
    return x * lax.rsqrt(jnp.mean(x * x, axis=-1, keepdims=True) + EPS) * g


GROUP_HEADS = 2
GROUP_W = GROUP_HEADS * RWKV_HEAD
RWKV_GROUPS = RWKV_W // GROUP_W
SEG_W = 2 * LANES


def _head_masks(rows, head_w, heads):
    lane = _iota((rows, heads * head_w), 1)
    return [jnp.where(lane // head_w == h, 1.0, 0.0).astype(BF16) for h in range(heads)]


def _stack_heads(x, hmasks):
    xb = x.astype(BF16)
    return jnp.concatenate([xb * m for m in hmasks], axis=0)


def _block_diag(x_cat, blk):
    heads = blk.shape[0] // x_cat.shape[0]
    return jnp.concatenate([x_cat.astype(BF16)] * heads, axis=0) * blk


def _chunk_masks(rows, seq_len, heads):
    t = _iota((rows, heads * rows), 0)
    s = _iota((rows, heads * rows), 1) % rows
    strict, incl = t > s, t >= s
    t2 = _iota((2 * rows, rows), 0)
    s2 = _iota((2 * rows, rows), 1)
    cum_rows = (t2 < rows) & (t2 >= s2)
    tot_rows = t2 >= rows
    if seq_len < rows:
        same = (t // seq_len) == (s // seq_len)
        strict, incl = strict & same, incl & same
        same2 = ((t2 % rows) // seq_len) == (s2 // seq_len)
        cum_rows, tot_rows = cum_rows & same2, tot_rows & same2
    big = heads * rows
    blk = jnp.where(_iota((big, big), 0) // rows == _iota((big, big), 1) // rows, 1.0, 0.0)
    t4 = _iota((2 * rows, 2 * big), 0)
    s4 = _iota((2 * rows, 2 * big), 1) % rows
    quad = ((t4 % rows) > s4) | ((t4 >= rows) & ((t4 % rows) == s4))
    if seq_len < rows:
        quad = quad & (((t4 % rows) // seq_len) == (s4 // seq_len))
    return dict(strict=strict, incl=incl, quad=quad, eye=jnp.where(t == s, 1.0, 0.0),
                sums=jnp.where(cum_rows | tot_rows, 1.0, 0.0).astype(BF16), blk=blk.astype(BF16))


def _proj_kernel(x_ref, g_ref, wg_ref, wr_ref, ug_ref, ur_ref):
    h = _rms(x_ref[...], g_ref[...]).astype(BF16)
    ug_ref[...] = jnp.dot(h, wg_ref[...], preferred_element_type=F32)
    ur_ref[...] = jnp.dot(h, wr_ref[...], preferred_element_type=F32)


def _const_spec(shape):
    return pl.BlockSpec(shape, lambda *_: (0,) * len(shape), pipeline_mode=pl.Buffered(1))


def _proj(x2d, g, w_gla, w_rw, tm):
    n = x2d.shape[0]
    return pl.pallas_call(
        _proj_kernel,
        grid=(n // tm,),
        in_specs=[
            pl.BlockSpec((tm, D_MODEL), lambda i: (i, 0)),
            _const_spec((1, D_MODEL)),
            _const_spec((D_MODEL, GLA_COLS)),
            _const_spec((D_MODEL, RWKV_PROJ)),
        ],
        out_specs=[
            pl.BlockSpec((tm, GLA_COLS), lambda i: (i, 0)),
            pl.BlockSpec((tm, RWKV_PROJ), lambda i: (i, 0)),
        ],
        out_shape=[
            jax.ShapeDtypeStruct((n, GLA_COLS), F32),
            jax.ShapeDtypeStruct((n, RWKV_PROJ), F32),
        ],
        compiler_params=pltpu.CompilerParams(
            dimension_semantics=("arbitrary",), vmem_limit_bytes=VMEM_LIMIT),
        name="proj",
    )(x2d, g, w_gla, w_rw)


def _gla_prep(us, gkup, gkb, sums):
    rows = us[0].shape[0]
    z = [_mm(u[:, GLA_MAIN:GLA_COLS], gkup) for u in us]
    log_a = [_log_sigmoid(x + gkb) * (1.0 / GLA_GATE_NORM) for x in z]
    cums = [_mm(sums, x, NN, 1, 2) for x in log_a]
    out = []
    for u, x in zip(us, cums):
        cum, cum_end = x[:rows], x[rows:]
        q = u[:, 0:GLA_K_W] * (GLA_DK ** -0.5)
        k = u[:, GLA_K_W:2 * GLA_K_W]
        out.append((q * jnp.exp(cum), k * jnp.exp(-cum), k * jnp.exp(cum_end - cum),
                    u[:, 2 * GLA_K_W:2 * GLA_K_W + GLA_W], u[:, 2 * GLA_K_W + GLA_W:GLA_MAIN],
                    jnp.exp(cum_end)))
    return out


def _gla_intra(q_i, k_i, v, m, kmasks, vmasks):
    scores = jnp.where(m["incl"], _mm(q_i, _stack_heads(k_i, kmasks), NT), 0.0)
    return _mm(scores, _stack_heads(v, vmasks))


def _gla_out(o, gate, gnorm):
    heads = [slice(h * GLA_DV, (h + 1) * GLA_DV) for h in range(GLA_HEADS)]
    return jnp.concatenate(
        [_rms(o[:, hl], gnorm) * (gate[:, hl] * _sigmoid(gate[:, hl])) for hl in heads], axis=1)


def _gla_block_mask():
    return (_iota((GLA_K_W, GLA_W), 0) // GLA_DK) == (_iota((GLA_K_W, GLA_W), 1) // GLA_DV)


def _gla_state_in(s, blk):
    return jnp.where(blk, jnp.concatenate([s] * GLA_HEADS, axis=1), 0.0)


def _gla_state_out(s_bd):
    heads = [s_bd[:, h * GLA_DV:(h + 1) * GLA_DV] for h in range(GLA_HEADS)]
    return (heads[0] + heads[1]) + (heads[2] + heads[3])


def _lane_tiled_t(x):
    pad = LANES - x.shape[0]
    if pad:
        x = jnp.concatenate([x, jnp.zeros((pad, x.shape[1]), x.dtype)], axis=0)
    return x.T


def _gla_seq_kernel(u_ref, s0_ref, gkup_ref, gkb_ref, gn_ref, o_ref, sout_ref, s_scr):
    c = pl.program_id(1)
    n_seq, rows = u_ref.shape[0], u_ref.shape[1]
    blk = _gla_block_mask()

    @pl.when(c == 0)
    def _():
        for b in range(n_seq):
            s_scr[b] = _gla_state_in(s0_ref[b], blk)

    m = _chunk_masks(rows, rows, GLA_HEADS)
    kmasks = _head_masks(rows, GLA_DK, GLA_HEADS)
    vmasks = _head_masks(rows, GLA_DV, GLA_HEADS)
    tok = _gla_prep([u_ref[b] for b in range(n_seq)], gkup_ref[...], gkb_ref[...], m["sums"])
    s_old = [s_scr[b] for b in range(n_seq)]
    intra = [_gla_intra(q_i, k_i, v, m, kmasks, vmasks) for q_i, k_i, _, v, _, _ in tok]
    inter = [_mm(t[0], s) for t, s in zip(tok, s_old)]
    kv = [_mm_tn(k_e, v) for _, _, k_e, v, _, _ in tok]
    for b in range(n_seq):
        gate, g_end = tok[b][4], tok[b][5]
        o_ref[b] = _gla_out(intra[b] + inter[b], gate, gn_ref[...])
        dec = jnp.concatenate([_lane_tiled_t(jnp.broadcast_to(g_end[0:1], (LANES, GLA_K_W)))] * GLA_HEADS,
                              axis=1)
        s_scr[b] = dec * s_old[b] + jnp.where(blk, kv[b], 0.0)

    @pl.when(c == pl.num_programs(1) - 1)
    def _():
        for b in range(n_seq):
            sout_ref[b] = _gla_state_out(s_scr[b])


def _gla_dec_kernel(u_ref, s0_ref, gkup_ref, gkb_ref, gn_ref, o_ref, sout_ref, *, seq_len):
    rows = u_ref.shape[0]
    n_seq = rows // seq_len
    blk = _gla_block_mask()
    m = _chunk_masks(rows, seq_len, GLA_HEADS)
    q_i, k_i, k_e, v, gate, g_end = _gla_prep([u_ref[...]], gkup_ref[...], gkb_ref[...], m["sums"])[0]
    o = _gla_intra(q_i, k_i, v, m, _head_masks(rows, GLA_DK, GLA_HEADS),
                   _head_masks(rows, GLA_DV, GLA_HEADS))
    k_et = _lane_tiled_t(k_e).astype(BF16)
    dec_t = _lane_tiled_t(g_end)
    v_pad = v if rows == LANES else jnp.concatenate([v, jnp.zeros((LANES - rows, GLA_W), F32)], axis=0)
    row = _iota((rows, 1), 0)
    row_pad = _iota((LANES, 1), 0)
    for j in range(n_seq):
        s_bd = _gla_state_in(s0_ref[j], blk)
        o = o + _mm(jnp.where(row // seq_len == j, q_i, 0.0), s_bd)
        kv = _mm(k_et, jnp.where(row_pad // seq_len == j, v_pad, 0.0))
        first = j * seq_len
        sout_ref[j] = _gla_state_out(dec_t[:, first:first + 1] * s_bd + jnp.where(blk, kv, 0.0))
    o_ref[...] = _gla_out(o, gate, gn_ref[...])


def _gla_param_specs():
    return [_const_spec((LANES, GLA_K_W)), _const_spec((1, GLA_K_W)), _const_spec((1, GLA_DV))]


def _gla_seq(u3, s0, params, rows, n_seq):
    b, t, _ = u3.shape
    sdim = GLA_HEADS * GLA_DK
    state_spec = pl.BlockSpec((n_seq, sdim, GLA_DV), lambda i, j: (i, 0, 0))
    return pl.pallas_call(
        _gla_seq_kernel,
        grid=(b // n_seq, t // rows),
        in_specs=[pl.BlockSpec((n_seq, rows, GLA_COLS), lambda i, j: (i, j, 0)), state_spec]
        + _gla_param_specs(),
        out_specs=[pl.BlockSpec((n_seq, rows, GLA_W), lambda i, j: (i, j, 0)), state_spec],
        out_shape=[
            jax.ShapeDtypeStruct((b, t, GLA_W), F32),
            jax.ShapeDtypeStruct((b, sdim, GLA_DV), F32),
        ],
        scratch_shapes=[pltpu.VMEM((n_seq, sdim, GLA_W), F32)],
        compiler_params=pltpu.CompilerParams(
            dimension_semantics=("arbitrary", "arbitrary"), vmem_limit_bytes=VMEM_LIMIT),
        name="gla_seq",
    )(u3, s0, *params)


def _gla_dec(u2, s0, params, seq_len, n_seq):
    n = u2.shape[0]
    rows = n_seq * seq_len
    sdim = GLA_HEADS * GLA_DK
    state_spec = pl.BlockSpec((n_seq, sdim, GLA_DV), lambda i: (i, 0, 0))
    return pl.pallas_call(
        functools.partial(_gla_dec_kernel, seq_len=seq_len),
        grid=(n // rows,),
        in_specs=[pl.BlockSpec((rows, GLA_COLS), lambda i: (i, 0)), state_spec] + _gla_param_specs(),
        out_specs=[pl.BlockSpec((rows, GLA_W), lambda i: (i, 0)), state_spec],
        out_shape=[
            jax.ShapeDtypeStruct((n, GLA_W), F32),
            jax.ShapeDtypeStruct(s0.shape, F32),
        ],
        compiler_params=pltpu.CompilerParams(
            dimension_semantics=("arbitrary",), vmem_limit_bytes=VMEM_LIMIT),
        name="gla_dec",
    )(u2, s0, *params)


def _seg_sum(x, seg, pa=1):
    return jnp.concatenate(
        [_mm(x[:, i:i + SEG_W], seg, NN, pa, 1) for i in range(0, RWKV_W, SEG_W)], axis=1)


W_OFFSET_SCALE = 0.6065306597126334


def _rwkv_tokens(u, prev, mu, w0, w2, a0, a2, g2, k_k, k_a, rk, seg):
    xr = u + mu * (prev - u)
    r = xr[:, 0:RWKV_W]
    kr = xr[:, RWKV_W:2 * RWKV_W]
    vr = xr[:, 2 * RWKV_W:3 * RWKV_W]
    wa = xr[:, 3 * RWKV_W:3 * RWKV_W + LANES]
    gd = xr[:, 3 * RWKV_W + LANES:RWKV_PROJ]
    lw = _sigmoid(w0 + _mm(_tanh(wa), w2)) * (-W_OFFSET_SCALE)
    a_sig = _sigmoid(a0 + _mm(wa, a2))
    gate = _mm(_sigmoid(gd), g2)
    kk = kr * k_k
    kk = kk * lax.rsqrt(jnp.maximum(_seg_sum(kk * kk, seg), 1e-24))
    kr = kr * (1.0 + (a_sig - 1.0) * k_a)
    bonus = _seg_sum(r * kr * rk, seg) * vr
    return r, kr, vr, lw, -kk, kk * a_sig, gate, bonus


def _rwkv_prep(u, prev, mu, w0, w2, a0, a2, g2, k_k, k_a, rk, seg, sums):
    rows = u.shape[0]
    r, kr, vr, lw, a_vec, b_vec, gate, bonus = _rwkv_tokens(
        u, prev, mu, w0, w2, a0, a2, g2, k_k, k_a, rk, seg)
    block = sums.shape[1]
    cums = [_mm(sums, lw[i:i + block], NN, 1, 2) for i in range(0, rows, block)]
    cum = jnp.concatenate([x[:block] for x in cums], axis=0)
    cum_end = jnp.concatenate([x[block:] for x in cums], axis=0)
    e_neg = jnp.exp(-cum)
    g_end = jnp.exp(cum_end)
    e_end = g_end * e_neg
    bf = lambda x: x.astype(BF16)
    return (bf(a_vec * jnp.exp(cum - lw)), bf(r * jnp.exp(cum)), bf(kr * e_neg), bf(b_vec * e_neg),
            bf(kr * e_end), bf(b_vec * e_end), bf(vr), gate, bonus, g_end)


def _rwkv_intra(units, m, hmasks, n_double):
    rows = units[0][0].shape[0]
    stack = lambda x: _stack_heads(x, hmasks)
    cat_w = m["strict"].shape[1]
    g = [_mm(jnp.concatenate([at, rt], axis=0), jnp.concatenate([stack(bt), stack(kt)], axis=0), NT)
         for at, rt, kt, bt, _ in units]
    g = [jnp.where(m["quad"], x, 0.0) for x in g]
    a_ab = [x[:rows, :cat_w] for x in g]
    a_rb = [x[rows:, :cat_w] for x in g]
    a_ak = [x[:rows, cat_w:] for x in g]
    a_rk = [x[rows:, cat_w:] for x in g]
    tinv = [m["eye"] + a for a in a_ab]
    apow = [_mm(a, _block_diag(a, m["blk"])) for a in a_ab]
    for _ in range(n_double - 1):
        both = [_mm(jnp.concatenate([t, a], axis=0), _block_diag(a, m["blk"])) for t, a in zip(tinv, apow)]
        tinv = [t + x[:rows] for t, x in zip(tinv, both)]
        apow = [x[rows:] for x in both]
    tinv = [t + _mm(t, _block_diag(a, m["blk"])) for t, a in zip(tinv, apow)]
    v_s = [stack(vp) for _, _, _, _, vp in units]
    akv = [_mm(jnp.concatenate([a, b], axis=0), v) for a, b, v in zip(a_ak, a_rk, v_s)]
    wu = [_mm(t, jnp.concatenate([stack(at), stack(x[:rows])], axis=1))
          for t, (at, _, _, _, _), x in zip(tinv, units, akv)]
    z = [_mm(a, jnp.concatenate([stack(x[:, :GROUP_W]), stack(x[:, GROUP_W:])], axis=1))
         for a, x in zip(a_rb, wu)]
    return [(x[:, :GROUP_W], x[:, GROUP_W:], rt + zz[:, :GROUP_W], zz[:, GROUP_W:] + kv[rows:])
            for x, (_, rt, _, _, _), zz, kv in zip(wu, units, z, akv)]


def _rwkv_out(ys, gates, bonuses, lnw, lnb, seg):
    inv = 1.0 / RWKV_HEAD
    yc = [y - _seg_sum(y, seg, 2) * inv for y in ys]
    var = [_seg_sum(c * c, seg) * inv for c in yc]
    return [(c * lax.rsqrt(v + RWKV_GN_EPS) * lnw + lnb + bonus) * gate
            for c, v, bonus, gate in zip(yc, var, bonuses, gates)]


def _rwkv_state_in(s_ref, idx, g):
    zero = jnp.zeros((RWKV_HEAD, RWKV_HEAD), F32)
    blocks = []
    for h in range(GROUP_HEADS):
        parts = [zero] * GROUP_HEADS
        parts[h] = s_ref[idx, g * GROUP_HEADS + h]
        blocks.append(jnp.concatenate(parts, axis=1))
    return jnp.concatenate(blocks, axis=0)


def _rwkv_state_out(s_ref, idx, g, s2):
    for h in range(GROUP_HEADS):
        sl = slice(h * RWKV_HEAD, (h + 1) * RWKV_HEAD)
        s_ref[idx, g * GROUP_HEADS + h] = s2[sl, sl]


def _head_block_mask():
    return (_iota((GROUP_W, GROUP_W), 0) // RWKV_HEAD) == (_iota((GROUP_W, GROUP_W), 1) // RWKV_HEAD)


def _rwkv_seq_kernel(u_ref, shift0_ref, s0_ref, mu_ref, w0_ref, w2_ref, a0_ref, a2_ref,
                     g2_ref, kk_ref, ka_ref, rk_ref, lnw_ref, lnb_ref, seg_ref,
                     o_ref, sout_ref, s_scr, prev_scr):
    c = pl.program_id(1)
    n_seq, rows = u_ref.shape[0], u_ref.shape[1]
    groups = [slice(g * GROUP_W, (g + 1) * GROUP_W) for g in range(RWKV_GROUPS)]
    ids = [(b, g) for b in range(n_seq) for g in range(RWKV_GROUPS)]

    @pl.when(c == 0)
    def _():
        for b in range(n_seq):
            for g in range(RWKV_GROUPS):
                s_scr[b, g] = _rwkv_state_in(s0_ref, b, g)
            prev_scr[b] = shift0_ref[b]

    seg = seg_ref[...]
    m = _chunk_masks(rows, rows, GROUP_HEADS)
    hmasks = _head_masks(rows, RWKV_HEAD, GROUP_HEADS)
    head_blk = _head_block_mask()
    n_double = rows.bit_length() - 2

    us = [u_ref[b] for b in range(n_seq)]
    prev = jnp.concatenate([_shift_rows(u, prev_scr[b]) for b, u in enumerate(us)], axis=0)
    for b, u in enumerate(us):
        prev_scr[b] = u[rows - 1:rows]
    at, rt, kt, bt, ke, be, vb, gate, bonus, g_end = _rwkv_prep(
        jnp.concatenate(us, axis=0), prev, mu_ref[...], w0_ref[...], w2_ref[...], a0_ref[...],
        a2_ref[...], g2_ref[...], kk_ref[...], ka_ref[...], rk_ref[...], seg, m["sums"])
    tok, units = [], []
    for b in range(n_seq):
        sl = slice(b * rows, (b + 1) * rows)
        tok.append((vb[sl], ke[sl], be[sl], g_end[b * rows:b * rows + 1]))
        units += [(at[sl, gl], rt[sl, gl], kt[sl, gl], bt[sl, gl], vb[sl, gl]) for gl in groups]

    intra = _rwkv_intra(units, m, hmasks, n_double)
    s_old = [s_scr[b, g] for b, g in ids]
    uy = [_mm(jnp.concatenate([w_m, r_m], axis=0), s2, NT) for (w_m, _, r_m, _), s2 in zip(intra, s_old)]
    upd = []
    for (b, g), (_, u0, _, _), x in zip(ids, intra, uy):
        vb_b, ke_b, be_b, _ = tok[b]
        gl = groups[g]
        upd.append(_mm_tn(jnp.concatenate([x[:rows] + u0, vb_b[:, gl].astype(F32)], axis=0),
                          jnp.concatenate([be_b[:, gl], ke_b[:, gl]], axis=0)))
    for (b, g), s2, d in zip(ids, s_old, upd):
        s_scr[b, g] = s2 * tok[b][3][:, groups[g]] + jnp.where(head_blk, d, 0.0)
    ys = [jnp.concatenate([uy[i][rows:] + intra[i][3] for i, (bb, _) in enumerate(ids) if bb == b],
                          axis=1) for b in range(n_seq)]
    out = _rwkv_out([jnp.concatenate(ys, axis=0)], [gate], [bonus], lnw_ref[...], lnb_ref[...], seg)[0]
    for b in range(n_seq):
        o_ref[b] = out[b * rows:(b + 1) * rows]

    @pl.when(c == pl.num_programs(1) - 1)
    def _():
        for b, g in ids:
            _rwkv_state_out(sout_ref, b, g, s_scr[b, g])


STEP_HEADS = 2
STEP_ROWS = 4
N_STEP_OPERANDS = 6


def _sum_keys(x):
    t = x[0:SUBLANES]
    for i in range(SUBLANES, x.shape[0], SUBLANES):
        t = t + x[i:i + SUBLANES]
    for shift in (4, 2, 1):
        t = t + pltpu.roll(t, shift, axis=0)
    return t


def _rwkv_step_kernel(u_ref, shiftx_ref, s0_ref, mu_ref, w0_ref, w2_ref, a0_ref, a2_ref, g2_ref,
                      kk_ref, ka_ref, rk_ref, lnw_ref, lnb_ref, seg_ref,
                      o_ref, sout_ref, ops_scr, y_scr, tmp_scr, aux_scr, *, seq_len):
    p = pl.program_id(0)
    n_tok = u_ref.shape[0]
    n_seq = n_tok // seq_len
    seg = seg_ref[...]
    cols = [slice(c, c + LANES) for c in range(0, RWKV_W, LANES)]

    @pl.when(p == 0)
    def _():
        u = u_ref[...]
        row = _iota((n_tok, 1), 0)
        prev = jnp.where(row % seq_len == 0, shiftx_ref[...], pltpu.roll(u, 1, axis=0))
        r, kr, vr, lw, a_vec, b_vec, gate, bonus = _rwkv_tokens(
            u, prev, mu_ref[...], w0_ref[...], w2_ref[...], a0_ref[...], a2_ref[...], g2_ref[...],
            kk_ref[...], ka_ref[...], rk_ref[...], seg)
        aux_scr[0] = gate
        aux_scr[1] = bonus
        for i, x in enumerate((jnp.exp(lw), a_vec, b_vec, kr, r, vr)):
            for c, cl in enumerate(cols):
                tmp_scr[c] = x[:, cl]
            for t in range(seq_len):
                for c, cl in enumerate(cols):
                    ops_scr[i, t, cl, :] = tmp_scr[c, pl.ds(t, n_seq, stride=seq_len), :].T

    def rows_step(i, carry):
        for hh in range(STEP_HEADS):
            chan = pl.multiple_of((p * STEP_HEADS + hh) * RWKV_HEAD, RWKV_HEAD)
            for j in range(STEP_ROWS):
                v = i * STEP_ROWS + j
                s = s0_ref[hh, v]
                for t in range(seq_len):
                    w, a, b, k, r = (ops_scr[n, t, pl.ds(chan, RWKV_HEAD), :] for n in range(5))
                    v_t = ops_scr[5, t, pl.ds(chan + v, 1), :]
                    sa = jnp.concatenate([_sum_keys(s * a)] * (RWKV_HEAD // SUBLANES), axis=0)
                    s = s * w + sa * b + v_t * k
                    y_scr[t, pl.ds(chan + v, 1), :] = _sum_keys(s * r)[0:1]
                sout_ref[hh, v] = s
        return carry

    lax.fori_loop(0, RWKV_HEAD // STEP_ROWS, rows_step, 0)

    @pl.when(p == pl.num_programs(0) - 1)
    def _():
        for t in range(seq_len):
            for c, cl in enumerate(cols):
                tmp_scr[c, pl.ds(t, n_seq, stride=seq_len), :] = y_scr[t, cl, :].T
        y = jnp.concatenate([tmp_scr[c] for c in range(len(cols))], axis=1)
        o_ref[...] = _rwkv_out([y], [aux_scr[0]], [aux_scr[1]], lnw_ref[...], lnb_ref[...], seg)[0]


def _rwkv_step(u2, shiftx, s0_t, params, seq_len):
    n = u2.shape[0]
    n_seq = n // seq_len
    state_spec = pl.BlockSpec((STEP_HEADS, RWKV_HEAD, RWKV_HEAD, n_seq), lambda i: (i, 0, 0, 0))
    whole = lambda w: pl.BlockSpec((n, w), lambda i: (0, 0), pipeline_mode=pl.Buffered(1))
    return pl.pallas_call(
        functools.partial(_rwkv_step_kernel, seq_len=seq_len),
        grid=(RWKV_HEADS // STEP_HEADS,),
        in_specs=[whole(RWKV_PROJ), whole(RWKV_PROJ), state_spec] + _rwkv_param_specs(),
        out_specs=[pl.BlockSpec((n, RWKV_W), lambda i: (0, 0)), state_spec],
        out_shape=[
            jax.ShapeDtypeStruct((n, RWKV_W), F32),
            jax.ShapeDtypeStruct(s0_t.shape, F32),
        ],
        scratch_shapes=[
            pltpu.VMEM((N_STEP_OPERANDS, seq_len, RWKV_W, n_seq), F32),
            pltpu.VMEM((seq_len, RWKV_W, n_seq), F32),
            pltpu.VMEM((RWKV_W // LANES, n, LANES), F32),
            pltpu.VMEM((2, n, RWKV_W), F32),
        ],
        compiler_params=pltpu.CompilerParams(
            dimension_semantics=("arbitrary",), vmem_limit_bytes=VMEM_LIMIT),
        name="rwkv_step",
    )(u2, shiftx, s0_t, *params)


def _rwkv_param_specs():
    vec = lambda n: _const_spec((1, n))
    return [
        vec(RWKV_PROJ),
        vec(RWKV_W),
        _const_spec((LANES, RWKV_W)),
        vec(RWKV_W),
        _const_spec((LANES, RWKV_W)),
        _const_spec((LANES, RWKV_W)),
        vec(RWKV_W), vec(RWKV_W), vec(RWKV_W), vec(RWKV_W), vec(RWKV_W),
        _const_spec((SEG_W, SEG_W)),
    ]


def _rwkv_seq(u3, shift0, s0, params, rows, n_seq):
    b, t, _ = u3.shape
    state_spec = pl.BlockSpec((None, n_seq, RWKV_HEADS, RWKV_HEAD, RWKV_HEAD),
                              lambda i, j: (0, i, 0, 0, 0))
    return pl.pallas_call(
        _rwkv_seq_kernel,
        grid=(b // n_seq, t // rows),
        in_specs=[
            pl.BlockSpec((n_seq, rows, RWKV_PROJ), lambda i, j: (i, j, 0)),
            pl.BlockSpec((n_seq, 1, RWKV_PROJ), lambda i, j: (i, 0, 0)),
            state_spec,
        ] + _rwkv_param_specs(),
        out_specs=[pl.BlockSpec((n_seq, rows, RWKV_W), lambda i, j: (i, j, 0)), state_spec],
        out_shape=[
            jax.ShapeDtypeStruct((b, t, RWKV_W), F32),
            jax.ShapeDtypeStruct(s0.shape, F32),
        ],
        scratch_shapes=[
            pltpu.VMEM((n_seq, RWKV_GROUPS, GROUP_W, GROUP_W), F32),
            pltpu.VMEM((n_seq, 1, RWKV_PROJ), F32),
        ],
        compiler_params=pltpu.CompilerParams(
            dimension_semantics=("arbitrary", "arbitrary"), vmem_limit_bytes=VMEM_LIMIT),
        name="rwkv_seq",
    )(u3, shift0, s0, *params)


FF_CHUNK = D_FF // 2


def _post_kernel(x_ref, og_ref, or_ref, p_ref, wo_ref, nffn_ref, wg_ref, wu_ref, wd_ref,
                 nple_ref, wpg_ref, wpp_ref, nf_ref, y_ref):
    half = x_ref.shape[0] // 2
    parts = [slice(0, half), slice(half, 2 * half)]
    dot = lambda a, b: jnp.dot(a, b, preferred_element_type=F32)
    o = [jnp.concatenate([og_ref[p, :], or_ref[p, :]], axis=1).astype(BF16) for p in parts]
    x = [x_ref[p, :] + dot(oo, wo_ref[...]) for p, oo in zip(parts, o)]
    h2 = [_rms(xx, nffn_ref[...]).astype(BF16) for xx in x]
    for i in range(0, D_FF, FF_CHUNK):
        gate = [dot(h, wg_ref[:, i:i + FF_CHUNK]) for h in h2]
        up = [dot(h, wu_ref[:, i:i + FF_CHUNK]) for h in h2]
        act = [(g * _sigmoid(g) * u).astype(BF16) for g, u in zip(gate, up)]
        x = [xx + dot(a, wd_ref[i:i + FF_CHUNK, :]) for xx, a in zip(x, act)]
    h3 = [_rms(xx, nple_ref[...]).astype(BF16) for xx in x]
    pg = [_sigmoid(dot(h, wpg_ref[...])) for h in h3]
    pp = [dot(p_ref[p, :].astype(BF16), wpp_ref[...]) for p in parts]
    for p, xx, a, b in zip(parts, x, pg, pp):
        y_ref[p, :] = _rms(xx + a * b, nf_ref[...])


def _post(x2d, og, orw, p2d, weights, tm):
    n = x2d.shape[0]
    wo, nffn, wg, wu, wd, nple, wpg, wpp, nf = weights
    tok = lambda w: pl.BlockSpec((tm, w), lambda i: (i, 0))
    return pl.pallas_call(
        _post_kernel,
        grid=(n // tm,),
        in_specs=[
            tok(D_MODEL), tok(GLA_W), tok(RWKV_W), tok(PLE_DIM),
            _const_spec((D_MODEL, D_MODEL)), _const_spec((1, D_MODEL)),
            _const_spec((D_MODEL, D_FF)), _const_spec((D_MODEL, D_FF)), _const_spec((D_FF, D_MODEL)),
            _const_spec((1, D_MODEL)), _const_spec((D_MODEL, D_MODEL)), _const_spec((PLE_DIM, D_MODEL)),
            _const_spec((1, D_MODEL)),
        ],
        out_specs=tok(D_MODEL),
        out_shape=jax.ShapeDtypeStruct((n, D_MODEL), F32),
        compiler_params=pltpu.CompilerParams(
            dimension_semantics=("arbitrary",), vmem_limit_bytes=VMEM_LIMIT),
        name="post",
    )(x2d, og, orw, p2d, wo, nffn, wg, wu, wd, nple, wpg, wpp, nf)


PROMPT_CHUNK = 64
PROMPT_SEQS_PER_STEP = 8
DEC_TILE_SEQS = 32
TOKEN_TILE = 512
PROJ_TILE = 1024


def kernel(x_prompt, x_sample, state_gla, state_rwkv, state_shift, p_prompt, p_sample, norm_mix, w_in, gla_gk_up, gla_gk_bias, gla_norm, rwkv_mu, rwkv_w0, rwkv_w2, rwkv_a0, rwkv_a2, rwkv_g2, rwkv_k_k, rwkv_k_a, rwkv_r_k, rwkv_ln_w, rwkv_ln_b, w_out, norm_ffn, w_gate, w_up, w_down, norm_ple, w_ple_gate, w_ple_proj, norm_final):
    assert w_in.shape[0] == 1
    i = 0
    rowv = lambda a: a.astype(F32).reshape(1, -1)
    zeros = lambda r, c: jnp.zeros((r, c), F32)
    w_in_i = w_in[i]
    w_gla = jnp.concatenate(
        [w_in_i[:, :GLA_PROJ], zeros(D_MODEL, GLA_COLS - GLA_PROJ)], axis=1).astype(BF16)
    w_rw = w_in_i[:, GLA_PROJ:].astype(BF16)
    gk_up = jnp.concatenate(
        [gla_gk_up[i].astype(F32), zeros(LANES - GLA_GATE_RANK, GLA_K_W)], axis=0).astype(BF16)
    seg = jnp.arange(SEG_W)[:, None] // RWKV_HEAD == jnp.arange(SEG_W)[None, :] // RWKV_HEAD
    rwkv_params = (
        rowv(rwkv_mu[i]), rowv(rwkv_w0[i]),
        jnp.concatenate([rwkv_w2[i].astype(F32), zeros(64, RWKV_W)], axis=0).astype(BF16),
        rowv(rwkv_a0[i]),
        jnp.concatenate([zeros(64, RWKV_W), rwkv_a2[i].astype(F32)], axis=0).astype(BF16),
        rwkv_g2[i].astype(BF16),
        rowv(rwkv_k_k[i]), rowv(rwkv_k_a[i]), rowv(rwkv_r_k[i]), rowv(rwkv_ln_w[i]), rowv(rwkv_ln_b[i]),
        seg.astype(BF16),
    )
    post_w = (
        w_out[i].astype(BF16), rowv(norm_ffn[i]), w_gate[i].astype(BF16), w_up[i].astype(BF16),
        w_down[i].astype(BF16), rowv(norm_ple[i]), w_ple_gate[i].astype(BF16),
        w_ple_proj[i].astype(BF16), rowv(norm_final),
    )
    gla_w = (gk_up, rowv(gla_gk_bias[i]), rowv(gla_norm[i]))
    g_mix = rowv(norm_mix[i])

    bp, tp, _ = x_prompt.shape
    xp = x_prompt.astype(F32).reshape(bp * tp, D_MODEL)
    ug, ur = _proj(xp, g_mix, w_gla, w_rw, PROJ_TILE)
    og, gla_p = _gla_seq(ug.reshape(bp, tp, GLA_COLS),
                         jnp.zeros((bp, GLA_HEADS * GLA_DK, GLA_DV), F32), gla_w,
                         PROMPT_CHUNK, PROMPT_SEQS_PER_STEP)
    ur3 = ur.reshape(bp, tp, RWKV_PROJ)
    orw, rwkv_p = _rwkv_seq(ur3, jnp.zeros((bp, 1, RWKV_PROJ), F32),
                            jnp.zeros((1, bp, RWKV_HEADS, RWKV_HEAD, RWKV_HEAD), F32), rwkv_params,
                            PROMPT_CHUNK, PROMPT_SEQS_PER_STEP)
    shift_p = ur3[:, tp - 1]
    yp = _post(xp, og.reshape(bp * tp, GLA_W), orw.reshape(bp * tp, RWKV_W),
               p_prompt[i].reshape(bp * tp, PLE_DIM), post_w, TOKEN_TILE)

    bs, ts, _ = x_sample.shape
    xs = x_sample.astype(F32).reshape(bs * ts, D_MODEL)
    ug, ur = _proj(xs, g_mix, w_gla, w_rw, TOKEN_TILE)
    og, gla_s = _gla_dec(ug, state_gla[i].astype(F32).reshape(bs, GLA_HEADS * GLA_DK, GLA_DV),
                         gla_w, ts, DEC_TILE_SEQS)
    shiftx = jnp.pad(state_shift[i].astype(F32)[:, None, :], ((0, 0), (0, ts - 1), (0, 0)))
    orw, rwkv_s = _rwkv_step(ur, shiftx.reshape(bs * ts, RWKV_PROJ),
                             jnp.transpose(state_rwkv[i].astype(F32), (1, 2, 3, 0)), rwkv_params, ts)
    rwkv_s = jnp.transpose(rwkv_s, (3, 0, 1, 2))[None]
    shift_s = ur.reshape(bs, ts, RWKV_PROJ)[:, ts - 1]
    ys = _post(xs, og, orw, p_sample[i].reshape(bs * ts, PLE_DIM), post_w, TOKEN_TILE)

    gla_shape = (1, -1, GLA_HEADS, GLA_DK, GLA_DV)
    return (yp.reshape(bp, tp, D_MODEL).astype(x_prompt.dtype),
            ys.reshape(bs, ts, D_MODEL).astype(x_sample.dtype),
            gla_p.reshape(gla_shape).astype(state_gla.dtype), rwkv_p.astype(state_rwkv.dtype),
            shift_p[None].astype(state_shift.dtype),
            gla_s.reshape(gla_shape).astype(state_gla.dtype), rwkv_s.astype(state_rwkv.dtype),
            shift_s[None].astype(state_shift.dtype))
```

```python
import functools

import jax
import jax.numpy as jnp
from jax import lax
from jax.experimental import pallas as pl
from jax.experimental.pallas import tpu as pltpu

F32 = jnp.float32
BF16 = jnp.bfloat16

D_MODEL = 1024
GLA_HEADS = 4
GLA_DK = 64
GLA_DV = 128
GLA_K_W = GLA_HEADS * GLA_DK
GLA_W = GLA_HEADS * GLA_DV
GLA_GATE_RANK = 16
GLA_GATE_NORM = 16.0
GLA_MAIN = 2 * GLA_K_W + 2 * GLA_W
GLA_PROJ = GLA_MAIN + GLA_GATE_RANK
LANES = 128
GLA_COLS = GLA_MAIN + LANES
RWKV_HEAD = 64
RWKV_HEADS = 8
RWKV_W = RWKV_HEADS * RWKV_HEAD
RWKV_PROJ = 3 * RWKV_W + 64 + 64 + 128
D_FF = 2816
PLE_DIM = 256
EPS = 1e-6
RWKV_GN_EPS = 64e-5

VMEM_LIMIT = 56 * 1024 * 1024

NN = ((1,), (0,))
NT = ((1,), (1,))


def _split(x, n):
    parts = []
    r = x
    for i in range(n):
        p = r.astype(BF16)
        parts.append(p)
        if i + 1 < n:
            r = r - p.astype(F32)
    return parts


def _mm(a, b, dims=NN, pa=1, pb=1):
    pieces_a = _split(a, pa)
    pieces_b = _split(b, pb)
    n = max(pa, pb)
    acc = None
    for i, ai in enumerate(pieces_a):
        for j, bj in enumerate(pieces_b):
            if i + j < n:
                t = lax.dot_general(ai, bj, (dims, ((), ())), preferred_element_type=F32)
                acc = t if acc is None else acc + t
    return acc


def _mm_tn(a, b, pa=1, pb=1):
    rows = a.shape[0]
    pad = (-rows) % LANES
    if pad:
        a = jnp.concatenate([a, jnp.zeros((pad, a.shape[1]), a.dtype)], axis=0)
        b = jnp.concatenate([b, jnp.zeros((pad, b.shape[1]), b.dtype)], axis=0)
    return _mm(a.T, b, NN, pa, pb)


def _iota(shape, dim):
    return lax.broadcasted_iota(jnp.int32, shape, dim)


def _log_sigmoid(z):
    return jnp.minimum(z, 0.0) - jnp.log(1.0 + jnp.exp(-jnp.abs(z)))


SUBLANES = 8


def _shift_rows(u, first):
    r = pltpu.roll(u, 1, axis=0)
    head = jnp.where(_iota((SUBLANES, 1), 0) == 0, first, r[:SUBLANES])
    return jnp.concatenate([head, r[SUBLANES:]], axis=0)


def _sigmoid(z):
    return 1.0 / (1.0 + jnp.exp(-z))


def _tanh(z):
    return 2.0 * _sigmoid(2.0 * z) - 1.0


def _rms(x, g):
    return x * lax.rsqrt(jnp.mean(x * x, axis=-1, keepdims=True) + EPS) * g


GROUP_HEADS = 2
GROUP_W = GROUP_HEADS * RWKV_HEAD
RWKV_GROUPS = RWKV_W // GROUP_W
SEG_W = 2 * LANES


def _head_masks(rows, head_w, heads):
    lane = _iota((rows, heads * head_w), 1)
    return [jnp.where(lane // head_w == h, 1.0, 0.0).astype(BF16) for h in range(heads)]


def _stack_heads(x, hmasks):
    xb = x.astype(BF16)
    return jnp.concatenate([xb * m for m in hmasks], axis=0)


def _block_diag(x_cat, blk):
    heads = blk.shape[0] // x_cat.shape[0]
    return jnp.concatenate([x_cat.astype(BF16)] * heads, axis=0) * blk


def _chunk_masks(rows, seq_len, heads):
    t = _iota((rows, heads * rows), 0)
    s = _iota((rows, heads * rows), 1) % rows
    strict, incl = t > s, t >= s
    t2 = _iota((2 * rows, rows), 0)
    s2 = _iota((2 * rows, rows), 1)
    cum_rows = (t2 < rows) & (t2 >= s2)
    tot_rows = t2 >= rows
    if seq_len < rows:
        same = (t // seq_len) == (s // seq_len)
        strict, incl = strict & same, incl & same
        same2 = ((t2 % rows) // seq_len) == (s2 // seq_len)
        cum_rows, tot_rows = cum_rows & same2, tot_rows & same2
    big = heads * rows
    blk = jnp.where(_iota((big, big), 0) // rows == _iota((big, big), 1) // rows, 1.0, 0.0)
    t4 = _iota((2 * rows, 2 * big), 0)
    s4 = _iota((2 * rows, 2 * big), 1) % rows
    quad = ((t4 % rows) > s4) | ((t4 >= rows) & ((t4 % rows) == s4))
    if seq_len < rows:
        quad = quad & (((t4 % rows) // seq_len) == (s4 // seq_len))
    return dict(strict=strict, incl=incl, quad=quad, eye=jnp.where(t == s, 1.0, 0.0),
                sums=jnp.where(cum_rows | tot_rows, 1.0, 0.0).astype(BF16), blk=blk.astype(BF16))


def _proj_kernel(x_ref, g_ref, wg_ref, wr_ref, ug_ref, ur_ref):
    h = _rms(x_ref[...], g_ref[...]).astype(BF16)
    ug_ref[...] = jnp.dot(h, wg_ref[...], preferred_element_type=F32)
    ur_ref[...] = jnp.dot(h, wr_ref[...], preferred_element_type=F32)


def _const_spec(shape):
    return pl.BlockSpec(shape, lambda *_: (0,) * len(shape), pipeline_mode=pl.Buffered(1))


def _proj(x2d, g, w_gla, w_rw, tm):
    n = x2d.shape[0]
    return pl.pallas_call(
        _proj_kernel,
        grid=(n // tm,),
        in_specs=[
            pl.BlockSpec((tm, D_MODEL), lambda i: (i, 0)),
            _const_spec((1, D_MODEL)),
            _const_spec((D_MODEL, GLA_COLS)),
            _const_spec((D_MODEL, RWKV_PROJ)),
        ],
        out_specs=[
            pl.BlockSpec((tm, GLA_COLS), lambda i: (i, 0)),
            pl.BlockSpec((tm, RWKV_PROJ), lambda i: (i, 0)),
        ],
        out_shape=[
            jax.ShapeDtypeStruct((n, GLA_COLS), F32),
            jax.ShapeDtypeStruct((n, RWKV_PROJ), F32),
        ],
        compiler_params=pltpu.CompilerParams(
            dimension_semantics=("arbitrary",), vmem_limit_bytes=VMEM_LIMIT),
        name="proj",
    )(x2d, g, w_gla, w_rw)


def _gla_prep(us, gkup, gkb, sums):
    rows = us[0].shape[0]
    z = [_mm(u[:, GLA_MAIN:GLA_COLS], gkup) for u in us]
    log_a = [_log_sigmoid(x + gkb) * (1.0 / GLA_GATE_NORM) for x in z]
    cums = [_mm(sums, x, NN, 1, 2) for x in log_a]
    out = []
    for u, x in zip(us, cums):
        cum, cum_end = x[:rows], x[rows:]
        q = u[:, 0:GLA_K_W] * (GLA_DK ** -0.5)
        k = u[:, GLA_K_W:2 * GLA_K_W]
        out.append((q * jnp.exp(cum), k * jnp.exp(-cum), k * jnp.exp(cum_end - cum),
                    u[:, 2 * GLA_K_W:2 * GLA_K_W + GLA_W], u[:, 2 * GLA_K_W + GLA_W:GLA_MAIN],
                    jnp.exp(cum_end)))
    return out


def _gla_intra(q_i, k_i, v, m, kmasks, vmasks):
    scores = jnp.where(m["incl"], _mm(q_i, _stack_heads(k_i, kmasks), NT), 0.0)
    return _mm(scores, _stack_heads(v, vmasks))


def _gla_out(o, gate, gnorm):
    heads = [slice(h * GLA_DV, (h + 1) * GLA_DV) for h in range(GLA_HEADS)]
    return jnp.concatenate(
        [_rms(o[:, hl], gnorm) * (gate[:, hl] * _sigmoid(gate[:, hl])) for hl in heads], axis=1)


def _gla_block_mask():
    return (_iota((GLA_K_W, GLA_W), 0) // GLA_DK) == (_iota((GLA_K_W, GLA_W), 1) // GLA_DV)


def _gla_state_in(s, blk):
    return jnp.where(blk, jnp.concatenate([s] * GLA_HEADS, axis=1), 0.0)


def _gla_state_out(s_bd):
    heads = [s_bd[:, h * GLA_DV:(h + 1) * GLA_DV] for h in range(GLA_HEADS)]
    return (heads[0] + heads[1]) + (heads[2] + heads[3])


def _lane_tiled_t(x):
    pad = LANES - x.shape[0]
    if pad:
        x = jnp.concatenate([x, jnp.zeros((pad, x.shape[1]), x.dtype)], axis=0)
    return x.T


def _gla_seq_kernel(u_ref, s0_ref, gkup_ref, gkb_ref, gn_ref, o_ref, sout_ref, s_scr):
    c = pl.program_id(1)
    n_seq, rows = u_ref.shape[0], u_ref.shape[1]
    blk = _gla_block_mask()

    @pl.when(c == 0)
    def _():
        for b in range(n_seq):
            s_scr[b] = _gla_state_in(s0_ref[b], blk)

    m = _chunk_masks(rows, rows, GLA_HEADS)
    kmasks = _head_masks(rows, GLA_DK, GLA_HEADS)
    vmasks = _head_masks(rows, GLA_DV, GLA_HEADS)
    tok = _gla_prep([u_ref[b] for b in range(n_seq)], gkup_ref[...], gkb_ref[...], m["sums"])
    s_old = [s_scr[b] for b in range(n_seq)]
    intra = [_gla_intra(q_i, k_i, v, m, kmasks, vmasks) for q_i, k_i, _, v, _, _ in tok]
    inter = [_mm(t[0], s) for t, s in zip(tok, s_old)]
    kv = [_mm_tn(k_e, v) for _, _, k_e, v, _, _ in tok]
    for b in range(n_seq):
        gate, g_end = tok[b][4], tok[b][5]
        o_ref[b] = _gla_out(intra[b] + inter[b], gate, gn_ref[...])
        dec = jnp.concatenate([_lane_tiled_t(jnp.broadcast_to(g_end[0:1], (LANES, GLA_K_W)))] * GLA_HEADS,
                              axis=1)
        s_scr[b] = dec * s_old[b] + jnp.where(blk, kv[b], 0.0)

    @pl.when(c == pl.num_programs(1) - 1)
    def _():
        for b in range(n_seq):
            sout_ref[b] = _gla_state_out(s_scr[b])


def _gla_dec_kernel(u_ref, s0_ref, gkup_ref, gkb_ref, gn_ref, o_ref, sout_ref, *, seq_len):
    rows = u_ref.shape[0]
    n_seq = rows // seq_len
    blk = _gla_block_mask()
    m = _chunk_masks(rows, seq_len, GLA_HEADS)
    q_i, k_i, k_e, v, gate, g_end = _gla_prep([u_ref[...]], gkup_ref[...], gkb_ref[...], m["sums"])[0]
    o = _gla_intra(q_i, k_i, v, m, _head_masks(rows, GLA_DK, GLA_HEADS),
                   _head_masks(rows, GLA_DV, GLA_HEADS))
    k_et = _lane_tiled_t(k_e).astype(BF16)
    dec_t = _lane_tiled_t(g_end)
    v_pad = v if rows == LANES else jnp.concatenate([v, jnp.zeros((LANES - rows, GLA_W), F32)], axis=0)
    row = _iota((rows, 1), 0)
    row_pad = _iota((LANES, 1), 0)
    for j in range(n_seq):
        s_bd = _gla_state_in(s0_ref[j], blk)
        o = o + _mm(jnp.where(row // seq_len == j, q_i, 0.0), s_bd)
        kv = _mm(k_et, jnp.where(row_pad // seq_len == j, v_pad, 0.0))
        first = j * seq_len
        sout_ref[j] = _gla_state_out(dec_t[:, first:first + 1] * s_bd + jnp.where(blk, kv, 0.0))
    o_ref[...] = _gla_out(o, gate, gn_ref[...])


def _gla_param_specs():
    return [_const_spec((LANES, GLA_K_W)), _const_spec((1, GLA_K_W)), _const_spec((1, GLA_DV))]


def _gla_seq(u3, s0, params, rows, n_seq):
    b, t, _ = u3.shape
    sdim = GLA_HEADS * GLA_DK
    state_spec = pl.BlockSpec((n_seq, sdim, GLA_DV), lambda i, j: (i, 0, 0))
    return pl.pallas_call(
        _gla_seq_kernel,
        grid=(b // n_seq, t // rows),
        in_specs=[pl.BlockSpec((n_seq, rows, GLA_COLS), lambda i, j: (i, j, 0)), state_spec]
        + _gla_param_specs(),
        out_specs=[pl.BlockSpec((n_seq, rows, GLA_W), lambda i, j: (i, j, 0)), state_spec],
        out_shape=[
            jax.ShapeDtypeStruct((b, t, GLA_W), F32),
            jax.ShapeDtypeStruct((b, sdim, GLA_DV), F32),
        ],
        scratch_shapes=[pltpu.VMEM((n_seq, sdim, GLA_W), F32)],
        compiler_params=pltpu.CompilerParams(
            dimension_semantics=("arbitrary", "arbitrary"), vmem_limit_bytes=VMEM_LIMIT),
        name="gla_seq",
    )(u3, s0, *params)


def _gla_dec(u2, s0, params, seq_len, n_seq):
    n = u2.shape[0]
    rows = n_seq * seq_len
    sdim = GLA_HEADS * GLA_DK
    state_spec = pl.BlockSpec((n_seq, sdim, GLA_DV), lambda i: (i, 0, 0))
    return pl.pallas_call(
        functools.partial(_gla_dec_kernel, seq_len=seq_len),
        grid=(n // rows,),
        in_specs=[pl.BlockSpec((rows, GLA_COLS), lambda i: (i, 0)), state_spec] + _gla_param_specs(),
        out_specs=[pl.BlockSpec((rows, GLA_W), lambda i: (i, 0)), state_spec],
        out_shape=[
            jax.ShapeDtypeStruct((n, GLA_W), F32),
            jax.ShapeDtypeStruct(s0.shape, F32),
        ],
        compiler_params=pltpu.CompilerParams(
            dimension_semantics=("arbitrary",), vmem_limit_bytes=VMEM_LIMIT),
        name="gla_dec",
    )(u2, s0, *params)


def _seg_sum(x, seg, pa=1):
    return jnp.concatenate(
        [_mm(x[:, i:i + SEG_W], seg, NN, pa, 1) for i in range(0, RWKV_W, SEG_W)], axis=1)


W_OFFSET_SCALE = 0.6065306597126334


def _rwkv_tokens(u, prev, mu, w0, w2, a0, a2, g2, k_k, k_a, rk, seg):
    xr = u + mu * (prev - u)
    r = xr[:, 0:RWKV_W]
    kr = xr[:, RWKV_W:2 * RWKV_W]
    vr = xr[:, 2 * RWKV_W:3 * RWKV_W]
    wa = xr[:, 3 * RWKV_W:3 * RWKV_W + LANES]
    gd = xr[:, 3 * RWKV_W + LANES:RWKV_PROJ]
    lw = _sigmoid(w0 + _mm(_tanh(wa), w2)) * (-W_OFFSET_SCALE)
    a_sig = _sigmoid(a0 + _mm(wa, a2))
    gate = _mm(_sigmoid(gd), g2)
    kk = kr * k_k
    kk = kk * lax.rsqrt(jnp.maximum(_seg_sum(kk * kk, seg), 1e-24))
    kr = kr * (1.0 + (a_sig - 1.0) * k_a)
    bonus = _seg_sum(r * kr * rk, seg) * vr
    return r, kr, vr, lw, -kk, kk * a_sig, gate, bonus


def _rwkv_prep(u, prev, mu, w0, w2, a0, a2, g2, k_k, k_a, rk, seg, sums):
    rows = u.shape[0]
    r, kr, vr, lw, a_vec, b_vec, gate, bonus = _rwkv_tokens(
        u, prev, mu, w0, w2, a0, a2, g2, k_k, k_a, rk, seg)
    block = sums.shape[1]
    cums = [_mm(sums, lw[i:i + block], NN, 1, 2) for i in range(0, rows, block)]
    cum = jnp.concatenate([x[:block] for x in cums], axis=0)
    cum_end = jnp.concatenate([x[block:] for x in cums], axis=0)
    e_neg = jnp.exp(-cum)
    g_end = jnp.exp(cum_end)
    e_end = g_end * e_neg
    bf = lambda x: x.astype(BF16)
    return (bf(a_vec * jnp.exp(cum - lw)), bf(r * jnp.exp(cum)), bf(kr * e_neg), bf(b_vec * e_neg),
            bf(kr * e_end), bf(b_vec * e_end), bf(vr), gate, bonus, g_end)


def _rwkv_intra(units, m, hmasks, n_double):
    rows = units[0][0].shape[0]
    stack = lambda x: _stack_heads(x, hmasks)
    cat_w = m["strict"].shape[1]
    g = [_mm(jnp.concatenate([at, rt], axis=0), jnp.concatenate([stack(bt), stack(kt)], axis=0), NT)
         for at, rt, kt, bt, _ in units]
    g = [jnp.where(m["quad"], x, 0.0) for x in g]
    a_ab = [x[:rows, :cat_w] for x in g]
    a_rb = [x[rows:, :cat_w] for x in g]
    a_ak = [x[:rows, cat_w:] for x in g]
    a_rk = [x[rows:, cat_w:] for x in g]
    tinv = [m["eye"] + a for a in a_ab]
    apow = [_mm(a, _block_diag(a, m["blk"])) for a in a_ab]
    for _ in range(n_double - 1):
        both = [_mm(jnp.concatenate([t, a], axis=0), _block_diag(a, m["blk"])) for t, a in zip(tinv, apow)]
        tinv = [t + x[:rows] for t, x in zip(tinv, both)]
        apow = [x[rows:] for x in both]
    tinv = [t + _mm(t, _block_diag(a, m["blk"])) for t, a in zip(tinv, apow)]
    v_s = [stack(vp) for _, _, _, _, vp in units]
    akv = [_mm(jnp.concatenate([a, b], axis=0), v) for a, b, v in zip(a_ak, a_rk, v_s)]
    wu = [_mm(t, jnp.concatenate([stack(at), stack(x[:rows])], axis=1))
          for t, (at, _, _, _, _), x in zip(tinv, units, akv)]
    z = [_mm(a, jnp.concatenate([stack(x[:, :GROUP_W]), stack(x[:, GROUP_W:])], axis=1))
         for a, x in zip(a_rb, wu)]
    return [(x[:, :GROUP_W], x[:, GROUP_W:], rt + zz[:, :GROUP_W], zz[:, GROUP_W:] + kv[rows:])
            for x, (_, rt, _, _, _), zz, kv in zip(wu, units, z, akv)]


def _rwkv_out(ys, gates, bonuses, lnw, lnb, seg):
    inv = 1.0 / RWKV_HEAD
    yc = [y - _seg_sum(y, seg, 2) * inv for y in ys]
    var = [_seg_sum(c * c, seg) * inv for c in yc]
    return [(c * lax.rsqrt(v + RWKV_GN_EPS) * lnw + lnb + bonus) * gate
            for c, v, bonus, gate in zip(yc, var, bonuses, gates)]


def _rwkv_state_in(s_ref, idx, g):
    zero = jnp.zeros((RWKV_HEAD, RWKV_HEAD), F32)
    blocks = []
    for h in range(GROUP_HEADS):
        parts = [zero] * GROUP_HEADS
        parts[h] = s_ref[idx, g * GROUP_HEADS + h]
        blocks.append(jnp.concatenate(parts, axis=1))
    return jnp.concatenate(blocks, axis=0)


def _rwkv_state_out(s_ref, idx, g, s2):
    for h in range(GROUP_HEADS):
        sl = slice(h * RWKV_HEAD, (h + 1) * RWKV_HEAD)
        s_ref[idx, g * GROUP_HEADS + h] = s2[sl, sl]


def _head_block_mask():
    return (_iota((GROUP_W, GROUP_W), 0) // RWKV_HEAD) == (_iota((GROUP_W, GROUP_W), 1) // RWKV_HEAD)


def _rwkv_seq_kernel(u_ref, shift0_ref, s0_ref, mu_ref, w0_ref, w2_ref, a0_ref, a2_ref,
                     g2_ref, kk_ref, ka_ref, rk_ref, lnw_ref, lnb_ref, seg_ref,
                     o_ref, sout_ref, s_scr, prev_scr):
    c = pl.program_id(1)
    n_seq, rows = u_ref.shape[0], u_ref.shape[1]
    groups = [slice(g * GROUP_W, (g + 1) * GROUP_W) for g in range(RWKV_GROUPS)]
    ids = [(b, g) for b in range(n_seq) for g in range(RWKV_GROUPS)]

    @pl.when(c == 0)
    def _():
        for b in range(n_seq):
            for g in range(RWKV_GROUPS):
                s_scr[b, g] = _rwkv_state_in(s0_ref, b, g)
            prev_scr[b] = shift0_ref[b]

    seg = seg_ref[...]
    m = _chunk_masks(rows, rows, GROUP_HEADS)
    hmasks = _head_masks(rows, RWKV_HEAD, GROUP_HEADS)
    head_blk = _head_block_mask()
    n_double = rows.bit_length() - 2

    us = [u_ref[b] for b in range(n_seq)]
    prev = jnp.concatenate([_shift_rows(u, prev_scr[b]) for b, u in enumerate(us)], axis=0)
    for b, u in enumerate(us):
        prev_scr[b] = u[rows - 1:rows]
    at, rt, kt, bt, ke, be, vb, gate, bonus, g_end = _rwkv_prep(
        jnp.concatenate(us, axis=0), prev, mu_ref[...], w0_ref[...], w2_ref[...], a0_ref[...],
        a2_ref[...], g2_ref[...], kk_ref[...], ka_ref[...], rk_ref[...], seg, m["sums"])
    tok, units = [], []
    for b in range(n_seq):
        sl = slice(b * rows, (b + 1) * rows)
        tok.append((vb[sl], ke[sl], be[sl], g_end[b * rows:b * rows + 1]))
        units += [(at[sl, gl], rt[sl, gl], kt[sl, gl], bt[sl, gl], vb[sl, gl]) for gl in groups]

    intra = _rwkv_intra(units, m, hmasks, n_double)
    s_old = [s_scr[b, g] for b, g in ids]
    uy = [_mm(jnp.concatenate([w_m, r_m], axis=0), s2, NT) for (w_m, _, r_m, _), s2 in zip(intra, s_old)]
    upd = []
    for (b, g), (_, u0, _, _), x in zip(ids, intra, uy):
        vb_b, ke_b, be_b, _ = tok[b]
        gl = groups[g]
        upd.append(_mm_tn(jnp.concatenate([x[:rows] + u0, vb_b[:, gl].astype(F32)], axis=0),
                          jnp.concatenate([be_b[:, gl], ke_b[:, gl]], axis=0)))
    for (b, g), s2, d in zip(ids, s_old, upd):
        s_scr[b, g] = s2 * tok[b][3][:, groups[g]] + jnp.where(head_blk, d, 0.0)
    ys = [jnp.concatenate([uy[i][rows:] + intra[i][3] for i, (bb, _) in enumerate(ids) if bb == b],
                          axis=1) for b in range(n_seq)]
    out = _rwkv_out([jnp.concatenate(ys, axis=0)], [gate], [bonus], lnw_ref[...], lnb_ref[...], seg)[0]
    for b in range(n_seq):
        o_ref[b] = out[b * rows:(b + 1) * rows]

    @pl.when(c == pl.num_programs(1) - 1)
    def _():
        for b, g in ids:
            _rwkv_state_out(sout_ref, b, g, s_scr[b, g])


STEP_HEADS = 2
STEP_ROWS = 4
N_STEP_OPERANDS = 6


def _sum_keys(x):
    t = x[0:SUBLANES]
    for i in range(SUBLANES, x.shape[0], SUBLANES):
        t = t + x[i:i + SUBLANES]
    for shift in (4, 2, 1):
        t = t + pltpu.roll(t, shift, axis=0)
    return t


def _rwkv_step_kernel(u_ref, shiftx_ref, s0_ref, mu_ref, w0_ref, w2_ref, a0_ref, a2_ref, g2_ref,
                      kk_ref, ka_ref, rk_ref, lnw_ref, lnb_ref, seg_ref,
                      o_ref, sout_ref, ops_scr, y_scr, tmp_scr, aux_scr, *, seq_len):
    p = pl.program_id(0)
    n_tok = u_ref.shape[0]
    n_seq = n_tok // seq_len
    seg = seg_ref[...]
    cols = [slice(c, c + LANES) for c in range(0, RWKV_W, LANES)]

    @pl.when(p == 0)
    def _():
        u = u_ref[...]
        row = _iota((n_tok, 1), 0)
        prev = jnp.where(row % seq_len == 0, shiftx_ref[...], pltpu.roll(u, 1, axis=0))
        r, kr, vr, lw, a_vec, b_vec, gate, bonus = _rwkv_tokens(
            u, prev, mu_ref[...], w0_ref[...], w2_ref[...], a0_ref[...], a2_ref[...], g2_ref[...],
            kk_ref[...], ka_ref[...], rk_ref[...], seg)
        aux_scr[0] = gate
        aux_scr[1] = bonus
        for i, x in enumerate((jnp.exp(lw), a_vec, b_vec, kr, r, vr)):
            for c, cl in enumerate(cols):
                tmp_scr[c] = x[:, cl]
            for t in range(seq_len):
                for c, cl in enumerate(cols):
                    ops_scr[i, t, cl, :] = tmp_scr[c, pl.ds(t, n_seq, stride=seq_len), :].T

    def rows_step(i, carry):
        for hh in range(STEP_HEADS):
            chan = pl.multiple_of((p * STEP_HEADS + hh) * RWKV_HEAD, RWKV_HEAD)
            for j in range(STEP_ROWS):
                v = i * STEP_ROWS + j
                s = s0_ref[hh, v]
                for t in range(seq_len):
                    w, a, b, k, r = (ops_scr[n, t, pl.ds(chan, RWKV_HEAD), :] for n in range(5))
                    v_t = ops_scr[5, t, pl.ds(chan + v, 1), :]
                    sa = jnp.concatenate([_sum_keys(s * a)] * (RWKV_HEAD // SUBLANES), axis=0)
                    s = s * w + sa * b + v_t * k
                    y_scr[t, pl.ds(chan + v, 1), :] = _sum_keys(s * r)[0:1]
                sout_ref[hh, v] = s
        return carry

    lax.fori_loop(0, RWKV_HEAD // STEP_ROWS, rows_step, 0)

    @pl.when(p == pl.num_programs(0) - 1)
    def _():
        for t in range(seq_len):
            for c, cl in enumerate(cols):
                tmp_scr[c, pl.ds(t, n_seq, stride=seq_len), :] = y_scr[t, cl, :].T
        y = jnp.concatenate([tmp_scr[c] for c in range(len(cols))], axis=1)
        o_ref[...] = _rwkv_out([y], [aux_scr[0]], [aux_scr[1]], lnw_ref[...], lnb_ref[...], seg)[0]


def _rwkv_step(u2, shiftx, s0_t, params, seq_len):
    n = u2.shape[0]
    n_seq = n // seq_len
    state_spec = pl.BlockSpec((STEP_HEADS, RWKV_HEAD, RWKV_HEAD, n_seq), lambda i: (i, 0, 0, 0))
    whole = lambda w: pl.BlockSpec((n, w), lambda i: (0, 0), pipeline_mode=pl.Buffered(1))
    return pl.pallas_call(
        functools.partial(_rwkv_step_kernel, seq_len=seq_len),
        grid=(RWKV_HEADS // STEP_HEADS,),
        in_specs=[whole(RWKV_PROJ), whole(RWKV_PROJ), state_spec] + _rwkv_param_specs(),
        out_specs=[pl.BlockSpec((n, RWKV_W), lambda i: (0, 0)), state_spec],
        out_shape=[
            jax.ShapeDtypeStruct((n, RWKV_W), F32),
            jax.ShapeDtypeStruct(s0_t.shape, F32),
        ],
        scratch_shapes=[
            pltpu.VMEM((N_STEP_OPERANDS, seq_len, RWKV_W, n_seq), F32),
            pltpu.VMEM((seq_len, RWKV_W, n_seq), F32),
            pltpu.VMEM((RWKV_W // LANES, n, LANES), F32),
            pltpu.VMEM((2, n, RWKV_W), F32),
        ],
        compiler_params=pltpu.CompilerParams(
            dimension_semantics=("arbitrary",), vmem_limit_bytes=VMEM_LIMIT),
        name="rwkv_step",
    )(u2, shiftx, s0_t, *params)


def _rwkv_param_specs():
    vec = lambda n: _const_spec((1, n))
    return [
        vec(RWKV_PROJ),
        vec(RWKV_W),
        _const_spec((LANES, RWKV_W)),
        vec(RWKV_W),
        _const_spec((LANES, RWKV_W)),
        _const_spec((LANES, RWKV_W)),
        vec(RWKV_W), vec(RWKV_W), vec(RWKV_W), vec(RWKV_W), vec(RWKV_W),
        _const_spec((SEG_W, SEG_W)),
    ]


def _rwkv_seq(u3, shift0, s0, params, rows, n_seq):
    b, t, _ = u3.shape
    state_spec = pl.BlockSpec((None, n_seq, RWKV_HEADS, RWKV_HEAD, RWKV_HEAD),
                              lambda i, j: (0, i, 0, 0, 0))
    return pl.pallas_call(
        _rwkv_seq_kernel,
        grid=(b // n_seq, t // rows),
        in_specs=[
            pl.BlockSpec((n_seq, rows, RWKV_PROJ), lambda i, j: (i, j, 0)),
            pl.BlockSpec((n_seq, 1, RWKV_PROJ), lambda i, j: (i, 0, 0)),
            state_spec,
        ] + _rwkv_param_specs(),
        out_specs=[pl.BlockSpec((n_seq, rows, RWKV_W), lambda i, j: (i, j, 0)), state_spec],
        out_shape=[
            jax.ShapeDtypeStruct((b, t, RWKV_W), F32),
            jax.ShapeDtypeStruct(s0.shape, F32),
        ],
        scratch_shapes=[
            pltpu.VMEM((n_seq, RWKV_GROUPS, GROUP_W, GROUP_W), F32),
            pltpu.VMEM((n_seq, 1, RWKV_PROJ), F32),
        ],
        compiler_params=pltpu.CompilerParams(
            dimension_semantics=("arbitrary", "arbitrary"), vmem_limit_bytes=VMEM_LIMIT),
        name="rwkv_seq",
    )(u3, shift0, s0, *params)


FF_CHUNK = D_FF // 2


def _post_kernel(xa_ref, oga_ref, ora_ref, pa_ref, xb_ref, ogb_ref, orb_ref, pb_ref,
                 wo_ref, nffn_ref, wg_ref, wu_ref, wd_ref, nple_ref, wpg_ref, wpp_ref, nf_ref,
                 ya_ref, yb_ref):
    weights = (wo_ref, nffn_ref, wg_ref, wu_ref, wd_ref, nple_ref, wpg_ref, wpp_ref, nf_ref)
    last = pl.num_programs(0) - 1

    @pl.when(pl.program_id(0) < last)
    def _():
        _post_tile(xa_ref, oga_ref, ora_ref, pa_ref, *weights, ya_ref)

    @pl.when(pl.program_id(0) == last)
    def _():
        _post_tile(xb_ref, ogb_ref, orb_ref, pb_ref, *weights, yb_ref)


def _post_tile(x_ref, og_ref, or_ref, p_ref, wo_ref, nffn_ref, wg_ref, wu_ref, wd_ref,
               nple_ref, wpg_ref, wpp_ref, nf_ref, y_ref):
    half = x_ref.shape[0] // 2
    parts = [slice(0, half), slice(half, 2 * half)]
    dot = lambda a, b: jnp.dot(a, b, preferred_element_type=F32)
    o = [jnp.concatenate([og_ref[p, :], or_ref[p, :]], axis=1).astype(BF16) for p in parts]
    x = [x_ref[p, :] + dot(oo, wo_ref[...]) for p, oo in zip(parts, o)]
    h2 = [_rms(xx, nffn_ref[...]).astype(BF16) for xx in x]
    for i in range(0, D_FF, FF_CHUNK):
        gate = [dot(h, wg_ref[:, i:i + FF_CHUNK]) for h in h2]
        up = [dot(h, wu_ref[:, i:i + FF_CHUNK]) for h in h2]
        act = [(g * _sigmoid(g) * u).astype(BF16) for g, u in zip(gate, up)]
        x = [xx + dot(a, wd_ref[i:i + FF_CHUNK, :]) for xx, a in zip(x, act)]
    h3 = [_rms(xx, nple_ref[...]).astype(BF16) for xx in x]
    pg = [_sigmoid(dot(h, wpg_ref[...])) for h in h3]
    pp = [dot(p_ref[p, :].astype(BF16), wpp_ref[...]) for p in parts]
    for p, xx, a, b in zip(parts, x, pg, pp):
        y_ref[p, :] = _rms(xx + a * b, nf_ref[...])


def _post(tokens_a, tokens_b, weights, tm):
    n_a, n_b = tokens_a[0].shape[0], tokens_b[0].shape[0]
    tiles_a = n_a // tm
    widths = (D_MODEL, GLA_W, RWKV_W, PLE_DIM)
    tiled = lambda w: pl.BlockSpec((tm, w), lambda i: (jnp.minimum(i, tiles_a - 1), 0))
    whole = lambda w: _const_spec((n_b, w))
    return pl.pallas_call(
        _post_kernel,
        grid=(tiles_a + 1,),
        in_specs=[tiled(w) for w in widths] + [whole(w) for w in widths] + [
            _const_spec((D_MODEL, D_MODEL)), _const_spec((1, D_MODEL)),
            _const_spec((D_MODEL, D_FF)), _const_spec((D_MODEL, D_FF)), _const_spec((D_FF, D_MODEL)),
            _const_spec((1, D_MODEL)), _const_spec((D_MODEL, D_MODEL)), _const_spec((PLE_DIM, D_MODEL)),
            _const_spec((1, D_MODEL)),
        ],
        out_specs=[tiled(D_MODEL), pl.BlockSpec((n_b, D_MODEL), lambda i: (0, 0))],
        out_shape=[jax.ShapeDtypeStruct((n_a, D_MODEL), F32), jax.ShapeDtypeStruct((n_b, D_MODEL), F32)],
        compiler_params=pltpu.CompilerParams(
            dimension_semantics=("arbitrary",), vmem_limit_bytes=VMEM_LIMIT),
        name="post",
    )(*tokens_a, *tokens_b, *weights)


PROMPT_CHUNK = 64
PROMPT_SEQS_PER_STEP = 8
DEC_TILE_SEQS = 32
TOKEN_TILE = 512
PROJ_TILE = 1024


def kernel(x_prompt, x_sample, state_gla, state_rwkv, state_shift, p_prompt, p_sample, norm_mix, w_in, gla_gk_up, gla_gk_bias, gla_norm, rwkv_mu, rwkv_w0, rwkv_w2, rwkv_a0, rwkv_a2, rwkv_g2, rwkv_k_k, rwkv_k_a, rwkv_r_k, rwkv_ln_w, rwkv_ln_b, w_out, norm_ffn, w_gate, w_up, w_down, norm_ple, w_ple_gate, w_ple_proj, norm_final):
    assert w_in.shape[0] == 1
    i = 0
    rowv = lambda a: a.astype(F32).reshape(1, -1)
    zeros = lambda r, c: jnp.zeros((r, c), F32)
    w_in_i = w_in[i]
    w_gla = jnp.concatenate(
        [w_in_i[:, :GLA_PROJ], zeros(D_MODEL, GLA_COLS - GLA_PROJ)], axis=1).astype(BF16)
    w_rw = w_in_i[:, GLA_PROJ:].astype(BF16)
    gk_up = jnp.concatenate(
        [gla_gk_up[i].astype(F32), zeros(LANES - GLA_GATE_RANK, GLA_K_W)], axis=0).astype(BF16)
    seg = jnp.arange(SEG_W)[:, None] // RWKV_HEAD == jnp.arange(SEG_W)[None, :] // RWKV_HEAD
    rwkv_params = (
        rowv(rwkv_mu[i]), rowv(rwkv_w0[i]),
        jnp.concatenate([rwkv_w2[i].astype(F32), zeros(64, RWKV_W)], axis=0).astype(BF16),
        rowv(rwkv_a0[i]),
        jnp.concatenate([zeros(64, RWKV_W), rwkv_a2[i].astype(F32)], axis=0).astype(BF16),
        rwkv_g2[i].astype(BF16),
        rowv(rwkv_k_k[i]), rowv(rwkv_k_a[i]), rowv(rwkv_r_k[i]), rowv(rwkv_ln_w[i]), rowv(rwkv_ln_b[i]),
        seg.astype(BF16),
    )
    post_w = (
        w_out[i].astype(BF16), rowv(norm_ffn[i]), w_gate[i].astype(BF16), w_up[i].astype(BF16),
        w_down[i].astype(BF16), rowv(norm_ple[i]), w_ple_gate[i].astype(BF16),
        w_ple_proj[i].astype(BF16), rowv(norm_final),
    )
    gla_w = (gk_up, rowv(gla_gk_bias[i]), rowv(gla_norm[i]))
    g_mix = rowv(norm_mix[i])

    bp, tp, _ = x_prompt.shape
    xp = x_prompt.astype(F32).reshape(bp * tp, D_MODEL)
    ug, ur = _proj(xp, g_mix, w_gla, w_rw, PROJ_TILE)
    og, gla_p = _gla_seq(ug.reshape(bp, tp, GLA_COLS),
                         jnp.zeros((bp, GLA_HEADS * GLA_DK, GLA_DV), F32), gla_w,
                         PROMPT_CHUNK, PROMPT_SEQS_PER_STEP)
    ur3 = ur.reshape(bp, tp, RWKV_PROJ)
    orw, rwkv_p = _rwkv_seq(ur3, jnp.zeros((bp, 1, RWKV_PROJ), F32),
                            jnp.zeros((1, bp, RWKV_HEADS, RWKV_HEAD, RWKV_HEAD), F32), rwkv_params,
                            PROMPT_CHUNK, PROMPT_SEQS_PER_STEP)
    shift_p = ur3[:, tp - 1]
    prompt_tokens = (xp, og.reshape(bp * tp, GLA_W), orw.reshape(bp * tp, RWKV_W),
                     p_prompt[i].reshape(bp * tp, PLE_DIM))

    bs, ts, _ = x_sample.shape
    xs = x_sample.astype(F32).reshape(bs * ts, D_MODEL)
    ug, ur = _proj(xs, g_mix, w_gla, w_rw, TOKEN_TILE)
    og, gla_s = _gla_dec(ug, state_gla[i].astype(F32).reshape(bs, GLA_HEADS * GLA_DK, GLA_DV),
                         gla_w, ts, DEC_TILE_SEQS)
    shiftx = jnp.pad(state_shift[i].astype(F32)[:, None, :], ((0, 0), (0, ts - 1), (0, 0)))
    orw, rwkv_s = _rwkv_step(ur, shiftx.reshape(bs * ts, RWKV_PROJ),
                             jnp.transpose(state_rwkv[i].astype(F32), (1, 2, 3, 0)), rwkv_params, ts)
    rwkv_s = jnp.transpose(rwkv_s, (3, 0, 1, 2))[None]
    shift_s = ur.reshape(bs, ts, RWKV_PROJ)[:, ts - 1]

    yp, ys = _post(prompt_tokens, (xs, og, orw, p_sample[i].reshape(bs * ts, PLE_DIM)), post_w,
                   TOKEN_TILE)

    gla_shape = (1, -1, GLA_HEADS, GLA_DK, GLA_DV)
    return (yp.reshape(bp, tp, D_MODEL).astype(x_prompt.dtype),
            ys.reshape(bs, ts, D_MODEL).astype(x_sample.dtype),
            gla_p.reshape(gla_shape).astype(state_gla.dtype), rwkv_p.astype(state_rwkv.dtype),
            shift_p[None].astype(state_shift.dtype),
            gla_s.reshape(gla_shape).astype(state_gla.dtype), rwkv_s.astype(state_rwkv.dtype),
            shift_s[None].astype(state_shift.dtype))
```

```python
import functools

import jax
import jax.numpy as jnp
from jax import lax
from jax.experimental import pallas as pl
from jax.experimental.pallas import tpu as pltpu

F32 = jnp.float32
BF16 = jnp.bfloat16

D_MODEL = 1024
GLA_HEADS = 4
GLA_DK = 64
GLA_DV = 128
GLA_K_W = GLA_HEADS * GLA_DK
GLA_W = GLA_HEADS * GLA_DV
GLA_GATE_RANK = 16
GLA_GATE_NORM = 16.0
GLA_MAIN = 2 * GLA_K_W + 2 * GLA_W
GLA_PROJ = GLA_MAIN + GLA_GATE_RANK
LANES = 128
SUBLANES = 8
GLA_COLS = GLA_MAIN + LANES
RWKV_HEAD = 64
RWKV_HEADS = 8
RWKV_W = RWKV_HEADS * RWKV_HEAD
RWKV_PROJ = 3 * RWKV_W + 64 + 64 + 128
D_FF = 2816
PLE_DIM = 256
EPS = 1e-6
RWKV_GN_EPS = 64e-5

VMEM_LIMIT = 56 * 1024 * 1024

NN = ((1,), (0,))
NT = ((1,), (1,))


def _split(x, n):
    parts = []
    r = x
    for i in range(n):
        p = r.astype(BF16)
        parts.append(p)
        if i + 1 < n:
            r = r - p.astype(F32)
    return parts


def _mm(a, b, dims=NN, pa=1, pb=1):
    pieces_a = _split(a, pa)
    pieces_b = _split(b, pb)
    n = max(pa, pb)
    acc = None
    for i, ai in enumerate(pieces_a):
        for j, bj in enumerate(pieces_b):
            if i + j < n:
                t = lax.dot_general(ai, bj, (dims, ((), ())), preferred_element_type=F32)
                acc = t if acc is None else acc + t
    return acc


def _mm_tn(a, b, pa=1, pb=1):
    rows = a.shape[0]
    pad = (-rows) % LANES
    if pad:
        a = jnp.concatenate([a, jnp.zeros((pad, a.shape[1]), a.dtype)], axis=0)
        b = jnp.concatenate([b, jnp.zeros((pad, b.shape[1]), b.dtype)], axis=0)
    return _mm(a.T, b, NN, pa, pb)


def _iota(shape, dim):
    return lax.broadcasted_iota(jnp.int32, shape, dim)


def _log_sigmoid(z):
    return jnp.minimum(z, 0.0) - jnp.log(1.0 + jnp.exp(-jnp.abs(z)))


def _shift_rows(u, first):
    r = pltpu.roll(u, 1, axis=0)
    head = jnp.where(_iota((SUBLANES, 1), 0) == 0, first, r[:SUBLANES])
    return jnp.concatenate([head, r[SUBLANES:]], axis=0)


def _sigmoid(z):
    return 1.0 / (1.0 + jnp.exp(-z))


def _tanh(z):
    return 2.0 * _sigmoid(2.0 * z) - 1.0


def _rms(x, g):
    return x * lax.rsqrt(jnp.mean(x * x, axis=-1, keepdims=True) + EPS) * g


GROUP_HEADS = 2
GROUP_W = GROUP_HEADS * RWKV_HEAD
RWKV_GROUPS = RWKV_W // GROUP_W
SEG_W = 2 * LANES


def _head_masks(rows, head_w, heads):
    lane = _iota((rows, heads * head_w), 1)
    return [jnp.where(lane // head_w == h, 1.0, 0.0).astype(BF16) for h in range(heads)]


def _stack_heads(x, hmasks):
    xb = x.astype(BF16)
    return jnp.concatenate([xb * m for m in hmasks], axis=0)


def _block_diag(x_cat, blk):
    heads = blk.shape[0] // x_cat.shape[0]
    return jnp.concatenate([x_cat.astype(BF16)] * heads, axis=0) * blk


def _chunk_masks(rows, seq_len, heads):
    t = _iota((rows, heads * rows), 0)
    s = _iota((rows, heads * rows), 1) % rows
    incl = t >= s
    t2 = _iota((2 * rows, rows), 0)
    s2 = _iota((2 * rows, rows), 1)
    cum_rows = (t2 < rows) & (t2 >= s2)
    tot_rows = t2 >= rows
    if seq_len < rows:
        incl = incl & ((t // seq_len) == (s // seq_len))
        same2 = ((t2 % rows) // seq_len) == (s2 // seq_len)
        cum_rows, tot_rows = cum_rows & same2, tot_rows & same2
    big = heads * rows
    blk = jnp.where(_iota((big, big), 0) // rows == _iota((big, big), 1) // rows, 1.0, 0.0)
    t4 = _iota((2 * rows, 2 * big), 0)
    s4 = _iota((2 * rows, 2 * big), 1) % rows
    quad = ((t4 % rows) > s4) | ((t4 >= rows) & ((t4 % rows) == s4))
    if seq_len < rows:
        quad = quad & (((t4 % rows) // seq_len) == (s4 // seq_len))
    return dict(incl=incl, quad=quad, eye=jnp.where(t == s, 1.0, 0.0),
                sums=jnp.where(cum_rows | tot_rows, 1.0, 0.0).astype(BF16), blk=blk.astype(BF16))


def _proj_kernel(x_ref, g_ref, wg_ref, wr_ref, ug_ref, ur_ref):
    h = _rms(x_ref[...], g_ref[...]).astype(BF16)
    ug_ref[...] = jnp.dot(h, wg_ref[...], preferred_element_type=F32)
    ur_ref[...] = jnp.dot(h, wr_ref[...], preferred_element_type=F32)


def _const_spec(shape):
    return pl.BlockSpec(shape, lambda *_: (0,) * len(shape), pipeline_mode=pl.Buffered(1))


def _proj(x2d, g, w_gla, w_rw, tm):
    n = x2d.shape[0]
    return pl.pallas_call(
        _proj_kernel,
        grid=(n // tm,),
        in_specs=[
            pl.BlockSpec((tm, D_MODEL), lambda i: (i, 0)),
            _const_spec((1, D_MODEL)),
            _const_spec((D_MODEL, GLA_COLS)),
            _const_spec((D_MODEL, RWKV_PROJ)),
        ],
        out_specs=[
            pl.BlockSpec((tm, GLA_COLS), lambda i: (i, 0)),
            pl.BlockSpec((tm, RWKV_PROJ), lambda i: (i, 0)),
        ],
        out_shape=[
            jax.ShapeDtypeStruct((n, GLA_COLS), F32),
            jax.ShapeDtypeStruct((n, RWKV_PROJ), F32),
        ],
        compiler_params=pltpu.CompilerParams(
            dimension_semantics=("arbitrary",), vmem_limit_bytes=VMEM_LIMIT),
        name="proj",
    )(x2d, g, w_gla, w_rw)


def _gla_prep(us, gkup, gkb, sums):
    rows = us[0].shape[0]
    z = [_mm(u[:, GLA_MAIN:GLA_COLS], gkup) for u in us]
    log_a = [_log_sigmoid(x + gkb) * (1.0 / GLA_GATE_NORM) for x in z]
    cums = [_mm(sums, x, NN, 1, 2) for x in log_a]
    out = []
    for u, x in zip(us, cums):
        cum, cum_end = x[:rows], x[rows:]
        q = u[:, 0:GLA_K_W] * (GLA_DK ** -0.5)
        k = u[:, GLA_K_W:2 * GLA_K_W]
        out.append((q * jnp.exp(cum), k * jnp.exp(-cum), k * jnp.exp(cum_end - cum),
                    u[:, 2 * GLA_K_W:2 * GLA_K_W + GLA_W], u[:, 2 * GLA_K_W + GLA_W:GLA_MAIN],
                    jnp.exp(cum_end)))
    return out


def _gla_intra(q_i, k_i, v, m, kmasks, vmasks):
    scores = jnp.where(m["incl"], _mm(q_i, _stack_heads(k_i, kmasks), NT), 0.0)
    return _mm(scores, _stack_heads(v, vmasks))


def _gla_out(o, gate, gnorm):
    heads = [slice(h * GLA_DV, (h + 1) * GLA_DV) for h in range(GLA_HEADS)]
    return jnp.concatenate(
        [_rms(o[:, hl], gnorm) * (gate[:, hl] * _sigmoid(gate[:, hl])) for hl in heads],
        axis=1).astype(BF16)


def _gla_block_mask():
    return (_iota((GLA_K_W, GLA_W), 0) // GLA_DK) == (_iota((GLA_K_W, GLA_W), 1) // GLA_DV)


def _gla_state_in(s, blk):
    return jnp.where(blk, jnp.concatenate([s] * GLA_HEADS, axis=1), 0.0)


def _gla_state_out(s_bd):
    heads = [s_bd[:, h * GLA_DV:(h + 1) * GLA_DV] for h in range(GLA_HEADS)]
    return (heads[0] + heads[1]) + (heads[2] + heads[3])


def _lane_tiled_t(x):
    pad = LANES - x.shape[0]
    if pad:
        x = jnp.concatenate([x, jnp.zeros((pad, x.shape[1]), x.dtype)], axis=0)
    return x.T


def _gla_seq_kernel(u_ref, s0_ref, gkup_ref, gkb_ref, gn_ref, o_ref, sout_ref, s_scr):
    c = pl.program_id(1)
    n_seq, rows = u_ref.shape[0], u_ref.shape[1]
    blk = _gla_block_mask()

    @pl.when(c == 0)
    def _():
        for b in range(n_seq):
            s_scr[b] = _gla_state_in(s0_ref[b], blk)

    m = _chunk_masks(rows, rows, GLA_HEADS)
    kmasks = _head_masks(rows, GLA_DK, GLA_HEADS)
    vmasks = _head_masks(rows, GLA_DV, GLA_HEADS)
    tok = _gla_prep([u_ref[b] for b in range(n_seq)], gkup_ref[...], gkb_ref[...], m["sums"])
    s_old = [s_scr[b] for b in range(n_seq)]
    intra = [_gla_intra(q_i, k_i, v, m, kmasks, vmasks) for q_i, k_i, _, v, _, _ in tok]
    inter = [_mm(t[0], s) for t, s in zip(tok, s_old)]
    kv = [_mm_tn(k_e, v) for _, _, k_e, v, _, _ in tok]
    for b in range(n_seq):
        gate, g_end = tok[b][4], tok[b][5]
        o_ref[b] = _gla_out(intra[b] + inter[b], gate, gn_ref[...])
        dec = jnp.concatenate([_lane_tiled_t(jnp.broadcast_to(g_end[0:1], (LANES, GLA_K_W)))] * GLA_HEADS,
                              axis=1)
        s_scr[b] = dec * s_old[b] + jnp.where(blk, kv[b], 0.0)

    @pl.when(c == pl.num_programs(1) - 1)
    def _():
        for b in range(n_seq):
            sout_ref[b] = _gla_state_out(s_scr[b])


def _gla_dec_kernel(u_ref, s0_ref, gkup_ref, gkb_ref, gn_ref, o_ref, sout_ref, *, seq_len):
    rows = u_ref.shape[0]
    n_seq = rows // seq_len
    blk = _gla_block_mask()
    m = _chunk_masks(rows, seq_len, GLA_HEADS)
    q_i, k_i, k_e, v, gate, g_end = _gla_prep([u_ref[...]], gkup_ref[...], gkb_ref[...], m["sums"])[0]
    o = _gla_intra(q_i, k_i, v, m, _head_masks(rows, GLA_DK, GLA_HEADS),
                   _head_masks(rows, GLA_DV, GLA_HEADS))
    k_et = _lane_tiled_t(k_e).astype(BF16)
    dec_t = _lane_tiled_t(g_end)
    v_pad = v if rows == LANES else jnp.concatenate([v, jnp.zeros((LANES - rows, GLA_W), F32)], axis=0)
    row = _iota((rows, 1), 0)
    row_pad = _iota((LANES, 1), 0)
    for j in range(n_seq):
        s_bd = _gla_state_in(s0_ref[j], blk)
        o = o + _mm(jnp.where(row // seq_len == j, q_i, 0.0), s_bd)
        kv = _mm(k_et, jnp.where(row_pad // seq_len == j, v_pad, 0.0))
        first = j * seq_len
        sout_ref[j] = _gla_state_out(dec_t[:, first:first + 1] * s_bd + jnp.where(blk, kv, 0.0))
    o_ref[...] = _gla_out(o, gate, gn_ref[...])


def _gla_param_specs():
    return [_const_spec((LANES, GLA_K_W)), _const_spec((1, GLA_K_W)), _const_spec((1, GLA_DV))]


def _gla_seq(u3, s0, params, rows, n_seq):
    b, t, _ = u3.shape
    sdim = GLA_HEADS * GLA_DK
    state_spec = pl.BlockSpec((n_seq, sdim, GLA_DV), lambda i, j: (i, 0, 0))
    return pl.pallas_call(
        _gla_seq_kernel,
        grid=(b // n_seq, t // rows),
        in_specs=[pl.BlockSpec((n_seq, rows, GLA_COLS), lambda i, j: (i, j, 0)), state_spec]
        + _gla_param_specs(),
        out_specs=[pl.BlockSpec((n_seq, rows, GLA_W), lambda i, j: (i, j, 0)), state_spec],
        out_shape=[
            jax.ShapeDtypeStruct((b, t, GLA_W), BF16),
            jax.ShapeDtypeStruct((b, sdim, GLA_DV), F32),
        ],
        scratch_shapes=[pltpu.VMEM((n_seq, sdim, GLA_W), F32)],
        compiler_params=pltpu.CompilerParams(
            dimension_semantics=("arbitrary", "arbitrary"), vmem_limit_bytes=VMEM_LIMIT),
        name="gla_seq",
    )(u3, s0, *params)


def _gla_dec(u2, s0, params, seq_len, n_seq):
    n = u2.shape[0]
    rows = n_seq * seq_len
    sdim = GLA_HEADS * GLA_DK
    state_spec = pl.BlockSpec((n_seq, sdim, GLA_DV), lambda i: (i, 0, 0))
    return pl.pallas_call(
        functools.partial(_gla_dec_kernel, seq_len=seq_len),
        grid=(n // rows,),
        in_specs=[pl.BlockSpec((rows, GLA_COLS), lambda i: (i, 0)), state_spec] + _gla_param_specs(),
        out_specs=[pl.BlockSpec((rows, GLA_W), lambda i: (i, 0)), state_spec],
        out_shape=[
            jax.ShapeDtypeStruct((n, GLA_W), BF16),
            jax.ShapeDtypeStruct(s0.shape, F32),
        ],
        compiler_params=pltpu.CompilerParams(
            dimension_semantics=("arbitrary",), vmem_limit_bytes=VMEM_LIMIT),
        name="gla_dec",
    )(u2, s0, *params)


def _seg_sum(x, seg, pa=1):
    return jnp.concatenate(
        [_mm(x[:, i:i + SEG_W], seg, NN, pa, 1) for i in range(0, RWKV_W, SEG_W)], axis=1)


W_OFFSET_SCALE = 0.6065306597126334


def _rwkv_tokens(u, prev, mu, w0, w2, a0, a2, g2, k_k, k_a, rk, seg):
    xr = u + mu * (prev - u)
    r = xr[:, 0:RWKV_W]
    kr = xr[:, RWKV_W:2 * RWKV_W]
    vr = xr[:, 2 * RWKV_W:3 * RWKV_W]
    wa = xr[:, 3 * RWKV_W:3 * RWKV_W + LANES]
    gd = xr[:, 3 * RWKV_W + LANES:RWKV_PROJ]
    lw = _sigmoid(w0 + _mm(_tanh(wa), w2)) * (-W_OFFSET_SCALE)
    a_sig = _sigmoid(a0 + _mm(wa, a2))
    gate = _mm(_sigmoid(gd), g2)
    kk = kr * k_k
    kk = kk * lax.rsqrt(jnp.maximum(_seg_sum(kk * kk, seg), 1e-24))
    kr = kr * (1.0 + (a_sig - 1.0) * k_a)
    bonus = _seg_sum(r * kr * rk, seg) * vr
    return r, kr, vr, lw, -kk, kk * a_sig, gate, bonus


def _rwkv_prep(u, prev, mu, w0, w2, a0, a2, g2, k_k, k_a, rk, seg, sums):
    rows = u.shape[0]
    r, kr, vr, lw, a_vec, b_vec, gate, bonus = _rwkv_tokens(
        u, prev, mu, w0, w2, a0, a2, g2, k_k, k_a, rk, seg)
    block = sums.shape[1]
    cums = [_mm(sums, lw[i:i + block], NN, 1, 2) for i in range(0, rows, block)]
    cum = jnp.concatenate([x[:block] for x in cums], axis=0)
    cum_end = jnp.concatenate([x[block:] for x in cums], axis=0)
    e_neg = jnp.exp(-cum)
    g_end = jnp.exp(cum_end)
    e_end = g_end * e_neg
    bf = lambda x: x.astype(BF16)
    return (bf(a_vec * jnp.exp(cum - lw)), bf(r * jnp.exp(cum)), bf(kr * e_neg), bf(b_vec * e_neg),
            bf(kr * e_end), bf(b_vec * e_end), bf(vr), gate, bonus, g_end)


def _rwkv_intra(units, m, hmasks, n_double):
    rows = units[0][0].shape[0]
    stack = lambda x: _stack_heads(x, hmasks)
    cat_w = m["incl"].shape[1]
    g = [_mm(jnp.concatenate([at, rt], axis=0), jnp.concatenate([stack(bt), stack(kt)], axis=0), NT)
         for at, rt, kt, bt, _ in units]
    g = [jnp.where(m["quad"], x, 0.0) for x in g]
    a_ab = [x[:rows, :cat_w] for x in g]
    a_rb = [x[rows:, :cat_w] for x in g]
    a_ak = [x[:rows, cat_w:] for x in g]
    a_rk = [x[rows:, cat_w:] for x in g]
    tinv = [m["eye"] + a for a in a_ab]
    apow = [_mm(a, _block_diag(a, m["blk"])) for a in a_ab]
    for _ in range(n_double - 1):
        both = [_mm(jnp.concatenate([t, a], axis=0), _block_diag(a, m["blk"])) for t, a in zip(tinv, apow)]
        tinv = [t + x[:rows] for t, x in zip(tinv, both)]
        apow = [x[rows:] for x in both]
    tinv = [t + _mm(t, _block_diag(a, m["blk"])) for t, a in zip(tinv, apow)]
    v_s = [stack(vp) for _, _, _, _, vp in units]
    akv = [_mm(jnp.concatenate([a, b], axis=0), v) for a, b, v in zip(a_ak, a_rk, v_s)]
    wu = [_mm(t, jnp.concatenate([stack(at), stack(x[:rows])], axis=1))
          for t, (at, _, _, _, _), x in zip(tinv, units, akv)]
    z = [_mm(a, jnp.concatenate([stack(x[:, :GROUP_W]), stack(x[:, GROUP_W:])], axis=1))
         for a, x in zip(a_rb, wu)]
    return [(x[:, :GROUP_W], x[:, GROUP_W:], rt + zz[:, :GROUP_W], zz[:, GROUP_W:] + kv[rows:])
            for x, (_, rt, _, _, _), zz, kv in zip(wu, units, z, akv)]


def _rwkv_out(y, gate, bonus, lnw, lnb, seg):
    inv = 1.0 / RWKV_HEAD
    yc = y - _seg_sum(y, seg, 2) * inv
    var = _seg_sum(yc * yc, seg) * inv
    return ((yc * lax.rsqrt(var + RWKV_GN_EPS) * lnw + lnb + bonus) * gate).astype(BF16)


def _rwkv_state_in(s_ref, idx, g):
    zero = jnp.zeros((RWKV_HEAD, RWKV_HEAD), F32)
    blocks = []
    for h in range(GROUP_HEADS):
        parts = [zero] * GROUP_HEADS
        parts[h] = s_ref[idx, g * GROUP_HEADS + h]
        blocks.append(jnp.concatenate(parts, axis=1))
    return jnp.concatenate(blocks, axis=0)


def _rwkv_state_out(s_ref, idx, g, s2):
    for h in range(GROUP_HEADS):
        sl = slice(h * RWKV_HEAD, (h + 1) * RWKV_HEAD)
        s_ref[idx, g * GROUP_HEADS + h] = s2[sl, sl]


def _head_block_mask():
    return (_iota((GROUP_W, GROUP_W), 0) // RWKV_HEAD) == (_iota((GROUP_W, GROUP_W), 1) // RWKV_HEAD)


def _rwkv_seq_kernel(u_ref, shift0_ref, s0_ref, mu_ref, w0_ref, w2_ref, a0_ref, a2_ref,
                     g2_ref, kk_ref, ka_ref, rk_ref, lnw_ref, lnb_ref, seg_ref,
                     o_ref, sout_ref, s_scr, prev_scr):
    c = pl.program_id(1)
    n_seq, rows = u_ref.shape[0], u_ref.shape[1]
    groups = [slice(g * GROUP_W, (g + 1) * GROUP_W) for g in range(RWKV_GROUPS)]
    ids = [(b, g) for b in range(n_seq) for g in range(RWKV_GROUPS)]

    @pl.when(c == 0)
    def _():
        for b in range(n_seq):
            for g in range(RWKV_GROUPS):
                s_scr[b, g] = _rwkv_state_in(s0_ref, b, g)
            prev_scr[b] = shift0_ref[b]

    seg = seg_ref[...]
    m = _chunk_masks(rows, rows, GROUP_HEADS)
    hmasks = _head_masks(rows, RWKV_HEAD, GROUP_HEADS)
    head_blk = _head_block_mask()
    n_double = rows.bit_length() - 2

    us = [u_ref[b] for b in range(n_seq)]
    prev = jnp.concatenate([_shift_rows(u, prev_scr[b]) for b, u in enumerate(us)], axis=0)
    for b, u in enumerate(us):
        prev_scr[b] = u[rows - 1:rows]
    at, rt, kt, bt, ke, be, vb, gate, bonus, g_end = _rwkv_prep(
        jnp.concatenate(us, axis=0), prev, mu_ref[...], w0_ref[...], w2_ref[...], a0_ref[...],
        a2_ref[...], g2_ref[...], kk_ref[...], ka_ref[...], rk_ref[...], seg, m["sums"])
    tok, units = [], []
    for b in range(n_seq):
        sl = slice(b * rows, (b + 1) * rows)
        tok.append((vb[sl], ke[sl], be[sl], g_end[b * rows:b * rows + 1]))
        units += [(at[sl, gl], rt[sl, gl], kt[sl, gl], bt[sl, gl], vb[sl, gl]) for gl in groups]

    intra = _rwkv_intra(units, m, hmasks, n_double)
    s_old = [s_scr[b, g] for b, g in ids]
    uy = [_mm(jnp.concatenate([w_m, r_m], axis=0), s2, NT) for (w_m, _, r_m, _), s2 in zip(intra, s_old)]
    upd = []
    for (b, g), (_, u0, _, _), x in zip(ids, intra, uy):
        vb_b, ke_b, be_b, _ = tok[b]
        gl = groups[g]
        upd.append(_mm_tn(jnp.concatenate([x[:rows] + u0, vb_b[:, gl].astype(F32)], axis=0),
                          jnp.concatenate([be_b[:, gl], ke_b[:, gl]], axis=0)))
    for (b, g), s2, d in zip(ids, s_old, upd):
        s_scr[b, g] = s2 * tok[b][3][:, groups[g]] + jnp.where(head_blk, d, 0.0)
    ys = [jnp.concatenate([uy[i][rows:] + intra[i][3] for i, (bb, _) in enumerate(ids) if bb == b],
                          axis=1) for b in range(n_seq)]
    out = _rwkv_out(jnp.concatenate(ys, axis=0), gate, bonus, lnw_ref[...], lnb_ref[...], seg)
    for b in range(n_seq):
        o_ref[b] = out[b * rows:(b + 1) * rows]

    @pl.when(c == pl.num_programs(1) - 1)
    def _():
        for b, g in ids:
            _rwkv_state_out(sout_ref, b, g, s_scr[b, g])


STEP_HEADS = 2
STEP_ROWS = 4
N_STEP_OPERANDS = 6


def _sum_keys(x):
    t = x[0:SUBLANES]
    for i in range(SUBLANES, x.shape[0], SUBLANES):
        t = t + x[i:i + SUBLANES]
    for shift in (4, 2, 1):
        t = t + pltpu.roll(t, shift, axis=0)
    return t


def _rwkv_step_kernel(u_ref, shiftx_ref, s0_ref, mu_ref, w0_ref, w2_ref, a0_ref, a2_ref, g2_ref,
                      kk_ref, ka_ref, rk_ref, lnw_ref, lnb_ref, seg_ref,
                      o_ref, sout_ref, ops_scr, y_scr, tmp_scr, aux_scr, *, seq_len):
    p = pl.program_id(0)
    n_tok = u_ref.shape[0]
    n_seq = n_tok // seq_len
    seg = seg_ref[...]
    cols = [slice(c, c + LANES) for c in range(0, RWKV_W, LANES)]

    @pl.when(p == 0)
    def _():
        u = u_ref[...]
        row = _iota((n_tok, 1), 0)
        prev = jnp.where(row % seq_len == 0, shiftx_ref[...], pltpu.roll(u, 1, axis=0))
        r, kr, vr, lw, a_vec, b_vec, gate, bonus = _rwkv_tokens(
            u, prev, mu_ref[...], w0_ref[...], w2_ref[...], a0_ref[...], a2_ref[...], g2_ref[...],
            kk_ref[...], ka_ref[...], rk_ref[...], seg)
        aux_scr[0] = gate
        aux_scr[1] = bonus
        for i, x in enumerate((jnp.exp(lw), a_vec, b_vec, kr, r, vr)):
            for c, cl in enumerate(cols):
                tmp_scr[c] = x[:, cl]
            for t in range(seq_len):
                for c, cl in enumerate(cols):
                    ops_scr[i, t, cl, :] = tmp_scr[c, pl.ds(t, n_seq, stride=seq_len), :].T

    def rows_step(i, carry):
        for hh in range(STEP_HEADS):
            chan = pl.multiple_of((p * STEP_HEADS + hh) * RWKV_HEAD, RWKV_HEAD)
            for j in range(STEP_ROWS):
                v = i * STEP_ROWS + j
                s = s0_ref[hh, v]
                for t in range(seq_len):
                    w, a, b, k, r = (ops_scr[n, t, pl.ds(chan, RWKV_HEAD), :] for n in range(5))
                    v_t = ops_scr[5, t, pl.ds(chan + v, 1), :]
                    sa = jnp.concatenate([_sum_keys(s * a)] * (RWKV_HEAD // SUBLANES), axis=0)
                    s = s * w + sa * b + v_t * k
                    y_scr[t, pl.ds(chan + v, 1), :] = _sum_keys(s * r)[0:1]
                sout_ref[hh, v] = s
        return carry

    lax.fori_loop(0, RWKV_HEAD // STEP_ROWS, rows_step, 0)

    @pl.when(p == pl.num_programs(0) - 1)
    def _():
        for t in range(seq_len):
            for c, cl in enumerate(cols):
                tmp_scr[c, pl.ds(t, n_seq, stride=seq_len), :] = y_scr[t, cl, :].T
        y = jnp.concatenate([tmp_scr[c] for c in range(len(cols))], axis=1)
        o_ref[...] = _rwkv_out(y, aux_scr[0], aux_scr[1], lnw_ref[...], lnb_ref[...], seg)


def _rwkv_step(u2, shiftx, s0_t, params, seq_len):
    n = u2.shape[0]
    n_seq = n // seq_len
    state_spec = pl.BlockSpec((STEP_HEADS, RWKV_HEAD, RWKV_HEAD, n_seq), lambda i: (i, 0, 0, 0))
    whole = lambda w: pl.BlockSpec((n, w), lambda i: (0, 0), pipeline_mode=pl.Buffered(1))
    return pl.pallas_call(
        functools.partial(_rwkv_step_kernel, seq_len=seq_len),
        grid=(RWKV_HEADS // STEP_HEADS,),
        in_specs=[whole(RWKV_PROJ), whole(RWKV_PROJ), state_spec] + _rwkv_param_specs(),
        out_specs=[pl.BlockSpec((n, RWKV_W), lambda i: (0, 0)), state_spec],
        out_shape=[
            jax.ShapeDtypeStruct((n, RWKV_W), BF16),
            jax.ShapeDtypeStruct(s0_t.shape, F32),
        ],
        scratch_shapes=[
            pltpu.VMEM((N_STEP_OPERANDS, seq_len, RWKV_W, n_seq), F32),
            pltpu.VMEM((seq_len, RWKV_W, n_seq), F32),
            pltpu.VMEM((RWKV_W // LANES, n, LANES), F32),
            pltpu.VMEM((2, n, RWKV_W), F32),
        ],
        compiler_params=pltpu.CompilerParams(
            dimension_semantics=("arbitrary",), vmem_limit_bytes=VMEM_LIMIT),
        name="rwkv_step",
    )(u2, shiftx, s0_t, *params)


def _rwkv_param_specs():
    vec = lambda n: _const_spec((1, n))
    return [
        vec(RWKV_PROJ),
        vec(RWKV_W),
        _const_spec((LANES, RWKV_W)),
        vec(RWKV_W),
        _const_spec((LANES, RWKV_W)),
        _const_spec((LANES, RWKV_W)),
        vec(RWKV_W), vec(RWKV_W), vec(RWKV_W), vec(RWKV_W), vec(RWKV_W),
        _const_spec((SEG_W, SEG_W)),
    ]


def _rwkv_seq(u3, shift0, s0, params, rows, n_seq):
    b, t, _ = u3.shape
    state_spec = pl.BlockSpec((None, n_seq, RWKV_HEADS, RWKV_HEAD, RWKV_HEAD),
                              lambda i, j: (0, i, 0, 0, 0))
    return pl.pallas_call(
        _rwkv_seq_kernel,
        grid=(b // n_seq, t // rows),
        in_specs=[
            pl.BlockSpec((n_seq, rows, RWKV_PROJ), lambda i, j: (i, j, 0)),
            pl.BlockSpec((n_seq, 1, RWKV_PROJ), lambda i, j: (i, 0, 0)),
            state_spec,
        ] + _rwkv_param_specs(),
        out_specs=[pl.BlockSpec((n_seq, rows, RWKV_W), lambda i, j: (i, j, 0)), state_spec],
        out_shape=[
            jax.ShapeDtypeStruct((b, t, RWKV_W), BF16),
            jax.ShapeDtypeStruct(s0.shape, F32),
        ],
        scratch_shapes=[
            pltpu.VMEM((n_seq, RWKV_GROUPS, GROUP_W, GROUP_W), F32),
            pltpu.VMEM((n_seq, 1, RWKV_PROJ), F32),
        ],
        compiler_params=pltpu.CompilerParams(
            dimension_semantics=("arbitrary", "arbitrary"), vmem_limit_bytes=VMEM_LIMIT),
        name="rwkv_seq",
    )(u3, shift0, s0, *params)


FF_CHUNK = D_FF // 2


def _post_kernel(x_ref, og_ref, or_ref, p_ref, wo_ref, nffn_ref, wg_ref, wu_ref, wd_ref,
                 nple_ref, wpg_ref, wpp_ref, nf_ref, y_ref):
    half = x_ref.shape[0] // 2
    parts = [slice(0, half), slice(half, 2 * half)]
    dot = lambda a, b: jnp.dot(a, b, preferred_element_type=F32)
    o = [jnp.concatenate([og_ref[p, :], or_ref[p, :]], axis=1) for p in parts]
    x = [x_ref[p, :] + dot(oo, wo_ref[...]) for p, oo in zip(parts, o)]
    h2 = [_rms(xx, nffn_ref[...]).astype(BF16) for xx in x]
    for i in range(0, D_FF, FF_CHUNK):
        gate = [dot(h, wg_ref[:, i:i + FF_CHUNK]) for h in h2]
        up = [dot(h, wu_ref[:, i:i + FF_CHUNK]) for h in h2]
        act = [(g * _sigmoid(g) * u).astype(BF16) for g, u in zip(gate, up)]
        x = [xx + dot(a, wd_ref[i:i + FF_CHUNK, :]) for xx, a in zip(x, act)]
    h3 = [_rms(xx, nple_ref[...]).astype(BF16) for xx in x]
    pg = [_sigmoid(dot(h, wpg_ref[...])) for h in h3]
    pp = [dot(p_ref[p, :].astype(BF16), wpp_ref[...]) for p in parts]
    for p, xx, a, b in zip(parts, x, pg, pp):
        y_ref[p, :] = _rms(xx + a * b, nf_ref[...])


def _post(x2d, og, orw, p2d, weights, tm):
    n = x2d.shape[0]
    wo, nffn, wg, wu, wd, nple, wpg, wpp, nf = weights
    tok = lambda w: pl.BlockSpec((tm, w), lambda i: (i, 0))
    return pl.pallas_call(
        _post_kernel,
        grid=(n // tm,),
        in_specs=[
            tok(D_MODEL), tok(GLA_W), tok(RWKV_W), tok(PLE_DIM),
            _const_spec((D_MODEL, D_MODEL)), _const_spec((1, D_MODEL)),
            _const_spec((D_MODEL, D_FF)), _const_spec((D_MODEL, D_FF)), _const_spec((D_FF, D_MODEL)),
            _const_spec((1, D_MODEL)), _const_spec((D_MODEL, D_MODEL)), _const_spec((PLE_DIM, D_MODEL)),
            _const_spec((1, D_MODEL)),
        ],
        out_specs=tok(D_MODEL),
        out_shape=jax.ShapeDtypeStruct((n, D_MODEL), F32),
        compiler_params=pltpu.CompilerParams(
            dimension_semantics=("arbitrary",), vmem_limit_bytes=VMEM_LIMIT),
        name="post",
    )(x2d, og, orw, p2d, wo, nffn, wg, wu, wd, nple, wpg, wpp, nf)


PROMPT_CHUNK = 64
PROMPT_SEQS_PER_STEP = 8
DEC_TILE_SEQS = 32
TOKEN_TILE = 512
PROJ_TILE = 1024


def kernel(x_prompt, x_sample, state_gla, state_rwkv, state_shift, p_prompt, p_sample, norm_mix, w_in, gla_gk_up, gla_gk_bias, gla_norm, rwkv_mu, rwkv_w0, rwkv_w2, rwkv_a0, rwkv_a2, rwkv_g2, rwkv_k_k, rwkv_k_a, rwkv_r_k, rwkv_ln_w, rwkv_ln_b, w_out, norm_ffn, w_gate, w_up, w_down, norm_ple, w_ple_gate, w_ple_proj, norm_final):
    assert w_in.shape[0] == 1
    i = 0
    rowv = lambda a: a.astype(F32).reshape(1, -1)
    zeros = lambda r, c: jnp.zeros((r, c), F32)
    w_in_i = w_in[i]
    w_gla = jnp.concatenate(
        [w_in_i[:, :GLA_PROJ], zeros(D_MODEL, GLA_COLS - GLA_PROJ)], axis=1).astype(BF16)
    w_rw = w_in_i[:, GLA_PROJ:].astype(BF16)
    gk_up = jnp.concatenate(
        [gla_gk_up[i].astype(F32), zeros(LANES - GLA_GATE_RANK, GLA_K_W)], axis=0).astype(BF16)
    seg = jnp.arange(SEG_W)[:, None] // RWKV_HEAD == jnp.arange(SEG_W)[None, :] // RWKV_HEAD
    rwkv_params = (
        rowv(rwkv_mu[i]), rowv(rwkv_w0[i]),
        jnp.concatenate([rwkv_w2[i].astype(F32), zeros(64, RWKV_W)], axis=0).astype(BF16),
        rowv(rwkv_a0[i]),
        jnp.concatenate([zeros(64, RWKV_W), rwkv_a2[i].astype(F32)], axis=0).astype(BF16),
        rwkv_g2[i].astype(BF16),
        rowv(rwkv_k_k[i]), rowv(rwkv_k_a[i]), rowv(rwkv_r_k[i]), rowv(rwkv_ln_w[i]), rowv(rwkv_ln_b[i]),
        seg.astype(BF16),
    )
    post_w = (
        w_out[i].astype(BF16), rowv(norm_ffn[i]), w_gate[i].astype(BF16), w_up[i].astype(BF16),
        w_down[i].astype(BF16), rowv(norm_ple[i]), w_ple_gate[i].astype(BF16),
        w_ple_proj[i].astype(BF16), rowv(norm_final),
    )
    gla_w = (gk_up, rowv(gla_gk_bias[i]), rowv(gla_norm[i]))
    g_mix = rowv(norm_mix[i])

    bp, tp, _ = x_prompt.shape
    xp = x_prompt.astype(F32).reshape(bp * tp, D_MODEL)
    ug, ur = _proj(xp, g_mix, w_gla, w_rw, PROJ_TILE)
    og, gla_p = _gla_seq(ug.reshape(bp, tp, GLA_COLS),
                         jnp.zeros((bp, GLA_HEADS * GLA_DK, GLA_DV), F32), gla_w,
                         PROMPT_CHUNK, PROMPT_SEQS_PER_STEP)
    ur3 = ur.reshape(bp, tp, RWKV_PROJ)
    orw, rwkv_p = _rwkv_seq(ur3, jnp.zeros((bp, 1, RWKV_PROJ), F32),
                            jnp.zeros((1, bp, RWKV_HEADS, RWKV_HEAD, RWKV_HEAD), F32), rwkv_params,
                            PROMPT_CHUNK, PROMPT_SEQS_PER_STEP)
    shift_p = ur3[:, tp - 1]
    yp = _post(xp, og.reshape(bp * tp, GLA_W), orw.reshape(bp * tp, RWKV_W),
               p_prompt[i].reshape(bp * tp, PLE_DIM), post_w, TOKEN_TILE)

    bs, ts, _ = x_sample.shape
    xs = x_sample.astype(F32).reshape(bs * ts, D_MODEL)
    ug, ur = _proj(xs, g_mix, w_gla, w_rw, TOKEN_TILE)
    og, gla_s = _gla_dec(ug, state_gla[i].astype(F32).reshape(bs, GLA_HEADS * GLA_DK, GLA_DV),
                         gla_w, ts, DEC_TILE_SEQS)
    shiftx = jnp.pad(state_shift[i].astype(F32)[:, None, :], ((0, 0), (0, ts - 1), (0, 0)))
    orw, rwkv_s = _rwkv_step(ur, shiftx.reshape(bs * ts, RWKV_PROJ),
                             jnp.transpose(state_rwkv[i].astype(F32), (1, 2, 3, 0)), rwkv_params, ts)
    rwkv_s = jnp.transpose(rwkv_s, (3, 0, 1, 2))[None]
    shift_s = ur.reshape(bs, ts, RWKV_PROJ)[:, ts - 1]
    ys = _post(xs, og, orw, p_sample[i].reshape(bs * ts, PLE_DIM), post_w, TOKEN_TILE)

    gla_shape = (1, -1, GLA_HEADS, GLA_DK, GLA_DV)
    return (yp.reshape(bp, tp, D_MODEL).astype(x_prompt.dtype),
            ys.reshape(bs, ts, D_MODEL).astype(x_sample.dtype),
            gla_p.reshape(gla_shape).astype(state_gla.dtype), rwkv_p.astype(state_rwkv.dtype),
            shift_p[None].astype(state_shift.dtype),
            gla_s.reshape(gla_shape).astype(state_gla.dtype), rwkv_s.astype(state_rwkv.dtype),
            shift_s[None].astype(state_shift.dtype))
```

```python
import functools

import jax
import jax.numpy as jnp
from jax import lax
from jax.experimental import pallas as pl
from jax.experimental.pallas import tpu as pltpu

F32 = jnp.float32
BF16 = jnp.bfloat16

D_MODEL = 1024
GLA_HEADS = 4
GLA_DK = 64
GLA_DV = 128
GLA_K_W = GLA_HEADS * GLA_DK
GLA_W = GLA_HEADS * GLA_DV
GLA_GATE_RANK = 16
GLA_GATE_NORM = 16.0
GLA_MAIN = 2 * GLA_K_W + 2 * GLA_W
GLA_PROJ = GLA_MAIN + GLA_GATE_RANK
LANES = 128
SUBLANES = 8
GLA_COLS = GLA_MAIN + LANES
RWKV_HEAD = 64
RWKV_HEADS = 8
RWKV_W = RWKV_HEADS * RWKV_HEAD
RWKV_PROJ = 3 * RWKV_W + 64 + 64 + 128
D_FF = 2816
PLE_DIM = 256
EPS = 1e-6
RWKV_GN_EPS = 64e-5

VMEM_LIMIT = 56 * 1024 * 1024

NN = ((1,), (0,))
NT = ((1,), (1,))


def _split(x, n):
    parts = []
    r = x
    for i in range(n):
        p = r.astype(BF16)
        parts.append(p)
        if i + 1 < n:
            r = r - p.astype(F32)
    return parts


def _mm(a, b, dims=NN, pa=1, pb=1):
    pieces_a = _split(a, pa)
    pieces_b = _split(b, pb)
    n = max(pa, pb)
    acc = None
    for i, ai in enumerate(pieces_a):
        for j, bj in enumerate(pieces_b):
            if i + j < n:
                t = lax.dot_general(ai, bj, (dims, ((), ())), preferred_element_type=F32)
                acc = t if acc is None else acc + t
    return acc


def _mm_tn(a, b, pa=1, pb=1):
    rows = a.shape[0]
    pad = (-rows) % LANES
    if pad:
        a = jnp.concatenate([a, jnp.zeros((pad, a.shape[1]), a.dtype)], axis=0)
        b = jnp.concatenate([b, jnp.zeros((pad, b.shape[1]), b.dtype)], axis=0)
    return _mm(a.T, b, NN, pa, pb)


def _iota(shape, dim):
    return lax.broadcasted_iota(jnp.int32, shape, dim)


def _log_sigmoid(z):
    return jnp.minimum(z, 0.0) - jnp.log(1.0 + jnp.exp(-jnp.abs(z)))


def _shift_rows(u, first):
    r = pltpu.roll(u, 1, axis=0)
    head = jnp.where(_iota((SUBLANES, 1), 0) == 0, first, r[:SUBLANES])
    return jnp.concatenate([head, r[SUBLANES:]], axis=0)


def _sigmoid(z):
    return 1.0 / (1.0 + jnp.exp(-z))


def _tanh(z):
    return 2.0 * _sigmoid(2.0 * z) - 1.0


def _rms(x, g):
    return x * lax.rsqrt(jnp.mean(x * x, axis=-1, keepdims=True) + EPS) * g


GROUP_HEADS = 2
GROUP_W = GROUP_HEADS * RWKV_HEAD
RWKV_GROUPS = RWKV_W // GROUP_W
SEG_W = 2 * LANES


def _head_masks(rows, head_w, heads):
    lane = _iota((rows, heads * head_w), 1)
    return [jnp.where(lane // head_w == h, 1.0, 0.0).astype(BF16) for h in range(heads)]


def _stack_heads(x, hmasks):
    xb = x.astype(BF16)
    return jnp.concatenate([xb * m for m in hmasks], axis=0)


def _block_diag(x_cat, blk):
    heads = blk.shape[0] // x_cat.shape[0]
    return jnp.concatenate([x_cat.astype(BF16)] * heads, axis=0) * blk


def _chunk_masks(rows, seq_len, heads):
    t = _iota((rows, heads * rows), 0)
    s = _iota((rows, heads * rows), 1) % rows
    incl = t >= s
    t2 = _iota((2 * rows, rows), 0)
    s2 = _iota((2 * rows, rows), 1)
    cum_rows = (t2 < rows) & (t2 >= s2)
    tot_rows = t2 >= rows
    if seq_len < rows:
        incl = incl & ((t // seq_len) == (s // seq_len))
        same2 = ((t2 % rows) // seq_len) == (s2 // seq_len)
        cum_rows, tot_rows = cum_rows & same2, tot_rows & same2
    big = heads * rows
    blk = jnp.where(_iota((big, big), 0) // rows == _iota((big, big), 1) // rows, 1.0, 0.0)
    t4 = _iota((2 * rows, 2 * big), 0)
    s4 = _iota((2 * rows, 2 * big), 1) % rows
    quad = ((t4 % rows) > s4) | ((t4 >= rows) & ((t4 % rows) == s4))
    if seq_len < rows:
        quad = quad & (((t4 % rows) // seq_len) == (s4 // seq_len))
    return dict(incl=incl, quad=quad, eye=jnp.where(t == s, 1.0, 0.0),
                sums=jnp.where(cum_rows | tot_rows, 1.0, 0.0).astype(BF16), blk=blk.astype(BF16))


def _proj_kernel(x_ref, g_ref, wg_ref, wr_ref, ug_ref, ur_ref):
    h = _rms(x_ref[...], g_ref[...]).astype(BF16)
    ug_ref[...] = jnp.dot(h, wg_ref[...], preferred_element_type=F32)
    ur_ref[...] = jnp.dot(h, wr_ref[...], preferred_element_type=F32)


def _const_spec(shape):
    return pl.BlockSpec(shape, lambda *_: (0,) * len(shape), pipeline_mode=pl.Buffered(1))


def _proj(x2d, g, w_gla, w_rw, tm):
    n = x2d.shape[0]
    return pl.pallas_call(
        _proj_kernel,
        grid=(n // tm,),
        in_specs=[
            pl.BlockSpec((tm, D_MODEL), lambda i: (i, 0)),
            _const_spec((1, D_MODEL)),
            _const_spec((D_MODEL, GLA_COLS)),
            _const_spec((D_MODEL, RWKV_PROJ)),
        ],
        out_specs=[
            pl.BlockSpec((tm, GLA_COLS), lambda i: (i, 0)),
            pl.BlockSpec((tm, RWKV_PROJ), lambda i: (i, 0)),
        ],
        out_shape=[
            jax.ShapeDtypeStruct((n, GLA_COLS), F32),
            jax.ShapeDtypeStruct((n, RWKV_PROJ), F32),
        ],
        compiler_params=pltpu.CompilerParams(
            dimension_semantics=("arbitrary",), vmem_limit_bytes=VMEM_LIMIT),
        name="proj",
    )(x2d, g, w_gla, w_rw)


def _gla_prep(us, gkup, gkb, sums):
    rows = us[0].shape[0]
    z = [_mm(u[:, GLA_MAIN:GLA_COLS], gkup) for u in us]
    log_a = [_log_sigmoid(x + gkb) * (1.0 / GLA_GATE_NORM) for x in z]
    cums = [_mm(sums, x, NN, 1, 2) for x in log_a]
    out = []
    for u, x in zip(us, cums):
        cum, cum_end = x[:rows], x[rows:]
        q = u[:, 0:GLA_K_W] * (GLA_DK ** -0.5)
        k = u[:, GLA_K_W:2 * GLA_K_W]
        out.append((q * jnp.exp(cum), k * jnp.exp(-cum), k * jnp.exp(cum_end - cum),
                    u[:, 2 * GLA_K_W:2 * GLA_K_W + GLA_W], u[:, 2 * GLA_K_W + GLA_W:GLA_MAIN],
                    jnp.exp(cum_end)))
    return out


def _gla_intra(q_i, k_i, v, m, kmasks, vmasks):
    scores = jnp.where(m["incl"], _mm(q_i, _stack_heads(k_i, kmasks), NT), 0.0)
    return _mm(scores, _stack_heads(v, vmasks))


def _gla_out(o, gate, gnorm):
    heads = [slice(h * GLA_DV, (h + 1) * GLA_DV) for h in range(GLA_HEADS)]
    return jnp.concatenate(
        [_rms(o[:, hl], gnorm) * (gate[:, hl] * _sigmoid(gate[:, hl])) for hl in heads],
        axis=1).astype(BF16)


def _gla_block_mask():
    return (_iota((GLA_K_W, GLA_W), 0) // GLA_DK) == (_iota((GLA_K_W, GLA_W), 1) // GLA_DV)


def _gla_state_in(s, blk):
    return jnp.where(blk, jnp.concatenate([s] * GLA_HEADS, axis=1), 0.0)


def _gla_state_out(s_bd):
    heads = [s_bd[:, h * GLA_DV:(h + 1) * GLA_DV] for h in range(GLA_HEADS)]
    return (heads[0] + heads[1]) + (heads[2] + heads[3])


def _lane_tiled_t(x):
    pad = LANES - x.shape[0]
    if pad:
        x = jnp.concatenate([x, jnp.zeros((pad, x.shape[1]), x.dtype)], axis=0)
    return x.T


def _gla_seq_kernel(u_ref, s0_ref, gkup_ref, gkb_ref, gn_ref, o_ref, sout_ref, s_scr, phase=None):
    c = pl.program_id(1)
    n_seq, rows = u_ref.shape[0], u_ref.shape[1]
    blk = _gla_block_mask()

    def init():
        for b in range(n_seq):
            s_scr[b] = _gla_state_in(s0_ref[b], blk)

    def final():
        for b in range(n_seq):
            sout_ref[b] = _gla_state_out(s_scr[b])

    if phase == "init":
        return init()
    if phase == "final":
        return final()
    if phase is None:
        pl.when(c == 0)(init)

    m = _chunk_masks(rows, rows, GLA_HEADS)
    kmasks = _head_masks(rows, GLA_DK, GLA_HEADS)
    vmasks = _head_masks(rows, GLA_DV, GLA_HEADS)
    tok = _gla_prep([u_ref[b] for b in range(n_seq)], gkup_ref[...], gkb_ref[...], m["sums"])
    s_old = [s_scr[b] for b in range(n_seq)]
    intra = [_gla_intra(q_i, k_i, v, m, kmasks, vmasks) for q_i, k_i, _, v, _, _ in tok]
    inter = [_mm(t[0], s) for t, s in zip(tok, s_old)]
    kv = [_mm_tn(k_e, v) for _, _, k_e, v, _, _ in tok]
    for b in range(n_seq):
        gate, g_end = tok[b][4], tok[b][5]
        o_ref[b] = _gla_out(intra[b] + inter[b], gate, gn_ref[...])
        dec = jnp.concatenate([_lane_tiled_t(jnp.broadcast_to(g_end[0:1], (LANES, GLA_K_W)))] * GLA_HEADS,
                              axis=1)
        s_scr[b] = dec * s_old[b] + jnp.where(blk, kv[b], 0.0)

    if phase is None:
        pl.when(c == pl.num_programs(1) - 1)(final)


def _gla_dec_kernel(u_ref, s0_ref, gkup_ref, gkb_ref, gn_ref, o_ref, sout_ref, *, seq_len):
    rows = u_ref.shape[0]
    n_seq = rows // seq_len
    blk = _gla_block_mask()
    m = _chunk_masks(rows, seq_len, GLA_HEADS)
    q_i, k_i, k_e, v, gate, g_end = _gla_prep([u_ref[...]], gkup_ref[...], gkb_ref[...], m["sums"])[0]
    o = _gla_intra(q_i, k_i, v, m, _head_masks(rows, GLA_DK, GLA_HEADS),
                   _head_masks(rows, GLA_DV, GLA_HEADS))
    k_et = _lane_tiled_t(k_e).astype(BF16)
    dec_t = _lane_tiled_t(g_end)
    v_pad = v if rows == LANES else jnp.concatenate([v, jnp.zeros((LANES - rows, GLA_W), F32)], axis=0)
    row = _iota((rows, 1), 0)
    row_pad = _iota((LANES, 1), 0)
    for j in range(n_seq):
        s_bd = _gla_state_in(s0_ref[j], blk)
        o = o + _mm(jnp.where(row // seq_len == j, q_i, 0.0), s_bd)
        kv = _mm(k_et, jnp.where(row_pad // seq_len == j, v_pad, 0.0))
        first = j * seq_len
        sout_ref[j] = _gla_state_out(dec_t[:, first:first + 1] * s_bd + jnp.where(blk, kv, 0.0))
    o_ref[...] = _gla_out(o, gate, gn_ref[...])


def _gla_param_specs():
    return [_const_spec((LANES, GLA_K_W)), _const_spec((1, GLA_K_W)), _const_spec((1, GLA_DV))]


def _gla_dec(u2, s0, params, seq_len, n_seq):
    n = u2.shape[0]
    rows = n_seq * seq_len
    sdim = GLA_HEADS * GLA_DK
    state_spec = pl.BlockSpec((n_seq, sdim, GLA_DV), lambda i: (i, 0, 0))
    return pl.pallas_call(
        functools.partial(_gla_dec_kernel, seq_len=seq_len),
        grid=(n // rows,),
        in_specs=[pl.BlockSpec((rows, GLA_COLS), lambda i: (i, 0)), state_spec] + _gla_param_specs(),
        out_specs=[pl.BlockSpec((rows, GLA_W), lambda i: (i, 0)), state_spec],
        out_shape=[
            jax.ShapeDtypeStruct((n, GLA_W), BF16),
            jax.ShapeDtypeStruct(s0.shape, F32),
        ],
        compiler_params=pltpu.CompilerParams(
            dimension_semantics=("arbitrary",), vmem_limit_bytes=VMEM_LIMIT),
        name="gla_dec",
    )(u2, s0, *params)


def _seg_sum(x, seg, pa=1):
    return jnp.concatenate(
        [_mm(x[:, i:i + SEG_W], seg, NN, pa, 1) for i in range(0, RWKV_W, SEG_W)], axis=1)


W_OFFSET_SCALE = 0.6065306597126334


def _rwkv_tokens(u, prev, mu, w0, w2, a0, a2, g2, k_k, k_a, rk, seg):
    xr = u + mu * (prev - u)
    r = xr[:, 0:RWKV_W]
    kr = xr[:, RWKV_W:2 * RWKV_W]
    vr = xr[:, 2 * RWKV_W:3 * RWKV_W]
    wa = xr[:, 3 * RWKV_W:3 * RWKV_W + LANES]
    gd = xr[:, 3 * RWKV_W + LANES:RWKV_PROJ]
    lw = _sigmoid(w0 + _mm(_tanh(wa), w2)) * (-W_OFFSET_SCALE)
    a_sig = _sigmoid(a0 + _mm(wa, a2))
    gate = _mm(_sigmoid(gd), g2)
    kk = kr * k_k
    kk = kk * lax.rsqrt(jnp.maximum(_seg_sum(kk * kk, seg), 1e-24))
    kr = kr * (1.0 + (a_sig - 1.0) * k_a)
    bonus = _seg_sum(r * kr * rk, seg) * vr
    return r, kr, vr, lw, -kk, kk * a_sig, gate, bonus


def _rwkv_prep(u, prev, mu, w0, w2, a0, a2, g2, k_k, k_a, rk, seg, sums):
    rows = u.shape[0]
    r, kr, vr, lw, a_vec, b_vec, gate, bonus = _rwkv_tokens(
        u, prev, mu, w0, w2, a0, a2, g2, k_k, k_a, rk, seg)
    block = sums.shape[1]
    cums = [_mm(sums, lw[i:i + block], NN, 1, 2) for i in range(0, rows, block)]
    cum = jnp.concatenate([x[:block] for x in cums], axis=0)
    cum_end = jnp.concatenate([x[block:] for x in cums], axis=0)
    e_neg = jnp.exp(-cum)
    g_end = jnp.exp(cum_end)
    e_end = g_end * e_neg
    bf = lambda x: x.astype(BF16)
    return (bf(a_vec * jnp.exp(cum - lw)), bf(r * jnp.exp(cum)), bf(kr * e_neg), bf(b_vec * e_neg),
            bf(kr * e_end), bf(b_vec * e_end), bf(vr), gate, bonus, g_end)


def _rwkv_intra(units, m, hmasks, n_double):
    rows = units[0][0].shape[0]
    stack = lambda x: _stack_heads(x, hmasks)
    cat_w = m["incl"].shape[1]
    g = [_mm(jnp.concatenate([at, rt], axis=0), jnp.concatenate([stack(bt), stack(kt)], axis=0), NT)
         for at, rt, kt, bt, _ in units]
    g = [jnp.where(m["quad"], x, 0.0) for x in g]
    a_ab = [x[:rows, :cat_w] for x in g]
    a_rb = [x[rows:, :cat_w] for x in g]
    a_ak = [x[:rows, cat_w:] for x in g]
    a_rk = [x[rows:, cat_w:] for x in g]
    tinv = [m["eye"] + a for a in a_ab]
    apow = [_mm(a, _block_diag(a, m["blk"])) for a in a_ab]
    for _ in range(n_double - 1):
        both = [_mm(jnp.concatenate([t, a], axis=0), _block_diag(a, m["blk"])) for t, a in zip(tinv, apow)]
        tinv = [t + x[:rows] for t, x in zip(tinv, both)]
        apow = [x[rows:] for x in both]
    tinv = [t + _mm(t, _block_diag(a, m["blk"])) for t, a in zip(tinv, apow)]
    v_s = [stack(vp) for _, _, _, _, vp in units]
    akv = [_mm(jnp.concatenate([a, b], axis=0), v) for a, b, v in zip(a_ak, a_rk, v_s)]
    wu = [_mm(t, jnp.concatenate([stack(at), stack(x[:rows])], axis=1))
          for t, (at, _, _, _, _), x in zip(tinv, units, akv)]
    z = [_mm(a, jnp.concatenate([stack(x[:, :GROUP_W]), stack(x[:, GROUP_W:])], axis=1))
         for a, x in zip(a_rb, wu)]
    return [(x[:, :GROUP_W], x[:, GROUP_W:], rt + zz[:, :GROUP_W], zz[:, GROUP_W:] + kv[rows:])
            for x, (_, rt, _, _, _), zz, kv in zip(wu, units, z, akv)]


def _rwkv_out(y, gate, bonus, lnw, lnb, seg):
    inv = 1.0 / RWKV_HEAD
    yc = y - _seg_sum(y, seg, 2) * inv
    var = _seg_sum(yc * yc, seg) * inv
    return ((yc * lax.rsqrt(var + RWKV_GN_EPS) * lnw + lnb + bonus) * gate).astype(BF16)


def _rwkv_state_in(s_ref, idx, g):
    zero = jnp.zeros((RWKV_HEAD, RWKV_HEAD), F32)
    blocks = []
    for h in range(GROUP_HEADS):
        parts = [zero] * GROUP_HEADS
        parts[h] = s_ref[idx, g * GROUP_HEADS + h]
        blocks.append(jnp.concatenate(parts, axis=1))
    return jnp.concatenate(blocks, axis=0)


def _rwkv_state_out(s_ref, idx, g, s2):
    for h in range(GROUP_HEADS):
        sl = slice(h * RWKV_HEAD, (h + 1) * RWKV_HEAD)
        s_ref[idx, g * GROUP_HEADS + h] = s2[sl, sl]


def _head_block_mask():
    return (_iota((GROUP_W, GROUP_W), 0) // RWKV_HEAD) == (_iota((GROUP_W, GROUP_W), 1) // RWKV_HEAD)


def _rwkv_seq_kernel(u_ref, shift0_ref, s0_ref, mu_ref, w0_ref, w2_ref, a0_ref, a2_ref,
                     g2_ref, kk_ref, ka_ref, rk_ref, lnw_ref, lnb_ref, seg_ref,
                     o_ref, sout_ref, s_scr, prev_scr, phase=None):
    c = pl.program_id(1)
    n_seq, rows = u_ref.shape[0], u_ref.shape[1]
    groups = [slice(g * GROUP_W, (g + 1) * GROUP_W) for g in range(RWKV_GROUPS)]
    ids = [(b, g) for b in range(n_seq) for g in range(RWKV_GROUPS)]

    def init():
        for b in range(n_seq):
            for g in range(RWKV_GROUPS):
                s_scr[b, g] = _rwkv_state_in(s0_ref, b, g)
            prev_scr[b] = shift0_ref[b]

    def final():
        for b, g in ids:
            _rwkv_state_out(sout_ref, b, g, s_scr[b, g])

    if phase == "init":
        return init()
    if phase == "final":
        return final()
    if phase is None:
        pl.when(c == 0)(init)

    seg = seg_ref[...]
    m = _chunk_masks(rows, rows, GROUP_HEADS)
    hmasks = _head_masks(rows, RWKV_HEAD, GROUP_HEADS)
    head_blk = _head_block_mask()
    n_double = rows.bit_length() - 2

    us = [u_ref[b] for b in range(n_seq)]
    prev = jnp.concatenate([_shift_rows(u, prev_scr[b]) for b, u in enumerate(us)], axis=0)
    for b, u in enumerate(us):
        prev_scr[b] = u[rows - 1:rows]
    at, rt, kt, bt, ke, be, vb, gate, bonus, g_end = _rwkv_prep(
        jnp.concatenate(us, axis=0), prev, mu_ref[...], w0_ref[...], w2_ref[...], a0_ref[...],
        a2_ref[...], g2_ref[...], kk_ref[...], ka_ref[...], rk_ref[...], seg, m["sums"])
    tok, units = [], []
    for b in range(n_seq):
        sl = slice(b * rows, (b + 1) * rows)
        tok.append((vb[sl], ke[sl], be[sl], g_end[b * rows:b * rows + 1]))
        units += [(at[sl, gl], rt[sl, gl], kt[sl, gl], bt[sl, gl], vb[sl, gl]) for gl in groups]

    intra = _rwkv_intra(units, m, hmasks, n_double)
    s_old = [s_scr[b, g] for b, g in ids]
    uy = [_mm(jnp.concatenate([w_m, r_m], axis=0), s2, NT) for (w_m, _, r_m, _), s2 in zip(intra, s_old)]
    upd = []
    for (b, g), (_, u0, _, _), x in zip(ids, intra, uy):
        vb_b, ke_b, be_b, _ = tok[b]
        gl = groups[g]
        upd.append(_mm_tn(jnp.concatenate([x[:rows] + u0, vb_b[:, gl].astype(F32)], axis=0),
                          jnp.concatenate([be_b[:, gl], ke_b[:, gl]], axis=0)))
    for (b, g), s2, d in zip(ids, s_old, upd):
        s_scr[b, g] = s2 * tok[b][3][:, groups[g]] + jnp.where(head_blk, d, 0.0)
    ys = [jnp.concatenate([uy[i][rows:] + intra[i][3] for i, (bb, _) in enumerate(ids) if bb == b],
                          axis=1) for b in range(n_seq)]
    out = _rwkv_out(jnp.concatenate(ys, axis=0), gate, bonus, lnw_ref[...], lnb_ref[...], seg)
    for b in range(n_seq):
        o_ref[b] = out[b * rows:(b + 1) * rows]

    if phase is None:
        pl.when(c == pl.num_programs(1) - 1)(final)


STEP_HEADS = 2
STEP_ROWS = 4
N_STEP_OPERANDS = 6


def _sum_keys(x):
    t = x[0:SUBLANES]
    for i in range(SUBLANES, x.shape[0], SUBLANES):
        t = t + x[i:i + SUBLANES]
    for shift in (4, 2, 1):
        t = t + pltpu.roll(t, shift, axis=0)
    return t


def _rwkv_step_kernel(u_ref, shiftx_ref, s0_ref, mu_ref, w0_ref, w2_ref, a0_ref, a2_ref, g2_ref,
                      kk_ref, ka_ref, rk_ref, lnw_ref, lnb_ref, seg_ref,
                      o_ref, sout_ref, ops_scr, y_scr, tmp_scr, aux_scr, *, seq_len):
    p = pl.program_id(0)
    n_tok = u_ref.shape[0]
    n_seq = n_tok // seq_len
    seg = seg_ref[...]
    cols = [slice(c, c + LANES) for c in range(0, RWKV_W, LANES)]

    @pl.when(p == 0)
    def _():
        u = u_ref[...]
        row = _iota((n_tok, 1), 0)
        prev = jnp.where(row % seq_len == 0, shiftx_ref[...], pltpu.roll(u, 1, axis=0))
        r, kr, vr, lw, a_vec, b_vec, gate, bonus = _rwkv_tokens(
            u, prev, mu_ref[...], w0_ref[...], w2_ref[...], a0_ref[...], a2_ref[...], g2_ref[...],
            kk_ref[...], ka_ref[...], rk_ref[...], seg)
        aux_scr[0] = gate
        aux_scr[1] = bonus
        for i, x in enumerate((jnp.exp(lw), a_vec, b_vec, kr, r, vr)):
            for c, cl in enumerate(cols):
                tmp_scr[c] = x[:, cl]
            for t in range(seq_len):
                for c, cl in enumerate(cols):
                    ops_scr[i, t, cl, :] = tmp_scr[c, pl.ds(t, n_seq, stride=seq_len), :].T

    def rows_step(i, carry):
        for hh in range(STEP_HEADS):
            chan = pl.multiple_of((p * STEP_HEADS + hh) * RWKV_HEAD, RWKV_HEAD)
            for j in range(STEP_ROWS):
                v = i * STEP_ROWS + j
                s = s0_ref[hh, v]
                for t in range(seq_len):
                    w, a, b, k, r = (ops_scr[n, t, pl.ds(chan, RWKV_HEAD), :] for n in range(5))
                    v_t = ops_scr[5, t, pl.ds(chan + v, 1), :]
                    sa = jnp.concatenate([_sum_keys(s * a)] * (RWKV_HEAD // SUBLANES), axis=0)
                    s = s * w + sa * b + v_t * k
                    y_scr[t, pl.ds(chan + v, 1), :] = _sum_keys(s * r)[0:1]
                sout_ref[hh, v] = s
        return carry

    lax.fori_loop(0, RWKV_HEAD // STEP_ROWS, rows_step, 0)

    @pl.when(p == pl.num_programs(0) - 1)
    def _():
        for t in range(seq_len):
            for c, cl in enumerate(cols):
                tmp_scr[c, pl.ds(t, n_seq, stride=seq_len), :] = y_scr[t, cl, :].T
        y = jnp.concatenate([tmp_scr[c] for c in range(len(cols))], axis=1)
        o_ref[...] = _rwkv_out(y, aux_scr[0], aux_scr[1], lnw_ref[...], lnb_ref[...], seg)


def _rwkv_step(u2, shiftx, s0_t, params, seq_len):
    n = u2.shape[0]
    n_seq = n // seq_len
    state_spec = pl.BlockSpec((STEP_HEADS, RWKV_HEAD, RWKV_HEAD, n_seq), lambda i: (i, 0, 0, 0))
    whole = lambda w: pl.BlockSpec((n, w), lambda i: (0, 0), pipeline_mode=pl.Buffered(1))
    return pl.pallas_call(
        functools.partial(_rwkv_step_kernel, seq_len=seq_len),
        grid=(RWKV_HEADS // STEP_HEADS,),
        in_specs=[whole(RWKV_PROJ), whole(RWKV_PROJ), state_spec] + _rwkv_param_specs(),
        out_specs=[pl.BlockSpec((n, RWKV_W), lambda i: (0, 0)), state_spec],
        out_shape=[
            jax.ShapeDtypeStruct((n, RWKV_W), BF16),
            jax.ShapeDtypeStruct(s0_t.shape, F32),
        ],
        scratch_shapes=[
            pltpu.VMEM((N_STEP_OPERANDS, seq_len, RWKV_W, n_seq), F32),
            pltpu.VMEM((seq_len, RWKV_W, n_seq), F32),
            pltpu.VMEM((RWKV_W // LANES, n, LANES), F32),
            pltpu.VMEM((2, n, RWKV_W), F32),
        ],
        compiler_params=pltpu.CompilerParams(
            dimension_semantics=("arbitrary",), vmem_limit_bytes=VMEM_LIMIT),
        name="rwkv_step",
    )(u2, shiftx, s0_t, *params)


def _rwkv_param_specs():
    vec = lambda n: _const_spec((1, n))
    return [
        vec(RWKV_PROJ),
        vec(RWKV_W),
        _const_spec((LANES, RWKV_W)),
        vec(RWKV_W),
        _const_spec((LANES, RWKV_W)),
        _const_spec((LANES, RWKV_W)),
        vec(RWKV_W), vec(RWKV_W), vec(RWKV_W), vec(RWKV_W), vec(RWKV_W),
        _const_spec((SEG_W, SEG_W)),
    ]


N_GLA_SEQ_IN = 5
N_RWKV_SEQ_IN = 15


def _mix_seq_kernel(*refs):
    n_in = N_GLA_SEQ_IN + N_RWKV_SEQ_IN
    gla = refs[:N_GLA_SEQ_IN] + refs[n_in:n_in + 2] + refs[n_in + 4:n_in + 5]
    rwkv = refs[N_GLA_SEQ_IN:n_in] + refs[n_in + 2:n_in + 4] + refs[n_in + 5:]
    c = pl.program_id(1)

    @pl.when(c == 0)
    def _():
        _gla_seq_kernel(*gla, phase="init")
        _rwkv_seq_kernel(*rwkv, phase="init")

    _rwkv_seq_kernel(*rwkv, phase="body")
    _gla_seq_kernel(*gla, phase="body")

    @pl.when(c == pl.num_programs(1) - 1)
    def _():
        _gla_seq_kernel(*gla, phase="final")
        _rwkv_seq_kernel(*rwkv, phase="final")


def _mix_seq(ug3, gla_s0, gla_params, ur3, shift0, rwkv_s0, rwkv_params, rows, n_seq):
    b, t, _ = ug3.shape
    sdim = GLA_HEADS * GLA_DK
    gla_state = pl.BlockSpec((n_seq, sdim, GLA_DV), lambda i, j: (i, 0, 0))
    rwkv_state = pl.BlockSpec((None, n_seq, RWKV_HEADS, RWKV_HEAD, RWKV_HEAD),
                              lambda i, j: (0, i, 0, 0, 0))
    tok = lambda w: pl.BlockSpec((n_seq, rows, w), lambda i, j: (i, j, 0))
    return pl.pallas_call(
        _mix_seq_kernel,
        grid=(b // n_seq, t // rows),
        in_specs=[tok(GLA_COLS), gla_state] + _gla_param_specs()
        + [tok(RWKV_PROJ), pl.BlockSpec((n_seq, 1, RWKV_PROJ), lambda i, j: (i, 0, 0)), rwkv_state]
        + _rwkv_param_specs(),
        out_specs=[tok(GLA_W), gla_state, tok(RWKV_W), rwkv_state],
        out_shape=[
            jax.ShapeDtypeStruct((b, t, GLA_W), BF16),
            jax.ShapeDtypeStruct((b, sdim, GLA_DV), F32),
            jax.ShapeDtypeStruct((b, t, RWKV_W), BF16),
            jax.ShapeDtypeStruct(rwkv_s0.shape, F32),
        ],
        scratch_shapes=[
            pltpu.VMEM((n_seq, sdim, GLA_W), F32),
            pltpu.VMEM((n_seq, RWKV_GROUPS, GROUP_W, GROUP_W), F32),
            pltpu.VMEM((n_seq, 1, RWKV_PROJ), F32),
        ],
        compiler_params=pltpu.CompilerParams(
            dimension_semantics=("arbitrary", "arbitrary"), vmem_limit_bytes=VMEM_LIMIT),
        name="mix_seq",
    )(ug3, gla_s0, *gla_params, ur3, shift0, rwkv_s0, *rwkv_params)


FF_CHUNK = D_FF // 2


def _post_kernel(x_ref, og_ref, or_ref, p_ref, wo_ref, nffn_ref, wg_ref, wu_ref, wd_ref,
                 nple_ref, wpg_ref, wpp_ref, nf_ref, y_ref):
    half = x_ref.shape[0] // 2
    parts = [slice(0, half), slice(half, 2 * half)]
    dot = lambda a, b: jnp.dot(a, b, preferred_element_type=F32)
    o = [jnp.concatenate([og_ref[p, :], or_ref[p, :]], axis=1) for p in parts]
    x = [x_ref[p, :] + dot(oo, wo_ref[...]) for p, oo in zip(parts, o)]
    h2 = [_rms(xx, nffn_ref[...]).astype(BF16) for xx in x]
    for i in range(0, D_FF, FF_CHUNK):
        gate = [dot(h, wg_ref[:, i:i + FF_CHUNK]) for h in h2]
        up = [dot(h, wu_ref[:, i:i + FF_CHUNK]) for h in h2]
        act = [(g * _sigmoid(g) * u).astype(BF16) for g, u in zip(gate, up)]
        x = [xx + dot(a, wd_ref[i:i + FF_CHUNK, :]) for xx, a in zip(x, act)]
    h3 = [_rms(xx, nple_ref[...]).astype(BF16) for xx in x]
    pg = [_sigmoid(dot(h, wpg_ref[...])) for h in h3]
    pp = [dot(p_ref[p, :].astype(BF16), wpp_ref[...]) for p in parts]
    for p, xx, a, b in zip(parts, x, pg, pp):
        y_ref[p, :] = _rms(xx + a * b, nf_ref[...])


def _post(x2d, og, orw, p2d, weights, tm):
    n = x2d.shape[0]
    wo, nffn, wg, wu, wd, nple, wpg, wpp, nf = weights
    tok = lambda w: pl.BlockSpec((tm, w), lambda i: (i, 0))
    return pl.pallas_call(
        _post_kernel,
        grid=(n // tm,),
        in_specs=[
            tok(D_MODEL), tok(GLA_W), tok(RWKV_W), tok(PLE_DIM),
            _const_spec((D_MODEL, D_MODEL)), _const_spec((1, D_MODEL)),
            _const_spec((D_MODEL, D_FF)), _const_spec((D_MODEL, D_FF)), _const_spec((D_FF, D_MODEL)),
            _const_spec((1, D_MODEL)), _const_spec((D_MODEL, D_MODEL)), _const_spec((PLE_DIM, D_MODEL)),
            _const_spec((1, D_MODEL)),
        ],
        out_specs=tok(D_MODEL),
        out_shape=jax.ShapeDtypeStruct((n, D_MODEL), F32),
        compiler_params=pltpu.CompilerParams(
            dimension_semantics=("arbitrary",), vmem_limit_bytes=VMEM_LIMIT),
        name="post",
    )(x2d, og, orw, p2d, wo, nffn, wg, wu, wd, nple, wpg, wpp, nf)


PROMPT_CHUNK = 64
PROMPT_SEQS_PER_STEP = 8
DEC_TILE_SEQS = 32
TOKEN_TILE = 512
PROJ_TILE = 1024


def kernel(x_prompt, x_sample, state_gla, state_rwkv, state_shift, p_prompt, p_sample, norm_mix, w_in, gla_gk_up, gla_gk_bias, gla_norm, rwkv_mu, rwkv_w0, rwkv_w2, rwkv_a0, rwkv_a2, rwkv_g2, rwkv_k_k, rwkv_k_a, rwkv_r_k, rwkv_ln_w, rwkv_ln_b, w_out, norm_ffn, w_gate, w_up, w_down, norm_ple, w_ple_gate, w_ple_proj, norm_final):
    assert w_in.shape[0] == 1
    i = 0
    rowv = lambda a: a.astype(F32).reshape(1, -1)
    zeros = lambda r, c: jnp.zeros((r, c), F32)
    w_in_i = w_in[i]
    w_gla = jnp.concatenate(
        [w_in_i[:, :GLA_PROJ], zeros(D_MODEL, GLA_COLS - GLA_PROJ)], axis=1).astype(BF16)
    w_rw = w_in_i[:, GLA_PROJ:].astype(BF16)
    gk_up = jnp.concatenate(
        [gla_gk_up[i].astype(F32), zeros(LANES - GLA_GATE_RANK, GLA_K_W)], axis=0).astype(BF16)
    seg = jnp.arange(SEG_W)[:, None] // RWKV_HEAD == jnp.arange(SEG_W)[None, :] // RWKV_HEAD
    rwkv_params = (
        rowv(rwkv_mu[i]), rowv(rwkv_w0[i]),
        jnp.concatenate([rwkv_w2[i].astype(F32), zeros(64, RWKV_W)], axis=0).astype(BF16),
        rowv(rwkv_a0[i]),
        jnp.concatenate([zeros(64, RWKV_W), rwkv_a2[i].astype(F32)], axis=0).astype(BF16),
        rwkv_g2[i].astype(BF16),
        rowv(rwkv_k_k[i]), rowv(rwkv_k_a[i]), rowv(rwkv_r_k[i]), rowv(rwkv_ln_w[i]), rowv(rwkv_ln_b[i]),
        seg.astype(BF16),
    )
    post_w = (
        w_out[i].astype(BF16), rowv(norm_ffn[i]), w_gate[i].astype(BF16), w_up[i].astype(BF16),
        w_down[i].astype(BF16), rowv(norm_ple[i]), w_ple_gate[i].astype(BF16),
        w_ple_proj[i].astype(BF16), rowv(norm_final),
    )
    gla_w = (gk_up, rowv(gla_gk_bias[i]), rowv(gla_norm[i]))
    g_mix = rowv(norm_mix[i])

    bp, tp, _ = x_prompt.shape
    xp = x_prompt.astype(F32).reshape(bp * tp, D_MODEL)
    ug, ur = _proj(xp, g_mix, w_gla, w_rw, PROJ_TILE)
    ur3 = ur.reshape(bp, tp, RWKV_PROJ)
    og, gla_p, orw, rwkv_p = _mix_seq(
        ug.reshape(bp, tp, GLA_COLS), jnp.zeros((bp, GLA_HEADS * GLA_DK, GLA_DV), F32), gla_w,
        ur3, jnp.zeros((bp, 1, RWKV_PROJ), F32),
        jnp.zeros((1, bp, RWKV_HEADS, RWKV_HEAD, RWKV_HEAD), F32), rwkv_params,
        PROMPT_CHUNK, PROMPT_SEQS_PER_STEP)
    shift_p = ur3[:, tp - 1]
    yp = _post(xp, og.reshape(bp * tp, GLA_W), orw.reshape(bp * tp, RWKV_W),
               p_prompt[i].reshape(bp * tp, PLE_DIM), post_w, TOKEN_TILE)

    bs, ts, _ = x_sample.shape
    xs = x_sample.astype(F32).reshape(bs * ts, D_MODEL)
    ug, ur = _proj(xs, g_mix, w_gla, w_rw, TOKEN_TILE)
    og, gla_s = _gla_dec(ug, state_gla[i].astype(F32).reshape(bs, GLA_HEADS * GLA_DK, GLA_DV),
                         gla_w, ts, DEC_TILE_SEQS)
    shiftx = jnp.pad(state_shift[i].astype(F32)[:, None, :], ((0, 0), (0, ts - 1), (0, 0)))
    orw, rwkv_s = _rwkv_step(ur, shiftx.reshape(bs * ts, RWKV_PROJ),
                             jnp.transpose(state_rwkv[i].astype(F32), (1, 2, 3, 0)), rwkv_params, ts)
    rwkv_s = jnp.transpose(rwkv_s, (3, 0, 1, 2))[None]
    shift_s = ur.reshape(bs, ts, RWKV_PROJ)[:, ts - 1]
    ys = _post(xs, og, orw, p_sample[i].reshape(bs * ts, PLE_DIM), post_w, TOKEN_TILE)

    gla_shape = (1, -1, GLA_HEADS, GLA_DK, GLA_DV)
    return (yp.reshape(bp, tp, D_MODEL).astype(x_prompt.dtype),
            ys.reshape(bs, ts, D_MODEL).astype(x_sample.dtype),
            gla_p.reshape(gla_shape).astype(state_gla.dtype), rwkv_p.astype(state_rwkv.dtype),
            shift_p[None].astype(state_shift.dtype),
            gla_s.reshape(gla_shape).astype(state_gla.dtype), rwkv_s.astype(state_rwkv.dtype),
            shift_s[None].astype(state_shift.dtype))
```

```python
import functools

import jax
import jax.numpy as jnp
from jax import lax
from jax.experimental import pallas as pl
from jax.experimental.pallas import tpu as pltpu

F32 = jnp.float32
BF16 = jnp.bfloat16

D_MODEL = 1024
GLA_HEADS = 4
GLA_DK = 64
GLA_DV = 128
GLA_K_W = GLA_HEADS * GLA_DK
GLA_W = GLA_HEADS * GLA_DV
GLA_GATE_RANK = 16
GLA_GATE_NORM = 16.0
GLA_MAIN = 2 * GLA_K_W + 2 * GLA_W
GLA_PROJ = GLA_MAIN + GLA_GATE_RANK
LANES = 128
SUBLANES = 8
GLA_COLS = GLA_MAIN + LANES
RWKV_HEAD = 64
RWKV_HEADS = 8
RWKV_W = RWKV_HEADS * RWKV_HEAD
RWKV_PROJ = 3 * RWKV_W + 64 + 64 + 128
D_FF = 2816
PLE_DIM = 256
EPS = 1e-6
RWKV_GN_EPS = 64e-5

VMEM_LIMIT = 56 * 1024 * 1024

NN = ((1,), (0,))
NT = ((1,), (1,))


def _split(x, n):
    parts = []
    r = x
    for i in range(n):
        p = r.astype(BF16)
        parts.append(p)
        if i + 1 < n:
            r = r - p.astype(F32)
    return parts


def _mm(a, b, dims=NN, pa=1, pb=1):
    pieces_a = _split(a, pa)
    pieces_b = _split(b, pb)
    n = max(pa, pb)
    acc = None
    for i, ai in enumerate(pieces_a):
        for j, bj in enumerate(pieces_b):
            if i + j < n:
                t = lax.dot_general(ai, bj, (dims, ((), ())), preferred_element_type=F32)
                acc = t if acc is None else acc + t
    return acc


def _mm_tn(a, b, pa=1, pb=1):
    rows = a.shape[0]
    pad = (-rows) % LANES
    if pad:
        a = jnp.concatenate([a, jnp.zeros((pad, a.shape[1]), a.dtype)], axis=0)
        b = jnp.concatenate([b, jnp.zeros((pad, b.shape[1]), b.dtype)], axis=0)
    return _mm(a.T, b, NN, pa, pb)


def _iota(shape, dim):
    return lax.broadcasted_iota(jnp.int32, shape, dim)


def _log_sigmoid(z):
    return jnp.minimum(z, 0.0) - jnp.log(1.0 + jnp.exp(-jnp.abs(z)))


def _shift_rows(u, first):
    r = pltpu.roll(u, 1, axis=0)
    head = jnp.where(_iota((SUBLANES, 1), 0) == 0, first, r[:SUBLANES])
    return jnp.concatenate([head, r[SUBLANES:]], axis=0)


def _sigmoid(z):
    return 1.0 / (1.0 + jnp.exp(-z))


def _tanh(z):
    return 2.0 * _sigmoid(2.0 * z) - 1.0


def _rms(x, g):
    return x * lax.rsqrt(jnp.mean(x * x, axis=-1, keepdims=True) + EPS) * g


GROUP_HEADS = 2
GROUP_W = GROUP_HEADS * RWKV_HEAD
RWKV_GROUPS = RWKV_W // GROUP_W
SEG_W = 2 * LANES


def _head_masks(rows, head_w, heads):
    lane = _iota((rows, heads * head_w), 1)
    return [jnp.where(lane // head_w == h, 1.0, 0.0).astype(BF16) for h in range(heads)]


def _stack_heads(x, hmasks):
    xb = x.astype(BF16)
    return jnp.concatenate([xb * m for m in hmasks], axis=0)


def _block_diag(x_cat, blk):
    heads = blk.shape[0] // x_cat.shape[0]
    return jnp.concatenate([x_cat.astype(BF16)] * heads, axis=0) * blk


def _chunk_masks(rows, seq_len, heads):
    t = _iota((rows, heads * rows), 0)
    s = _iota((rows, heads * rows), 1) % rows
    incl = t >= s
    t2 = _iota((2 * rows, rows), 0)
    s2 = _iota((2 * rows, rows), 1)
    cum_rows = (t2 < rows) & (t2 >= s2)
    tot_rows = t2 >= rows
    if seq_len < rows:
        incl = incl & ((t // seq_len) == (s // seq_len))
        same2 = ((t2 % rows) // seq_len) == (s2 // seq_len)
        cum_rows, tot_rows = cum_rows & same2, tot_rows & same2
    big = heads * rows
    blk = jnp.where(_iota((big, big), 0) // rows == _iota((big, big), 1) // rows, 1.0, 0.0)
    t4 = _iota((2 * rows, 2 * big), 0)
    s4 = _iota((2 * rows, 2 * big), 1) % rows
    quad = ((t4 % rows) > s4) | ((t4 >= rows) & ((t4 % rows) == s4))
    if seq_len < rows:
        quad = quad & (((t4 % rows) // seq_len) == (s4 // seq_len))
    return dict(incl=incl, quad=quad, eye=jnp.where(t == s, 1.0, 0.0),
                sums=jnp.where(cum_rows | tot_rows, 1.0, 0.0).astype(BF16), blk=blk.astype(BF16))


def _proj_kernel(x_ref, g_ref, wg_ref, wr_ref, ug_ref, ur_ref):
    h = _rms(x_ref[...], g_ref[...]).astype(BF16)
    ug_ref[...] = jnp.dot(h, wg_ref[...], preferred_element_type=F32)
    ur_ref[...] = jnp.dot(h, wr_ref[...], preferred_element_type=F32)


def _const_spec(shape):
    return pl.BlockSpec(shape, lambda *_: (0,) * len(shape), pipeline_mode=pl.Buffered(1))


def _proj(x2d, g, w_gla, w_rw, tm):
    n = x2d.shape[0]
    return pl.pallas_call(
        _proj_kernel,
        grid=(n // tm,),
        in_specs=[
            pl.BlockSpec((tm, D_MODEL), lambda i: (i, 0)),
            _const_spec((1, D_MODEL)),
            _const_spec((D_MODEL, GLA_COLS)),
            _const_spec((D_MODEL, RWKV_PROJ)),
        ],
        out_specs=[
            pl.BlockSpec((tm, GLA_COLS), lambda i: (i, 0)),
            pl.BlockSpec((tm, RWKV_PROJ), lambda i: (i, 0)),
        ],
        out_shape=[
            jax.ShapeDtypeStruct((n, GLA_COLS), F32),
            jax.ShapeDtypeStruct((n, RWKV_PROJ), F32),
        ],
        compiler_params=pltpu.CompilerParams(
            dimension_semantics=("arbitrary",), vmem_limit_bytes=VMEM_LIMIT),
        name="proj",
    )(x2d, g, w_gla, w_rw)


def _gla_prep(us, gkup, gkb, sums):
    rows = us[0].shape[0]
    z = [_mm(u[:, GLA_MAIN:GLA_COLS], gkup) for u in us]
    log_a = [_log_sigmoid(x + gkb) * (1.0 / GLA_GATE_NORM) for x in z]
    cums = [_mm(sums, x, NN, 1, 2) for x in log_a]
    out = []
    for u, x in zip(us, cums):
        cum, cum_end = x[:rows], x[rows:]
        q = u[:, 0:GLA_K_W] * (GLA_DK ** -0.5)
        k = u[:, GLA_K_W:2 * GLA_K_W]
        out.append((q * jnp.exp(cum), k * jnp.exp(-cum), k * jnp.exp(cum_end - cum),
                    u[:, 2 * GLA_K_W:2 * GLA_K_W + GLA_W], u[:, 2 * GLA_K_W + GLA_W:GLA_MAIN],
                    jnp.exp(cum_end)))
    return out


def _gla_intra(q_i, k_i, v, m, kmasks, vmasks):
    scores = jnp.where(m["incl"], _mm(q_i, _stack_heads(k_i, kmasks), NT), 0.0)
    return _mm(scores, _stack_heads(v, vmasks))


def _gla_out(o, gate, gnorm):
    heads = [slice(h * GLA_DV, (h + 1) * GLA_DV) for h in range(GLA_HEADS)]
    return jnp.concatenate(
        [_rms(o[:, hl], gnorm) * (gate[:, hl] * _sigmoid(gate[:, hl])) for hl in heads],
        axis=1).astype(BF16)


def _gla_block_mask():
    return (_iota((GLA_K_W, GLA_W), 0) // GLA_DK) == (_iota((GLA_K_W, GLA_W), 1) // GLA_DV)


def _gla_state_in(s, blk):
    return jnp.where(blk, jnp.concatenate([s] * GLA_HEADS, axis=1), 0.0)


def _gla_state_out(s_bd):
    heads = [s_bd[:, h * GLA_DV:(h + 1) * GLA_DV] for h in range(GLA_HEADS)]
    return (heads[0] + heads[1]) + (heads[2] + heads[3])


def _lane_tiled_t(x):
    pad = LANES - x.shape[0]
    if pad:
        x = jnp.concatenate([x, jnp.zeros((pad, x.shape[1]), x.dtype)], axis=0)
    return x.T


def _gla_seq_part(u_ref, s0_ref, gkup_ref, gkb_ref, gn_ref, o_ref, sout_ref, s_scr, *, phase):
    n_seq, rows = u_ref.shape[0], u_ref.shape[1]
    blk = _gla_block_mask()

    def init():
        for b in range(n_seq):
            s_scr[b] = _gla_state_in(s0_ref[b], blk)

    def final():
        for b in range(n_seq):
            sout_ref[b] = _gla_state_out(s_scr[b])

    if phase == "init":
        return init()
    if phase == "final":
        return final()

    m = _chunk_masks(rows, rows, GLA_HEADS)
    kmasks = _head_masks(rows, GLA_DK, GLA_HEADS)
    vmasks = _head_masks(rows, GLA_DV, GLA_HEADS)
    tok = _gla_prep([u_ref[b] for b in range(n_seq)], gkup_ref[...], gkb_ref[...], m["sums"])
    s_old = [s_scr[b] for b in range(n_seq)]
    intra = [_gla_intra(q_i, k_i, v, m, kmasks, vmasks) for q_i, k_i, _, v, _, _ in tok]
    inter = [_mm(t[0], s) for t, s in zip(tok, s_old)]
    kv = [_mm_tn(k_e, v) for _, _, k_e, v, _, _ in tok]
    for b in range(n_seq):
        gate, g_end = tok[b][4], tok[b][5]
        o_ref[b] = _gla_out(intra[b] + inter[b], gate, gn_ref[...])
        dec = jnp.concatenate([_lane_tiled_t(jnp.broadcast_to(g_end[0:1], (LANES, GLA_K_W)))] * GLA_HEADS,
                              axis=1)
        s_scr[b] = dec * s_old[b] + jnp.where(blk, kv[b], 0.0)


def _gla_dec_kernel(u_ref, s0_ref, gkup_ref, gkb_ref, gn_ref, o_ref, sout_ref, *, seq_len):
    rows = u_ref.shape[0]
    n_seq = rows // seq_len
    blk = _gla_block_mask()
    m = _chunk_masks(rows, seq_len, GLA_HEADS)
    q_i, k_i, k_e, v, gate, g_end = _gla_prep([u_ref[...]], gkup_ref[...], gkb_ref[...], m["sums"])[0]
    o = _gla_intra(q_i, k_i, v, m, _head_masks(rows, GLA_DK, GLA_HEADS),
                   _head_masks(rows, GLA_DV, GLA_HEADS))
    k_et = _lane_tiled_t(k_e).astype(BF16)
    dec_t = _lane_tiled_t(g_end)
    v_pad = v if rows == LANES else jnp.concatenate([v, jnp.zeros((LANES - rows, GLA_W), F32)], axis=0)
    row = _iota((rows, 1), 0)
    row_pad = _iota((LANES, 1), 0)
    for j in range(n_seq):
        s_bd = _gla_state_in(s0_ref[j], blk)
        o = o + _mm(jnp.where(row // seq_len == j, q_i, 0.0), s_bd)
        kv = _mm(k_et, jnp.where(row_pad // seq_len == j, v_pad, 0.0))
        first = j * seq_len
        sout_ref[j] = _gla_state_out(dec_t[:, first:first + 1] * s_bd + jnp.where(blk, kv, 0.0))
    o_ref[...] = _gla_out(o, gate, gn_ref[...])


def _gla_param_specs():
    return [_const_spec((LANES, GLA_K_W)), _const_spec((1, GLA_K_W)), _const_spec((1, GLA_DV))]


def _gla_dec(u2, s0, params, seq_len, n_seq):
    n = u2.shape[0]
    rows = n_seq * seq_len
    sdim = GLA_HEADS * GLA_DK
    state_spec = pl.BlockSpec((n_seq, sdim, GLA_DV), lambda i: (i, 0, 0))
    return pl.pallas_call(
        functools.partial(_gla_dec_kernel, seq_len=seq_len),
        grid=(n // rows,),
        in_specs=[pl.BlockSpec((rows, GLA_COLS), lambda i: (i, 0)), state_spec] + _gla_param_specs(),
        out_specs=[pl.BlockSpec((rows, GLA_W), lambda i: (i, 0)), state_spec],
        out_shape=[
            jax.ShapeDtypeStruct((n, GLA_W), BF16),
            jax.ShapeDtypeStruct(s0.shape, F32),
        ],
        compiler_params=pltpu.CompilerParams(
            dimension_semantics=("arbitrary",), vmem_limit_bytes=VMEM_LIMIT),
        name="gla_dec",
    )(u2, s0, *params)


def _seg_sum(x, seg, pa=1):
    return jnp.concatenate(
        [_mm(x[:, i:i + SEG_W], seg, NN, pa, 1) for i in range(0, RWKV_W, SEG_W)], axis=1)


W_OFFSET_SCALE = 0.6065306597126334


def _rwkv_tokens(u, prev, mu, w0, w2, a0, a2, g2, k_k, k_a, rk, seg):
    xr = u + mu * (prev - u)
    r = xr[:, 0:RWKV_W]
    kr = xr[:, RWKV_W:2 * RWKV_W]
    vr = xr[:, 2 * RWKV_W:3 * RWKV_W]
    wa = xr[:, 3 * RWKV_W:3 * RWKV_W + LANES]
    gd = xr[:, 3 * RWKV_W + LANES:RWKV_PROJ]
    lw = _sigmoid(w0 + _mm(_tanh(wa), w2)) * (-W_OFFSET_SCALE)
    a_sig = _sigmoid(a0 + _mm(wa, a2))
    gate = _mm(_sigmoid(gd), g2)
    kk = kr * k_k
    kk = kk * lax.rsqrt(jnp.maximum(_seg_sum(kk * kk, seg), 1e-24))
    kr = kr * (1.0 + (a_sig - 1.0) * k_a)
    bonus = _seg_sum(r * kr * rk, seg) * vr
    return r, kr, vr, lw, -kk, kk * a_sig, gate, bonus


def _rwkv_prep(u, prev, mu, w0, w2, a0, a2, g2, k_k, k_a, rk, seg, sums):
    rows = u.shape[0]
    r, kr, vr, lw, a_vec, b_vec, gate, bonus = _rwkv_tokens(
        u, prev, mu, w0, w2, a0, a2, g2, k_k, k_a, rk, seg)
    block = sums.shape[1]
    cums = [_mm(sums, lw[i:i + block], NN, 1, 2) for i in range(0, rows, block)]
    cum = jnp.concatenate([x[:block] for x in cums], axis=0)
    cum_end = jnp.concatenate([x[block:] for x in cums], axis=0)
    e_neg = jnp.exp(-cum)
    g_end = jnp.exp(cum_end)
    e_end = g_end * e_neg
    bf = lambda x: x.astype(BF16)
    return (bf(a_vec * jnp.exp(cum - lw)), bf(r * jnp.exp(cum)), bf(kr * e_neg), bf(b_vec * e_neg),
            bf(kr * e_end), bf(b_vec * e_end), bf(vr), gate, bonus, g_end)


def _rwkv_intra(units, m, hmasks, n_double):
    rows = units[0][0].shape[0]
    stack = lambda x: _stack_heads(x, hmasks)
    cat_w = m["incl"].shape[1]
    g = [_mm(jnp.concatenate([at, rt], axis=0), jnp.concatenate([stack(bt), stack(kt)], axis=0), NT)
         for at, rt, kt, bt, _ in units]
    g = [jnp.where(m["quad"], x, 0.0) for x in g]
    a_ab = [x[:rows, :cat_w] for x in g]
    a_rb = [x[rows:, :cat_w] for x in g]
    a_ak = [x[:rows, cat_w:] for x in g]
    a_rk = [x[rows:, cat_w:] for x in g]
    tinv = [m["eye"] + a for a in a_ab]
    apow = [_mm(a, _block_diag(a, m["blk"])) for a in a_ab]
    for _ in range(n_double - 1):
        both = [_mm(jnp.concatenate([t, a], axis=0), _block_diag(a, m["blk"])) for t, a in zip(tinv, apow)]
        tinv = [t + x[:rows] for t, x in zip(tinv, both)]
        apow = [x[rows:] for x in both]
    tinv = [t + _mm(t, _block_diag(a, m["blk"])) for t, a in zip(tinv, apow)]
    v_s = [stack(vp) for _, _, _, _, vp in units]
    akv = [_mm(jnp.concatenate([a, b], axis=0), v) for a, b, v in zip(a_ak, a_rk, v_s)]
    wu = [_mm(t, jnp.concatenate([stack(at), stack(x[:rows])], axis=1))
          for t, (at, _, _, _, _), x in zip(tinv, units, akv)]
    z = [_mm(a, jnp.concatenate([stack(x[:, :GROUP_W]), stack(x[:, GROUP_W:])], axis=1))
         for a, x in zip(a_rb, wu)]
    return [(x[:, :GROUP_W], x[:, GROUP_W:], rt + zz[:, :GROUP_W], zz[:, GROUP_W:] + kv[rows:])
            for x, (_, rt, _, _, _), zz, kv in zip(wu, units, z, akv)]


def _rwkv_out(y, gate, bonus, lnw, lnb, seg):
    inv = 1.0 / RWKV_HEAD
    yc = y - _seg_sum(y, seg, 2) * inv
    var = _seg_sum(yc * yc, seg) * inv
    return ((yc * lax.rsqrt(var + RWKV_GN_EPS) * lnw + lnb + bonus) * gate).astype(BF16)


def _rwkv_state_in(s_ref, idx, g):
    zero = jnp.zeros((RWKV_HEAD, RWKV_HEAD), F32)
    blocks = []
    for h in range(GROUP_HEADS):
        parts = [zero] * GROUP_HEADS
        parts[h] = s_ref[idx, g * GROUP_HEADS + h]
        blocks.append(jnp.concatenate(parts, axis=1))
    return jnp.concatenate(blocks, axis=0)


def _rwkv_state_out(s_ref, idx, g, s2):
    for h in range(GROUP_HEADS):
        sl = slice(h * RWKV_HEAD, (h + 1) * RWKV_HEAD)
        s_ref[idx, g * GROUP_HEADS + h] = s2[sl, sl]


def _head_block_mask():
    return (_iota((GROUP_W, GROUP_W), 0) // RWKV_HEAD) == (_iota((GROUP_W, GROUP_W), 1) // RWKV_HEAD)


def _rwkv_seq_part(u_ref, shift0_ref, s0_ref, mu_ref, w0_ref, w2_ref, a0_ref, a2_ref,
                   g2_ref, kk_ref, ka_ref, rk_ref, lnw_ref, lnb_ref, seg_ref,
                   o_ref, sout_ref, s_scr, prev_scr, *, phase):
    n_seq, rows = u_ref.shape[0], u_ref.shape[1]
    groups = [slice(g * GROUP_W, (g + 1) * GROUP_W) for g in range(RWKV_GROUPS)]
    ids = [(b, g) for b in range(n_seq) for g in range(RWKV_GROUPS)]

    def init():
        for b in range(n_seq):
            for g in range(RWKV_GROUPS):
                s_scr[b, g] = _rwkv_state_in(s0_ref, b, g)
            prev_scr[b] = shift0_ref[b]

    def final():
        for b, g in ids:
            _rwkv_state_out(sout_ref, b, g, s_scr[b, g])

    if phase == "init":
        return init()
    if phase == "final":
        return final()

    seg = seg_ref[...]
    m = _chunk_masks(rows, rows, GROUP_HEADS)
    hmasks = _head_masks(rows, RWKV_HEAD, GROUP_HEADS)
    head_blk = _head_block_mask()
    n_double = rows.bit_length() - 2

    us = [u_ref[b] for b in range(n_seq)]
    prev = jnp.concatenate([_shift_rows(u, prev_scr[b]) for b, u in enumerate(us)], axis=0)
    for b, u in enumerate(us):
        prev_scr[b] = u[rows - 1:rows]
    at, rt, kt, bt, ke, be, vb, gate, bonus, g_end = _rwkv_prep(
        jnp.concatenate(us, axis=0), prev, mu_ref[...], w0_ref[...], w2_ref[...], a0_ref[...],
        a2_ref[...], g2_ref[...], kk_ref[...], ka_ref[...], rk_ref[...], seg, m["sums"])
    tok, units = [], []
    for b in range(n_seq):
        sl = slice(b * rows, (b + 1) * rows)
        tok.append((vb[sl], ke[sl], be[sl], g_end[b * rows:b * rows + 1]))
        units += [(at[sl, gl], rt[sl, gl], kt[sl, gl], bt[sl, gl], vb[sl, gl]) for gl in groups]

    intra = _rwkv_intra(units, m, hmasks, n_double)
    s_old = [s_scr[b, g] for b, g in ids]
    uy = [_mm(jnp.concatenate([w_m, r_m], axis=0), s2, NT) for (w_m, _, r_m, _), s2 in zip(intra, s_old)]
    upd = []
    for (b, g), (_, u0, _, _), x in zip(ids, intra, uy):
        vb_b, ke_b, be_b, _ = tok[b]
        gl = groups[g]
        upd.append(_mm_tn(jnp.concatenate([x[:rows] + u0, vb_b[:, gl].astype(F32)], axis=0),
                          jnp.concatenate([be_b[:, gl], ke_b[:, gl]], axis=0)))
    for (b, g), s2, d in zip(ids, s_old, upd):
        s_scr[b, g] = s2 * tok[b][3][:, groups[g]] + jnp.where(head_blk, d, 0.0)
    ys = [jnp.concatenate([uy[i][rows:] + intra[i][3] for i, (bb, _) in enumerate(ids) if bb == b],
                          axis=1) for b in range(n_seq)]
    out = _rwkv_out(jnp.concatenate(ys, axis=0), gate, bonus, lnw_ref[...], lnb_ref[...], seg)
    for b in range(n_seq):
        o_ref[b] = out[b * rows:(b + 1) * rows]


STEP_HEADS = 2
STEP_ROWS = 4
N_STEP_OPERANDS = 6


def _sum_keys(x):
    t = x[0:SUBLANES]
    for i in range(SUBLANES, x.shape[0], SUBLANES):
        t = t + x[i:i + SUBLANES]
    for shift in (4, 2, 1):
        t = t + pltpu.roll(t, shift, axis=0)
    return t


def _rwkv_step_kernel(u_ref, shiftx_ref, s0_ref, mu_ref, w0_ref, w2_ref, a0_ref, a2_ref, g2_ref,
                      kk_ref, ka_ref, rk_ref, lnw_ref, lnb_ref, seg_ref,
                      o_ref, sout_ref, ops_scr, y_scr, tmp_scr, aux_scr, *, seq_len):
    p = pl.program_id(0)
    n_tok = u_ref.shape[0]
    n_seq = n_tok // seq_len
    seg = seg_ref[...]
    cols = [slice(c, c + LANES) for c in range(0, RWKV_W, LANES)]

    @pl.when(p == 0)
    def _():
        u = u_ref[...]
        row = _iota((n_tok, 1), 0)
        prev = jnp.where(row % seq_len == 0, shiftx_ref[...], pltpu.roll(u, 1, axis=0))
        r, kr, vr, lw, a_vec, b_vec, gate, bonus = _rwkv_tokens(
            u, prev, mu_ref[...], w0_ref[...], w2_ref[...], a0_ref[...], a2_ref[...], g2_ref[...],
            kk_ref[...], ka_ref[...], rk_ref[...], seg)
        aux_scr[0] = gate
        aux_scr[1] = bonus
        for i, x in enumerate((jnp.exp(lw), a_vec, b_vec, kr, r, vr)):
            for c, cl in enumerate(cols):
                tmp_scr[c] = x[:, cl]
            for t in range(seq_len):
                for c, cl in enumerate(cols):
                    ops_scr[i, t, cl, :] = tmp_scr[c, pl.ds(t, n_seq, stride=seq_len), :].T

    def rows_step(i, carry):
        for hh in range(STEP_HEADS):
            chan = pl.multiple_of((p * STEP_HEADS + hh) * RWKV_HEAD, RWKV_HEAD)
            for j in range(STEP_ROWS):
                v = i * STEP_ROWS + j
                s = s0_ref[hh, v]
                for t in range(seq_len):
                    w, a, b, k, r = (ops_scr[n, t, pl.ds(chan, RWKV_HEAD), :] for n in range(5))
                    v_t = ops_scr[5, t, pl.ds(chan + v, 1), :]
                    sa = jnp.concatenate([_sum_keys(s * a)] * (RWKV_HEAD // SUBLANES), axis=0)
                    s = s * w + sa * b + v_t * k
                    y_scr[t, pl.ds(chan + v, 1), :] = _sum_keys(s * r)[0:1]
                sout_ref[hh, v] = s
        return carry

    lax.fori_loop(0, RWKV_HEAD // STEP_ROWS, rows_step, 0)

    @pl.when(p == pl.num_programs(0) - 1)
    def _():
        for t in range(seq_len):
            for c, cl in enumerate(cols):
                tmp_scr[c, pl.ds(t, n_seq, stride=seq_len), :] = y_scr[t, cl, :].T
        y = jnp.concatenate([tmp_scr[c] for c in range(len(cols))], axis=1)
        o_ref[...] = _rwkv_out(y, aux_scr[0], aux_scr[1], lnw_ref[...], lnb_ref[...], seg)


def _rwkv_step(u2, shiftx, s0_t, params, seq_len):
    n = u2.shape[0]
    n_seq = n // seq_len
    state_spec = pl.BlockSpec((STEP_HEADS, RWKV_HEAD, RWKV_HEAD, n_seq), lambda i: (i, 0, 0, 0))
    whole = lambda w: pl.BlockSpec((n, w), lambda i: (0, 0), pipeline_mode=pl.Buffered(1))
    return pl.pallas_call(
        functools.partial(_rwkv_step_kernel, seq_len=seq_len),
        grid=(RWKV_HEADS // STEP_HEADS,),
        in_specs=[whole(RWKV_PROJ), whole(RWKV_PROJ), state_spec] + _rwkv_param_specs(),
        out_specs=[pl.BlockSpec((n, RWKV_W), lambda i: (0, 0)), state_spec],
        out_shape=[
            jax.ShapeDtypeStruct((n, RWKV_W), BF16),
            jax.ShapeDtypeStruct(s0_t.shape, F32),
        ],
        scratch_shapes=[
            pltpu.VMEM((N_STEP_OPERANDS, seq_len, RWKV_W, n_seq), F32),
            pltpu.VMEM((seq_len, RWKV_W, n_seq), F32),
            pltpu.VMEM((RWKV_W // LANES, n, LANES), F32),
            pltpu.VMEM((2, n, RWKV_W), F32),
        ],
        compiler_params=pltpu.CompilerParams(
            dimension_semantics=("arbitrary",), vmem_limit_bytes=VMEM_LIMIT),
        name="rwkv_step",
    )(u2, shiftx, s0_t, *params)


def _rwkv_param_specs():
    vec = lambda n: _const_spec((1, n))
    return [
        vec(RWKV_PROJ),
        vec(RWKV_W),
        _const_spec((LANES, RWKV_W)),
        vec(RWKV_W),
        _const_spec((LANES, RWKV_W)),
        _const_spec((LANES, RWKV_W)),
        vec(RWKV_W), vec(RWKV_W), vec(RWKV_W), vec(RWKV_W), vec(RWKV_W),
        _const_spec((SEG_W, SEG_W)),
    ]


N_GLA_SEQ_IN = 5
N_RWKV_SEQ_IN = 15


def _mix_seq_kernel(*refs):
    n_in = N_GLA_SEQ_IN + N_RWKV_SEQ_IN
    gla = refs[:N_GLA_SEQ_IN] + refs[n_in:n_in + 2] + refs[n_in + 4:n_in + 5]
    rwkv = refs[N_GLA_SEQ_IN:n_in] + refs[n_in + 2:n_in + 4] + refs[n_in + 5:]
    c = pl.program_id(1)

    @pl.when(c == 0)
    def _():
        _gla_seq_part(*gla, phase="init")
        _rwkv_seq_part(*rwkv, phase="init")

    _rwkv_seq_part(*rwkv, phase="body")
    _gla_seq_part(*gla, phase="body")

    @pl.when(c == pl.num_programs(1) - 1)
    def _():
        _gla_seq_part(*gla, phase="final")
        _rwkv_seq_part(*rwkv, phase="final")


def _mix_seq(ug3, gla_s0, gla_params, ur3, shift0, rwkv_s0, rwkv_params, rows, n_seq):
    b, t, _ = ug3.shape
    sdim = GLA_HEADS * GLA_DK
    gla_state = pl.BlockSpec((n_seq, sdim, GLA_DV), lambda i, j: (i, 0, 0))
    rwkv_state = pl.BlockSpec((None, n_seq, RWKV_HEADS, RWKV_HEAD, RWKV_HEAD),
                              lambda i, j: (0, i, 0, 0, 0))
    tok = lambda w: pl.BlockSpec((n_seq, rows, w), lambda i, j: (i, j, 0))
    return pl.pallas_call(
        _mix_seq_kernel,
        grid=(b // n_seq, t // rows),
        in_specs=[tok(GLA_COLS), gla_state] + _gla_param_specs()
        + [tok(RWKV_PROJ), pl.BlockSpec((n_seq, 1, RWKV_PROJ), lambda i, j: (i, 0, 0)), rwkv_state]
        + _rwkv_param_specs(),
        out_specs=[tok(GLA_W), gla_state, tok(RWKV_W), rwkv_state],
        out_shape=[
            jax.ShapeDtypeStruct((b, t, GLA_W), BF16),
            jax.ShapeDtypeStruct((b, sdim, GLA_DV), F32),
            jax.ShapeDtypeStruct((b, t, RWKV_W), BF16),
            jax.ShapeDtypeStruct(rwkv_s0.shape, F32),
        ],
        scratch_shapes=[
            pltpu.VMEM((n_seq, sdim, GLA_W), F32),
            pltpu.VMEM((n_seq, RWKV_GROUPS, GROUP_W, GROUP_W), F32),
            pltpu.VMEM((n_seq, 1, RWKV_PROJ), F32),
        ],
        compiler_params=pltpu.CompilerParams(
            dimension_semantics=("arbitrary", "arbitrary"), vmem_limit_bytes=VMEM_LIMIT),
        name="mix_seq",
    )(ug3, gla_s0, *gla_params, ur3, shift0, rwkv_s0, *rwkv_params)


FF_CHUNK = D_FF // 2


def _post_kernel(x_ref, og_ref, or_ref, p_ref, wo_ref, nffn_ref, wg_ref, wu_ref, wd_ref,
                 nple_ref, wpg_ref, wpp_ref, nf_ref, y_ref):
    half = x_ref.shape[0] // 2
    parts = [slice(0, half), slice(half, 2 * half)]
    dot = lambda a, b: jnp.dot(a, b, preferred_element_type=F32)
    o = [jnp.concatenate([og_ref[p, :], or_ref[p, :]], axis=1) for p in parts]
    x = [x_ref[p, :] + dot(oo, wo_ref[...]) for p, oo in zip(parts, o)]
    h2 = [_rms(xx, nffn_ref[...]).astype(BF16) for xx in x]
    for i in range(0, D_FF, FF_CHUNK):
        gate = [dot(h, wg_ref[:, i:i + FF_CHUNK]) for h in h2]
        up = [dot(h, wu_ref[:, i:i + FF_CHUNK]) for h in h2]
        act = [(g * _sigmoid(g) * u).astype(BF16) for g, u in zip(gate, up)]
        x = [xx + dot(a, wd_ref[i:i + FF_CHUNK, :]) for xx, a in zip(x, act)]
    h3 = [_rms(xx, nple_ref[...]).astype(BF16) for xx in x]
    pg = [_sigmoid(dot(h, wpg_ref[...])) for h in h3]
    pp = [dot(p_ref[p, :].astype(BF16), wpp_ref[...]) for p in parts]
    for p, xx, a, b in zip(parts, x, pg, pp):
        y_ref[p, :] = _rms(xx + a * b, nf_ref[...])


def _post(x2d, og, orw, p2d, weights, tm):
    n = x2d.shape[0]
    wo, nffn, wg, wu, wd, nple, wpg, wpp, nf = weights
    tok = lambda w: pl.BlockSpec((tm, w), lambda i: (i, 0))
    return pl.pallas_call(
        _post_kernel,
        grid=(n // tm,),
        in_specs=[
            tok(D_MODEL), tok(GLA_W), tok(RWKV_W), tok(PLE_DIM),
            _const_spec((D_MODEL, D_MODEL)), _const_spec((1, D_MODEL)),
            _const_spec((D_MODEL, D_FF)), _const_spec((D_MODEL, D_FF)), _const_spec((D_FF, D_MODEL)),
            _const_spec((1, D_MODEL)), _const_spec((D_MODEL, D_MODEL)), _const_spec((PLE_DIM, D_MODEL)),
            _const_spec((1, D_MODEL)),
        ],
        out_specs=tok(D_MODEL),
        out_shape=jax.ShapeDtypeStruct((n, D_MODEL), F32),
        compiler_params=pltpu.CompilerParams(
            dimension_semantics=("arbitrary",), vmem_limit_bytes=VMEM_LIMIT),
        name="post",
    )(x2d, og, orw, p2d, wo, nffn, wg, wu, wd, nple, wpg, wpp, nf)


PROMPT_CHUNK = 64
PROMPT_SEQS_PER_STEP = 8
DEC_TILE_SEQS = 32
TOKEN_TILE = 512
PROJ_TILE = 1024


def kernel(x_prompt, x_sample, state_gla, state_rwkv, state_shift, p_prompt, p_sample, norm_mix, w_in, gla_gk_up, gla_gk_bias, gla_norm, rwkv_mu, rwkv_w0, rwkv_w2, rwkv_a0, rwkv_a2, rwkv_g2, rwkv_k_k, rwkv_k_a, rwkv_r_k, rwkv_ln_w, rwkv_ln_b, w_out, norm_ffn, w_gate, w_up, w_down, norm_ple, w_ple_gate, w_ple_proj, norm_final):
    assert w_in.shape[0] == 1
    i = 0
    rowv = lambda a: a.astype(F32).reshape(1, -1)
    zeros = lambda r, c: jnp.zeros((r, c), F32)
    w_in_i = w_in[i]
    w_gla = jnp.concatenate(
        [w_in_i[:, :GLA_PROJ], zeros(D_MODEL, GLA_COLS - GLA_PROJ)], axis=1).astype(BF16)
    w_rw = w_in_i[:, GLA_PROJ:].astype(BF16)
    gk_up = jnp.concatenate(
        [gla_gk_up[i].astype(F32), zeros(LANES - GLA_GATE_RANK, GLA_K_W)], axis=0).astype(BF16)
    seg = jnp.arange(SEG_W)[:, None] // RWKV_HEAD == jnp.arange(SEG_W)[None, :] // RWKV_HEAD
    rwkv_params = (
        rowv(rwkv_mu[i]), rowv(rwkv_w0[i]),
        jnp.concatenate([rwkv_w2[i].astype(F32), zeros(64, RWKV_W)], axis=0).astype(BF16),
        rowv(rwkv_a0[i]),
        jnp.concatenate([zeros(64, RWKV_W), rwkv_a2[i].astype(F32)], axis=0).astype(BF16),
        rwkv_g2[i].astype(BF16),
        rowv(rwkv_k_k[i]), rowv(rwkv_k_a[i]), rowv(rwkv_r_k[i]), rowv(rwkv_ln_w[i]), rowv(rwkv_ln_b[i]),
        seg.astype(BF16),
    )
    post_w = (
        w_out[i].astype(BF16), rowv(norm_ffn[i]), w_gate[i].astype(BF16), w_up[i].astype(BF16),
        w_down[i].astype(BF16), rowv(norm_ple[i]), w_ple_gate[i].astype(BF16),
        w_ple_proj[i].astype(BF16), rowv(norm_final),
    )
    gla_w = (gk_up, rowv(gla_gk_bias[i]), rowv(gla_norm[i]))
    g_mix = rowv(norm_mix[i])

    bp, tp, _ = x_prompt.shape
    xp = x_prompt.astype(F32).reshape(bp * tp, D_MODEL)
    ug, ur = _proj(xp, g_mix, w_gla, w_rw, PROJ_TILE)
    ur3 = ur.reshape(bp, tp, RWKV_PROJ)
    og, gla_p, orw, rwkv_p = _mix_seq(
        ug.reshape(bp, tp, GLA_COLS), jnp.zeros((bp, GLA_HEADS * GLA_DK, GLA_DV), F32), gla_w,
        ur3, jnp.zeros((bp, 1, RWKV_PROJ), F32),
        jnp.zeros((1, bp, RWKV_HEADS, RWKV_HEAD, RWKV_HEAD), F32), rwkv_params,
        PROMPT_CHUNK, PROMPT_SEQS_PER_STEP)
    shift_p = ur3[:, tp - 1]
    yp = _post(xp, og.reshape(bp * tp, GLA_W), orw.reshape(bp * tp, RWKV_W),
               p_prompt[i].reshape(bp * tp, PLE_DIM), post_w, TOKEN_TILE)

    bs, ts, _ = x_sample.shape
    xs = x_sample.astype(F32).reshape(bs * ts, D_MODEL)
    ug, ur = _proj(xs, g_mix, w_gla, w_rw, TOKEN_TILE)
    og, gla_s = _gla_dec(ug, state_gla[i].astype(F32).reshape(bs, GLA_HEADS * GLA_DK, GLA_DV),
                         gla_w, ts, DEC_TILE_SEQS)
    shiftx = jnp.pad(state_shift[i].astype(F32)[:, None, :], ((0, 0), (0, ts - 1), (0, 0)))
    orw, rwkv_s = _rwkv_step(ur, shiftx.reshape(bs * ts, RWKV_PROJ),
                             jnp.transpose(state_rwkv[i].astype(F32), (1, 2, 3, 0)), rwkv_params, ts)
    rwkv_s = jnp.transpose(rwkv_s, (3, 0, 1, 2))[None]
    shift_s = ur.reshape(bs, ts, RWKV_PROJ)[:, ts - 1]
    ys = _post(xs, og, orw, p_sample[i].reshape(bs * ts, PLE_DIM), post_w, TOKEN_TILE)

    gla_shape = (1, -1, GLA_HEADS, GLA_DK, GLA_DV)
    return (yp.reshape(bp, tp, D_MODEL).astype(x_prompt.dtype),
            ys.reshape(bs, ts, D_MODEL).astype(x_sample.dtype),
            gla_p.reshape(gla_shape).astype(state_gla.dtype), rwkv_p.astype(state_rwkv.dtype),
            shift_p[None].astype(state_shift.dtype),
            gla_s.reshape(gla_shape).astype(state_gla.dtype), rwkv_s.astype(state_rwkv.dtype),
            shift_s[None].astype(state_shift.dtype))
```

```python
import functools

import jax
import jax.numpy as jnp
from jax import lax
from jax.experimental import pallas as pl
from jax.experimental.pallas import tpu as pltpu

F32 = jnp.float32
BF16 = jnp.bfloat16

D_MODEL = 1024
GLA_HEADS = 4
GLA_DK = 64
GLA_DV = 128
GLA_K_W = GLA_HEADS * GLA_DK
GLA_W = GLA_HEADS * GLA_DV
GLA_GATE_RANK = 16
GLA_GATE_NORM = 16.0
GLA_MAIN = 2 * GLA_K_W + 2 * GLA_W
GLA_PROJ = GLA_MAIN + GLA_GATE_RANK
LANES = 128
SUBLANES = 8
GLA_COLS = GLA_MAIN + LANES
RWKV_HEAD = 64
RWKV_HEADS = 8
RWKV_W = RWKV_HEADS * RWKV_HEAD
RWKV_PROJ = 3 * RWKV_W + 64 + 64 + 128
D_FF = 2816
PLE_DIM = 256
EPS = 1e-6
RWKV_GN_EPS = 64e-5

VMEM_LIMIT = 56 * 1024 * 1024

NN = ((1,), (0,))
NT = ((1,), (1,))


def _split(x, n):
    parts = []
    r = x
    for i in range(n):
        p = r.astype(BF16)
        parts.append(p)
        if i + 1 < n:
            r = r - p.astype(F32)
    return parts


def _mm(a, b, dims=NN, pa=1, pb=1):
    pieces_a = _split(a, pa)
    pieces_b = _split(b, pb)
    n = max(pa, pb)
    acc = None
    for i, ai in enumerate(pieces_a):
        for j, bj in enumerate(pieces_b):
            if i + j < n:
                t = lax.dot_general(ai, bj, (dims, ((), ())), preferred_element_type=F32)
                acc = t if acc is None else acc + t
    return acc


def _mm_tn(a, b, pa=1, pb=1):
    rows = a.shape[0]
    pad = (-rows) % LANES
    if pad:
        a = jnp.concatenate([a, jnp.zeros((pad, a.shape[1]), a.dtype)], axis=0)
        b = jnp.concatenate([b, jnp.zeros((pad, b.shape[1]), b.dtype)], axis=0)
    return _mm(a.T, b, NN, pa, pb)


def _iota(shape, dim):
    return lax.broadcasted_iota(jnp.int32, shape, dim)


def _log_sigmoid(z):
    return jnp.minimum(z, 0.0) - jnp.log(1.0 + jnp.exp(-jnp.abs(z)))


def _shift_rows(u, first):
    r = pltpu.roll(u, 1, axis=0)
    head = jnp.where(_iota((SUBLANES, 1), 0) == 0, first, r[:SUBLANES])
    return jnp.concatenate([head, r[SUBLANES:]], axis=0)


def _sigmoid(z):
    return 1.0 / (1.0 + jnp.exp(-z))


def _tanh(z):
    return 2.0 * _sigmoid(2.0 * z) - 1.0


def _rms(x, g):
    return x * lax.rsqrt(jnp.mean(x * x, axis=-1, keepdims=True) + EPS) * g


GROUP_HEADS = 2
GROUP_W = GROUP_HEADS * RWKV_HEAD
RWKV_GROUPS = RWKV_W // GROUP_W
SEG_W = 2 * LANES


def _head_masks(rows, head_w, heads):
    lane = _iota((rows, heads * head_w), 1)
    return [jnp.where(lane // head_w == h, 1.0, 0.0).astype(BF16) for h in range(heads)]


def _stack_heads(x, hmasks):
    xb = x.astype(BF16)
    return jnp.concatenate([xb * m for m in hmasks], axis=0)


def _block_diag(x_cat, blk):
    heads = blk.shape[0] // x_cat.shape[0]
    return jnp.concatenate([x_cat.astype(BF16)] * heads, axis=0) * blk


def _chunk_masks(rows, seq_len, heads):
    t = _iota((rows, heads * rows), 0)
    s = _iota((rows, heads * rows), 1) % rows
    incl = t >= s
    t2 = _iota((2 * rows, rows), 0)
    s2 = _iota((2 * rows, rows), 1)
    cum_rows = (t2 < rows) & (t2 >= s2)
    tot_rows = t2 >= rows
    if seq_len < rows:
        incl = incl & ((t // seq_len) == (s // seq_len))
        same2 = ((t2 % rows) // seq_len) == (s2 // seq_len)
        cum_rows, tot_rows = cum_rows & same2, tot_rows & same2
    big = heads * rows
    blk = jnp.where(_iota((big, big), 0) // rows == _iota((big, big), 1) // rows, 1.0, 0.0)
    t4 = _iota((2 * rows, 2 * big), 0)
    s4 = _iota((2 * rows, 2 * big), 1) % rows
    quad = ((t4 % rows) > s4) | ((t4 >= rows) & ((t4 % rows) == s4))
    if seq_len < rows:
        quad = quad & (((t4 % rows) // seq_len) == (s4 // seq_len))
    return dict(incl=incl, quad=quad, eye=jnp.where(t == s, 1.0, 0.0),
                sums=jnp.where(cum_rows | tot_rows, 1.0, 0.0).astype(BF16), blk=blk.astype(BF16))


def _proj_kernel(x_ref, g_ref, wg_ref, wr_ref, *refs):
    n_cast = (len(refs) - 2) // 2
    ug_ref, ur_ref = refs[n_cast], refs[n_cast + 1]
    h = _rms(x_ref[...], g_ref[...]).astype(BF16)
    ug_ref[...] = jnp.dot(h, wg_ref[...], preferred_element_type=F32)
    ur_ref[...] = jnp.dot(h, wr_ref[...], preferred_element_type=F32)
    for src_ref, dst_ref in zip(refs[:n_cast], refs[n_cast + 2:]):
        dst_ref[...] = src_ref[...].astype(BF16)


def _const_spec(shape):
    return pl.BlockSpec(shape, lambda *_: (0,) * len(shape), pipeline_mode=pl.Buffered(1))


def _proj(x2d, g, w_gla, w_rw, tm, to_bf16=()):
    n = x2d.shape[0]
    steps = n // tm
    slabs = [pl.BlockSpec((w.shape[0] // steps, w.shape[1]), lambda i: (i, 0)) for w in to_bf16]
    return pl.pallas_call(
        _proj_kernel,
        grid=(steps,),
        in_specs=[
            pl.BlockSpec((tm, D_MODEL), lambda i: (i, 0)),
            _const_spec((1, D_MODEL)),
            _const_spec((D_MODEL, GLA_COLS)),
            _const_spec((D_MODEL, RWKV_PROJ)),
        ] + slabs,
        out_specs=[
            pl.BlockSpec((tm, GLA_COLS), lambda i: (i, 0)),
            pl.BlockSpec((tm, RWKV_PROJ), lambda i: (i, 0)),
        ] + slabs,
        out_shape=[
            jax.ShapeDtypeStruct((n, GLA_COLS), F32),
            jax.ShapeDtypeStruct((n, RWKV_PROJ), F32),
        ] + [jax.ShapeDtypeStruct(w.shape, BF16) for w in to_bf16],
        compiler_params=pltpu.CompilerParams(
            dimension_semantics=("arbitrary",), vmem_limit_bytes=VMEM_LIMIT),
        name="proj",
    )(x2d, g, w_gla, w_rw, *to_bf16)


def _gla_prep(us, gkup, gkb, sums):
    rows = us[0].shape[0]
    z = [_mm(u[:, GLA_MAIN:GLA_COLS], gkup) for u in us]
    log_a = [_log_sigmoid(x + gkb) * (1.0 / GLA_GATE_NORM) for x in z]
    cums = [_mm(sums, x, NN, 1, 2) for x in log_a]
    out = []
    for u, x in zip(us, cums):
        cum, cum_end = x[:rows], x[rows:]
        q = u[:, 0:GLA_K_W] * (GLA_DK ** -0.5)
        k = u[:, GLA_K_W:2 * GLA_K_W]
        out.append((q * jnp.exp(cum), k * jnp.exp(-cum), k * jnp.exp(cum_end - cum),
                    u[:, 2 * GLA_K_W:2 * GLA_K_W + GLA_W], u[:, 2 * GLA_K_W + GLA_W:GLA_MAIN],
                    jnp.exp(cum_end)))
    return out


def _gla_intra(q_i, k_i, v, m, kmasks, vmasks):
    scores = jnp.where(m["incl"], _mm(q_i, _stack_heads(k_i, kmasks), NT), 0.0)
    return _mm(scores, _stack_heads(v, vmasks))


def _gla_out(o, gate, gnorm):
    heads = [slice(h * GLA_DV, (h + 1) * GLA_DV) for h in range(GLA_HEADS)]
    return jnp.concatenate(
        [_rms(o[:, hl], gnorm) * (gate[:, hl] * _sigmoid(gate[:, hl])) for hl in heads],
        axis=1).astype(BF16)


def _gla_block_mask():
    return (_iota((GLA_K_W, GLA_W), 0) // GLA_DK) == (_iota((GLA_K_W, GLA_W), 1) // GLA_DV)


def _gla_state_in(s, blk):
    return jnp.where(blk, jnp.concatenate([s] * GLA_HEADS, axis=1), 0.0)


def _gla_state_out(s_bd):
    heads = [s_bd[:, h * GLA_DV:(h + 1) * GLA_DV] for h in range(GLA_HEADS)]
    return (heads[0] + heads[1]) + (heads[2] + heads[3])


def _lane_tiled_t(x):
    pad = LANES - x.shape[0]
    if pad:
        x = jnp.concatenate([x, jnp.zeros((pad, x.shape[1]), x.dtype)], axis=0)
    return x.T


def _gla_seq_part(u_ref, s0_ref, gkup_ref, gkb_ref, gn_ref, o_ref, sout_ref, s_scr, *, phase):
    n_seq, rows = u_ref.shape[0], u_ref.shape[1]
    blk = _gla_block_mask()

    def init():
        for b in range(n_seq):
            s_scr[b] = _gla_state_in(s0_ref[b], blk)

    def final():
        for b in range(n_seq):
            sout_ref[b] = _gla_state_out(s_scr[b])

    if phase == "init":
        return init()
    if phase == "final":
        return final()

    m = _chunk_masks(rows, rows, GLA_HEADS)
    kmasks = _head_masks(rows, GLA_DK, GLA_HEADS)
    vmasks = _head_masks(rows, GLA_DV, GLA_HEADS)
    tok = _gla_prep([u_ref[b] for b in range(n_seq)], gkup_ref[...], gkb_ref[...], m["sums"])
    s_old = [s_scr[b] for b in range(n_seq)]
    intra = [_gla_intra(q_i, k_i, v, m, kmasks, vmasks) for q_i, k_i, _, v, _, _ in tok]
    inter = [_mm(t[0], s) for t, s in zip(tok, s_old)]
    kv = [_mm_tn(k_e, v) for _, _, k_e, v, _, _ in tok]
    for b in range(n_seq):
        gate, g_end = tok[b][4], tok[b][5]
        o_ref[b] = _gla_out(intra[b] + inter[b], gate, gn_ref[...])
        dec = jnp.concatenate([_lane_tiled_t(jnp.broadcast_to(g_end[0:1], (LANES, GLA_K_W)))] * GLA_HEADS,
                              axis=1)
        s_scr[b] = dec * s_old[b] + jnp.where(blk, kv[b], 0.0)


def _gla_dec_kernel(u_ref, s0_ref, gkup_ref, gkb_ref, gn_ref, o_ref, sout_ref, *, seq_len):
    rows = u_ref.shape[0]
    n_seq = rows // seq_len
    blk = _gla_block_mask()
    m = _chunk_masks(rows, seq_len, GLA_HEADS)
    q_i, k_i, k_e, v, gate, g_end = _gla_prep([u_ref[...]], gkup_ref[...], gkb_ref[...], m["sums"])[0]
    o = _gla_intra(q_i, k_i, v, m, _head_masks(rows, GLA_DK, GLA_HEADS),
                   _head_masks(rows, GLA_DV, GLA_HEADS))
    k_et = _lane_tiled_t(k_e).astype(BF16)
    dec_t = _lane_tiled_t(g_end)
    v_pad = v if rows == LANES else jnp.concatenate([v, jnp.zeros((LANES - rows, GLA_W), F32)], axis=0)
    row = _iota((rows, 1), 0)
    row_pad = _iota((LANES, 1), 0)
    for j in range(n_seq):
        s_bd = _gla_state_in(s0_ref[j], blk)
        o = o + _mm(jnp.where(row // seq_len == j, q_i, 0.0), s_bd)
        kv = _mm(k_et, jnp.where(row_pad // seq_len == j, v_pad, 0.0))
        first = j * seq_len
        sout_ref[j] = _gla_state_out(dec_t[:, first:first + 1] * s_bd + jnp.where(blk, kv, 0.0))
    o_ref[...] = _gla_out(o, gate, gn_ref[...])


def _gla_param_specs():
    return [_const_spec((LANES, GLA_K_W)), _const_spec((1, GLA_K_W)), _const_spec((1, GLA_DV))]


def _gla_dec(u2, s0, params, seq_len, n_seq):
    n = u2.shape[0]
    rows = n_seq * seq_len
    sdim = GLA_HEADS * GLA_DK
    state_spec = pl.BlockSpec((n_seq, sdim, GLA_DV), lambda i: (i, 0, 0))
    return pl.pallas_call(
        functools.partial(_gla_dec_kernel, seq_len=seq_len),
        grid=(n // rows,),
        in_specs=[pl.BlockSpec((rows, GLA_COLS), lambda i: (i, 0)), state_spec] + _gla_param_specs(),
        out_specs=[pl.BlockSpec((rows, GLA_W), lambda i: (i, 0)), state_spec],
        out_shape=[
            jax.ShapeDtypeStruct((n, GLA_W), BF16),
            jax.ShapeDtypeStruct(s0.shape, F32),
        ],
        compiler_params=pltpu.CompilerParams(
            dimension_semantics=("arbitrary",), vmem_limit_bytes=VMEM_LIMIT),
        name="gla_dec",
    )(u2, s0, *params)


def _seg_sum(x, seg, pa=1):
    return jnp.concatenate(
        [_mm(x[:, i:i + SEG_W], seg, NN, pa, 1) for i in range(0, RWKV_W, SEG_W)], axis=1)


W_OFFSET_SCALE = 0.6065306597126334


def _rwkv_tokens(u, prev, mu, w0, w2, a0, a2, g2, k_k, k_a, rk, seg):
    xr = u + mu * (prev - u)
    r = xr[:, 0:RWKV_W]
    kr = xr[:, RWKV_W:2 * RWKV_W]
    vr = xr[:, 2 * RWKV_W:3 * RWKV_W]
    wa = xr[:, 3 * RWKV_W:3 * RWKV_W + LANES]
    gd = xr[:, 3 * RWKV_W + LANES:RWKV_PROJ]
    lw = _sigmoid(w0 + _mm(_tanh(wa), w2)) * (-W_OFFSET_SCALE)
    a_sig = _sigmoid(a0 + _mm(wa, a2))
    gate = _mm(_sigmoid(gd), g2)
    kk = kr * k_k
    kk = kk * lax.rsqrt(jnp.maximum(_seg_sum(kk * kk, seg), 1e-24))
    kr = kr * (1.0 + (a_sig - 1.0) * k_a)
    bonus = _seg_sum(r * kr * rk, seg) * vr
    return r, kr, vr, lw, -kk, kk * a_sig, gate, bonus


def _rwkv_prep(u, prev, mu, w0, w2, a0, a2, g2, k_k, k_a, rk, seg, sums):
    rows = u.shape[0]
    r, kr, vr, lw, a_vec, b_vec, gate, bonus = _rwkv_tokens(
        u, prev, mu, w0, w2, a0, a2, g2, k_k, k_a, rk, seg)
    block = sums.shape[1]
    cums = [_mm(sums, lw[i:i + block], NN, 1, 2) for i in range(0, rows, block)]
    cum = jnp.concatenate([x[:block] for x in cums], axis=0)
    cum_end = jnp.concatenate([x[block:] for x in cums], axis=0)
    e_neg = jnp.exp(-cum)
    g_end = jnp.exp(cum_end)
    e_end = g_end * e_neg
    bf = lambda x: x.astype(BF16)
    return (bf(a_vec * jnp.exp(cum - lw)), bf(r * jnp.exp(cum)), bf(kr * e_neg), bf(b_vec * e_neg),
            bf(kr * e_end), bf(b_vec * e_end), bf(vr), gate, bonus, g_end)


def _rwkv_intra(units, m, hmasks, n_double):
    rows = units[0][0].shape[0]
    stack = lambda x: _stack_heads(x, hmasks)
    cat_w = m["incl"].shape[1]
    g = [_mm(jnp.concatenate([at, rt], axis=0), jnp.concatenate([stack(bt), stack(kt)], axis=0), NT)
         for at, rt, kt, bt, _ in units]
    g = [jnp.where(m["quad"], x, 0.0) for x in g]
    a_ab = [x[:rows, :cat_w] for x in g]
    a_rb = [x[rows:, :cat_w] for x in g]
    a_ak = [x[:rows, cat_w:] for x in g]
    a_rk = [x[rows:, cat_w:] for x in g]
    tinv = [m["eye"] + a for a in a_ab]
    apow = [_mm(a, _block_diag(a, m["blk"])) for a in a_ab]
    for _ in range(n_double - 1):
        both = [_mm(jnp.concatenate([t, a], axis=0), _block_diag(a, m["blk"])) for t, a in zip(tinv, apow)]
        tinv = [t + x[:rows] for t, x in zip(tinv, both)]
        apow = [x[rows:] for x in both]
    tinv = [t + _mm(t, _block_diag(a, m["blk"])) for t, a in zip(tinv, apow)]
    v_s = [stack(vp) for _, _, _, _, vp in units]
    akv = [_mm(jnp.concatenate([a, b], axis=0), v) for a, b, v in zip(a_ak, a_rk, v_s)]
    wu = [_mm(t, jnp.concatenate([stack(at), stack(x[:rows])], axis=1))
          for t, (at, _, _, _, _), x in zip(tinv, units, akv)]
    z = [_mm(a, jnp.concatenate([stack(x[:, :GROUP_W]), stack(x[:, GROUP_W:])], axis=1))
         for a, x in zip(a_rb, wu)]
    return [(x[:, :GROUP_W], x[:, GROUP_W:], rt + zz[:, :GROUP_W], zz[:, GROUP_W:] + kv[rows:])
            for x, (_, rt, _, _, _), zz, kv in zip(wu, units, z, akv)]


def _rwkv_out(y, gate, bonus, lnw, lnb, seg):
    inv = 1.0 / RWKV_HEAD
    yc = y - _seg_sum(y, seg, 2) * inv
    var = _seg_sum(yc * yc, seg) * inv
    return ((yc * lax.rsqrt(var + RWKV_GN_EPS) * lnw + lnb + bonus) * gate).astype(BF16)


def _rwkv_state_in(s_ref, idx, g):
    zero = jnp.zeros((RWKV_HEAD, RWKV_HEAD), F32)
    blocks = []
    for h in range(GROUP_HEADS):
        parts = [zero] * GROUP_HEADS
        parts[h] = s_ref[idx, g * GROUP_HEADS + h]
        blocks.append(jnp.concatenate(parts, axis=1))
    return jnp.concatenate(blocks, axis=0)


def _rwkv_state_out(s_ref, idx, g, s2):
    for h in range(GROUP_HEADS):
        sl = slice(h * RWKV_HEAD, (h + 1) * RWKV_HEAD)
        s_ref[idx, g * GROUP_HEADS + h] = s2[sl, sl]


def _head_block_mask():
    return (_iota((GROUP_W, GROUP_W), 0) // RWKV_HEAD) == (_iota((GROUP_W, GROUP_W), 1) // RWKV_HEAD)


def _rwkv_seq_part(u_ref, shift0_ref, s0_ref, mu_ref, w0_ref, w2_ref, a0_ref, a2_ref,
                   g2_ref, kk_ref, ka_ref, rk_ref, lnw_ref, lnb_ref, seg_ref,
                   o_ref, sout_ref, s_scr, prev_scr, *, phase):
    n_seq, rows = u_ref.shape[0], u_ref.shape[1]
    groups = [slice(g * GROUP_W, (g + 1) * GROUP_W) for g in range(RWKV_GROUPS)]
    ids = [(b, g) for b in range(n_seq) for g in range(RWKV_GROUPS)]

    def init():
        for b in range(n_seq):
            for g in range(RWKV_GROUPS):
                s_scr[b, g] = _rwkv_state_in(s0_ref, b, g)
            prev_scr[b] = shift0_ref[b]

    def final():
        for b, g in ids:
            _rwkv_state_out(sout_ref, b, g, s_scr[b, g])

    if phase == "init":
        return init()
    if phase == "final":
        return final()

    seg = seg_ref[...]
    m = _chunk_masks(rows, rows, GROUP_HEADS)
    hmasks = _head_masks(rows, RWKV_HEAD, GROUP_HEADS)
    head_blk = _head_block_mask()
    n_double = rows.bit_length() - 2

    us = [u_ref[b] for b in range(n_seq)]
    prev = jnp.concatenate([_shift_rows(u, prev_scr[b]) for b, u in enumerate(us)], axis=0)
    for b, u in enumerate(us):
        prev_scr[b] = u[rows - 1:rows]
    at, rt, kt, bt, ke, be, vb, gate, bonus, g_end = _rwkv_prep(
        jnp.concatenate(us, axis=0), prev, mu_ref[...], w0_ref[...], w2_ref[...], a0_ref[...],
        a2_ref[...], g2_ref[...], kk_ref[...], ka_ref[...], rk_ref[...], seg, m["sums"])
    tok, units = [], []
    for b in range(n_seq):
        sl = slice(b * rows, (b + 1) * rows)
        tok.append((vb[sl], ke[sl], be[sl], g_end[b * rows:b * rows + 1]))
        units += [(at[sl, gl], rt[sl, gl], kt[sl, gl], bt[sl, gl], vb[sl, gl]) for gl in groups]

    intra = _rwkv_intra(units, m, hmasks, n_double)
    s_old = [s_scr[b, g] for b, g in ids]
    uy = [_mm(jnp.concatenate([w_m, r_m], axis=0), s2, NT) for (w_m, _, r_m, _), s2 in zip(intra, s_old)]
    upd = []
    for (b, g), (_, u0, _, _), x in zip(ids, intra, uy):
        vb_b, ke_b, be_b, _ = tok[b]
        gl = groups[g]
        upd.append(_mm_tn(jnp.concatenate([x[:rows] + u0, vb_b[:, gl].astype(F32)], axis=0),
                          jnp.concatenate([be_b[:, gl], ke_b[:, gl]], axis=0)))
    for (b, g), s2, d in zip(ids, s_old, upd):
        s_scr[b, g] = s2 * tok[b][3][:, groups[g]] + jnp.where(head_blk, d, 0.0)
    ys = [jnp.concatenate([uy[i][rows:] + intra[i][3] for i, (bb, _) in enumerate(ids) if bb == b],
                          axis=1) for b in range(n_seq)]
    out = _rwkv_out(jnp.concatenate(ys, axis=0), gate, bonus, lnw_ref[...], lnb_ref[...], seg)
    for b in range(n_seq):
        o_ref[b] = out[b * rows:(b + 1) * rows]


STEP_HEADS = 2
STEP_ROWS = 4
N_STEP_OPERANDS = 6


def _sum_keys(x):
    t = x[0:SUBLANES]
    for i in range(SUBLANES, x.shape[0], SUBLANES):
        t = t + x[i:i + SUBLANES]
    for shift in (4, 2, 1):
        t = t + pltpu.roll(t, shift, axis=0)
    return t


def _rwkv_step_kernel(u_ref, shiftx_ref, s0_ref, mu_ref, w0_ref, w2_ref, a0_ref, a2_ref, g2_ref,
                      kk_ref, ka_ref, rk_ref, lnw_ref, lnb_ref, seg_ref,
                      o_ref, sout_ref, ops_scr, y_scr, tmp_scr, aux_scr, *, seq_len):
    p = pl.program_id(0)
    n_tok = u_ref.shape[0]
    n_seq = n_tok // seq_len
    seg = seg_ref[...]
    cols = [slice(c, c + LANES) for c in range(0, RWKV_W, LANES)]

    @pl.when(p == 0)
    def _():
        u = u_ref[...]
        row = _iota((n_tok, 1), 0)
        prev = jnp.where(row % seq_len == 0, shiftx_ref[...], pltpu.roll(u, 1, axis=0))
        r, kr, vr, lw, a_vec, b_vec, gate, bonus = _rwkv_tokens(
            u, prev, mu_ref[...], w0_ref[...], w2_ref[...], a0_ref[...], a2_ref[...], g2_ref[...],
            kk_ref[...], ka_ref[...], rk_ref[...], seg)
        aux_scr[0] = gate
        aux_scr[1] = bonus
        for i, x in enumerate((jnp.exp(lw), a_vec, b_vec, kr, r, vr)):
            for c, cl in enumerate(cols):
                tmp_scr[c] = x[:, cl]
            for t in range(seq_len):
                for c, cl in enumerate(cols):
                    ops_scr[i, t, cl, :] = tmp_scr[c, pl.ds(t, n_seq, stride=seq_len), :].T

    def rows_step(i, carry):
        for hh in range(STEP_HEADS):
            chan = pl.multiple_of((p * STEP_HEADS + hh) * RWKV_HEAD, RWKV_HEAD)
            for j in range(STEP_ROWS):
                v = i * STEP_ROWS + j
                s = s0_ref[hh, v]
                for t in range(seq_len):
                    w, a, b, k, r = (ops_scr[n, t, pl.ds(chan, RWKV_HEAD), :] for n in range(5))
                    v_t = ops_scr[5, t, pl.ds(chan + v, 1), :]
                    sa = jnp.concatenate([_sum_keys(s * a)] * (RWKV_HEAD // SUBLANES), axis=0)
                    s = s * w + sa * b + v_t * k
                    y_scr[t, pl.ds(chan + v, 1), :] = _sum_keys(s * r)[0:1]
                sout_ref[hh, v] = s
        return carry

    lax.fori_loop(0, RWKV_HEAD // STEP_ROWS, rows_step, 0)

    @pl.when(p == pl.num_programs(0) - 1)
    def _():
        for t in range(seq_len):
            for c, cl in enumerate(cols):
                tmp_scr[c, pl.ds(t, n_seq, stride=seq_len), :] = y_scr[t, cl, :].T
        y = jnp.concatenate([tmp_scr[c] for c in range(len(cols))], axis=1)
        o_ref[...] = _rwkv_out(y, aux_scr[0], aux_scr[1], lnw_ref[...], lnb_ref[...], seg)


def _rwkv_step(u2, shiftx, s0_t, params, seq_len):
    n = u2.shape[0]
    n_seq = n // seq_len
    state_spec = pl.BlockSpec((STEP_HEADS, RWKV_HEAD, RWKV_HEAD, n_seq), lambda i: (i, 0, 0, 0))
    whole = lambda w: pl.BlockSpec((n, w), lambda i: (0, 0), pipeline_mode=pl.Buffered(1))
    return pl.pallas_call(
        functools.partial(_rwkv_step_kernel, seq_len=seq_len),
        grid=(RWKV_HEADS // STEP_HEADS,),
        in_specs=[whole(RWKV_PROJ), whole(RWKV_PROJ), state_spec] + _rwkv_param_specs(),
        out_specs=[pl.BlockSpec((n, RWKV_W), lambda i: (0, 0)), state_spec],
        out_shape=[
            jax.ShapeDtypeStruct((n, RWKV_W), BF16),
            jax.ShapeDtypeStruct(s0_t.shape, F32),
        ],
        scratch_shapes=[
            pltpu.VMEM((N_STEP_OPERANDS, seq_len, RWKV_W, n_seq), F32),
            pltpu.VMEM((seq_len, RWKV_W, n_seq), F32),
            pltpu.VMEM((RWKV_W // LANES, n, LANES), F32),
            pltpu.VMEM((2, n, RWKV_W), F32),
        ],
        compiler_params=pltpu.CompilerParams(
            dimension_semantics=("arbitrary",), vmem_limit_bytes=VMEM_LIMIT),
        name="rwkv_step",
    )(u2, shiftx, s0_t, *params)


def _rwkv_param_specs():
    vec = lambda n: _const_spec((1, n))
    return [
        vec(RWKV_PROJ),
        vec(RWKV_W),
        _const_spec((LANES, RWKV_W)),
        vec(RWKV_W),
        _const_spec((LANES, RWKV_W)),
        _const_spec((LANES, RWKV_W)),
        vec(RWKV_W), vec(RWKV_W), vec(RWKV_W), vec(RWKV_W), vec(RWKV_W),
        _const_spec((SEG_W, SEG_W)),
    ]


N_GLA_SEQ_IN = 5
N_RWKV_SEQ_IN = 15


def _mix_seq_kernel(*refs):
    n_in = N_GLA_SEQ_IN + N_RWKV_SEQ_IN
    gla = refs[:N_GLA_SEQ_IN] + refs[n_in:n_in + 2] + refs[n_in + 4:n_in + 5]
    rwkv = refs[N_GLA_SEQ_IN:n_in] + refs[n_in + 2:n_in + 4] + refs[n_in + 5:]
    c = pl.program_id(1)

    @pl.when(c == 0)
    def _():
        _gla_seq_part(*gla, phase="init")
        _rwkv_seq_part(*rwkv, phase="init")

    _rwkv_seq_part(*rwkv, phase="body")
    _gla_seq_part(*gla, phase="body")

    @pl.when(c == pl.num_programs(1) - 1)
    def _():
        _gla_seq_part(*gla, phase="final")
        _rwkv_seq_part(*rwkv, phase="final")


def _mix_seq(ug3, gla_s0, gla_params, ur3, shift0, rwkv_s0, rwkv_params, rows, n_seq):
    b, t, _ = ug3.shape
    sdim = GLA_HEADS * GLA_DK
    gla_state = pl.BlockSpec((n_seq, sdim, GLA_DV), lambda i, j: (i, 0, 0))
    rwkv_state = pl.BlockSpec((None, n_seq, RWKV_HEADS, RWKV_HEAD, RWKV_HEAD),
                              lambda i, j: (0, i, 0, 0, 0))
    tok = lambda w: pl.BlockSpec((n_seq, rows, w), lambda i, j: (i, j, 0))
    return pl.pallas_call(
        _mix_seq_kernel,
        grid=(b // n_seq, t // rows),
        in_specs=[tok(GLA_COLS), gla_state] + _gla_param_specs()
        + [tok(RWKV_PROJ), pl.BlockSpec((n_seq, 1, RWKV_PROJ), lambda i, j: (i, 0, 0)), rwkv_state]
        + _rwkv_param_specs(),
        out_specs=[tok(GLA_W), gla_state, tok(RWKV_W), rwkv_state],
        out_shape=[
            jax.ShapeDtypeStruct((b, t, GLA_W), BF16),
            jax.ShapeDtypeStruct((b, sdim, GLA_DV), F32),
            jax.ShapeDtypeStruct((b, t, RWKV_W), BF16),
            jax.ShapeDtypeStruct(rwkv_s0.shape, F32),
        ],
        scratch_shapes=[
            pltpu.VMEM((n_seq, sdim, GLA_W), F32),
            pltpu.VMEM((n_seq, RWKV_GROUPS, GROUP_W, GROUP_W), F32),
            pltpu.VMEM((n_seq, 1, RWKV_PROJ), F32),
        ],
        compiler_params=pltpu.CompilerParams(
            dimension_semantics=("arbitrary", "arbitrary"), vmem_limit_bytes=VMEM_LIMIT),
        name="mix_seq",
    )(ug3, gla_s0, *gla_params, ur3, shift0, rwkv_s0, *rwkv_params)


FF_CHUNK = D_FF // 2


def _post_kernel(x_ref, og_ref, or_ref, p_ref, wo_ref, nffn_ref, wg_ref, wu_ref, wd_ref,
                 nple_ref, wpg_ref, wpp_ref, nf_ref, y_ref):
    half = x_ref.shape[0] // 2
    parts = [slice(0, half), slice(half, 2 * half)]
    dot = lambda a, b: jnp.dot(a, b, preferred_element_type=F32)
    o = [jnp.concatenate([og_ref[p, :], or_ref[p, :]], axis=1) for p in parts]
    x = [x_ref[p, :] + dot(oo, wo_ref[...]) for p, oo in zip(parts, o)]
    h2 = [_rms(xx, nffn_ref[...]).astype(BF16) for xx in x]
    for i in range(0, D_FF, FF_CHUNK):
        gate = [dot(h, wg_ref[:, i:i + FF_CHUNK]) for h in h2]
        up = [dot(h, wu_ref[:, i:i + FF_CHUNK]) for h in h2]
        act = [(g * _sigmoid(g) * u).astype(BF16) for g, u in zip(gate, up)]
        x = [xx + dot(a, wd_ref[i:i + FF_CHUNK, :]) for xx, a in zip(x, act)]
    h3 = [_rms(xx, nple_ref[...]).astype(BF16) for xx in x]
    pg = [_sigmoid(dot(h, wpg_ref[...])) for h in h3]
    pp = [dot(p_ref[p, :].astype(BF16), wpp_ref[...]) for p in parts]
    for p, xx, a, b in zip(parts, x, pg, pp):
        y_ref[p, :] = _rms(xx + a * b, nf_ref[...])


def _post(x2d, og, orw, p2d, weights, tm):
    n = x2d.shape[0]
    wo, nffn, wg, wu, wd, nple, wpg, wpp, nf = weights
    tok = lambda w: pl.BlockSpec((tm, w), lambda i: (i, 0))
    return pl.pallas_call(
        _post_kernel,
        grid=(n // tm,),
        in_specs=[
            tok(D_MODEL), tok(GLA_W), tok(RWKV_W), tok(PLE_DIM),
            _const_spec((D_MODEL, D_MODEL)), _const_spec((1, D_MODEL)),
            _const_spec((D_MODEL, D_FF)), _const_spec((D_MODEL, D_FF)), _const_spec((D_FF, D_MODEL)),
            _const_spec((1, D_MODEL)), _const_spec((D_MODEL, D_MODEL)), _const_spec((PLE_DIM, D_MODEL)),
            _const_spec((1, D_MODEL)),
        ],
        out_specs=tok(D_MODEL),
        out_shape=jax.ShapeDtypeStruct((n, D_MODEL), F32),
        compiler_params=pltpu.CompilerParams(
            dimension_semantics=("arbitrary",), vmem_limit_bytes=VMEM_LIMIT),
        name="post",
    )(x2d, og, orw, p2d, wo, nffn, wg, wu, wd, nple, wpg, wpp, nf)


PROMPT_CHUNK = 64
PROMPT_SEQS_PER_STEP = 8
DEC_TILE_SEQS = 32
TOKEN_TILE = 512
PROJ_TILE = 1024


def kernel(x_prompt, x_sample, state_gla, state_rwkv, state_shift, p_prompt, p_sample, norm_mix, w_in, gla_gk_up, gla_gk_bias, gla_norm, rwkv_mu, rwkv_w0, rwkv_w2, rwkv_a0, rwkv_a2, rwkv_g2, rwkv_k_k, rwkv_k_a, rwkv_r_k, rwkv_ln_w, rwkv_ln_b, w_out, norm_ffn, w_gate, w_up, w_down, norm_ple, w_ple_gate, w_ple_proj, norm_final):
    assert w_in.shape[0] == 1
    i = 0
    rowv = lambda a: a.astype(F32).reshape(1, -1)
    zeros = lambda r, c: jnp.zeros((r, c), F32)
    w_in_i = w_in[i]
    w_gla = jnp.concatenate(
        [w_in_i[:, :GLA_PROJ], zeros(D_MODEL, GLA_COLS - GLA_PROJ)], axis=1).astype(BF16)
    w_rw = w_in_i[:, GLA_PROJ:].astype(BF16)
    gk_up = jnp.concatenate(
        [gla_gk_up[i].astype(F32), zeros(LANES - GLA_GATE_RANK, GLA_K_W)], axis=0).astype(BF16)
    seg = jnp.arange(SEG_W)[:, None] // RWKV_HEAD == jnp.arange(SEG_W)[None, :] // RWKV_HEAD
    rwkv_params = (
        rowv(rwkv_mu[i]), rowv(rwkv_w0[i]),
        jnp.concatenate([rwkv_w2[i].astype(F32), zeros(64, RWKV_W)], axis=0).astype(BF16),
        rowv(rwkv_a0[i]),
        jnp.concatenate([zeros(64, RWKV_W), rwkv_a2[i].astype(F32)], axis=0).astype(BF16),
        rwkv_g2[i].astype(BF16),
        rowv(rwkv_k_k[i]), rowv(rwkv_k_a[i]), rowv(rwkv_r_k[i]), rowv(rwkv_ln_w[i]), rowv(rwkv_ln_b[i]),
        seg.astype(BF16),
    )
    gla_w = (gk_up, rowv(gla_gk_bias[i]), rowv(gla_norm[i]))
    g_mix = rowv(norm_mix[i])

    bp, tp, _ = x_prompt.shape
    xp = x_prompt.astype(F32).reshape(bp * tp, D_MODEL)
    tail_w = tuple(w[i].astype(F32) for w in (w_out, w_gate, w_up, w_down, w_ple_gate, w_ple_proj))
    ug, ur, wo_b, wgate_b, wup_b, wdown_b, wpg_b, wpp_b = _proj(
        xp, g_mix, w_gla, w_rw, PROJ_TILE, tail_w)
    post_w = (wo_b, rowv(norm_ffn[i]), wgate_b, wup_b, wdown_b, rowv(norm_ple[i]), wpg_b, wpp_b,
              rowv(norm_final))
    ur3 = ur.reshape(bp, tp, RWKV_PROJ)
    og, gla_p, orw, rwkv_p = _mix_seq(
        ug.reshape(bp, tp, GLA_COLS), jnp.zeros((bp, GLA_HEADS * GLA_DK, GLA_DV), F32), gla_w,
        ur3, jnp.zeros((bp, 1, RWKV_PROJ), F32),
        jnp.zeros((1, bp, RWKV_HEADS, RWKV_HEAD, RWKV_HEAD), F32), rwkv_params,
        PROMPT_CHUNK, PROMPT_SEQS_PER_STEP)
    shift_p = ur3[:, tp - 1]
    yp = _post(xp, og.reshape(bp * tp, GLA_W), orw.reshape(bp * tp, RWKV_W),
               p_prompt[i].reshape(bp * tp, PLE_DIM), post_w, TOKEN_TILE)

    bs, ts, _ = x_sample.shape
    xs = x_sample.astype(F32).reshape(bs * ts, D_MODEL)
    ug, ur = _proj(xs, g_mix, w_gla, w_rw, TOKEN_TILE)
    og, gla_s = _gla_dec(ug, state_gla[i].astype(F32).reshape(bs, GLA_HEADS * GLA_DK, GLA_DV),
                         gla_w, ts, DEC_TILE_SEQS)
    shiftx = jnp.pad(state_shift[i].astype(F32)[:, None, :], ((0, 0), (0, ts - 1), (0, 0)))
    orw, rwkv_s = _rwkv_step(ur, shiftx.reshape(bs * ts, RWKV_PROJ),
                             jnp.transpose(state_rwkv[i].astype(F32), (1, 2, 3, 0)), rwkv_params, ts)
    rwkv_s = jnp.transpose(rwkv_s, (3, 0, 1, 2))[None]
    shift_s = ur.reshape(bs, ts, RWKV_PROJ)[:, ts - 1]
    ys = _post(xs, og, orw, p_sample[i].reshape(bs * ts, PLE_DIM), post_w, TOKEN_TILE)

    gla_shape = (1, -1, GLA_HEADS, GLA_DK, GLA_DV)
    return (yp.reshape(bp, tp, D_MODEL).astype(x_prompt.dtype),
            ys.reshape(bs, ts, D_MODEL).astype(x_sample.dtype),
            gla_p.reshape(gla_shape).astype(state_gla.dtype), rwkv_p.astype(state_rwkv.dtype),
            shift_p[None].astype(state_shift.dtype),
            gla_s.reshape(gla_shape).astype(state_gla.dtype), rwkv_s.astype(state_rwkv.dtype),
            shift_s[None].astype(state_shift.dtype))
```

```python
import functools

import jax
import jax.numpy as jnp
from jax import lax
from jax.experimental import pallas as pl
from jax.experimental.pallas import tpu as pltpu

F32 = jnp.float32
BF16 = jnp.bfloat16

D_MODEL = 1024
GLA_HEADS = 4
GLA_DK = 64
GLA_DV = 128
GLA_K_W = GLA_HEADS * GLA_DK
GLA_W = GLA_HEADS * GLA_DV
GLA_GATE_RANK = 16
GLA_GATE_NORM = 16.0
GLA_MAIN = 2 * GLA_K_W + 2 * GLA_W
GLA_PROJ = GLA_MAIN + GLA_GATE_RANK
LANES = 128
SUBLANES = 8
GLA_COLS = GLA_MAIN + LANES
RWKV_HEAD = 64
RWKV_HEADS = 8
RWKV_W = RWKV_HEADS * RWKV_HEAD
RWKV_PROJ = 3 * RWKV_W + 64 + 64 + 128
D_FF = 2816
PLE_DIM = 256
EPS = 1e-6
RWKV_GN_EPS = 64e-5

VMEM_LIMIT = 56 * 1024 * 1024

NN = ((1,), (0,))
NT = ((1,), (1,))


def _split(x, n):
    parts = []
    r = x
    for i in range(n):
        p = r.astype(BF16)
        parts.append(p)
        if i + 1 < n:
            r = r - p.astype(F32)
    return parts


def _mm(a, b, dims=NN, pa=1, pb=1):
    pieces_a = _split(a, pa)
    pieces_b = _split(b, pb)
    n = max(pa, pb)
    acc = None
    for i, ai in enumerate(pieces_a):
        for j, bj in enumerate(pieces_b):
            if i + j < n:
                t = lax.dot_general(ai, bj, (dims, ((), ())), preferred_element_type=F32)
                acc = t if acc is None else acc + t
    return acc


def _mm_tn(a, b, pa=1, pb=1):
    rows = a.shape[0]
    pad = (-rows) % LANES
    if pad:
        a = jnp.concatenate([a, jnp.zeros((pad, a.shape[1]), a.dtype)], axis=0)
        b = jnp.concatenate([b, jnp.zeros((pad, b.shape[1]), b.dtype)], axis=0)
    return _mm(a.T, b, NN, pa, pb)


def _iota(shape, dim):
    return lax.broadcasted_iota(jnp.int32, shape, dim)


def _log_sigmoid(z):
    return jnp.minimum(z, 0.0) - jnp.log(1.0 + jnp.exp(-jnp.abs(z)))


def _shift_rows(u, first):
    r = pltpu.roll(u, 1, axis=0)
    head = jnp.where(_iota((SUBLANES, 1), 0) == 0, first, r[:SUBLANES])
    return jnp.concatenate([head, r[SUBLANES:]], axis=0)


def _sigmoid(z):
    return 1.0 / (1.0 + jnp.exp(-z))


def _tanh(z):
    return 2.0 * _sigmoid(2.0 * z) - 1.0


def _rms(x, g):
    return x * lax.rsqrt(jnp.mean(x * x, axis=-1, keepdims=True) + EPS) * g


GROUP_HEADS = 2
GROUP_W = GROUP_HEADS * RWKV_HEAD
RWKV_GROUPS = RWKV_W // GROUP_W
SEG_W = 2 * LANES


def _head_masks(rows, head_w, heads):
    lane = _iota((rows, heads * head_w), 1)
    return [jnp.where(lane // head_w == h, 1.0, 0.0).astype(BF16) for h in range(heads)]


def _stack_heads(x, hmasks):
    xb = x.astype(BF16)
    return jnp.concatenate([xb * m for m in hmasks], axis=0)


def _block_diag(x_cat, blk):
    heads = blk.shape[0] // x_cat.shape[0]
    return jnp.concatenate([x_cat.astype(BF16)] * heads, axis=0) * blk


def _chunk_masks(rows, seq_len, heads):
    t = _iota((rows, heads * rows), 0)
    s = _iota((rows, heads * rows), 1) % rows
    incl = t >= s
    t2 = _iota((2 * rows, rows), 0)
    s2 = _iota((2 * rows, rows), 1)
    cum_rows = (t2 < rows) & (t2 >= s2)
    tot_rows = t2 >= rows
    if seq_len < rows:
        incl = incl & ((t // seq_len) == (s // seq_len))
        same2 = ((t2 % rows) // seq_len) == (s2 // seq_len)
        cum_rows, tot_rows = cum_rows & same2, tot_rows & same2
    big = heads * rows
    blk = jnp.where(_iota((big, big), 0) // rows == _iota((big, big), 1) // rows, 1.0, 0.0)
    t4 = _iota((2 * rows, 2 * big), 0)
    s4 = _iota((2 * rows, 2 * big), 1) % rows
    quad = ((t4 % rows) > s4) | ((t4 >= rows) & ((t4 % rows) == s4))
    if seq_len < rows:
        quad = quad & (((t4 % rows) // seq_len) == (s4 // seq_len))
    return dict(incl=incl, quad=quad, eye=jnp.where(t == s, 1.0, 0.0),
                sums=jnp.where(cum_rows | tot_rows, 1.0, 0.0).astype(BF16), blk=blk.astype(BF16))


def _proj_kernel(x_ref, g_ref, wt_ref, *refs):
    n_cast = (len(refs) - 2) // 2
    ug_ref, ur_ref = refs[n_cast], refs[n_cast + 1]
    h = _rms(x_ref[...], g_ref[...]).astype(BF16)
    times_t = lambda w: lax.dot_general(h, w, (NT, ((), ())), preferred_element_type=F32)
    ug_ref[:, 0:GLA_MAIN] = times_t(wt_ref[0:GLA_MAIN])
    gate_rank = times_t(wt_ref[GLA_MAIN:GLA_PROJ])
    ug_ref[:, GLA_MAIN:GLA_COLS] = jnp.concatenate(
        [gate_rank, jnp.zeros((h.shape[0], GLA_COLS - GLA_PROJ), F32)], axis=1)
    ur_ref[...] = times_t(wt_ref[GLA_PROJ:GLA_PROJ + RWKV_PROJ])
    for src_ref, dst_ref in zip(refs[:n_cast], refs[n_cast + 2:]):
        dst_ref[...] = src_ref[...].astype(BF16)


def _const_spec(shape):
    return pl.BlockSpec(shape, lambda *_: (0,) * len(shape), pipeline_mode=pl.Buffered(1))


def _proj(x2d, g, w_in_t, tm, to_bf16=()):
    n = x2d.shape[0]
    steps = n // tm
    slabs = [pl.BlockSpec((w.shape[0] // steps, w.shape[1]), lambda i: (i, 0)) for w in to_bf16]
    return pl.pallas_call(
        _proj_kernel,
        grid=(steps,),
        in_specs=[
            pl.BlockSpec((tm, D_MODEL), lambda i: (i, 0)),
            _const_spec((1, D_MODEL)),
            _const_spec(w_in_t.shape),
        ] + slabs,
        out_specs=[
            pl.BlockSpec((tm, GLA_COLS), lambda i: (i, 0)),
            pl.BlockSpec((tm, RWKV_PROJ), lambda i: (i, 0)),
        ] + slabs,
        out_shape=[
            jax.ShapeDtypeStruct((n, GLA_COLS), F32),
            jax.ShapeDtypeStruct((n, RWKV_PROJ), F32),
        ] + [jax.ShapeDtypeStruct(w.shape, BF16) for w in to_bf16],
        compiler_params=pltpu.CompilerParams(
            dimension_semantics=("arbitrary",), vmem_limit_bytes=VMEM_LIMIT),
        name="proj",
    )(x2d, g, w_in_t, *to_bf16)


def _gla_prep(us, gkup, gkb, sums):
    rows = us[0].shape[0]
    z = [_mm(u[:, GLA_MAIN:GLA_COLS], gkup) for u in us]
    log_a = [_log_sigmoid(x + gkb) * (1.0 / GLA_GATE_NORM) for x in z]
    cums = [_mm(sums, x, NN, 1, 2) for x in log_a]
    out = []
    for u, x in zip(us, cums):
        cum, cum_end = x[:rows], x[rows:]
        q = u[:, 0:GLA_K_W] * (GLA_DK ** -0.5)
        k = u[:, GLA_K_W:2 * GLA_K_W]
        out.append((q * jnp.exp(cum), k * jnp.exp(-cum), k * jnp.exp(cum_end - cum),
                    u[:, 2 * GLA_K_W:2 * GLA_K_W + GLA_W], u[:, 2 * GLA_K_W + GLA_W:GLA_MAIN],
                    jnp.exp(cum_end)))
    return out


def _gla_intra(q_i, k_i, v, m, kmasks, vmasks):
    scores = jnp.where(m["incl"], _mm(q_i, _stack_heads(k_i, kmasks), NT), 0.0)
    return _mm(scores, _stack_heads(v, vmasks))


def _gla_out(o, gate, gnorm):
    heads = [slice(h * GLA_DV, (h + 1) * GLA_DV) for h in range(GLA_HEADS)]
    return jnp.concatenate(
        [_rms(o[:, hl], gnorm) * (gate[:, hl] * _sigmoid(gate[:, hl])) for hl in heads],
        axis=1).astype(BF16)


def _gla_block_mask():
    return (_iota((GLA_K_W, GLA_W), 0) // GLA_DK) == (_iota((GLA_K_W, GLA_W), 1) // GLA_DV)


def _gla_state_in(s, blk):
    return jnp.where(blk, jnp.concatenate([s] * GLA_HEADS, axis=1), 0.0)


def _gla_state_out(s_bd):
    heads = [s_bd[:, h * GLA_DV:(h + 1) * GLA_DV] for h in range(GLA_HEADS)]
    return (heads[0] + heads[1]) + (heads[2] + heads[3])


def _lane_tiled_t(x):
    pad = LANES - x.shape[0]
    if pad:
        x = jnp.concatenate([x, jnp.zeros((pad, x.shape[1]), x.dtype)], axis=0)
    return x.T


def _gla_seq_part(u_ref, s0_ref, gkup_ref, gkb_ref, gn_ref, o_ref, sout_ref, s_scr, *, phase):
    n_seq, rows = u_ref.shape[0], u_ref.shape[1]
    blk = _gla_block_mask()

    def init():
        for b in range(n_seq):
            s_scr[b] = _gla_state_in(s0_ref[b], blk)

    def final():
        for b in range(n_seq):
            sout_ref[b] = _gla_state_out(s_scr[b])

    if phase == "init":
        return init()
    if phase == "final":
        return final()

    m = _chunk_masks(rows, rows, GLA_HEADS)
    kmasks = _head_masks(rows, GLA_DK, GLA_HEADS)
    vmasks = _head_masks(rows, GLA_DV, GLA_HEADS)
    tok = _gla_prep([u_ref[b] for b in range(n_seq)], gkup_ref[...], gkb_ref[...], m["sums"])
    s_old = [s_scr[b] for b in range(n_seq)]
    intra = [_gla_intra(q_i, k_i, v, m, kmasks, vmasks) for q_i, k_i, _, v, _, _ in tok]
    inter = [_mm(t[0], s) for t, s in zip(tok, s_old)]
    kv = [_mm_tn(k_e, v) for _, _, k_e, v, _, _ in tok]
    for b in range(n_seq):
        gate, g_end = tok[b][4], tok[b][5]
        o_ref[b] = _gla_out(intra[b] + inter[b], gate, gn_ref[...])
        dec = jnp.concatenate([_lane_tiled_t(jnp.broadcast_to(g_end[0:1], (LANES, GLA_K_W)))] * GLA_HEADS,
                              axis=1)
        s_scr[b] = dec * s_old[b] + jnp.where(blk, kv[b], 0.0)


def _gla_dec_kernel(u_ref, s0_ref, gkup_ref, gkb_ref, gn_ref, o_ref, sout_ref, *, seq_len):
    rows = u_ref.shape[0]
    n_seq = rows // seq_len
    blk = _gla_block_mask()
    m = _chunk_masks(rows, seq_len, GLA_HEADS)
    q_i, k_i, k_e, v, gate, g_end = _gla_prep([u_ref[...]], gkup_ref[...], gkb_ref[...], m["sums"])[0]
    o = _gla_intra(q_i, k_i, v, m, _head_masks(rows, GLA_DK, GLA_HEADS),
                   _head_masks(rows, GLA_DV, GLA_HEADS))
    k_et = _lane_tiled_t(k_e).astype(BF16)
    dec_t = _lane_tiled_t(g_end)
    v_pad = v if rows == LANES else jnp.concatenate([v, jnp.zeros((LANES - rows, GLA_W), F32)], axis=0)
    row = _iota((rows, 1), 0)
    row_pad = _iota((LANES, 1), 0)
    for j in range(n_seq):
        s_bd = _gla_state_in(s0_ref[j], blk)
        o = o + _mm(jnp.where(row // seq_len == j, q_i, 0.0), s_bd)
        kv = _mm(k_et, jnp.where(row_pad // seq_len == j, v_pad, 0.0))
        first = j * seq_len
        sout_ref[j] = _gla_state_out(dec_t[:, first:first + 1] * s_bd + jnp.where(blk, kv, 0.0))
    o_ref[...] = _gla_out(o, gate, gn_ref[...])


def _gla_param_specs():
    return [_const_spec((LANES, GLA_K_W)), _const_spec((1, GLA_K_W)), _const_spec((1, GLA_DV))]


def _gla_dec(u2, s0, params, seq_len, n_seq):
    n = u2.shape[0]
    rows = n_seq * seq_len
    sdim = GLA_HEADS * GLA_DK
    state_spec = pl.BlockSpec((n_seq, sdim, GLA_DV), lambda i: (i, 0, 0))
    return pl.pallas_call(
        functools.partial(_gla_dec_kernel, seq_len=seq_len),
        grid=(n // rows,),
        in_specs=[pl.BlockSpec((rows, GLA_COLS), lambda i: (i, 0)), state_spec] + _gla_param_specs(),
        out_specs=[pl.BlockSpec((rows, GLA_W), lambda i: (i, 0)), state_spec],
        out_shape=[
            jax.ShapeDtypeStruct((n, GLA_W), BF16),
            jax.ShapeDtypeStruct(s0.shape, F32),
        ],
        compiler_params=pltpu.CompilerParams(
            dimension_semantics=("arbitrary",), vmem_limit_bytes=VMEM_LIMIT),
        name="gla_dec",
    )(u2, s0, *params)


def _seg_sum(x, seg, pa=1):
    return jnp.concatenate(
        [_mm(x[:, i:i + SEG_W], seg, NN, pa, 1) for i in range(0, RWKV_W, SEG_W)], axis=1)


W_OFFSET_SCALE = 0.6065306597126334


def _rwkv_tokens(u, prev, mu, w0, w2, a0, a2, g2, k_k, k_a, rk, seg):
    xr = u + mu * (prev - u)
    r = xr[:, 0:RWKV_W]
    kr = xr[:, RWKV_W:2 * RWKV_W]
    vr = xr[:, 2 * RWKV_W:3 * RWKV_W]
    wa = xr[:, 3 * RWKV_W:3 * RWKV_W + LANES]
    gd = xr[:, 3 * RWKV_W + LANES:RWKV_PROJ]
    lw = _sigmoid(w0 + _mm(_tanh(wa), w2)) * (-W_OFFSET_SCALE)
    a_sig = _sigmoid(a0 + _mm(wa, a2))
    gate = _mm(_sigmoid(gd), g2)
    kk = kr * k_k
    kk = kk * lax.rsqrt(jnp.maximum(_seg_sum(kk * kk, seg), 1e-24))
    kr = kr * (1.0 + (a_sig - 1.0) * k_a)
    bonus = _seg_sum(r * kr * rk, seg) * vr
    return r, kr, vr, lw, -kk, kk * a_sig, gate, bonus


def _rwkv_prep(u, prev, mu, w0, w2, a0, a2, g2, k_k, k_a, rk, seg, sums):
    rows = u.shape[0]
    r, kr, vr, lw, a_vec, b_vec, gate, bonus = _rwkv_tokens(
        u, prev, mu, w0, w2, a0, a2, g2, k_k, k_a, rk, seg)
    block = sums.shape[1]
    cums = [_mm(sums, lw[i:i + block], NN, 1, 2) for i in range(0, rows, block)]
    cum = jnp.concatenate([x[:block] for x in cums], axis=0)
    cum_end = jnp.concatenate([x[block:] for x in cums], axis=0)
    e_neg = jnp.exp(-cum)
    g_end = jnp.exp(cum_end)
    e_end = g_end * e_neg
    bf = lambda x: x.astype(BF16)
    return (bf(a_vec * jnp.exp(cum - lw)), bf(r * jnp.exp(cum)), bf(kr * e_neg), bf(b_vec * e_neg),
            bf(kr * e_end), bf(b_vec * e_end), bf(vr), gate, bonus, g_end)


def _rwkv_intra(units, m, hmasks, n_double):
    rows = units[0][0].shape[0]
    stack = lambda x: _stack_heads(x, hmasks)
    cat_w = m["incl"].shape[1]
    g = [_mm(jnp.concatenate([at, rt], axis=0), jnp.concatenate([stack(bt), stack(kt)], axis=0), NT)
         for at, rt, kt, bt, _ in units]
    g = [jnp.where(m["quad"], x, 0.0) for x in g]
    a_ab = [x[:rows, :cat_w] for x in g]
    a_rb = [x[rows:, :cat_w] for x in g]
    a_ak = [x[:rows, cat_w:] for x in g]
    a_rk = [x[rows:, cat_w:] for x in g]
    tinv = [m["eye"] + a for a in a_ab]
    apow = [_mm(a, _block_diag(a, m["blk"])) for a in a_ab]
    for _ in range(n_double - 1):
        both = [_mm(jnp.concatenate([t, a], axis=0), _block_diag(a, m["blk"])) for t, a in zip(tinv, apow)]
        tinv = [t + x[:rows] for t, x in zip(tinv, both)]
        apow = [x[rows:] for x in both]
    tinv = [t + _mm(t, _block_diag(a, m["blk"])) for t, a in zip(tinv, apow)]
    v_s = [stack(vp) for _, _, _, _, vp in units]
    akv = [_mm(jnp.concatenate([a, b], axis=0), v) for a, b, v in zip(a_ak, a_rk, v_s)]
    wu = [_mm(t, jnp.concatenate([stack(at), stack(x[:rows])], axis=1))
          for t, (at, _, _, _, _), x in zip(tinv, units, akv)]
    z = [_mm(a, jnp.concatenate([stack(x[:, :GROUP_W]), stack(x[:, GROUP_W:])], axis=1))
         for a, x in zip(a_rb, wu)]
    return [(x[:, :GROUP_W], x[:, GROUP_W:], rt + zz[:, :GROUP_W], zz[:, GROUP_W:] + kv[rows:])
            for x, (_, rt, _, _, _), zz, kv in zip(wu, units, z, akv)]


def _rwkv_out(y, gate, bonus, lnw, lnb, seg):
    inv = 1.0 / RWKV_HEAD
    yc = y - _seg_sum(y, seg, 2) * inv
    var = _seg_sum(yc * yc, seg) * inv
    return ((yc * lax.rsqrt(var + RWKV_GN_EPS) * lnw + lnb + bonus) * gate).astype(BF16)


def _rwkv_state_in(s_ref, idx, g):
    zero = jnp.zeros((RWKV_HEAD, RWKV_HEAD), F32)
    blocks = []
    for h in range(GROUP_HEADS):
        parts = [zero] * GROUP_HEADS
        parts[h] = s_ref[idx, g * GROUP_HEADS + h]
        blocks.append(jnp.concatenate(parts, axis=1))
    return jnp.concatenate(blocks, axis=0)


def _rwkv_state_out(s_ref, idx, g, s2):
    for h in range(GROUP_HEADS):
        sl = slice(h * RWKV_HEAD, (h + 1) * RWKV_HEAD)
        s_ref[idx, g * GROUP_HEADS + h] = s2[sl, sl]


def _head_block_mask():
    return (_iota((GROUP_W, GROUP_W), 0) // RWKV_HEAD) == (_iota((GROUP_W, GROUP_W), 1) // RWKV_HEAD)


def _rwkv_seq_part(u_ref, shift0_ref, s0_ref, mu_ref, w0_ref, w2_ref, a0_ref, a2_ref,
                   g2_ref, kk_ref, ka_ref, rk_ref, lnw_ref, lnb_ref, seg_ref,
                   o_ref, sout_ref, s_scr, prev_scr, *, phase):
    n_seq, rows = u_ref.shape[0], u_ref.shape[1]
    groups = [slice(g * GROUP_W, (g + 1) * GROUP_W) for g in range(RWKV_GROUPS)]
    ids = [(b, g) for b in range(n_seq) for g in range(RWKV_GROUPS)]

    def init():
        for b in range(n_seq):
            for g in range(RWKV_GROUPS):
                s_scr[b, g] = _rwkv_state_in(s0_ref, b, g)
            prev_scr[b] = shift0_ref[b]

    def final():
        for b, g in ids:
            _rwkv_state_out(sout_ref, b, g, s_scr[b, g])

    if phase == "init":
        return init()
    if phase == "final":
        return final()

    seg = seg_ref[...]
    m = _chunk_masks(rows, rows, GROUP_HEADS)
    hmasks = _head_masks(rows, RWKV_HEAD, GROUP_HEADS)
    head_blk = _head_block_mask()
    n_double = rows.bit_length() - 2

    us = [u_ref[b] for b in range(n_seq)]
    prev = jnp.concatenate([_shift_rows(u, prev_scr[b]) for b, u in enumerate(us)], axis=0)
    for b, u in enumerate(us):
        prev_scr[b] = u[rows - 1:rows]
    at, rt, kt, bt, ke, be, vb, gate, bonus, g_end = _rwkv_prep(
        jnp.concatenate(us, axis=0), prev, mu_ref[...], w0_ref[...], w2_ref[...], a0_ref[...],
        a2_ref[...], g2_ref[...], kk_ref[...], ka_ref[...], rk_ref[...], seg, m["sums"])
    tok, units = [], []
    for b in range(n_seq):
        sl = slice(b * rows, (b + 1) * rows)
        tok.append((vb[sl], ke[sl], be[sl], g_end[b * rows:b * rows + 1]))
        units += [(at[sl, gl], rt[sl, gl], kt[sl, gl], bt[sl, gl], vb[sl, gl]) for gl in groups]

    intra = _rwkv_intra(units, m, hmasks, n_double)
    s_old = [s_scr[b, g] for b, g in ids]
    uy = [_mm(jnp.concatenate([w_m, r_m], axis=0), s2, NT) for (w_m, _, r_m, _), s2 in zip(intra, s_old)]
    upd = []
    for (b, g), (_, u0, _, _), x in zip(ids, intra, uy):
        vb_b, ke_b, be_b, _ = tok[b]
        gl = groups[g]
        upd.append(_mm_tn(jnp.concatenate([x[:rows] + u0, vb_b[:, gl].astype(F32)], axis=0),
                          jnp.concatenate([be_b[:, gl], ke_b[:, gl]], axis=0)))
    for (b, g), s2, d in zip(ids, s_old, upd):
        s_scr[b, g] = s2 * tok[b][3][:, groups[g]] + jnp.where(head_blk, d, 0.0)
    ys = [jnp.concatenate([uy[i][rows:] + intra[i][3] for i, (bb, _) in enumerate(ids) if bb == b],
                          axis=1) for b in range(n_seq)]
    out = _rwkv_out(jnp.concatenate(ys, axis=0), gate, bonus, lnw_ref[...], lnb_ref[...], seg)
    for b in range(n_seq):
        o_ref[b] = out[b * rows:(b + 1) * rows]


STEP_HEADS = 2
STEP_ROWS = 4
N_STEP_OPERANDS = 6


def _sum_keys(x):
    t = x[0:SUBLANES]
    for i in range(SUBLANES, x.shape[0], SUBLANES):
        t = t + x[i:i + SUBLANES]
    for shift in (4, 2, 1):
        t = t + pltpu.roll(t, shift, axis=0)
    return t


def _rwkv_step_kernel(u_ref, shiftx_ref, s0_ref, mu_ref, w0_ref, w2_ref, a0_ref, a2_ref, g2_ref,
                      kk_ref, ka_ref, rk_ref, lnw_ref, lnb_ref, seg_ref,
                      o_ref, sout_ref, ops_scr, y_scr, tmp_scr, aux_scr, *, seq_len):
    p = pl.program_id(0)
    n_tok = u_ref.shape[0]
    n_seq = n_tok // seq_len
    seg = seg_ref[...]
    cols = [slice(c, c + LANES) for c in range(0, RWKV_W, LANES)]

    @pl.when(p == 0)
    def _():
        u = u_ref[...]
        row = _iota((n_tok, 1), 0)
        prev = jnp.where(row % seq_len == 0, shiftx_ref[...], pltpu.roll(u, 1, axis=0))
        r, kr, vr, lw, a_vec, b_vec, gate, bonus = _rwkv_tokens(
            u, prev, mu_ref[...], w0_ref[...], w2_ref[...], a0_ref[...], a2_ref[...], g2_ref[...],
            kk_ref[...], ka_ref[...], rk_ref[...], seg)
        aux_scr[0] = gate
        aux_scr[1] = bonus
        for i, x in enumerate((jnp.exp(lw), a_vec, b_vec, kr, r, vr)):
            for c, cl in enumerate(cols):
                tmp_scr[c] = x[:, cl]
            for t in range(seq_len):
                for c, cl in enumerate(cols):
                    ops_scr[i, t, cl, :] = tmp_scr[c, pl.ds(t, n_seq, stride=seq_len), :].T

    def rows_step(i, carry):
        for hh in range(STEP_HEADS):
            chan = pl.multiple_of((p * STEP_HEADS + hh) * RWKV_HEAD, RWKV_HEAD)
            for j in range(STEP_ROWS):
                v = i * STEP_ROWS + j
                s = s0_ref[hh, v]
                for t in range(seq_len):
                    w, a, b, k, r = (ops_scr[n, t, pl.ds(chan, RWKV_HEAD), :] for n in range(5))
                    v_t = ops_scr[5, t, pl.ds(chan + v, 1), :]
                    sa = jnp.concatenate([_sum_keys(s * a)] * (RWKV_HEAD // SUBLANES), axis=0)
                    s = s * w + sa * b + v_t * k
                    y_scr[t, pl.ds(chan + v, 1), :] = _sum_keys(s * r)[0:1]
                sout_ref[hh, v] = s
        return carry

    lax.fori_loop(0, RWKV_HEAD // STEP_ROWS, rows_step, 0)

    @pl.when(p == pl.num_programs(0) - 1)
    def _():
        for t in range(seq_len):
            for c, cl in enumerate(cols):
                tmp_scr[c, pl.ds(t, n_seq, stride=seq_len), :] = y_scr[t, cl, :].T
        y = jnp.concatenate([tmp_scr[c] for c in range(len(cols))], axis=1)
        o_ref[...] = _rwkv_out(y, aux_scr[0], aux_scr[1], lnw_ref[...], lnb_ref[...], seg)


def _rwkv_step(u2, shiftx, s0_t, params, seq_len):
    n = u2.shape[0]
    n_seq = n // seq_len
    state_spec = pl.BlockSpec((STEP_HEADS, RWKV_HEAD, RWKV_HEAD, n_seq), lambda i: (i, 0, 0, 0))
    whole = lambda w: pl.BlockSpec((n, w), lambda i: (0, 0), pipeline_mode=pl.Buffered(1))
    return pl.pallas_call(
        functools.partial(_rwkv_step_kernel, seq_len=seq_len),
        grid=(RWKV_HEADS // STEP_HEADS,),
        in_specs=[whole(RWKV_PROJ), whole(RWKV_PROJ), state_spec] + _rwkv_param_specs(),
        out_specs=[pl.BlockSpec((n, RWKV_W), lambda i: (0, 0)), state_spec],
        out_shape=[
            jax.ShapeDtypeStruct((n, RWKV_W), BF16),
            jax.ShapeDtypeStruct(s0_t.shape, F32),
        ],
        scratch_shapes=[
            pltpu.VMEM((N_STEP_OPERANDS, seq_len, RWKV_W, n_seq), F32),
            pltpu.VMEM((seq_len, RWKV_W, n_seq), F32),
            pltpu.VMEM((RWKV_W // LANES, n, LANES), F32),
            pltpu.VMEM((2, n, RWKV_W), F32),
        ],
        compiler_params=pltpu.CompilerParams(
            dimension_semantics=("arbitrary",), vmem_limit_bytes=VMEM_LIMIT),
        name="rwkv_step",
    )(u2, shiftx, s0_t, *params)


def _rwkv_param_specs():
    vec = lambda n: _const_spec((1, n))
    return [
        vec(RWKV_PROJ),
        vec(RWKV_W),
        _const_spec((LANES, RWKV_W)),
        vec(RWKV_W),
        _const_spec((LANES, RWKV_W)),
        _const_spec((LANES, RWKV_W)),
        vec(RWKV_W), vec(RWKV_W), vec(RWKV_W), vec(RWKV_W), vec(RWKV_W),
        _const_spec((SEG_W, SEG_W)),
    ]


N_GLA_SEQ_IN = 5
N_RWKV_SEQ_IN = 15


def _mix_seq_kernel(*refs):
    n_in = N_GLA_SEQ_IN + N_RWKV_SEQ_IN
    gla = refs[:N_GLA_SEQ_IN] + refs[n_in:n_in + 2] + refs[n_in + 4:n_in + 5]
    rwkv = refs[N_GLA_SEQ_IN:n_in] + refs[n_in + 2:n_in + 4] + refs[n_in + 5:]
    c = pl.program_id(1)

    @pl.when(c == 0)
    def _():
        _gla_seq_part(*gla, phase="init")
        _rwkv_seq_part(*rwkv, phase="init")

    _rwkv_seq_part(*rwkv, phase="body")
    _gla_seq_part(*gla, phase="body")

    @pl.when(c == pl.num_programs(1) - 1)
    def _():
        _gla_seq_part(*gla, phase="final")
        _rwkv_seq_part(*rwkv, phase="final")


def _mix_seq(ug3, gla_s0, gla_params, ur3, shift0, rwkv_s0, rwkv_params, rows, n_seq):
    b, t, _ = ug3.shape
    sdim = GLA_HEADS * GLA_DK
    gla_state = pl.BlockSpec((n_seq, sdim, GLA_DV), lambda i, j: (i, 0, 0))
    rwkv_state = pl.BlockSpec((None, n_seq, RWKV_HEADS, RWKV_HEAD, RWKV_HEAD),
                              lambda i, j: (0, i, 0, 0, 0))
    tok = lambda w: pl.BlockSpec((n_seq, rows, w), lambda i, j: (i, j, 0))
    return pl.pallas_call(
        _mix_seq_kernel,
        grid=(b // n_seq, t // rows),
        in_specs=[tok(GLA_COLS), gla_state] + _gla_param_specs()
        + [tok(RWKV_PROJ), pl.BlockSpec((n_seq, 1, RWKV_PROJ), lambda i, j: (i, 0, 0)), rwkv_state]
        + _rwkv_param_specs(),
        out_specs=[tok(GLA_W), gla_state, tok(RWKV_W), rwkv_state],
        out_shape=[
            jax.ShapeDtypeStruct((b, t, GLA_W), BF16),
            jax.ShapeDtypeStruct((b, sdim, GLA_DV), F32),
            jax.ShapeDtypeStruct((b, t, RWKV_W), BF16),
            jax.ShapeDtypeStruct(rwkv_s0.shape, F32),
        ],
        scratch_shapes=[
            pltpu.VMEM((n_seq, sdim, GLA_W), F32),
            pltpu.VMEM((n_seq, RWKV_GROUPS, GROUP_W, GROUP_W), F32),
            pltpu.VMEM((n_seq, 1, RWKV_PROJ), F32),
        ],
        compiler_params=pltpu.CompilerParams(
            dimension_semantics=("arbitrary", "arbitrary"), vmem_limit_bytes=VMEM_LIMIT),
        name="mix_seq",
    )(ug3, gla_s0, *gla_params, ur3, shift0, rwkv_s0, *rwkv_params)


FF_CHUNK = D_FF // 2


def _post_kernel(x_ref, og_ref, or_ref, p_ref, wo_ref, nffn_ref, wg_ref, wu_ref, wd_ref,
                 nple_ref, wpg_ref, wpp_ref, nf_ref, y_ref):
    half = x_ref.shape[0] // 2
    parts = [slice(0, half), slice(half, 2 * half)]
    dot = lambda a, b: jnp.dot(a, b, preferred_element_type=F32)
    o = [jnp.concatenate([og_ref[p, :], or_ref[p, :]], axis=1) for p in parts]
    x = [x_ref[p, :] + dot(oo, wo_ref[...]) for p, oo in zip(parts, o)]
    h2 = [_rms(xx, nffn_ref[...]).astype(BF16) for xx in x]
    for i in range(0, D_FF, FF_CHUNK):
        gate = [dot(h, wg_ref[:, i:i + FF_CHUNK]) for h in h2]
        up = [dot(h, wu_ref[:, i:i + FF_CHUNK]) for h in h2]
        act = [(g * _sigmoid(g) * u).astype(BF16) for g, u in zip(gate, up)]
        x = [xx + dot(a, wd_ref[i:i + FF_CHUNK, :]) for xx, a in zip(x, act)]
    h3 = [_rms(xx, nple_ref[...]).astype(BF16) for xx in x]
    pg = [_sigmoid(dot(h, wpg_ref[...])) for h in h3]
    pp = [dot(p_ref[p, :].astype(BF16), wpp_ref[...]) for p in parts]
    for p, xx, a, b in zip(parts, x, pg, pp):
        y_ref[p, :] = _rms(xx + a * b, nf_ref[...])


def _post(x2d, og, orw, p2d, weights, tm):
    n = x2d.shape[0]
    wo, nffn, wg, wu, wd, nple, wpg, wpp, nf = weights
    tok = lambda w: pl.BlockSpec((tm, w), lambda i: (i, 0))
    return pl.pallas_call(
        _post_kernel,
        grid=(n // tm,),
        in_specs=[
            tok(D_MODEL), tok(GLA_W), tok(RWKV_W), tok(PLE_DIM),
            _const_spec((D_MODEL, D_MODEL)), _const_spec((1, D_MODEL)),
            _const_spec((D_MODEL, D_FF)), _const_spec((D_MODEL, D_FF)), _const_spec((D_FF, D_MODEL)),
            _const_spec((1, D_MODEL)), _const_spec((D_MODEL, D_MODEL)), _const_spec((PLE_DIM, D_MODEL)),
            _const_spec((1, D_MODEL)),
        ],
        out_specs=tok(D_MODEL),
        out_shape=jax.ShapeDtypeStruct((n, D_MODEL), F32),
        compiler_params=pltpu.CompilerParams(
            dimension_semantics=("arbitrary",), vmem_limit_bytes=VMEM_LIMIT),
        name="post",
    )(x2d, og, orw, p2d, wo, nffn, wg, wu, wd, nple, wpg, wpp, nf)


PROMPT_CHUNK = 64
PROMPT_SEQS_PER_STEP = 8
DEC_TILE_SEQS = 32
TOKEN_TILE = 512
PROJ_TILE = 1024


def kernel(x_prompt, x_sample, state_gla, state_rwkv, state_shift, p_prompt, p_sample, norm_mix, w_in, gla_gk_up, gla_gk_bias, gla_norm, rwkv_mu, rwkv_w0, rwkv_w2, rwkv_a0, rwkv_a2, rwkv_g2, rwkv_k_k, rwkv_k_a, rwkv_r_k, rwkv_ln_w, rwkv_ln_b, w_out, norm_ffn, w_gate, w_up, w_down, norm_ple, w_ple_gate, w_ple_proj, norm_final):
    assert w_in.shape[0] == 1
    i = 0
    rowv = lambda a: a.astype(F32).reshape(1, -1)
    zeros = lambda r, c: jnp.zeros((r, c), F32)
    w_in_t = jnp.transpose(w_in[i]).astype(BF16)
    gk_up = jnp.concatenate(
        [gla_gk_up[i].astype(F32), zeros(LANES - GLA_GATE_RANK, GLA_K_W)], axis=0).astype(BF16)
    seg = jnp.arange(SEG_W)[:, None] // RWKV_HEAD == jnp.arange(SEG_W)[None, :] // RWKV_HEAD
    rwkv_params = (
        rowv(rwkv_mu[i]), rowv(rwkv_w0[i]),
        jnp.concatenate([rwkv_w2[i].astype(F32), zeros(64, RWKV_W)], axis=0).astype(BF16),
        rowv(rwkv_a0[i]),
        jnp.concatenate([zeros(64, RWKV_W), rwkv_a2[i].astype(F32)], axis=0).astype(BF16),
        rwkv_g2[i].astype(BF16),
        rowv(rwkv_k_k[i]), rowv(rwkv_k_a[i]), rowv(rwkv_r_k[i]), rowv(rwkv_ln_w[i]), rowv(rwkv_ln_b[i]),
        seg.astype(BF16),
    )
    gla_w = (gk_up, rowv(gla_gk_bias[i]), rowv(gla_norm[i]))
    g_mix = rowv(norm_mix[i])

    bp, tp, _ = x_prompt.shape
    xp = x_prompt.astype(F32).reshape(bp * tp, D_MODEL)
    tail_w = tuple(w[i].astype(F32) for w in (w_out, w_gate, w_up, w_down, w_ple_gate, w_ple_proj))
    ug, ur, wo_b, wgate_b, wup_b, wdown_b, wpg_b, wpp_b = _proj(
        xp, g_mix, w_in_t, PROJ_TILE, tail_w)
    post_w = (wo_b, rowv(norm_ffn[i]), wgate_b, wup_b, wdown_b, rowv(norm_ple[i]), wpg_b, wpp_b,
              rowv(norm_final))
    ur3 = ur.reshape(bp, tp, RWKV_PROJ)
    og, gla_p, orw, rwkv_p = _mix_seq(
        ug.reshape(bp, tp, GLA_COLS), jnp.zeros((bp, GLA_HEADS * GLA_DK, GLA_DV), F32), gla_w,
        ur3, jnp.zeros((bp, 1, RWKV_PROJ), F32),
        jnp.zeros((1, bp, RWKV_HEADS, RWKV_HEAD, RWKV_HEAD), F32), rwkv_params,
        PROMPT_CHUNK, PROMPT_SEQS_PER_STEP)
    shift_p = ur3[:, tp - 1]
    yp = _post(xp, og.reshape(bp * tp, GLA_W), orw.reshape(bp * tp, RWKV_W),
               p_prompt[i].reshape(bp * tp, PLE_DIM), post_w, TOKEN_TILE)

    bs, ts, _ = x_sample.shape
    xs = x_sample.astype(F32).reshape(bs * ts, D_MODEL)
    ug, ur = _proj(xs, g_mix, w_in_t, TOKEN_TILE)
    og, gla_s = _gla_dec(ug, state_gla[i].astype(F32).reshape(bs, GLA_HEADS * GLA_DK, GLA_DV),
                         gla_w, ts, DEC_TILE_SEQS)
    shiftx = jnp.pad(state_shift[i].astype(F32)[:, None, :], ((0, 0), (0, ts - 1), (0, 0)))
    orw, rwkv_s = _rwkv_step(ur, shiftx.reshape(bs * ts, RWKV_PROJ),
                             jnp.transpose(state_rwkv[i].astype(F32), (1, 2, 3, 0)), rwkv_params, ts)
    rwkv_s = jnp.transpose(rwkv_s, (3, 0, 1, 2))[None]
    shift_s = ur.reshape(bs, ts, RWKV_PROJ)[:, ts - 1]
    ys = _post(xs, og, orw, p_sample[i].reshape(bs * ts, PLE_DIM), post_w, TOKEN_TILE)

    gla_shape = (1, -1, GLA_HEADS, GLA_DK, GLA_DV)
    return (yp.reshape(bp, tp, D_MODEL).astype(x_prompt.dtype),
            ys.reshape(bs, ts, D_MODEL).astype(x_sample.dtype),
            gla_p.reshape(gla_shape).astype(state_gla.dtype), rwkv_p.astype(state_rwkv.dtype),
            shift_p[None].astype(state_shift.dtype),
            gla_s.reshape(gla_shape).astype(state_gla.dtype), rwkv_s.astype(state_rwkv.dtype),
            shift_s[None].astype(state_shift.dtype))
```

```python
import functools

import jax
import jax.numpy as jnp
from jax import lax
from jax.experimental import pallas as pl
from jax.experimental.pallas import tpu as pltpu

F32 = jnp.float32
BF16 = jnp.bfloat16

D_MODEL = 1024
GLA_HEADS = 4
GLA_DK = 64
GLA_DV = 128
GLA_K_W = GLA_HEADS * GLA_DK
GLA_W = GLA_HEADS * GLA_DV
GLA_GATE_RANK = 16
GLA_GATE_NORM = 16.0
GLA_MAIN = 2 * GLA_K_W + 2 * GLA_W
GLA_PROJ = GLA_MAIN + GLA_GATE_RANK
LANES = 128
SUBLANES = 8
GLA_COLS = GLA_MAIN + LANES
RWKV_HEAD = 64
RWKV_HEADS = 8
RWKV_W = RWKV_HEADS * RWKV_HEAD
RWKV_PROJ = 3 * RWKV_W + 64 + 64 + 128
D_FF = 2816
PLE_DIM = 256
EPS = 1e-6
RWKV_GN_EPS = 64e-5

VMEM_LIMIT = 56 * 1024 * 1024

NN = ((1,), (0,))
NT = ((1,), (1,))


def _split(x, n):
    parts = []
    r = x
    for i in range(n):
        p = r.astype(BF16)
        parts.append(p)
        if i + 1 < n:
            r = r - p.astype(F32)
    return parts


def _mm(a, b, dims=NN, pa=1, pb=1):
    pieces_a = _split(a, pa)
    pieces_b = _split(b, pb)
    n = max(pa, pb)
    acc = None
    for i, ai in enumerate(pieces_a):
        for j, bj in enumerate(pieces_b):
            if i + j < n:
                t = lax.dot_general(ai, bj, (dims, ((), ())), preferred_element_type=F32)
                acc = t if acc is None else acc + t
    return acc


def _mm_tn(a, b, pa=1, pb=1):
    rows = a.shape[0]
    pad = (-rows) % LANES
    if pad:
        a = jnp.concatenate([a, jnp.zeros((pad, a.shape[1]), a.dtype)], axis=0)
        b = jnp.concatenate([b, jnp.zeros((pad, b.shape[1]), b.dtype)], axis=0)
    return _mm(a.T, b, NN, pa, pb)


def _iota(shape, dim):
    return lax.broadcasted_iota(jnp.int32, shape, dim)


def _log_sigmoid(z):
    return jnp.minimum(z, 0.0) - jnp.log(1.0 + jnp.exp(-jnp.abs(z)))


def _shift_rows(u, first):
    r = pltpu.roll(u, 1, axis=0)
    head = jnp.where(_iota((SUBLANES, 1), 0) == 0, first, r[:SUBLANES])
    return jnp.concatenate([head, r[SUBLANES:]], axis=0)


def _sigmoid(z):
    return 1.0 / (1.0 + jnp.exp(-z))


def _tanh(z):
    return 2.0 * _sigmoid(2.0 * z) - 1.0


def _rms(x, g):
    return x * lax.rsqrt(jnp.mean(x * x, axis=-1, keepdims=True) + EPS) * g


GROUP_HEADS = 2
GROUP_W = GROUP_HEADS * RWKV_HEAD
RWKV_GROUPS = RWKV_W // GROUP_W
SEG_W = 2 * LANES


def _head_masks(rows, head_w, heads):
    lane = _iota((rows, heads * head_w), 1)
    return [jnp.where(lane // head_w == h, 1.0, 0.0).astype(BF16) for h in range(heads)]


def _stack_heads(x, hmasks):
    xb = x.astype(BF16)
    return jnp.concatenate([xb * m for m in hmasks], axis=0)


def _block_diag(x_cat, blk):
    heads = blk.shape[0] // x_cat.shape[0]
    return jnp.concatenate([x_cat.astype(BF16)] * heads, axis=0) * blk


def _chunk_masks(rows, seq_len, heads):
    t = _iota((rows, heads * rows), 0)
    s = _iota((rows, heads * rows), 1) % rows
    incl = t >= s
    t2 = _iota((2 * rows, rows), 0)
    s2 = _iota((2 * rows, rows), 1)
    cum_rows = (t2 < rows) & (t2 >= s2)
    tot_rows = t2 >= rows
    if seq_len < rows:
        incl = incl & ((t // seq_len) == (s // seq_len))
        same2 = ((t2 % rows) // seq_len) == (s2 // seq_len)
        cum_rows, tot_rows = cum_rows & same2, tot_rows & same2
    big = heads * rows
    blk = jnp.where(_iota((big, big), 0) // rows == _iota((big, big), 1) // rows, 1.0, 0.0)
    t4 = _iota((2 * rows, 2 * big), 0)
    s4 = _iota((2 * rows, 2 * big), 1) % rows
    quad = ((t4 % rows) > s4) | ((t4 >= rows) & ((t4 % rows) == s4))
    if seq_len < rows:
        quad = quad & (((t4 % rows) // seq_len) == (s4 // seq_len))
    return dict(incl=incl, quad=quad, eye=jnp.where(t == s, 1.0, 0.0),
                sums=jnp.where(cum_rows | tot_rows, 1.0, 0.0).astype(BF16), blk=blk.astype(BF16))


def _proj_kernel(x_ref, g_ref, wt_ref, *refs):
    n_cast = (len(refs) - 2) // 2
    ug_ref, ur_ref = refs[n_cast], refs[n_cast + 1]
    h = _rms(x_ref[...], g_ref[...]).astype(BF16)
    times_t = lambda w: lax.dot_general(h, w, (NT, ((), ())), preferred_element_type=F32)
    ug_ref[:, 0:GLA_MAIN] = times_t(wt_ref[0:GLA_MAIN])
    gate_rank = times_t(wt_ref[GLA_MAIN:GLA_PROJ])
    ug_ref[:, GLA_MAIN:GLA_COLS] = jnp.concatenate(
        [gate_rank, jnp.zeros((h.shape[0], GLA_COLS - GLA_PROJ), F32)], axis=1)
    ur_ref[...] = times_t(wt_ref[GLA_PROJ:GLA_PROJ + RWKV_PROJ])
    for src_ref, dst_ref in zip(refs[:n_cast], refs[n_cast + 2:]):
        dst_ref[...] = src_ref[...].astype(BF16)


def _const_spec(shape):
    return pl.BlockSpec(shape, lambda *_: (0,) * len(shape), pipeline_mode=pl.Buffered(1))


def _proj(x2d, g, w_in_t, tm, to_bf16=()):
    n = x2d.shape[0]
    steps = n // tm
    slabs = [pl.BlockSpec((w.shape[0] // steps, w.shape[1]), lambda i: (i, 0)) for w in to_bf16]
    return pl.pallas_call(
        _proj_kernel,
        grid=(steps,),
        in_specs=[
            pl.BlockSpec((tm, D_MODEL), lambda i: (i, 0)),
            _const_spec((1, D_MODEL)),
            _const_spec(w_in_t.shape),
        ] + slabs,
        out_specs=[
            pl.BlockSpec((tm, GLA_COLS), lambda i: (i, 0)),
            pl.BlockSpec((tm, RWKV_PROJ), lambda i: (i, 0)),
        ] + slabs,
        out_shape=[
            jax.ShapeDtypeStruct((n, GLA_COLS), F32),
            jax.ShapeDtypeStruct((n, RWKV_PROJ), F32),
        ] + [jax.ShapeDtypeStruct(w.shape, BF16) for w in to_bf16],
        compiler_params=pltpu.CompilerParams(
            dimension_semantics=("arbitrary",), vmem_limit_bytes=VMEM_LIMIT),
        name="proj",
    )(x2d, g, w_in_t, *to_bf16)


def _gla_prep(us, gkup, gkb, sums):
    rows = us[0].shape[0]
    z = [_mm(u[:, GLA_MAIN:GLA_COLS], gkup) for u in us]
    log_a = [_log_sigmoid(x + gkb) * (1.0 / GLA_GATE_NORM) for x in z]
    cums = [_mm(sums, x, NN, 1, 2) for x in log_a]
    out = []
    for u, x in zip(us, cums):
        cum, cum_end = x[:rows], x[rows:]
        q = u[:, 0:GLA_K_W] * (GLA_DK ** -0.5)
        k = u[:, GLA_K_W:2 * GLA_K_W]
        out.append((q * jnp.exp(cum), k * jnp.exp(-cum), k * jnp.exp(cum_end - cum),
                    u[:, 2 * GLA_K_W:2 * GLA_K_W + GLA_W], u[:, 2 * GLA_K_W + GLA_W:GLA_MAIN],
                    jnp.exp(cum_end)))
    return out


def _gla_intra(q_i, k_i, v, m, kmasks, vmasks):
    scores = jnp.where(m["incl"], _mm(q_i, _stack_heads(k_i, kmasks), NT), 0.0)
    return _mm(scores, _stack_heads(v, vmasks))


def _gla_out(o, gate, gnorm):
    heads = [slice(h * GLA_DV, (h + 1) * GLA_DV) for h in range(GLA_HEADS)]
    return jnp.concatenate(
        [_rms(o[:, hl], gnorm) * (gate[:, hl] * _sigmoid(gate[:, hl])) for hl in heads],
        axis=1).astype(BF16)


def _gla_block_mask():
    return (_iota((GLA_K_W, GLA_W), 0) // GLA_DK) == (_iota((GLA_K_W, GLA_W), 1) // GLA_DV)


def _gla_state_in(s, blk):
    return jnp.where(blk, jnp.concatenate([s] * GLA_HEADS, axis=1), 0.0)


def _gla_state_out(s_bd):
    heads = [s_bd[:, h * GLA_DV:(h + 1) * GLA_DV] for h in range(GLA_HEADS)]
    return (heads[0] + heads[1]) + (heads[2] + heads[3])


def _lane_tiled_t(x):
    pad = LANES - x.shape[0]
    if pad:
        x = jnp.concatenate([x, jnp.zeros((pad, x.shape[1]), x.dtype)], axis=0)
    return x.T


def _gla_seq_part(u_ref, s0_ref, gkup_ref, gkb_ref, gn_ref, o_ref, sout_ref, s_scr, *, phase):
    n_seq, rows = u_ref.shape[0], u_ref.shape[1]
    blk = _gla_block_mask()

    def init():
        for b in range(n_seq):
            s_scr[b] = _gla_state_in(s0_ref[b], blk)

    def final():
        for b in range(n_seq):
            sout_ref[b] = _gla_state_out(s_scr[b])

    if phase == "init":
        return init()
    if phase == "final":
        return final()

    m = _chunk_masks(rows, rows, GLA_HEADS)
    kmasks = _head_masks(rows, GLA_DK, GLA_HEADS)
    vmasks = _head_masks(rows, GLA_DV, GLA_HEADS)
    tok = _gla_prep([u_ref[b] for b in range(n_seq)], gkup_ref[...], gkb_ref[...], m["sums"])
    s_old = [s_scr[b] for b in range(n_seq)]
    intra = [_gla_intra(q_i, k_i, v, m, kmasks, vmasks) for q_i, k_i, _, v, _, _ in tok]
    inter = [_mm(t[0], s) for t, s in zip(tok, s_old)]
    kv = [_mm_tn(k_e, v) for _, _, k_e, v, _, _ in tok]
    for b in range(n_seq):
        gate, g_end = tok[b][4], tok[b][5]
        o_ref[b] = _gla_out(intra[b] + inter[b], gate, gn_ref[...])
        dec = jnp.concatenate([_lane_tiled_t(jnp.broadcast_to(g_end[0:1], (LANES, GLA_K_W)))] * GLA_HEADS,
                              axis=1)
        s_scr[b] = dec * s_old[b] + jnp.where(blk, kv[b], 0.0)


def _gla_dec_kernel(u_ref, s0_ref, gkup_ref, gkb_ref, gn_ref, o_ref, sout_ref, *, seq_len):
    rows = u_ref.shape[0]
    n_seq = rows // seq_len
    blk = _gla_block_mask()
    m = _chunk_masks(rows, seq_len, GLA_HEADS)
    q_i, k_i, k_e, v, gate, g_end = _gla_prep([u_ref[...]], gkup_ref[...], gkb_ref[...], m["sums"])[0]
    o = _gla_intra(q_i, k_i, v, m, _head_masks(rows, GLA_DK, GLA_HEADS),
                   _head_masks(rows, GLA_DV, GLA_HEADS))
    k_et = _lane_tiled_t(k_e).astype(BF16)
    dec_t = _lane_tiled_t(g_end)
    v_pad = v if rows == LANES else jnp.concatenate([v, jnp.zeros((LANES - rows, GLA_W), F32)], axis=0)
    row = _iota((rows, 1), 0)
    row_pad = _iota((LANES, 1), 0)
    for j in range(n_seq):
        s_bd = _gla_state_in(s0_ref[j], blk)
        o = o + _mm(jnp.where(row // seq_len == j, q_i, 0.0), s_bd)
        kv = _mm(k_et, jnp.where(row_pad // seq_len == j, v_pad, 0.0))
        first = j * seq_len
        sout_ref[j] = _gla_state_out(dec_t[:, first:first + 1] * s_bd + jnp.where(blk, kv, 0.0))
    o_ref[...] = _gla_out(o, gate, gn_ref[...])


def _gla_param_specs():
    return [_const_spec((LANES, GLA_K_W)), _const_spec((1, GLA_K_W)), _const_spec((1, GLA_DV))]


def _gla_dec(u2, s0, params, seq_len, n_seq):
    n = u2.shape[0]
    rows = n_seq * seq_len
    sdim = GLA_HEADS * GLA_DK
    state_spec = pl.BlockSpec((n_seq, sdim, GLA_DV), lambda i: (i, 0, 0))
    return pl.pallas_call(
        functools.partial(_gla_dec_kernel, seq_len=seq_len),
        grid=(n // rows,),
        in_specs=[pl.BlockSpec((rows, GLA_COLS), lambda i: (i, 0)), state_spec] + _gla_param_specs(),
        out_specs=[pl.BlockSpec((rows, GLA_W), lambda i: (i, 0)), state_spec],
        out_shape=[
            jax.ShapeDtypeStruct((n, GLA_W), BF16),
            jax.ShapeDtypeStruct(s0.shape, F32),
        ],
        compiler_params=pltpu.CompilerParams(
            dimension_semantics=("arbitrary",), vmem_limit_bytes=VMEM_LIMIT),
        name="gla_dec",
    )(u2, s0, *params)


def _seg_sum(x, seg, pa=1):
    return jnp.concatenate(
        [_mm(x[:, i:i + SEG_W], seg, NN, pa, 1) for i in range(0, RWKV_W, SEG_W)], axis=1)


W_OFFSET_SCALE = 0.6065306597126334


def _rwkv_tokens(u, prev, mu, w0, w2, a0, a2, g2, k_k, k_a, rk, seg):
    xr = u + mu * (prev - u)
    r = xr[:, 0:RWKV_W]
    kr = xr[:, RWKV_W:2 * RWKV_W]
    vr = xr[:, 2 * RWKV_W:3 * RWKV_W]
    wa = xr[:, 3 * RWKV_W:3 * RWKV_W + LANES]
    gd = xr[:, 3 * RWKV_W + LANES:RWKV_PROJ]
    lw = _sigmoid(w0 + _mm(_tanh(wa), w2)) * (-W_OFFSET_SCALE)
    a_sig = _sigmoid(a0 + _mm(wa, a2))
    gate = _mm(_sigmoid(gd), g2)
    kk = kr * k_k
    kk = kk * lax.rsqrt(jnp.maximum(_seg_sum(kk * kk, seg), 1e-24))
    kr = kr * (1.0 + (a_sig - 1.0) * k_a)
    bonus = _seg_sum(r * kr * rk, seg) * vr
    return r, kr, vr, lw, -kk, kk * a_sig, gate, bonus


def _rwkv_prep(u, prev, mu, w0, w2, a0, a2, g2, k_k, k_a, rk, seg, sums):
    rows = u.shape[0]
    r, kr, vr, lw, a_vec, b_vec, gate, bonus = _rwkv_tokens(
        u, prev, mu, w0, w2, a0, a2, g2, k_k, k_a, rk, seg)
    block = sums.shape[1]
    cums = [_mm(sums, lw[i:i + block], NN, 1, 2) for i in range(0, rows, block)]
    cum = jnp.concatenate([x[:block] for x in cums], axis=0)
    cum_end = jnp.concatenate([x[block:] for x in cums], axis=0)
    e_neg = jnp.exp(-cum)
    g_end = jnp.exp(cum_end)
    e_end = g_end * e_neg
    bf = lambda x: x.astype(BF16)
    return (bf(a_vec * jnp.exp(cum - lw)), bf(r * jnp.exp(cum)), bf(kr * e_neg), bf(b_vec * e_neg),
            bf(kr * e_end), bf(b_vec * e_end), bf(vr), gate, bonus, g_end)


def _rwkv_intra(units, m, hmasks, n_double):
    rows = units[0][0].shape[0]
    stack = lambda x: _stack_heads(x, hmasks)
    cat_w = m["incl"].shape[1]
    g = [_mm(jnp.concatenate([at, rt], axis=0), jnp.concatenate([stack(bt), stack(kt)], axis=0), NT)
         for at, rt, kt, bt, _ in units]
    g = [jnp.where(m["quad"], x, 0.0) for x in g]
    a_ab = [x[:rows, :cat_w] for x in g]
    a_rb = [x[rows:, :cat_w] for x in g]
    a_ak = [x[:rows, cat_w:] for x in g]
    a_rk = [x[rows:, cat_w:] for x in g]
    tinv = [m["eye"] + a for a in a_ab]
    apow = [_mm(a, _block_diag(a, m["blk"])) for a in a_ab]
    for _ in range(n_double - 1):
        both = [_mm(jnp.concatenate([t, a], axis=0), _block_diag(a, m["blk"])) for t, a in zip(tinv, apow)]
        tinv = [t + x[:rows] for t, x in zip(tinv, both)]
        apow = [x[rows:] for x in both]
    tinv = [t + _mm(t, _block_diag(a, m["blk"])) for t, a in zip(tinv, apow)]
    v_s = [stack(vp) for _, _, _, _, vp in units]
    akv = [_mm(jnp.concatenate([a, b], axis=0), v) for a, b, v in zip(a_ak, a_rk, v_s)]
    wu = [_mm(t, jnp.concatenate([stack(at), stack(x[:rows])], axis=1))
          for t, (at, _, _, _, _), x in zip(tinv, units, akv)]
    z = [_mm(a, jnp.concatenate([stack(x[:, :GROUP_W]), stack(x[:, GROUP_W:])], axis=1))
         for a, x in zip(a_rb, wu)]
    return [(x[:, :GROUP_W], x[:, GROUP_W:], rt + zz[:, :GROUP_W], zz[:, GROUP_W:] + kv[rows:])
            for x, (_, rt, _, _, _), zz, kv in zip(wu, units, z, akv)]


def _rwkv_out(y, gate, bonus, lnw, lnb, seg):
    inv = 1.0 / RWKV_HEAD
    yc = y - _seg_sum(y, seg, 2) * inv
    var = _seg_sum(yc * yc, seg) * inv
    return ((yc * lax.rsqrt(var + RWKV_GN_EPS) * lnw + lnb + bonus) * gate).astype(BF16)


def _rwkv_state_in(s_ref, idx, g):
    zero = jnp.zeros((RWKV_HEAD, RWKV_HEAD), F32)
    blocks = []
    for h in range(GROUP_HEADS):
        parts = [zero] * GROUP_HEADS
        parts[h] = s_ref[idx, g * GROUP_HEADS + h]
        blocks.append(jnp.concatenate(parts, axis=1))
    return jnp.concatenate(blocks, axis=0)


def _rwkv_state_out(s_ref, idx, g, s2):
    for h in range(GROUP_HEADS):
        sl = slice(h * RWKV_HEAD, (h + 1) * RWKV_HEAD)
        s_ref[idx, g * GROUP_HEADS + h] = s2[sl, sl]


def _head_block_mask():
    return (_iota((GROUP_W, GROUP_W), 0) // RWKV_HEAD) == (_iota((GROUP_W, GROUP_W), 1) // RWKV_HEAD)


def _rwkv_seq_part(u_ref, shift0_ref, s0_ref, mu_ref, w0_ref, w2_ref, a0_ref, a2_ref,
                   g2_ref, kk_ref, ka_ref, rk_ref, lnw_ref, lnb_ref, seg_ref,
                   o_ref, sout_ref, s_scr, prev_scr, *, phase):
    n_seq, rows = u_ref.shape[0], u_ref.shape[1]
    groups = [slice(g * GROUP_W, (g + 1) * GROUP_W) for g in range(RWKV_GROUPS)]
    ids = [(b, g) for b in range(n_seq) for g in range(RWKV_GROUPS)]

    def init():
        for b in range(n_seq):
            for g in range(RWKV_GROUPS):
                s_scr[b, g] = _rwkv_state_in(s0_ref, b, g)
            prev_scr[b] = shift0_ref[b]

    def final():
        for b, g in ids:
            _rwkv_state_out(sout_ref, b, g, s_scr[b, g])

    if phase == "init":
        return init()
    if phase == "final":
        return final()

    seg = seg_ref[...]
    m = _chunk_masks(rows, rows, GROUP_HEADS)
    hmasks = _head_masks(rows, RWKV_HEAD, GROUP_HEADS)
    head_blk = _head_block_mask()
    n_double = rows.bit_length() - 2

    us = [u_ref[b] for b in range(n_seq)]
    prev = jnp.concatenate([_shift_rows(u, prev_scr[b]) for b, u in enumerate(us)], axis=0)
    for b, u in enumerate(us):
        prev_scr[b] = u[rows - 1:rows]
    at, rt, kt, bt, ke, be, vb, gate, bonus, g_end = _rwkv_prep(
        jnp.concatenate(us, axis=0), prev, mu_ref[...], w0_ref[...], w2_ref[...], a0_ref[...],
        a2_ref[...], g2_ref[...], kk_ref[...], ka_ref[...], rk_ref[...], seg, m["sums"])
    tok, units = [], []
    for b in range(n_seq):
        sl = slice(b * rows, (b + 1) * rows)
        tok.append((vb[sl], ke[sl], be[sl], g_end[b * rows:b * rows + 1]))
        units += [(at[sl, gl], rt[sl, gl], kt[sl, gl], bt[sl, gl], vb[sl, gl]) for gl in groups]

    intra = _rwkv_intra(units, m, hmasks, n_double)
    s_old = [s_scr[b, g] for b, g in ids]
    uy = [_mm(jnp.concatenate([w_m, r_m], axis=0), s2, NT) for (w_m, _, r_m, _), s2 in zip(intra, s_old)]
    upd = []
    for (b, g), (_, u0, _, _), x in zip(ids, intra, uy):
        vb_b, ke_b, be_b, _ = tok[b]
        gl = groups[g]
        upd.append(_mm_tn(jnp.concatenate([x[:rows] + u0, vb_b[:, gl].astype(F32)], axis=0),
                          jnp.concatenate([be_b[:, gl], ke_b[:, gl]], axis=0)))
    for (b, g), s2, d in zip(ids, s_old, upd):
        s_scr[b, g] = s2 * tok[b][3][:, groups[g]] + jnp.where(head_blk, d, 0.0)
    ys = [jnp.concatenate([uy[i][rows:] + intra[i][3] for i, (bb, _) in enumerate(ids) if bb == b],
                          axis=1) for b in range(n_seq)]
    out = _rwkv_out(jnp.concatenate(ys, axis=0), gate, bonus, lnw_ref[...], lnb_ref[...], seg)
    for b in range(n_seq):
        o_ref[b] = out[b * rows:(b + 1) * rows]


STEP_HEADS = 2
STEP_ROWS = 4
N_STEP_OPERANDS = 6


def _sum_keys(x):
    t = x[0:SUBLANES]
    for i in range(SUBLANES, x.shape[0], SUBLANES):
        t = t + x[i:i + SUBLANES]
    for shift in (4, 2, 1):
        t = t + pltpu.roll(t, shift, axis=0)
    return t


def _rwkv_step_kernel(u_ref, shiftx_ref, s0_ref, mu_ref, w0_ref, w2_ref, a0_ref, a2_ref, g2_ref,
                      kk_ref, ka_ref, rk_ref, lnw_ref, lnb_ref, seg_ref,
                      o_ref, sout_ref, ops_scr, y_scr, tmp_scr, aux_scr, *, seq_len):
    p = pl.program_id(0)
    n_tok = u_ref.shape[0]
    n_seq = n_tok // seq_len
    seg = seg_ref[...]
    cols = [slice(c, c + LANES) for c in range(0, RWKV_W, LANES)]

    @pl.when(p == 0)
    def _():
        u = u_ref[...]
        row = _iota((n_tok, 1), 0)
        prev = jnp.where(row % seq_len == 0, shiftx_ref[...], pltpu.roll(u, 1, axis=0))
        r, kr, vr, lw, a_vec, b_vec, gate, bonus = _rwkv_tokens(
            u, prev, mu_ref[...], w0_ref[...], w2_ref[...], a0_ref[...], a2_ref[...], g2_ref[...],
            kk_ref[...], ka_ref[...], rk_ref[...], seg)
        aux_scr[0] = gate
        aux_scr[1] = bonus
        for i, x in enumerate((jnp.exp(lw), a_vec, b_vec, kr, r, vr)):
            for c, cl in enumerate(cols):
                tmp_scr[c] = x[:, cl]
            for t in range(seq_len):
                for c, cl in enumerate(cols):
                    ops_scr[i, t, cl, :] = tmp_scr[c, pl.ds(t, n_seq, stride=seq_len), :].T

    def rows_step(i, carry):
        for hh in range(STEP_HEADS):
            chan = pl.multiple_of((p * STEP_HEADS + hh) * RWKV_HEAD, RWKV_HEAD)
            for j in range(STEP_ROWS):
                v = i * STEP_ROWS + j
                s = s0_ref[hh, v]
                for t in range(seq_len):
                    w, a, b, k, r = (ops_scr[n, t, pl.ds(chan, RWKV_HEAD), :] for n in range(5))
                    v_t = ops_scr[5, t, pl.ds(chan + v, 1), :]
                    sa = jnp.concatenate([_sum_keys(s * a)] * (RWKV_HEAD // SUBLANES), axis=0)
                    s = s * w + sa * b + v_t * k
                    y_scr[t, pl.ds(chan + v, 1), :] = _sum_keys(s * r)[0:1]
                sout_ref[hh, v] = s
        return carry

    lax.fori_loop(0, RWKV_HEAD // STEP_ROWS, rows_step, 0)

    @pl.when(p == pl.num_programs(0) - 1)
    def _():
        for t in range(seq_len):
            for c, cl in enumerate(cols):
                tmp_scr[c, pl.ds(t, n_seq, stride=seq_len), :] = y_scr[t, cl, :].T
        y = jnp.concatenate([tmp_scr[c] for c in range(len(cols))], axis=1)
        o_ref[...] = _rwkv_out(y, aux_scr[0], aux_scr[1], lnw_ref[...], lnb_ref[...], seg)


def _rwkv_step(u2, shiftx, s0_t, params, seq_len):
    n = u2.shape[0]
    n_seq = n // seq_len
    state_spec = pl.BlockSpec((STEP_HEADS, RWKV_HEAD, RWKV_HEAD, n_seq), lambda i: (i, 0, 0, 0))
    whole = lambda w: pl.BlockSpec((n, w), lambda i: (0, 0), pipeline_mode=pl.Buffered(1))
    return pl.pallas_call(
        functools.partial(_rwkv_step_kernel, seq_len=seq_len),
        grid=(RWKV_HEADS // STEP_HEADS,),
        in_specs=[whole(RWKV_PROJ), whole(RWKV_PROJ), state_spec] + _rwkv_param_specs(),
        out_specs=[pl.BlockSpec((n, RWKV_W), lambda i: (0, 0)), state_spec],
        out_shape=[
            jax.ShapeDtypeStruct((n, RWKV_W), BF16),
            jax.ShapeDtypeStruct(s0_t.shape, F32),
        ],
        scratch_shapes=[
            pltpu.VMEM((N_STEP_OPERANDS, seq_len, RWKV_W, n_seq), F32),
            pltpu.VMEM((seq_len, RWKV_W, n_seq), F32),
            pltpu.VMEM((RWKV_W // LANES, n, LANES), F32),
            pltpu.VMEM((2, n, RWKV_W), F32),
        ],
        compiler_params=pltpu.CompilerParams(
            dimension_semantics=("arbitrary",), vmem_limit_bytes=VMEM_LIMIT),
        name="rwkv_step",
    )(u2, shiftx, s0_t, *params)


def _rwkv_param_specs():
    vec = lambda n: _const_spec((1, n))
    return [
        vec(RWKV_PROJ),
        vec(RWKV_W),
        _const_spec((LANES, RWKV_W)),
        vec(RWKV_W),
        _const_spec((LANES, RWKV_W)),
        _const_spec((LANES, RWKV_W)),
        vec(RWKV_W), vec(RWKV_W), vec(RWKV_W), vec(RWKV_W), vec(RWKV_W),
        _const_spec((SEG_W, SEG_W)),
    ]


N_GLA_SEQ_IN = 5
N_RWKV_SEQ_IN = 15


def _mix_seq_kernel(*refs):
    n_in = N_GLA_SEQ_IN + N_RWKV_SEQ_IN
    gla = refs[:N_GLA_SEQ_IN] + refs[n_in:n_in + 2] + refs[n_in + 4:n_in + 5]
    rwkv = refs[N_GLA_SEQ_IN:n_in] + refs[n_in + 2:n_in + 4] + refs[n_in + 5:]
    c = pl.program_id(1)

    @pl.when(c == 0)
    def _():
        _gla_seq_part(*gla, phase="init")
        _rwkv_seq_part(*rwkv, phase="init")

    _rwkv_seq_part(*rwkv, phase="body")
    _gla_seq_part(*gla, phase="body")

    @pl.when(c == pl.num_programs(1) - 1)
    def _():
        _gla_seq_part(*gla, phase="final")
        _rwkv_seq_part(*rwkv, phase="final")


def _mix_seq(ug3, gla_s0, gla_params, ur3, shift0, rwkv_s0, rwkv_params, rows, n_seq):
    b, t, _ = ug3.shape
    sdim = GLA_HEADS * GLA_DK
    gla_state = pl.BlockSpec((n_seq, sdim, GLA_DV), lambda i, j: (i, 0, 0))
    rwkv_state = pl.BlockSpec((None, n_seq, RWKV_HEADS, RWKV_HEAD, RWKV_HEAD),
                              lambda i, j: (0, i, 0, 0, 0))
    tok = lambda w: pl.BlockSpec((n_seq, rows, w), lambda i, j: (i, j, 0))
    return pl.pallas_call(
        _mix_seq_kernel,
        grid=(b // n_seq, t // rows),
        in_specs=[tok(GLA_COLS), gla_state] + _gla_param_specs()
        + [tok(RWKV_PROJ), pl.BlockSpec((n_seq, 1, RWKV_PROJ), lambda i, j: (i, 0, 0)), rwkv_state]
        + _rwkv_param_specs(),
        out_specs=[tok(GLA_W), gla_state, tok(RWKV_W), rwkv_state],
        out_shape=[
            jax.ShapeDtypeStruct((b, t, GLA_W), BF16),
            jax.ShapeDtypeStruct((b, sdim, GLA_DV), F32),
            jax.ShapeDtypeStruct((b, t, RWKV_W), BF16),
            jax.ShapeDtypeStruct(rwkv_s0.shape, F32),
        ],
        scratch_shapes=[
            pltpu.VMEM((n_seq, sdim, GLA_W), F32),
            pltpu.VMEM((n_seq, RWKV_GROUPS, GROUP_W, GROUP_W), F32),
            pltpu.VMEM((n_seq, 1, RWKV_PROJ), F32),
        ],
        compiler_params=pltpu.CompilerParams(
            dimension_semantics=("arbitrary", "arbitrary"), vmem_limit_bytes=VMEM_LIMIT),
        name="mix_seq",
    )(ug3, gla_s0, *gla_params, ur3, shift0, rwkv_s0, *rwkv_params)


FF_CHUNK = D_FF // 2


def _post_kernel(x_ref, og_ref, or_ref, p_ref, wo_ref, nffn_ref, wg_ref, wu_ref, wd_ref,
                 nple_ref, wpg_ref, wpp_ref, nf_ref, y_ref):
    half = x_ref.shape[0] // 2
    parts = [slice(0, half), slice(half, 2 * half)]
    dot = lambda a, b: jnp.dot(a, b, preferred_element_type=F32)
    o = [jnp.concatenate([og_ref[p, :], or_ref[p, :]], axis=1) for p in parts]
    x = [x_ref[p, :] + dot(oo, wo_ref[...]) for p, oo in zip(parts, o)]
    h2 = [_rms(xx, nffn_ref[...]).astype(BF16) for xx in x]
    for i in range(0, D_FF, FF_CHUNK):
        gate = [dot(h, wg_ref[:, i:i + FF_CHUNK]) for h in h2]
        up = [dot(h, wu_ref[:, i:i + FF_CHUNK]) for h in h2]
        act = [(g * _sigmoid(g) * u).astype(BF16) for g, u in zip(gate, up)]
        x = [xx + dot(a, wd_ref[i:i + FF_CHUNK, :]) for xx, a in zip(x, act)]
    h3 = [_rms(xx, nple_ref[...]).astype(BF16) for xx in x]
    pg = [_sigmoid(dot(h, wpg_ref[...])) for h in h3]
    pp = [dot(p_ref[p, :].astype(BF16), wpp_ref[...]) for p in parts]
    for p, xx, a, b in zip(parts, x, pg, pp):
        y_ref[p, :] = _rms(xx + a * b, nf_ref[...])


def _post(x2d, og, orw, p2d, weights, tm):
    n = x2d.shape[0]
    wo, nffn, wg, wu, wd, nple, wpg, wpp, nf = weights
    tok = lambda w: pl.BlockSpec((tm, w), lambda i: (i, 0))
    return pl.pallas_call(
        _post_kernel,
        grid=(n // tm,),
        in_specs=[
            tok(D_MODEL), tok(GLA_W), tok(RWKV_W), tok(PLE_DIM),
            _const_spec((D_MODEL, D_MODEL)), _const_spec((1, D_MODEL)),
            _const_spec((D_MODEL, D_FF)), _const_spec((D_MODEL, D_FF)), _const_spec((D_FF, D_MODEL)),
            _const_spec((1, D_MODEL)), _const_spec((D_MODEL, D_MODEL)), _const_spec((PLE_DIM, D_MODEL)),
            _const_spec((1, D_MODEL)),
        ],
        out_specs=tok(D_MODEL),
        out_shape=jax.ShapeDtypeStruct((n, D_MODEL), F32),
        compiler_params=pltpu.CompilerParams(
            dimension_semantics=("arbitrary",), vmem_limit_bytes=VMEM_LIMIT),
        name="post",
    )(x2d, og, orw, p2d, wo, nffn, wg, wu, wd, nple, wpg, wpp, nf)


PROMPT_CHUNK = 64
PROMPT_SEQS_PER_STEP = 8
DEC_TILE_SEQS = 32
TOKEN_TILE = 512
PROJ_TILE = 1024


def kernel(x_prompt, x_sample, state_gla, state_rwkv, state_shift, p_prompt, p_sample, norm_mix, w_in, gla_gk_up, gla_gk_bias, gla_norm, rwkv_mu, rwkv_w0, rwkv_w2, rwkv_a0, rwkv_a2, rwkv_g2, rwkv_k_k, rwkv_k_a, rwkv_r_k, rwkv_ln_w, rwkv_ln_b, w_out, norm_ffn, w_gate, w_up, w_down, norm_ple, w_ple_gate, w_ple_proj, norm_final):
    assert w_in.shape[0] == 1
    i = 0
    rowv = lambda a: a.astype(F32).reshape(1, -1)
    zeros = lambda r, c: jnp.zeros((r, c), F32)
    w_in_t = jnp.transpose(w_in[i]).astype(BF16)
    gk_up = jnp.concatenate(
        [gla_gk_up[i].astype(F32), zeros(LANES - GLA_GATE_RANK, GLA_K_W)], axis=0).astype(BF16)
    seg = jnp.arange(SEG_W)[:, None] // RWKV_HEAD == jnp.arange(SEG_W)[None, :] // RWKV_HEAD
    rwkv_params = (
        rowv(rwkv_mu[i]), rowv(rwkv_w0[i]),
        jnp.concatenate([rwkv_w2[i].astype(F32), zeros(64, RWKV_W)], axis=0).astype(BF16),
        rowv(rwkv_a0[i]),
        jnp.concatenate([zeros(64, RWKV_W), rwkv_a2[i].astype(F32)], axis=0).astype(BF16),
        rwkv_g2[i].astype(BF16),
        rowv(rwkv_k_k[i]), rowv(rwkv_k_a[i]), rowv(rwkv_r_k[i]), rowv(rwkv_ln_w[i]), rowv(rwkv_ln_b[i]),
        seg.astype(BF16),
    )
    gla_w = (gk_up, rowv(gla_gk_bias[i]), rowv(gla_norm[i]))
    g_mix = rowv(norm_mix[i])

    bp, tp, _ = x_prompt.shape
    xp = x_prompt.astype(F32).reshape(bp * tp, D_MODEL)
    tail_w = tuple(w[i].astype(F32) for w in (w_out, w_gate, w_up, w_down, w_ple_gate, w_ple_proj))
    ug, ur, wo_b, wgate_b, wup_b, wdown_b, wpg_b, wpp_b = _proj(
        xp, g_mix, w_in_t, PROJ_TILE, tail_w)
    post_w = (wo_b, rowv(norm_ffn[i]), wgate_b, wup_b, wdown_b, rowv(norm_ple[i]), wpg_b, wpp_b,
              rowv(norm_final))
    ur3 = ur.reshape(bp, tp, RWKV_PROJ)
    og, gla_p, orw, rwkv_p = _mix_seq(
        ug.reshape(bp, tp, GLA_COLS), jnp.zeros((bp, GLA_HEADS * GLA_DK, GLA_DV), F32), gla_w,
        ur3, jnp.zeros((bp, 1, RWKV_PROJ), F32),
        jnp.zeros((1, bp, RWKV_HEADS, RWKV_HEAD, RWKV_HEAD), F32), rwkv_params,
        PROMPT_CHUNK, PROMPT_SEQS_PER_STEP)
    shift_p = ur3[:, tp - 1]
    yp = _post(xp, og.reshape(bp * tp, GLA_W), orw.reshape(bp * tp, RWKV_W),
               p_prompt[i].reshape(bp * tp, PLE_DIM), post_w, TOKEN_TILE)

    bs, ts, _ = x_sample.shape
    xs = x_sample.astype(F32).reshape(bs * ts, D_MODEL)
    ug, ur = _proj(xs, g_mix, w_in_t, TOKEN_TILE)
    og, gla_s = _gla_dec(ug, state_gla[i].astype(F32).reshape(bs, GLA_HEADS * GLA_DK, GLA_DV),
                         gla_w, ts, DEC_TILE_SEQS)
    shiftx = jnp.pad(state_shift[i].astype(F32)[:, None, :], ((0, 0), (0, ts - 1), (0, 0)))
    orw, rwkv_s = _rwkv_step(ur, shiftx.reshape(bs * ts, RWKV_PROJ),
                             jnp.transpose(state_rwkv[i].astype(F32), (1, 2, 3, 0)), rwkv_params, ts)
    rwkv_s = jnp.transpose(rwkv_s, (3, 0, 1, 2))[None]
    shift_s = ur[ts - 1::ts]
    ys = _post(xs, og, orw, p_sample[i].reshape(bs * ts, PLE_DIM), post_w, TOKEN_TILE)

    gla_shape = (1, -1, GLA_HEADS, GLA_DK, GLA_DV)
    return (yp.reshape(bp, tp, D_MODEL).astype(x_prompt.dtype),
            ys.reshape(bs, ts, D_MODEL).astype(x_sample.dtype),
            gla_p.reshape(gla_shape).astype(state_gla.dtype), rwkv_p.astype(state_rwkv.dtype),
            shift_p[None].astype(state_shift.dtype),
            gla_s.reshape(gla_shape).astype(state_gla.dtype), rwkv_s.astype(state_rwkv.dtype),
            shift_s[None].astype(state_shift.dtype))
```

```python
import functools

import jax
import jax.numpy as jnp
from jax import lax
from jax.experimental import pallas as pl
from jax.experimental.pallas import tpu as pltpu

F32 = jnp.float32
BF16 = jnp.bfloat16

D_MODEL = 1024
GLA_HEADS = 4
GLA_DK = 64
GLA_DV = 128
GLA_K_W = GLA_HEADS * GLA_DK
GLA_W = GLA_HEADS * GLA_DV
GLA_GATE_RANK = 16
GLA_GATE_NORM = 16.0
GLA_MAIN = 2 * GLA_K_W + 2 * GLA_W
GLA_PROJ = GLA_MAIN + GLA_GATE_RANK
LANES = 128
SUBLANES = 8
GLA_COLS = GLA_MAIN + LANES
RWKV_HEAD = 64
RWKV_HEADS = 8
RWKV_W = RWKV_HEADS * RWKV_HEAD
RWKV_PROJ = 3 * RWKV_W + 64 + 64 + 128
D_FF = 2816
PLE_DIM = 256
EPS = 1e-6
RWKV_GN_EPS = 64e-5

VMEM_LIMIT = 56 * 1024 * 1024

NN = ((1,), (0,))
NT = ((1,), (1,))


def _split(x, n):
    parts = []
    r = x
    for i in range(n):
        p = r.astype(BF16)
        parts.append(p)
        if i + 1 < n:
            r = r - p.astype(F32)
    return parts


def _mm(a, b, dims=NN, pa=1, pb=1):
    pieces_a = _split(a, pa)
    pieces_b = _split(b, pb)
    n = max(pa, pb)
    acc = None
    for i, ai in enumerate(pieces_a):
        for j, bj in enumerate(pieces_b):
            if i + j < n:
                t = lax.dot_general(ai, bj, (dims, ((), ())), preferred_element_type=F32)
                acc = t if acc is None else acc + t
    return acc


def _mm_tn(a, b, pa=1, pb=1):
    rows = a.shape[0]
    pad = (-rows) % LANES
    if pad:
        a = jnp.concatenate([a, jnp.zeros((pad, a.shape[1]), a.dtype)], axis=0)
        b = jnp.concatenate([b, jnp.zeros((pad, b.shape[1]), b.dtype)], axis=0)
    return _mm(a.T, b, NN, pa, pb)


def _iota(shape, dim):
    return lax.broadcasted_iota(jnp.int32, shape, dim)


def _log_sigmoid(z):
    return jnp.minimum(z, 0.0) - jnp.log(1.0 + jnp.exp(-jnp.abs(z)))


def _shift_rows(u, first):
    r = pltpu.roll(u, 1, axis=0)
    head = jnp.where(_iota((SUBLANES, 1), 0) == 0, first, r[:SUBLANES])
    return jnp.concatenate([head, r[SUBLANES:]], axis=0)


def _sigmoid(z):
    return 1.0 / (1.0 + jnp.exp(-z))


def _tanh(z):
    return 2.0 * _sigmoid(2.0 * z) - 1.0


def _rms(x, g):
    return x * lax.rsqrt(jnp.mean(x * x, axis=-1, keepdims=True) + EPS) * g


GROUP_HEADS = 2
GROUP_W = GROUP_HEADS * RWKV_HEAD
RWKV_GROUPS = RWKV_W // GROUP_W
SEG_W = 2 * LANES


def _head_masks(rows, head_w, heads):
    lane = _iota((rows, heads * head_w), 1)
    return [jnp.where(lane // head_w == h, 1.0, 0.0).astype(BF16) for h in range(heads)]


def _stack_heads(x, hmasks):
    xb = x.astype(BF16)
    return jnp.concatenate([xb * m for m in hmasks], axis=0)


def _block_diag(x_cat, blk):
    heads = blk.shape[0] // x_cat.shape[0]
    return jnp.concatenate([x_cat.astype(BF16)] * heads, axis=0) * blk


def _chunk_masks(rows, seq_len, heads):
    t = _iota((rows, heads * rows), 0)
    s = _iota((rows, heads * rows), 1) % rows
    incl = t >= s
    t2 = _iota((2 * rows, rows), 0)
    s2 = _iota((2 * rows, rows), 1)
    cum_rows = (t2 < rows) & (t2 >= s2)
    tot_rows = t2 >= rows
    if seq_len < rows:
        incl = incl & ((t // seq_len) == (s // seq_len))
        same2 = ((t2 % rows) // seq_len) == (s2 // seq_len)
        cum_rows, tot_rows = cum_rows & same2, tot_rows & same2
    big = heads * rows
    blk = jnp.where(_iota((big, big), 0) // rows == _iota((big, big), 1) // rows, 1.0, 0.0)
    t4 = _iota((2 * rows, 2 * big), 0)
    s4 = _iota((2 * rows, 2 * big), 1) % rows
    quad = ((t4 % rows) > s4) | ((t4 >= rows) & ((t4 % rows) == s4))
    if seq_len < rows:
        quad = quad & (((t4 % rows) // seq_len) == (s4 // seq_len))
    return dict(incl=incl, quad=quad, eye=jnp.where(t == s, 1.0, 0.0),
                sums=jnp.where(cum_rows | tot_rows, 1.0, 0.0).astype(BF16), blk=blk.astype(BF16))


X_BUFFERS = 3


def _proj_kernel(x_hbm, g_ref, wt_ref, *refs, steps):
    x_buf, x_sem = refs[-2], refs[-1]
    refs = refs[:-2]
    n_cast = (len(refs) - 2) // 2
    ug_ref, ur_ref = refs[n_cast], refs[n_cast + 1]
    tm = x_buf.shape[1]
    i = pl.program_id(0)

    def tile_copy(step):
        slot = step % X_BUFFERS
        return pltpu.make_async_copy(x_hbm.at[pl.ds(step * tm, tm)], x_buf.at[slot], x_sem.at[slot])

    @pl.when(i == 0)
    def _():
        for s in range(min(X_BUFFERS - 1, steps)):
            tile_copy(s).start()

    @pl.when(i + (X_BUFFERS - 1) < steps)
    def _():
        tile_copy(i + (X_BUFFERS - 1)).start()

    tile_copy(i).wait()
    h = _rms(x_buf[i % X_BUFFERS], g_ref[...]).astype(BF16)
    times_t = lambda w: lax.dot_general(h, w, (NT, ((), ())), preferred_element_type=F32)
    ug_ref[:, 0:GLA_MAIN] = times_t(wt_ref[0:GLA_MAIN])
    gate_rank = times_t(wt_ref[GLA_MAIN:GLA_PROJ])
    ug_ref[:, GLA_MAIN:GLA_COLS] = jnp.concatenate(
        [gate_rank, jnp.zeros((h.shape[0], GLA_COLS - GLA_PROJ), F32)], axis=1)
    ur_ref[...] = times_t(wt_ref[GLA_PROJ:GLA_PROJ + RWKV_PROJ])
    for src_ref, dst_ref in zip(refs[:n_cast], refs[n_cast + 2:]):
        dst_ref[...] = src_ref[...].astype(BF16)


def _const_spec(shape):
    return pl.BlockSpec(shape, lambda *_: (0,) * len(shape), pipeline_mode=pl.Buffered(1))


def _proj(x2d, g, w_in_t, tm, to_bf16=()):
    n = x2d.shape[0]
    steps = n // tm
    slabs = [pl.BlockSpec((w.shape[0] // steps, w.shape[1]), lambda i: (i, 0)) for w in to_bf16]
    return pl.pallas_call(
        functools.partial(_proj_kernel, steps=steps),
        grid=(steps,),
        in_specs=[
            pl.BlockSpec(memory_space=pl.ANY),
            _const_spec((1, D_MODEL)),
            _const_spec(w_in_t.shape),
        ] + slabs,
        out_specs=[
            pl.BlockSpec((tm, GLA_COLS), lambda i: (i, 0)),
            pl.BlockSpec((tm, RWKV_PROJ), lambda i: (i, 0)),
        ] + slabs,
        out_shape=[
            jax.ShapeDtypeStruct((n, GLA_COLS), F32),
            jax.ShapeDtypeStruct((n, RWKV_PROJ), F32),
        ] + [jax.ShapeDtypeStruct(w.shape, BF16) for w in to_bf16],
        scratch_shapes=[pltpu.VMEM((X_BUFFERS, tm, D_MODEL), F32),
                        pltpu.SemaphoreType.DMA((X_BUFFERS,))],
        compiler_params=pltpu.CompilerParams(
            dimension_semantics=("arbitrary",), vmem_limit_bytes=VMEM_LIMIT),
        name="proj",
    )(x2d, g, w_in_t, *to_bf16)


def _gla_prep(us, gkup, gkb, sums):
    rows = us[0].shape[0]
    z = [_mm(u[:, GLA_MAIN:GLA_COLS], gkup) for u in us]
    log_a = [_log_sigmoid(x + gkb) * (1.0 / GLA_GATE_NORM) for x in z]
    cums = [_mm(sums, x, NN, 1, 2) for x in log_a]
    out = []
    for u, x in zip(us, cums):
        cum, cum_end = x[:rows], x[rows:]
        q = u[:, 0:GLA_K_W] * (GLA_DK ** -0.5)
        k = u[:, GLA_K_W:2 * GLA_K_W]
        out.append((q * jnp.exp(cum), k * jnp.exp(-cum), k * jnp.exp(cum_end - cum),
                    u[:, 2 * GLA_K_W:2 * GLA_K_W + GLA_W], u[:, 2 * GLA_K_W + GLA_W:GLA_MAIN],
                    jnp.exp(cum_end)))
    return out


def _gla_intra(q_i, k_i, v, m, kmasks, vmasks):
    scores = jnp.where(m["incl"], _mm(q_i, _stack_heads(k_i, kmasks), NT), 0.0)
    return _mm(scores, _stack_heads(v, vmasks))


def _gla_out(o, gate, gnorm):
    heads = [slice(h * GLA_DV, (h + 1) * GLA_DV) for h in range(GLA_HEADS)]
    return jnp.concatenate(
        [_rms(o[:, hl], gnorm) * (gate[:, hl] * _sigmoid(gate[:, hl])) for hl in heads],
        axis=1).astype(BF16)


def _gla_block_mask():
    return (_iota((GLA_K_W, GLA_W), 0) // GLA_DK) == (_iota((GLA_K_W, GLA_W), 1) // GLA_DV)


def _gla_state_in(s, blk):
    return jnp.where(blk, jnp.concatenate([s] * GLA_HEADS, axis=1), 0.0)


def _gla_state_out(s_bd):
    heads = [s_bd[:, h * GLA_DV:(h + 1) * GLA_DV] for h in range(GLA_HEADS)]
    return (heads[0] + heads[1]) + (heads[2] + heads[3])


def _lane_tiled_t(x):
    pad = LANES - x.shape[0]
    if pad:
        x = jnp.concatenate([x, jnp.zeros((pad, x.shape[1]), x.dtype)], axis=0)
    return x.T


def _gla_seq_part(u_ref, s0_ref, gkup_ref, gkb_ref, gn_ref, o_ref, sout_ref, s_scr, *, phase):
    n_seq, rows = u_ref.shape[0], u_ref.shape[1]
    blk = _gla_block_mask()

    def init():
        for b in range(n_seq):
            s_scr[b] = _gla_state_in(s0_ref[b], blk)

    def final():
        for b in range(n_seq):
            sout_ref[b] = _gla_state_out(s_scr[b])

    if phase == "init":
        return init()
    if phase == "final":
        return final()

    m = _chunk_masks(rows, rows, GLA_HEADS)
    kmasks = _head_masks(rows, GLA_DK, GLA_HEADS)
    vmasks = _head_masks(rows, GLA_DV, GLA_HEADS)
    tok = _gla_prep([u_ref[b] for b in range(n_seq)], gkup_ref[...], gkb_ref[...], m["sums"])
    s_old = [s_scr[b] for b in range(n_seq)]
    intra = [_gla_intra(q_i, k_i, v, m, kmasks, vmasks) for q_i, k_i, _, v, _, _ in tok]
    inter = [_mm(t[0], s) for t, s in zip(tok, s_old)]
    kv = [_mm_tn(k_e, v) for _, _, k_e, v, _, _ in tok]
    for b in range(n_seq):
        gate, g_end = tok[b][4], tok[b][5]
        o_ref[b] = _gla_out(intra[b] + inter[b], gate, gn_ref[...])
        dec = jnp.concatenate([_lane_tiled_t(jnp.broadcast_to(g_end[0:1], (LANES, GLA_K_W)))] * GLA_HEADS,
                              axis=1)
        s_scr[b] = dec * s_old[b] + jnp.where(blk, kv[b], 0.0)


def _gla_dec_kernel(u_ref, s0_ref, gkup_ref, gkb_ref, gn_ref, o_ref, sout_ref, *, seq_len):
    rows = u_ref.shape[0]
    n_seq = rows // seq_len
    blk = _gla_block_mask()
    m = _chunk_masks(rows, seq_len, GLA_HEADS)
    q_i, k_i, k_e, v, gate, g_end = _gla_prep([u_ref[...]], gkup_ref[...], gkb_ref[...], m["sums"])[0]
    o = _gla_intra(q_i, k_i, v, m, _head_masks(rows, GLA_DK, GLA_HEADS),
                   _head_masks(rows, GLA_DV, GLA_HEADS))
    k_et = _lane_tiled_t(k_e).astype(BF16)
    dec_t = _lane_tiled_t(g_end)
    v_pad = v if rows == LANES else jnp.concatenate([v, jnp.zeros((LANES - rows, GLA_W), F32)], axis=0)
    row = _iota((rows, 1), 0)
    row_pad = _iota((LANES, 1), 0)
    for j in range(n_seq):
        s_bd = _gla_state_in(s0_ref[j], blk)
        o = o + _mm(jnp.where(row // seq_len == j, q_i, 0.0), s_bd)
        kv = _mm(k_et, jnp.where(row_pad // seq_len == j, v_pad, 0.0))
        first = j * seq_len
        sout_ref[j] = _gla_state_out(dec_t[:, first:first + 1] * s_bd + jnp.where(blk, kv, 0.0))
    o_ref[...] = _gla_out(o, gate, gn_ref[...])


def _gla_param_specs():
    return [_const_spec((LANES, GLA_K_W)), _const_spec((1, GLA_K_W)), _const_spec((1, GLA_DV))]


def _gla_dec(u2, s0, params, seq_len, n_seq):
    n = u2.shape[0]
    rows = n_seq * seq_len
    sdim = GLA_HEADS * GLA_DK
    state_spec = pl.BlockSpec((n_seq, sdim, GLA_DV), lambda i: (i, 0, 0))
    return pl.pallas_call(
        functools.partial(_gla_dec_kernel, seq_len=seq_len),
        grid=(n // rows,),
        in_specs=[pl.BlockSpec((rows, GLA_COLS), lambda i: (i, 0)), state_spec] + _gla_param_specs(),
        out_specs=[pl.BlockSpec((rows, GLA_W), lambda i: (i, 0)), state_spec],
        out_shape=[
            jax.ShapeDtypeStruct((n, GLA_W), BF16),
            jax.ShapeDtypeStruct(s0.shape, F32),
        ],
        compiler_params=pltpu.CompilerParams(
            dimension_semantics=("arbitrary",), vmem_limit_bytes=VMEM_LIMIT),
        name="gla_dec",
    )(u2, s0, *params)


def _seg_sum(x, seg, pa=1):
    return jnp.concatenate(
        [_mm(x[:, i:i + SEG_W], seg, NN, pa, 1) for i in range(0, RWKV_W, SEG_W)], axis=1)


W_OFFSET_SCALE = 0.6065306597126334


def _rwkv_tokens(u, prev, mu, w0, w2, a0, a2, g2, k_k, k_a, rk, seg):
    xr = u + mu * (prev - u)
    r = xr[:, 0:RWKV_W]
    kr = xr[:, RWKV_W:2 * RWKV_W]
    vr = xr[:, 2 * RWKV_W:3 * RWKV_W]
    wa = xr[:, 3 * RWKV_W:3 * RWKV_W + LANES]
    gd = xr[:, 3 * RWKV_W + LANES:RWKV_PROJ]
    lw = _sigmoid(w0 + _mm(_tanh(wa), w2)) * (-W_OFFSET_SCALE)
    a_sig = _sigmoid(a0 + _mm(wa, a2))
    gate = _mm(_sigmoid(gd), g2)
    kk = kr * k_k
    kk = kk * lax.rsqrt(jnp.maximum(_seg_sum(kk * kk, seg), 1e-24))
    kr = kr * (1.0 + (a_sig - 1.0) * k_a)
    bonus = _seg_sum(r * kr * rk, seg) * vr
    return r, kr, vr, lw, -kk, kk * a_sig, gate, bonus


def _rwkv_prep(u, prev, mu, w0, w2, a0, a2, g2, k_k, k_a, rk, seg, sums):
    rows = u.shape[0]
    r, kr, vr, lw, a_vec, b_vec, gate, bonus = _rwkv_tokens(
        u, prev, mu, w0, w2, a0, a2, g2, k_k, k_a, rk, seg)
    block = sums.shape[1]
    cums = [_mm(sums, lw[i:i + block], NN, 1, 2) for i in range(0, rows, block)]
    cum = jnp.concatenate([x[:block] for x in cums], axis=0)
    cum_end = jnp.concatenate([x[block:] for x in cums], axis=0)
    e_neg = jnp.exp(-cum)
    g_end = jnp.exp(cum_end)
    e_end = g_end * e_neg
    bf = lambda x: x.astype(BF16)
    return (bf(a_vec * jnp.exp(cum - lw)), bf(r * jnp.exp(cum)), bf(kr * e_neg), bf(b_vec * e_neg),
            bf(kr * e_end), bf(b_vec * e_end), bf(vr), gate, bonus, g_end)


def _rwkv_intra(units, m, hmasks, n_double):
    rows = units[0][0].shape[0]
    stack = lambda x: _stack_heads(x, hmasks)
    cat_w = m["incl"].shape[1]
    g = [_mm(jnp.concatenate([at, rt], axis=0), jnp.concatenate([stack(bt), stack(kt)], axis=0), NT)
         for at, rt, kt, bt, _ in units]
    g = [jnp.where(m["quad"], x, 0.0) for x in g]
    a_ab = [x[:rows, :cat_w] for x in g]
    a_rb = [x[rows:, :cat_w] for x in g]
    a_ak = [x[:rows, cat_w:] for x in g]
    a_rk = [x[rows:, cat_w:] for x in g]
    tinv = [m["eye"] + a for a in a_ab]
    apow = [_mm(a, _block_diag(a, m["blk"])) for a in a_ab]
    for _ in range(n_double - 1):
        both = [_mm(jnp.concatenate([t, a], axis=0), _block_diag(a, m["blk"])) for t, a in zip(tinv, apow)]
        tinv = [t + x[:rows] for t, x in zip(tinv, both)]
        apow = [x[rows:] for x in both]
    tinv = [t + _mm(t, _block_diag(a, m["blk"])) for t, a in zip(tinv, apow)]
    v_s = [stack(vp) for _, _, _, _, vp in units]
    akv = [_mm(jnp.concatenate([a, b], axis=0), v) for a, b, v in zip(a_ak, a_rk, v_s)]
    wu = [_mm(t, jnp.concatenate([stack(at), stack(x[:rows])], axis=1))
          for t, (at, _, _, _, _), x in zip(tinv, units, akv)]
    z = [_mm(a, jnp.concatenate([stack(x[:, :GROUP_W]), stack(x[:, GROUP_W:])], axis=1))
         for a, x in zip(a_rb, wu)]
    return [(x[:, :GROUP_W], x[:, GROUP_W:], rt + zz[:, :GROUP_W], zz[:, GROUP_W:] + kv[rows:])
            for x, (_, rt, _, _, _), zz, kv in zip(wu, units, z, akv)]


def _rwkv_out(y, gate, bonus, lnw, lnb, seg):
    inv = 1.0 / RWKV_HEAD
    yc = y - _seg_sum(y, seg, 2) * inv
    var = _seg_sum(yc * yc, seg) * inv
    return ((yc * lax.rsqrt(var + RWKV_GN_EPS) * lnw + lnb + bonus) * gate).astype(BF16)


def _rwkv_state_in(s_ref, idx, g):
    zero = jnp.zeros((RWKV_HEAD, RWKV_HEAD), F32)
    blocks = []
    for h in range(GROUP_HEADS):
        parts = [zero] * GROUP_HEADS
        parts[h] = s_ref[idx, g * GROUP_HEADS + h]
        blocks.append(jnp.concatenate(parts, axis=1))
    return jnp.concatenate(blocks, axis=0)


def _rwkv_state_out(s_ref, idx, g, s2):
    for h in range(GROUP_HEADS):
        sl = slice(h * RWKV_HEAD, (h + 1) * RWKV_HEAD)
        s_ref[idx, g * GROUP_HEADS + h] = s2[sl, sl]


def _head_block_mask():
    return (_iota((GROUP_W, GROUP_W), 0) // RWKV_HEAD) == (_iota((GROUP_W, GROUP_W), 1) // RWKV_HEAD)


def _rwkv_seq_part(u_ref, shift0_ref, s0_ref, mu_ref, w0_ref, w2_ref, a0_ref, a2_ref,
                   g2_ref, kk_ref, ka_ref, rk_ref, lnw_ref, lnb_ref, seg_ref,
                   o_ref, sout_ref, s_scr, prev_scr, *, phase):
    n_seq, rows = u_ref.shape[0], u_ref.shape[1]
    groups = [slice(g * GROUP_W, (g + 1) * GROUP_W) for g in range(RWKV_GROUPS)]
    ids = [(b, g) for b in range(n_seq) for g in range(RWKV_GROUPS)]

    def init():
        for b in range(n_seq):
            for g in range(RWKV_GROUPS):
                s_scr[b, g] = _rwkv_state_in(s0_ref, b, g)
            prev_scr[b] = shift0_ref[b]

    def final():
        for b, g in ids:
            _rwkv_state_out(sout_ref, b, g, s_scr[b, g])

    if phase == "init":
        return init()
    if phase == "final":
        return final()

    seg = seg_ref[...]
    m = _chunk_masks(rows, rows, GROUP_HEADS)
    hmasks = _head_masks(rows, RWKV_HEAD, GROUP_HEADS)
    head_blk = _head_block_mask()
    n_double = rows.bit_length() - 2

    us = [u_ref[b] for b in range(n_seq)]
    prev = jnp.concatenate([_shift_rows(u, prev_scr[b]) for b, u in enumerate(us)], axis=0)
    for b, u in enumerate(us):
        prev_scr[b] = u[rows - 1:rows]
    at, rt, kt, bt, ke, be, vb, gate, bonus, g_end = _rwkv_prep(
        jnp.concatenate(us, axis=0), prev, mu_ref[...], w0_ref[...], w2_ref[...], a0_ref[...],
        a2_ref[...], g2_ref[...], kk_ref[...], ka_ref[...], rk_ref[...], seg, m["sums"])
    tok, units = [], []
    for b in range(n_seq):
        sl = slice(b * rows, (b + 1) * rows)
        tok.append((vb[sl], ke[sl], be[sl], g_end[b * rows:b * rows + 1]))
        units += [(at[sl, gl], rt[sl, gl], kt[sl, gl], bt[sl, gl], vb[sl, gl]) for gl in groups]

    intra = _rwkv_intra(units, m, hmasks, n_double)
    s_old = [s_scr[b, g] for b, g in ids]
    uy = [_mm(jnp.concatenate([w_m, r_m], axis=0), s2, NT) for (w_m, _, r_m, _), s2 in zip(intra, s_old)]
    upd = []
    for (b, g), (_, u0, _, _), x in zip(ids, intra, uy):
        vb_b, ke_b, be_b, _ = tok[b]
        gl = groups[g]
        upd.append(_mm_tn(jnp.concatenate([x[:rows] + u0, vb_b[:, gl].astype(F32)], axis=0),
                          jnp.concatenate([be_b[:, gl], ke_b[:, gl]], axis=0)))
    for (b, g), s2, d in zip(ids, s_old, upd):
        s_scr[b, g] = s2 * tok[b][3][:, groups[g]] + jnp.where(head_blk, d, 0.0)
    ys = [jnp.concatenate([uy[i][rows:] + intra[i][3] for i, (bb, _) in enumerate(ids) if bb == b],
                          axis=1) for b in range(n_seq)]
    out = _rwkv_out(jnp.concatenate(ys, axis=0), gate, bonus, lnw_ref[...], lnb_ref[...], seg)
    for b in range(n_seq):
        o_ref[b] = out[b * rows:(b + 1) * rows]


STEP_HEADS = 2
STEP_ROWS = 4
N_STEP_OPERANDS = 6


def _sum_keys(x):
    t = x[0:SUBLANES]
    for i in range(SUBLANES, x.shape[0], SUBLANES):
        t = t + x[i:i + SUBLANES]
    for shift in (4, 2, 1):
        t = t + pltpu.roll(t, shift, axis=0)
    return t


def _rwkv_step_kernel(u_ref, shiftx_ref, s0_ref, mu_ref, w0_ref, w2_ref, a0_ref, a2_ref, g2_ref,
                      kk_ref, ka_ref, rk_ref, lnw_ref, lnb_ref, seg_ref,
                      o_ref, sout_ref, ops_scr, y_scr, tmp_scr, aux_scr, *, seq_len):
    p = pl.program_id(0)
    n_tok = u_ref.shape[0]
    n_seq = n_tok // seq_len
    seg = seg_ref[...]
    cols = [slice(c, c + LANES) for c in range(0, RWKV_W, LANES)]

    @pl.when(p == 0)
    def _():
        u = u_ref[...]
        row = _iota((n_tok, 1), 0)
        prev = jnp.where(row % seq_len == 0, shiftx_ref[...], pltpu.roll(u, 1, axis=0))
        r, kr, vr, lw, a_vec, b_vec, gate, bonus = _rwkv_tokens(
            u, prev, mu_ref[...], w0_ref[...], w2_ref[...], a0_ref[...], a2_ref[...], g2_ref[...],
            kk_ref[...], ka_ref[...], rk_ref[...], seg)
        aux_scr[0] = gate
        aux_scr[1] = bonus
        for i, x in enumerate((jnp.exp(lw), a_vec, b_vec, kr, r, vr)):
            for c, cl in enumerate(cols):
                tmp_scr[c] = x[:, cl]
            for t in range(seq_len):
                for c, cl in enumerate(cols):
                    ops_scr[i, t, cl, :] = tmp_scr[c, pl.ds(t, n_seq, stride=seq_len), :].T

    def rows_step(i, carry):
        for hh in range(STEP_HEADS):
            chan = pl.multiple_of((p * STEP_HEADS + hh) * RWKV_HEAD, RWKV_HEAD)
            for j in range(STEP_ROWS):
                v = i * STEP_ROWS + j
                s = s0_ref[hh, v]
                for t in range(seq_len):
                    w, a, b, k, r = (ops_scr[n, t, pl.ds(chan, RWKV_HEAD), :] for n in range(5))
                    v_t = ops_scr[5, t, pl.ds(chan + v, 1), :]
                    sa = jnp.concatenate([_sum_keys(s * a)] * (RWKV_HEAD // SUBLANES), axis=0)
                    s = s * w + sa * b + v_t * k
                    y_scr[t, pl.ds(chan + v, 1), :] = _sum_keys(s * r)[0:1]
                sout_ref[hh, v] = s
        return carry

    lax.fori_loop(0, RWKV_HEAD // STEP_ROWS, rows_step, 0)

    @pl.when(p == pl.num_programs(0) - 1)
    def _():
        for t in range(seq_len):
            for c, cl in enumerate(cols):
                tmp_scr[c, pl.ds(t, n_seq, stride=seq_len), :] = y_scr[t, cl, :].T
        y = jnp.concatenate([tmp_scr[c] for c in range(len(cols))], axis=1)
        o_ref[...] = _rwkv_out(y, aux_scr[0], aux_scr[1], lnw_ref[...], lnb_ref[...], seg)


def _rwkv_step(u2, shiftx, s0_t, params, seq_len):
    n = u2.shape[0]
    n_seq = n // seq_len
    state_spec = pl.BlockSpec((STEP_HEADS, RWKV_HEAD, RWKV_HEAD, n_seq), lambda i: (i, 0, 0, 0))
    whole = lambda w: pl.BlockSpec((n, w), lambda i: (0, 0), pipeline_mode=pl.Buffered(1))
    return pl.pallas_call(
        functools.partial(_rwkv_step_kernel, seq_len=seq_len),
        grid=(RWKV_HEADS // STEP_HEADS,),
        in_specs=[whole(RWKV_PROJ), whole(RWKV_PROJ), state_spec] + _rwkv_param_specs(),
        out_specs=[pl.BlockSpec((n, RWKV_W), lambda i: (0, 0)), state_spec],
        out_shape=[
            jax.ShapeDtypeStruct((n, RWKV_W), BF16),
            jax.ShapeDtypeStruct(s0_t.shape, F32),
        ],
        scratch_shapes=[
            pltpu.VMEM((N_STEP_OPERANDS, seq_len, RWKV_W, n_seq), F32),
            pltpu.VMEM((seq_len, RWKV_W, n_seq), F32),
            pltpu.VMEM((RWKV_W // LANES, n, LANES), F32),
            pltpu.VMEM((2, n, RWKV_W), F32),
        ],
        compiler_params=pltpu.CompilerParams(
            dimension_semantics=("arbitrary",), vmem_limit_bytes=VMEM_LIMIT),
        name="rwkv_step",
    )(u2, shiftx, s0_t, *params)


def _rwkv_param_specs():
    vec = lambda n: _const_spec((1, n))
    return [
        vec(RWKV_PROJ),
        vec(RWKV_W),
        _const_spec((LANES, RWKV_W)),
        vec(RWKV_W),
        _const_spec((LANES, RWKV_W)),
        _const_spec((LANES, RWKV_W)),
        vec(RWKV_W), vec(RWKV_W), vec(RWKV_W), vec(RWKV_W), vec(RWKV_W),
        _const_spec((SEG_W, SEG_W)),
    ]


N_GLA_SEQ_IN = 5
N_RWKV_SEQ_IN = 15


def _mix_seq_kernel(*refs):
    n_in = N_GLA_SEQ_IN + N_RWKV_SEQ_IN
    gla = refs[:N_GLA_SEQ_IN] + refs[n_in:n_in + 2] + refs[n_in + 4:n_in + 5]
    rwkv = refs[N_GLA_SEQ_IN:n_in] + refs[n_in + 2:n_in + 4] + refs[n_in + 5:]
    c = pl.program_id(1)

    @pl.when(c == 0)
    def _():
        _gla_seq_part(*gla, phase="init")
        _rwkv_seq_part(*rwkv, phase="init")

    _rwkv_seq_part(*rwkv, phase="body")
    _gla_seq_part(*gla, phase="body")

    @pl.when(c == pl.num_programs(1) - 1)
    def _():
        _gla_seq_part(*gla, phase="final")
        _rwkv_seq_part(*rwkv, phase="final")


def _mix_seq(ug3, gla_s0, gla_params, ur3, shift0, rwkv_s0, rwkv_params, rows, n_seq):
    b, t, _ = ug3.shape
    sdim = GLA_HEADS * GLA_DK
    gla_state = pl.BlockSpec((n_seq, sdim, GLA_DV), lambda i, j: (i, 0, 0))
    rwkv_state = pl.BlockSpec((None, n_seq, RWKV_HEADS, RWKV_HEAD, RWKV_HEAD),
                              lambda i, j: (0, i, 0, 0, 0))
    tok = lambda w: pl.BlockSpec((n_seq, rows, w), lambda i, j: (i, j, 0))
    return pl.pallas_call(
        _mix_seq_kernel,
        grid=(b // n_seq, t // rows),
        in_specs=[tok(GLA_COLS), gla_state] + _gla_param_specs()
        + [tok(RWKV_PROJ), pl.BlockSpec((n_seq, 1, RWKV_PROJ), lambda i, j: (i, 0, 0)), rwkv_state]
        + _rwkv_param_specs(),
        out_specs=[tok(GLA_W), gla_state, tok(RWKV_W), rwkv_state],
        out_shape=[
            jax.ShapeDtypeStruct((b, t, GLA_W), BF16),
            jax.ShapeDtypeStruct((b, sdim, GLA_DV), F32),
            jax.ShapeDtypeStruct((b, t, RWKV_W), BF16),
            jax.ShapeDtypeStruct(rwkv_s0.shape, F32),
        ],
        scratch_shapes=[
            pltpu.VMEM((n_seq, sdim, GLA_W), F32),
            pltpu.VMEM((n_seq, RWKV_GROUPS, GROUP_W, GROUP_W), F32),
            pltpu.VMEM((n_seq, 1, RWKV_PROJ), F32),
        ],
        compiler_params=pltpu.CompilerParams(
            dimension_semantics=("arbitrary", "arbitrary"), vmem_limit_bytes=VMEM_LIMIT),
        name="mix_seq",
    )(ug3, gla_s0, *gla_params, ur3, shift0, rwkv_s0, *rwkv_params)


FF_CHUNK = D_FF // 2


def _post_kernel(x_ref, og_ref, or_ref, p_ref, wo_ref, nffn_ref, wg_ref, wu_ref, wd_ref,
                 nple_ref, wpg_ref, wpp_ref, nf_ref, y_ref):
    half = x_ref.shape[0] // 2
    parts = [slice(0, half), slice(half, 2 * half)]
    dot = lambda a, b: jnp.dot(a, b, preferred_element_type=F32)
    o = [jnp.concatenate([og_ref[p, :], or_ref[p, :]], axis=1) for p in parts]
    x = [x_ref[p, :] + dot(oo, wo_ref[...]) for p, oo in zip(parts, o)]
    h2 = [_rms(xx, nffn_ref[...]).astype(BF16) for xx in x]
    for i in range(0, D_FF, FF_CHUNK):
        gate = [dot(h, wg_ref[:, i:i + FF_CHUNK]) for h in h2]
        up = [dot(h, wu_ref[:, i:i + FF_CHUNK]) for h in h2]
        act = [(g * _sigmoid(g) * u).astype(BF16) for g, u in zip(gate, up)]
        x = [xx + dot(a, wd_ref[i:i + FF_CHUNK, :]) for xx, a in zip(x, act)]
    h3 = [_rms(xx, nple_ref[...]).astype(BF16) for xx in x]
    pg = [_sigmoid(dot(h, wpg_ref[...])) for h in h3]
    pp = [dot(p_ref[p, :].astype(BF16), wpp_ref[...]) for p in parts]
    for p, xx, a, b in zip(parts, x, pg, pp):
        y_ref[p, :] = _rms(xx + a * b, nf_ref[...])


def _post(x2d, og, orw, p2d, weights, tm):
    n = x2d.shape[0]
    wo, nffn, wg, wu, wd, nple, wpg, wpp, nf = weights
    tok = lambda w: pl.BlockSpec((tm, w), lambda i: (i, 0))
    return pl.pallas_call(
        _post_kernel,
        grid=(n // tm,),
        in_specs=[
            tok(D_MODEL), tok(GLA_W), tok(RWKV_W), tok(PLE_DIM),
            _const_spec((D_MODEL, D_MODEL)), _const_spec((1, D_MODEL)),
            _const_spec((D_MODEL, D_FF)), _const_spec((D_MODEL, D_FF)), _const_spec((D_FF, D_MODEL)),
            _const_spec((1, D_MODEL)), _const_spec((D_MODEL, D_MODEL)), _const_spec((PLE_DIM, D_MODEL)),
            _const_spec((1, D_MODEL)),
        ],
        out_specs=tok(D_MODEL),
        out_shape=jax.ShapeDtypeStruct((n, D_MODEL), F32),
        compiler_params=pltpu.CompilerParams(
            dimension_semantics=("arbitrary",), vmem_limit_bytes=VMEM_LIMIT),
        name="post",
    )(x2d, og, orw, p2d, wo, nffn, wg, wu, wd, nple, wpg, wpp, nf)


PROMPT_CHUNK = 64
PROMPT_SEQS_PER_STEP = 8
DEC_TILE_SEQS = 32
TOKEN_TILE = 512
PROJ_TILE = 1024


def kernel(x_prompt, x_sample, state_gla, state_rwkv, state_shift, p_prompt, p_sample, norm_mix, w_in, gla_gk_up, gla_gk_bias, gla_norm, rwkv_mu, rwkv_w0, rwkv_w2, rwkv_a0, rwkv_a2, rwkv_g2, rwkv_k_k, rwkv_k_a, rwkv_r_k, rwkv_ln_w, rwkv_ln_b, w_out, norm_ffn, w_gate, w_up, w_down, norm_ple, w_ple_gate, w_ple_proj, norm_final):
    assert w_in.shape[0] == 1
    i = 0
    rowv = lambda a: a.astype(F32).reshape(1, -1)
    zeros = lambda r, c: jnp.zeros((r, c), F32)
    w_in_t = jnp.transpose(w_in[i]).astype(BF16)
    gk_up = jnp.concatenate(
        [gla_gk_up[i].astype(F32), zeros(LANES - GLA_GATE_RANK, GLA_K_W)], axis=0).astype(BF16)
    seg = jnp.arange(SEG_W)[:, None] // RWKV_HEAD == jnp.arange(SEG_W)[None, :] // RWKV_HEAD
    rwkv_params = (
        rowv(rwkv_mu[i]), rowv(rwkv_w0[i]),
        jnp.concatenate([rwkv_w2[i].astype(F32), zeros(64, RWKV_W)], axis=0).astype(BF16),
        rowv(rwkv_a0[i]),
        jnp.concatenate([zeros(64, RWKV_W), rwkv_a2[i].astype(F32)], axis=0).astype(BF16),
        rwkv_g2[i].astype(BF16),
        rowv(rwkv_k_k[i]), rowv(rwkv_k_a[i]), rowv(rwkv_r_k[i]), rowv(rwkv_ln_w[i]), rowv(rwkv_ln_b[i]),
        seg.astype(BF16),
    )
    gla_w = (gk_up, rowv(gla_gk_bias[i]), rowv(gla_norm[i]))
    g_mix = rowv(norm_mix[i])

    bp, tp, _ = x_prompt.shape
    xp = x_prompt.astype(F32).reshape(bp * tp, D_MODEL)
    tail_w = tuple(w[i].astype(F32) for w in (w_out, w_gate, w_up, w_down, w_ple_gate, w_ple_proj))
    ug, ur, wo_b, wgate_b, wup_b, wdown_b, wpg_b, wpp_b = _proj(
        xp, g_mix, w_in_t, PROJ_TILE, tail_w)
    post_w = (wo_b, rowv(norm_ffn[i]), wgate_b, wup_b, wdown_b, rowv(norm_ple[i]), wpg_b, wpp_b,
              rowv(norm_final))
    ur3 = ur.reshape(bp, tp, RWKV_PROJ)
    og, gla_p, orw, rwkv_p = _mix_seq(
        ug.reshape(bp, tp, GLA_COLS), jnp.zeros((bp, GLA_HEADS * GLA_DK, GLA_DV), F32), gla_w,
        ur3, jnp.zeros((bp, 1, RWKV_PROJ), F32),
        jnp.zeros((1, bp, RWKV_HEADS, RWKV_HEAD, RWKV_HEAD), F32), rwkv_params,
        PROMPT_CHUNK, PROMPT_SEQS_PER_STEP)
    shift_p = ur3[:, tp - 1]
    yp = _post(xp, og.reshape(bp * tp, GLA_W), orw.reshape(bp * tp, RWKV_W),
               p_prompt[i].reshape(bp * tp, PLE_DIM), post_w, TOKEN_TILE)

    bs, ts, _ = x_sample.shape
    xs = x_sample.astype(F32).reshape(bs * ts, D_MODEL)
    ug, ur = _proj(xs, g_mix, w_in_t, TOKEN_TILE)
    og, gla_s = _gla_dec(ug, state_gla[i].astype(F32).reshape(bs, GLA_HEADS * GLA_DK, GLA_DV),
                         gla_w, ts, DEC_TILE_SEQS)
    shiftx = jnp.pad(state_shift[i].astype(F32)[:, None, :], ((0, 0), (0, ts - 1), (0, 0)))
    orw, rwkv_s = _rwkv_step(ur, shiftx.reshape(bs * ts, RWKV_PROJ),
                             jnp.transpose(state_rwkv[i].astype(F32), (1, 2, 3, 0)), rwkv_params, ts)
    rwkv_s = jnp.transpose(rwkv_s, (3, 0, 1, 2))[None]
    shift_s = ur.reshape(bs, ts, RWKV_PROJ)[:, ts - 1]
    ys = _post(xs, og, orw, p_sample[i].reshape(bs * ts, PLE_DIM), post_w, TOKEN_TILE)

    gla_shape = (1, -1, GLA_HEADS, GLA_DK, GLA_DV)
    return (yp.reshape(bp, tp, D_MODEL).astype(x_prompt.dtype),
            ys.reshape(bs, ts, D_MODEL).astype(x_sample.dtype),
            gla_p.reshape(gla_shape).astype(state_gla.dtype), rwkv_p.astype(state_rwkv.dtype),
            shift_p[None].astype(state_shift.dtype),
            gla_s.reshape(gla_shape).astype(state_gla.dtype), rwkv_s.astype(state_rwkv.dtype),
            shift_s[None].astype(state_shift.dtype))
```

```python
import functools

import jax
import jax.numpy as jnp
from jax import lax
from jax.experimental import pallas as pl
from jax.experimental.pallas import tpu as pltpu

F32 = jnp.float32
BF16 = jnp.bfloat16

D_MODEL = 1024
GLA_HEADS = 4
GLA_DK = 64
GLA_DV = 128
GLA_K_W = GLA_HEADS * GLA_DK
GLA_W = GLA_HEADS * GLA_DV
GLA_GATE_RANK = 16
GLA_GATE_NORM = 16.0
GLA_MAIN = 2 * GLA_K_W + 2 * GLA_W
GLA_PROJ = GLA_MAIN + GLA_GATE_RANK
LANES = 128
SUBLANES = 8
GLA_COLS = GLA_MAIN + LANES
RWKV_HEAD = 64
RWKV_HEADS = 8
RWKV_W = RWKV_HEADS * RWKV_HEAD
RWKV_PROJ = 3 * RWKV_W + 64 + 64 + 128
D_FF = 2816
PLE_DIM = 256
EPS = 1e-6
RWKV_GN_EPS = 64e-5

VMEM_LIMIT = 56 * 1024 * 1024

NN = ((1,), (0,))
NT = ((1,), (1,))


def _split(x, n):
    parts = []
    r = x
    for i in range(n):
        p = r.astype(BF16)
        parts.append(p)
        if i + 1 < n:
            r = r - p.astype(F32)
    return parts


def _mm(a, b, dims=NN, pa=1, pb=1):
    pieces_a = _split(a, pa)
    pieces_b = _split(b, pb)
    n = max(pa, pb)
    acc = None
    for i, ai in enumerate(pieces_a):
        for j, bj in enumerate(pieces_b):
            if i + j < n:
                t = lax.dot_general(ai, bj, (dims, ((), ())), preferred_element_type=F32)
                acc = t if acc is None else acc + t
    return acc


def _mm_tn(a, b, pa=1, pb=1):
    rows = a.shape[0]
    pad = (-rows) % LANES
    if pad:
        a = jnp.concatenate([a, jnp.zeros((pad, a.shape[1]), a.dtype)], axis=0)
        b = jnp.concatenate([b, jnp.zeros((pad, b.shape[1]), b.dtype)], axis=0)
    return _mm(a.T, b, NN, pa, pb)


def _iota(shape, dim):
    return lax.broadcasted_iota(jnp.int32, shape, dim)


def _log_sigmoid(z):
    return jnp.minimum(z, 0.0) - jnp.log(1.0 + jnp.exp(-jnp.abs(z)))


def _shift_rows(u, first):
    r = pltpu.roll(u, 1, axis=0)
    head = jnp.where(_iota((SUBLANES, 1), 0) == 0, first, r[:SUBLANES])
    return jnp.concatenate([head, r[SUBLANES:]], axis=0)


def _sigmoid(z):
    return 1.0 / (1.0 + jnp.exp(-z))


def _tanh(z):
    return 2.0 * _sigmoid(2.0 * z) - 1.0


def _rms(x, g):
    return x * lax.rsqrt(jnp.mean(x * x, axis=-1, keepdims=True) + EPS) * g


GROUP_HEADS = 2
GROUP_W = GROUP_HEADS * RWKV_HEAD
RWKV_GROUPS = RWKV_W // GROUP_W
SEG_W = 2 * LANES


def _head_masks(rows, head_w, heads):
    lane = _iota((rows, heads * head_w), 1)
    return [jnp.where(lane // head_w == h, 1.0, 0.0).astype(BF16) for h in range(heads)]


def _stack_heads(x, hmasks):
    xb = x.astype(BF16)
    return jnp.concatenate([xb * m for m in hmasks], axis=0)


def _block_diag(x_cat, blk):
    heads = blk.shape[0] // x_cat.shape[0]
    return jnp.concatenate([x_cat.astype(BF16)] * heads, axis=0) * blk


def _chunk_masks(rows, seq_len, heads):
    t = _iota((rows, heads * rows), 0)
    s = _iota((rows, heads * rows), 1) % rows
    incl = t >= s
    t2 = _iota((2 * rows, rows), 0)
    s2 = _iota((2 * rows, rows), 1)
    cum_rows = (t2 < rows) & (t2 >= s2)
    tot_rows = t2 >= rows
    if seq_len < rows:
        incl = incl & ((t // seq_len) == (s // seq_len))
        same2 = ((t2 % rows) // seq_len) == (s2 // seq_len)
        cum_rows, tot_rows = cum_rows & same2, tot_rows & same2
    big = heads * rows
    blk = jnp.where(_iota((big, big), 0) // rows == _iota((big, big), 1) // rows, 1.0, 0.0)
    t4 = _iota((2 * rows, 2 * big), 0)
    s4 = _iota((2 * rows, 2 * big), 1) % rows
    quad = ((t4 % rows) > s4) | ((t4 >= rows) & ((t4 % rows) == s4))
    if seq_len < rows:
        quad = quad & (((t4 % rows) // seq_len) == (s4 // seq_len))
    return dict(incl=incl, quad=quad, eye=jnp.where(t == s, 1.0, 0.0),
                sums=jnp.where(cum_rows | tot_rows, 1.0, 0.0).astype(BF16), blk=blk.astype(BF16))


def _proj_kernel(x_ref, g_ref, wt_ref, *refs):
    n_cast = (len(refs) - 2) // 2
    ug_ref, ur_ref = refs[n_cast], refs[n_cast + 1]
    h = _rms(x_ref[...], g_ref[...]).astype(BF16)
    times_t = lambda w: lax.dot_general(h, w, (NT, ((), ())), preferred_element_type=F32)
    ug_ref[:, 0:GLA_MAIN] = times_t(wt_ref[0:GLA_MAIN])
    gate_rank = times_t(wt_ref[GLA_MAIN:GLA_PROJ])
    ug_ref[:, GLA_MAIN:GLA_COLS] = jnp.concatenate(
        [gate_rank, jnp.zeros((h.shape[0], GLA_COLS - GLA_PROJ), F32)], axis=1)
    ur_ref[...] = times_t(wt_ref[GLA_PROJ:GLA_PROJ + RWKV_PROJ])
    for src_ref, dst_ref in zip(refs[:n_cast], refs[n_cast + 2:]):
        dst_ref[...] = src_ref[...].astype(BF16)


def _const_spec(shape):
    return pl.BlockSpec(shape, lambda *_: (0,) * len(shape), pipeline_mode=pl.Buffered(1))


def _proj(x2d, g, w_in_t, tm, to_bf16=()):
    n = x2d.shape[0]
    steps = n // tm
    slabs = [pl.BlockSpec((w.shape[0] // steps, w.shape[1]), lambda i: (i, 0)) for w in to_bf16]
    return pl.pallas_call(
        _proj_kernel,
        grid=(steps,),
        in_specs=[
            pl.BlockSpec((tm, D_MODEL), lambda i: (i, 0)),
            _const_spec((1, D_MODEL)),
            _const_spec(w_in_t.shape),
        ] + slabs,
        out_specs=[
            pl.BlockSpec((tm, GLA_COLS), lambda i: (i, 0)),
            pl.BlockSpec((tm, RWKV_PROJ), lambda i: (i, 0)),
        ] + slabs,
        out_shape=[
            jax.ShapeDtypeStruct((n, GLA_COLS), F32),
            jax.ShapeDtypeStruct((n, RWKV_PROJ), F32),
        ] + [jax.ShapeDtypeStruct(w.shape, BF16) for w in to_bf16],
        compiler_params=pltpu.CompilerParams(
            dimension_semantics=("arbitrary",), vmem_limit_bytes=VMEM_LIMIT),
        name="proj",
    )(x2d, g, w_in_t, *to_bf16)


def _gla_prep(us, gkup, gkb, sums):
    rows = us[0].shape[0]
    z = [_mm(u[:, GLA_MAIN:GLA_COLS], gkup) for u in us]
    log_a = [_log_sigmoid(x + gkb) * (1.0 / GLA_GATE_NORM) for x in z]
    cums = [_mm(sums, x, NN, 1, 2) for x in log_a]
    out = []
    for u, x in zip(us, cums):
        cum, cum_end = x[:rows], x[rows:]
        q = u[:, 0:GLA_K_W] * (GLA_DK ** -0.5)
        k = u[:, GLA_K_W:2 * GLA_K_W]
        out.append((q * jnp.exp(cum), k * jnp.exp(-cum), k * jnp.exp(cum_end - cum),
                    u[:, 2 * GLA_K_W:2 * GLA_K_W + GLA_W], u[:, 2 * GLA_K_W + GLA_W:GLA_MAIN],
                    jnp.exp(cum_end)))
    return out


def _gla_intra(q_i, k_i, v, m, kmasks, vmasks):
    scores = jnp.where(m["incl"], _mm(q_i, _stack_heads(k_i, kmasks), NT), 0.0)
    return _mm(scores, _stack_heads(v, vmasks))


def _gla_out(o, gate, gnorm):
    heads = [slice(h * GLA_DV, (h + 1) * GLA_DV) for h in range(GLA_HEADS)]
    return jnp.concatenate(
        [_rms(o[:, hl], gnorm) * (gate[:, hl] * _sigmoid(gate[:, hl])) for hl in heads],
        axis=1).astype(BF16)


def _gla_block_mask():
    return (_iota((GLA_K_W, GLA_W), 0) // GLA_DK) == (_iota((GLA_K_W, GLA_W), 1) // GLA_DV)


def _gla_state_in(s, blk):
    return jnp.where(blk, jnp.concatenate([s] * GLA_HEADS, axis=1), 0.0)


def _gla_state_out(s_bd):
    heads = [s_bd[:, h * GLA_DV:(h + 1) * GLA_DV] for h in range(GLA_HEADS)]
    return (heads[0] + heads[1]) + (heads[2] + heads[3])


def _lane_tiled_t(x):
    pad = LANES - x.shape[0]
    if pad:
        x = jnp.concatenate([x, jnp.zeros((pad, x.shape[1]), x.dtype)], axis=0)
    return x.T


def _gla_seq_part(u_ref, s0_ref, gkup_ref, gkb_ref, gn_ref, o_ref, sout_ref, s_scr, *, phase):
    n_seq, rows = u_ref.shape[0], u_ref.shape[1]
    blk = _gla_block_mask()

    def init():
        for b in range(n_seq):
            s_scr[b] = _gla_state_in(s0_ref[b], blk)

    def final():
        for b in range(n_seq):
            sout_ref[b] = _gla_state_out(s_scr[b])

    if phase == "init":
        return init()
    if phase == "final":
        return final()

    m = _chunk_masks(rows, rows, GLA_HEADS)
    kmasks = _head_masks(rows, GLA_DK, GLA_HEADS)
    vmasks = _head_masks(rows, GLA_DV, GLA_HEADS)
    tok = _gla_prep([u_ref[b] for b in range(n_seq)], gkup_ref[...], gkb_ref[...], m["sums"])
    s_old = [s_scr[b] for b in range(n_seq)]
    intra = [_gla_intra(q_i, k_i, v, m, kmasks, vmasks) for q_i, k_i, _, v, _, _ in tok]
    inter = [_mm(t[0], s) for t, s in zip(tok, s_old)]
    kv = [_mm_tn(k_e, v) for _, _, k_e, v, _, _ in tok]
    for b in range(n_seq):
        gate, g_end = tok[b][4], tok[b][5]
        o_ref[b] = _gla_out(intra[b] + inter[b], gate, gn_ref[...])
        dec = jnp.concatenate([_lane_tiled_t(jnp.broadcast_to(g_end[0:1], (LANES, GLA_K_W)))] * GLA_HEADS,
                              axis=1)
        s_scr[b] = dec * s_old[b] + jnp.where(blk, kv[b], 0.0)


def _gla_dec_kernel(u_ref, s0_ref, gkup_ref, gkb_ref, gn_ref, o_ref, sout_ref, *, seq_len):
    rows = u_ref.shape[0]
    n_seq = rows // seq_len
    blk = _gla_block_mask()
    m = _chunk_masks(rows, seq_len, GLA_HEADS)
    q_i, k_i, k_e, v, gate, g_end = _gla_prep([u_ref[...]], gkup_ref[...], gkb_ref[...], m["sums"])[0]
    o = _gla_intra(q_i, k_i, v, m, _head_masks(rows, GLA_DK, GLA_HEADS),
                   _head_masks(rows, GLA_DV, GLA_HEADS))
    k_et = _lane_tiled_t(k_e).astype(BF16)
    dec_t = _lane_tiled_t(g_end)
    v_pad = v if rows == LANES else jnp.concatenate([v, jnp.zeros((LANES - rows, GLA_W), F32)], axis=0)
    row = _iota((rows, 1), 0)
    row_pad = _iota((LANES, 1), 0)
    for j in range(n_seq):
        s_bd = _gla_state_in(s0_ref[j], blk)
        o = o + _mm(jnp.where(row // seq_len == j, q_i, 0.0), s_bd)
        kv = _mm(k_et, jnp.where(row_pad // seq_len == j, v_pad, 0.0))
        first = j * seq_len
        sout_ref[j] = _gla_state_out(dec_t[:, first:first + 1] * s_bd + jnp.where(blk, kv, 0.0))
    o_ref[...] = _gla_out(o, gate, gn_ref[...])


def _gla_param_specs():
    return [_const_spec((LANES, GLA_K_W)), _const_spec((1, GLA_K_W)), _const_spec((1, GLA_DV))]


def _gla_dec(u2, s0, params, seq_len, n_seq):
    n = u2.shape[0]
    rows = n_seq * seq_len
    sdim = GLA_HEADS * GLA_DK
    state_spec = pl.BlockSpec((n_seq, sdim, GLA_DV), lambda i: (i, 0, 0))
    return pl.pallas_call(
        functools.partial(_gla_dec_kernel, seq_len=seq_len),
        grid=(n // rows,),
        in_specs=[pl.BlockSpec((rows, GLA_COLS), lambda i: (i, 0)), state_spec] + _gla_param_specs(),
        out_specs=[pl.BlockSpec((rows, GLA_W), lambda i: (i, 0)), state_spec],
        out_shape=[
            jax.ShapeDtypeStruct((n, GLA_W), BF16),
            jax.ShapeDtypeStruct(s0.shape, F32),
        ],
        compiler_params=pltpu.CompilerParams(
            dimension_semantics=("arbitrary",), vmem_limit_bytes=VMEM_LIMIT),
        name="gla_dec",
    )(u2, s0, *params)


def _seg_sum(x, seg, pa=1):
    return jnp.concatenate(
        [_mm(x[:, i:i + SEG_W], seg, NN, pa, 1) for i in range(0, RWKV_W, SEG_W)], axis=1)


W_OFFSET_SCALE = 0.6065306597126334


def _rwkv_tokens(u, prev, mu, w0, w2, a0, a2, g2, k_k, k_a, rk, seg):
    xr = u + mu * (prev - u)
    r = xr[:, 0:RWKV_W]
    kr = xr[:, RWKV_W:2 * RWKV_W]
    vr = xr[:, 2 * RWKV_W:3 * RWKV_W]
    wa = xr[:, 3 * RWKV_W:3 * RWKV_W + LANES]
    gd = xr[:, 3 * RWKV_W + LANES:RWKV_PROJ]
    lw = _sigmoid(w0 + _mm(_tanh(wa), w2)) * (-W_OFFSET_SCALE)
    a_sig = _sigmoid(a0 + _mm(wa, a2))
    gate = _mm(_sigmoid(gd), g2)
    kk = kr * k_k
    kk = kk * lax.rsqrt(jnp.maximum(_seg_sum(kk * kk, seg), 1e-24))
    kr = kr * (1.0 + (a_sig - 1.0) * k_a)
    bonus = _seg_sum(r * kr * rk, seg) * vr
    return r, kr, vr, lw, -kk, kk * a_sig, gate, bonus


def _rwkv_prep(u, prev, mu, w0, w2, a0, a2, g2, k_k, k_a, rk, seg, sums):
    rows = u.shape[0]
    r, kr, vr, lw, a_vec, b_vec, gate, bonus = _rwkv_tokens(
        u, prev, mu, w0, w2, a0, a2, g2, k_k, k_a, rk, seg)
    block = sums.shape[1]
    cums = [_mm(sums, lw[i:i + block], NN, 1, 2) for i in range(0, rows, block)]
    cum = jnp.concatenate([x[:block] for x in cums], axis=0)
    cum_end = jnp.concatenate([x[block:] for x in cums], axis=0)
    e_neg = jnp.exp(-cum)
    g_end = jnp.exp(cum_end)
    e_end = g_end * e_neg
    bf = lambda x: x.astype(BF16)
    return (bf(a_vec * jnp.exp(cum - lw)), bf(r * jnp.exp(cum)), bf(kr * e_neg), bf(b_vec * e_neg),
            bf(kr * e_end), bf(b_vec * e_end), bf(vr), gate, bonus, g_end)


def _rwkv_intra(units, m, hmasks, n_double):
    rows = units[0][0].shape[0]
    stack = lambda x: _stack_heads(x, hmasks)
    cat_w = m["incl"].shape[1]
    g = [_mm(jnp.concatenate([at, rt], axis=0), jnp.concatenate([stack(bt), stack(kt)], axis=0), NT)
         for at, rt, kt, bt, _ in units]
    g = [jnp.where(m["quad"], x, 0.0) for x in g]
    a_ab = [x[:rows, :cat_w] for x in g]
    a_rb = [x[rows:, :cat_w] for x in g]
    a_ak = [x[:rows, cat_w:] for x in g]
    a_rk = [x[rows:, cat_w:] for x in g]
    tinv = [m["eye"] + a for a in a_ab]
    apow = [_mm(a, _block_diag(a, m["blk"])) for a in a_ab]
    for _ in range(n_double - 1):
        both = [_mm(jnp.concatenate([t, a], axis=0), _block_diag(a, m["blk"])) for t, a in zip(tinv, apow)]
        tinv = [t + x[:rows] for t, x in zip(tinv, both)]
        apow = [x[rows:] for x in both]
    tinv = [t + _mm(t, _block_diag(a, m["blk"])) for t, a in zip(tinv, apow)]
    v_s = [stack(vp) for _, _, _, _, vp in units]
    akv = [_mm(jnp.concatenate([a, b], axis=0), v) for a, b, v in zip(a_ak, a_rk, v_s)]
    wu = [_mm(t, jnp.concatenate([stack(at), stack(x[:rows])], axis=1))
          for t, (at, _, _, _, _), x in zip(tinv, units, akv)]
    z = [_mm(a, jnp.concatenate([stack(x[:, :GROUP_W]), stack(x[:, GROUP_W:])], axis=1))
         for a, x in zip(a_rb, wu)]
    return [(x[:, :GROUP_W], x[:, GROUP_W:], rt + zz[:, :GROUP_W], zz[:, GROUP_W:] + kv[rows:])
            for x, (_, rt, _, _, _), zz, kv in zip(wu, units, z, akv)]


def _rwkv_out(y, gate, bonus, lnw, lnb, seg):
    inv = 1.0 / RWKV_HEAD
    yc = y - _seg_sum(y, seg, 2) * inv
    var = _seg_sum(yc * yc, seg) * inv
    return ((yc * lax.rsqrt(var + RWKV_GN_EPS) * lnw + lnb + bonus) * gate).astype(BF16)


def _rwkv_state_in(s_ref, idx, g):
    zero = jnp.zeros((RWKV_HEAD, RWKV_HEAD), F32)
    blocks = []
    for h in range(GROUP_HEADS):
        parts = [zero] * GROUP_HEADS
        parts[h] = s_ref[idx, g * GROUP_HEADS + h]
        blocks.append(jnp.concatenate(parts, axis=1))
    return jnp.concatenate(blocks, axis=0)


def _rwkv_state_out(s_ref, idx, g, s2):
    for h in range(GROUP_HEADS):
        sl = slice(h * RWKV_HEAD, (h + 1) * RWKV_HEAD)
        s_ref[idx, g * GROUP_HEADS + h] = s2[sl, sl]


def _head_block_mask():
    return (_iota((GROUP_W, GROUP_W), 0) // RWKV_HEAD) == (_iota((GROUP_W, GROUP_W), 1) // RWKV_HEAD)


def _rwkv_seq_part(u_ref, shift0_ref, s0_ref, mu_ref, w0_ref, w2_ref, a0_ref, a2_ref,
                   g2_ref, kk_ref, ka_ref, rk_ref, lnw_ref, lnb_ref, seg_ref,
                   o_ref, sout_ref, s_scr, prev_scr, *, phase):
    n_seq, rows = u_ref.shape[0], u_ref.shape[1]
    groups = [slice(g * GROUP_W, (g + 1) * GROUP_W) for g in range(RWKV_GROUPS)]
    ids = [(b, g) for b in range(n_seq) for g in range(RWKV_GROUPS)]

    def init():
        for b in range(n_seq):
            for g in range(RWKV_GROUPS):
                s_scr[b, g] = _rwkv_state_in(s0_ref, b, g)
            prev_scr[b] = shift0_ref[b]

    def final():
        for b, g in ids:
            _rwkv_state_out(sout_ref, b, g, s_scr[b, g])

    if phase == "init":
        return init()
    if phase == "final":
        return final()

    seg = seg_ref[...]
    m = _chunk_masks(rows, rows, GROUP_HEADS)
    hmasks = _head_masks(rows, RWKV_HEAD, GROUP_HEADS)
    head_blk = _head_block_mask()
    n_double = rows.bit_length() - 2

    us = [u_ref[b] for b in range(n_seq)]
    prev = jnp.concatenate([_shift_rows(u, prev_scr[b]) for b, u in enumerate(us)], axis=0)
    for b, u in enumerate(us):
        prev_scr[b] = u[rows - 1:rows]
    at, rt, kt, bt, ke, be, vb, gate, bonus, g_end = _rwkv_prep(
        jnp.concatenate(us, axis=0), prev, mu_ref[...], w0_ref[...], w2_ref[...], a0_ref[...],
        a2_ref[...], g2_ref[...], kk_ref[...], ka_ref[...], rk_ref[...], seg, m["sums"])
    tok, units = [], []
    for b in range(n_seq):
        sl = slice(b * rows, (b + 1) * rows)
        tok.append((vb[sl], ke[sl], be[sl], g_end[b * rows:b * rows + 1]))
        units += [(at[sl, gl], rt[sl, gl], kt[sl, gl], bt[sl, gl], vb[sl, gl]) for gl in groups]

    intra = _rwkv_intra(units, m, hmasks, n_double)
    s_old = [s_scr[b, g] for b, g in ids]
    uy = [_mm(jnp.concatenate([w_m, r_m], axis=0), s2, NT) for (w_m, _, r_m, _), s2 in zip(intra, s_old)]
    upd = []
    for (b, g), (_, u0, _, _), x in zip(ids, intra, uy):
        vb_b, ke_b, be_b, _ = tok[b]
        gl = groups[g]
        upd.append(_mm_tn(jnp.concatenate([x[:rows] + u0, vb_b[:, gl].astype(F32)], axis=0),
                          jnp.concatenate([be_b[:, gl], ke_b[:, gl]], axis=0)))
    for (b, g), s2, d in zip(ids, s_old, upd):
        s_scr[b, g] = s2 * tok[b][3][:, groups[g]] + jnp.where(head_blk, d, 0.0)
    ys = [jnp.concatenate([uy[i][rows:] + intra[i][3] for i, (bb, _) in enumerate(ids) if bb == b],
                          axis=1) for b in range(n_seq)]
    out = _rwkv_out(jnp.concatenate(ys, axis=0), gate, bonus, lnw_ref[...], lnb_ref[...], seg)
    for b in range(n_seq):
        o_ref[b] = out[b * rows:(b + 1) * rows]


STEP_HEADS = 4
STEP_ROWS = 4
N_STEP_OPERANDS = 6


def _sum_keys(x):
    t = x[0:SUBLANES]
    for i in range(SUBLANES, x.shape[0], SUBLANES):
        t = t + x[i:i + SUBLANES]
    for shift in (4, 2, 1):
        t = t + pltpu.roll(t, shift, axis=0)
    return t


def _rwkv_step_kernel(u_ref, shiftx_ref, s0_ref, mu_ref, w0_ref, w2_ref, a0_ref, a2_ref, g2_ref,
                      kk_ref, ka_ref, rk_ref, lnw_ref, lnb_ref, seg_ref,
                      o_ref, sout_ref, ops_scr, y_scr, tmp_scr, aux_scr, *, seq_len):
    p = pl.program_id(0)
    n_tok = u_ref.shape[0]
    n_seq = n_tok // seq_len
    seg = seg_ref[...]
    cols = [slice(c, c + LANES) for c in range(0, RWKV_W, LANES)]

    @pl.when(p == 0)
    def _():
        u = u_ref[...]
        row = _iota((n_tok, 1), 0)
        prev = jnp.where(row % seq_len == 0, shiftx_ref[...], pltpu.roll(u, 1, axis=0))
        r, kr, vr, lw, a_vec, b_vec, gate, bonus = _rwkv_tokens(
            u, prev, mu_ref[...], w0_ref[...], w2_ref[...], a0_ref[...], a2_ref[...], g2_ref[...],
            kk_ref[...], ka_ref[...], rk_ref[...], seg)
        aux_scr[0] = gate
        aux_scr[1] = bonus
        for i, x in enumerate((jnp.exp(lw), a_vec, b_vec, kr, r, vr)):
            for c, cl in enumerate(cols):
                tmp_scr[c] = x[:, cl]
            for t in range(seq_len):
                for c, cl in enumerate(cols):
                    ops_scr[i, t, cl, :] = tmp_scr[c, pl.ds(t, n_seq, stride=seq_len), :].T

    def rows_step(i, carry):
        for hh in range(STEP_HEADS):
            chan = pl.multiple_of((p * STEP_HEADS + hh) * RWKV_HEAD, RWKV_HEAD)
            for j in range(STEP_ROWS):
                v = i * STEP_ROWS + j
                s = s0_ref[hh, v]
                for t in range(seq_len):
                    w, a, b, k, r = (ops_scr[n, t, pl.ds(chan, RWKV_HEAD), :] for n in range(5))
                    v_t = ops_scr[5, t, pl.ds(chan + v, 1), :]
                    sa = jnp.concatenate([_sum_keys(s * a)] * (RWKV_HEAD // SUBLANES), axis=0)
                    s = s * w + sa * b + v_t * k
                    y_scr[t, pl.ds(chan + v, 1), :] = _sum_keys(s * r)[0:1]
                sout_ref[hh, v] = s
        return carry

    lax.fori_loop(0, RWKV_HEAD // STEP_ROWS, rows_step, 0)

    @pl.when(p == pl.num_programs(0) - 1)
    def _():
        for t in range(seq_len):
            for c, cl in enumerate(cols):
                tmp_scr[c, pl.ds(t, n_seq, stride=seq_len), :] = y_scr[t, cl, :].T
        y = jnp.concatenate([tmp_scr[c] for c in range(len(cols))], axis=1)
        o_ref[...] = _rwkv_out(y, aux_scr[0], aux_scr[1], lnw_ref[...], lnb_ref[...], seg)


def _rwkv_step(u2, shiftx, s0_t, params, seq_len):
    n = u2.shape[0]
    n_seq = n // seq_len
    state_spec = pl.BlockSpec((STEP_HEADS, RWKV_HEAD, RWKV_HEAD, n_seq), lambda i: (i, 0, 0, 0))
    whole = lambda w: pl.BlockSpec((n, w), lambda i: (0, 0), pipeline_mode=pl.Buffered(1))
    return pl.pallas_call(
        functools.partial(_rwkv_step_kernel, seq_len=seq_len),
        grid=(RWKV_HEADS // STEP_HEADS,),
        in_specs=[whole(RWKV_PROJ), whole(RWKV_PROJ), state_spec] + _rwkv_param_specs(),
        out_specs=[pl.BlockSpec((n, RWKV_W), lambda i: (0, 0)), state_spec],
        out_shape=[
            jax.ShapeDtypeStruct((n, RWKV_W), BF16),
            jax.ShapeDtypeStruct(s0_t.shape, F32),
        ],
        scratch_shapes=[
            pltpu.VMEM((N_STEP_OPERANDS, seq_len, RWKV_W, n_seq), F32),
            pltpu.VMEM((seq_len, RWKV_W, n_seq), F32),
            pltpu.VMEM((RWKV_W // LANES, n, LANES), F32),
            pltpu.VMEM((2, n, RWKV_W), F32),
        ],
        compiler_params=pltpu.CompilerParams(
            dimension_semantics=("arbitrary",), vmem_limit_bytes=VMEM_LIMIT),
        name="rwkv_step",
    )(u2, shiftx, s0_t, *params)


def _rwkv_param_specs():
    vec = lambda n: _const_spec((1, n))
    return [
        vec(RWKV_PROJ),
        vec(RWKV_W),
        _const_spec((LANES, RWKV_W)),
        vec(RWKV_W),
        _const_spec((LANES, RWKV_W)),
        _const_spec((LANES, RWKV_W)),
        vec(RWKV_W), vec(RWKV_W), vec(RWKV_W), vec(RWKV_W), vec(RWKV_W),
        _const_spec((SEG_W, SEG_W)),
    ]


N_GLA_SEQ_IN = 5
N_RWKV_SEQ_IN = 15


def _mix_seq_kernel(*refs):
    n_in = N_GLA_SEQ_IN + N_RWKV_SEQ_IN
    gla = refs[:N_GLA_SEQ_IN] + refs[n_in:n_in + 2] + refs[n_in + 4:n_in + 5]
    rwkv = refs[N_GLA_SEQ_IN:n_in] + refs[n_in + 2:n_in + 4] + refs[n_in + 5:]
    c = pl.program_id(1)

    @pl.when(c == 0)
    def _():
        _gla_seq_part(*gla, phase="init")
        _rwkv_seq_part(*rwkv, phase="init")

    _rwkv_seq_part(*rwkv, phase="body")
    _gla_seq_part(*gla, phase="body")

    @pl.when(c == pl.num_programs(1) - 1)
    def _():
        _gla_seq_part(*gla, phase="final")
        _rwkv_seq_part(*rwkv, phase="final")


def _mix_seq(ug3, gla_s0, gla_params, ur3, shift0, rwkv_s0, rwkv_params, rows, n_seq):
    b, t, _ = ug3.shape
    sdim = GLA_HEADS * GLA_DK
    gla_state = pl.BlockSpec((n_seq, sdim, GLA_DV), lambda i, j: (i, 0, 0))
    rwkv_state = pl.BlockSpec((None, n_seq, RWKV_HEADS, RWKV_HEAD, RWKV_HEAD),
                              lambda i, j: (0, i, 0, 0, 0))
    tok = lambda w: pl.BlockSpec((n_seq, rows, w), lambda i, j: (i, j, 0))
    return pl.pallas_call(
        _mix_seq_kernel,
        grid=(b // n_seq, t // rows),
        in_specs=[tok(GLA_COLS), gla_state] + _gla_param_specs()
        + [tok(RWKV_PROJ), pl.BlockSpec((n_seq, 1, RWKV_PROJ), lambda i, j: (i, 0, 0)), rwkv_state]
        + _rwkv_param_specs(),
        out_specs=[tok(GLA_W), gla_state, tok(RWKV_W), rwkv_state],
        out_shape=[
            jax.ShapeDtypeStruct((b, t, GLA_W), BF16),
            jax.ShapeDtypeStruct((b, sdim, GLA_DV), F32),
            jax.ShapeDtypeStruct((b, t, RWKV_W), BF16),
            jax.ShapeDtypeStruct(rwkv_s0.shape, F32),
        ],
        scratch_shapes=[
            pltpu.VMEM((n_seq, sdim, GLA_W), F32),
            pltpu.VMEM((n_seq, RWKV_GROUPS, GROUP_W, GROUP_W), F32),
            pltpu.VMEM((n_seq, 1, RWKV_PROJ), F32),
        ],
        compiler_params=pltpu.CompilerParams(
            dimension_semantics=("arbitrary", "arbitrary"), vmem_limit_bytes=VMEM_LIMIT),
        name="mix_seq",
    )(ug3, gla_s0, *gla_params, ur3, shift0, rwkv_s0, *rwkv_params)


FF_CHUNK = D_FF // 2


def _post_kernel(x_ref, og_ref, or_ref, p_ref, wo_ref, nffn_ref, wg_ref, wu_ref, wd_ref,
                 nple_ref, wpg_ref, wpp_ref, nf_ref, y_ref):
    half = x_ref.shape[0] // 2
    parts = [slice(0, half), slice(half, 2 * half)]
    dot = lambda a, b: jnp.dot(a, b, preferred_element_type=F32)
    o = [jnp.concatenate([og_ref[p, :], or_ref[p, :]], axis=1) for p in parts]
    x = [x_ref[p, :] + dot(oo, wo_ref[...]) for p, oo in zip(parts, o)]
    h2 = [_rms(xx, nffn_ref[...]).astype(BF16) for xx in x]
    for i in range(0, D_FF, FF_CHUNK):
        gate = [dot(h, wg_ref[:, i:i + FF_CHUNK]) for h in h2]
        up = [dot(h, wu_ref[:, i:i + FF_CHUNK]) for h in h2]
        act = [(g * _sigmoid(g) * u).astype(BF16) for g, u in zip(gate, up)]
        x = [xx + dot(a, wd_ref[i:i + FF_CHUNK, :]) for xx, a in zip(x, act)]
    h3 = [_rms(xx, nple_ref[...]).astype(BF16) for xx in x]
    pg = [_sigmoid(dot(h, wpg_ref[...])) for h in h3]
    pp = [dot(p_ref[p, :].astype(BF16), wpp_ref[...]) for p in parts]
    for p, xx, a, b in zip(parts, x, pg, pp):
        y_ref[p, :] = _rms(xx + a * b, nf_ref[...])


def _post(x2d, og, orw, p2d, weights, tm):
    n = x2d.shape[0]
    wo, nffn, wg, wu, wd, nple, wpg, wpp, nf = weights
    tok = lambda w: pl.BlockSpec((tm, w), lambda i: (i, 0))
    return pl.pallas_call(
        _post_kernel,
        grid=(n // tm,),
        in_specs=[
            tok(D_MODEL), tok(GLA_W), tok(RWKV_W), tok(PLE_DIM),
            _const_spec((D_MODEL, D_MODEL)), _const_spec((1, D_MODEL)),
            _const_spec((D_MODEL, D_FF)), _const_spec((D_MODEL, D_FF)), _const_spec((D_FF, D_MODEL)),
            _const_spec((1, D_MODEL)), _const_spec((D_MODEL, D_MODEL)), _const_spec((PLE_DIM, D_MODEL)),
            _const_spec((1, D_MODEL)),
        ],
        out_specs=tok(D_MODEL),
        out_shape=jax.ShapeDtypeStruct((n, D_MODEL), F32),
        compiler_params=pltpu.CompilerParams(
            dimension_semantics=("arbitrary",), vmem_limit_bytes=VMEM_LIMIT),
        name="post",
    )(x2d, og, orw, p2d, wo, nffn, wg, wu, wd, nple, wpg, wpp, nf)


PROMPT_CHUNK = 64
PROMPT_SEQS_PER_STEP = 8
DEC_TILE_SEQS = 32
TOKEN_TILE = 512
PROJ_TILE = 1024


def kernel(x_prompt, x_sample, state_gla, state_rwkv, state_shift, p_prompt, p_sample, norm_mix, w_in, gla_gk_up, gla_gk_bias, gla_norm, rwkv_mu, rwkv_w0, rwkv_w2, rwkv_a0, rwkv_a2, rwkv_g2, rwkv_k_k, rwkv_k_a, rwkv_r_k, rwkv_ln_w, rwkv_ln_b, w_out, norm_ffn, w_gate, w_up, w_down, norm_ple, w_ple_gate, w_ple_proj, norm_final):
    assert w_in.shape[0] == 1
    i = 0
    rowv = lambda a: a.astype(F32).reshape(1, -1)
    zeros = lambda r, c: jnp.zeros((r, c), F32)
    w_in_t = jnp.transpose(w_in[i]).astype(BF16)
    gk_up = jnp.concatenate(
        [gla_gk_up[i].astype(F32), zeros(LANES - GLA_GATE_RANK, GLA_K_W)], axis=0).astype(BF16)
    seg = jnp.arange(SEG_W)[:, None] // RWKV_HEAD == jnp.arange(SEG_W)[None, :] // RWKV_HEAD
    rwkv_params = (
        rowv(rwkv_mu[i]), rowv(rwkv_w0[i]),
        jnp.concatenate([rwkv_w2[i].astype(F32), zeros(64, RWKV_W)], axis=0).astype(BF16),
        rowv(rwkv_a0[i]),
        jnp.concatenate([zeros(64, RWKV_W), rwkv_a2[i].astype(F32)], axis=0).astype(BF16),
        rwkv_g2[i].astype(BF16),
        rowv(rwkv_k_k[i]), rowv(rwkv_k_a[i]), rowv(rwkv_r_k[i]), rowv(rwkv_ln_w[i]), rowv(rwkv_ln_b[i]),
        seg.astype(BF16),
    )
    gla_w = (gk_up, rowv(gla_gk_bias[i]), rowv(gla_norm[i]))
    g_mix = rowv(norm_mix[i])

    bp, tp, _ = x_prompt.shape
    xp = x_prompt.astype(F32).reshape(bp * tp, D_MODEL)
    tail_w = tuple(w[i].astype(F32) for w in (w_out, w_gate, w_up, w_down, w_ple_gate, w_ple_proj))
    ug, ur, wo_b, wgate_b, wup_b, wdown_b, wpg_b, wpp_b = _proj(
        xp, g_mix, w_in_t, PROJ_TILE, tail_w)
    post_w = (wo_b, rowv(norm_ffn[i]), wgate_b, wup_b, wdown_b, rowv(norm_ple[i]), wpg_b, wpp_b,
              rowv(norm_final))
    ur3 = ur.reshape(bp, tp, RWKV_PROJ)
    og, gla_p, orw, rwkv_p = _mix_seq(
        ug.reshape(bp, tp, GLA_COLS), jnp.zeros((bp, GLA_HEADS * GLA_DK, GLA_DV), F32), gla_w,
        ur3, jnp.zeros((bp, 1, RWKV_PROJ), F32),
        jnp.zeros((1, bp, RWKV_HEADS, RWKV_HEAD, RWKV_HEAD), F32), rwkv_params,
        PROMPT_CHUNK, PROMPT_SEQS_PER_STEP)
    shift_p = ur3[:, tp - 1]
    yp = _post(xp, og.reshape(bp * tp, GLA_W), orw.reshape(bp * tp, RWKV_W),
               p_prompt[i].reshape(bp * tp, PLE_DIM), post_w, TOKEN_TILE)

    bs, ts, _ = x_sample.shape
    xs = x_sample.astype(F32).reshape(bs * ts, D_MODEL)
    ug, ur = _proj(xs, g_mix, w_in_t, TOKEN_TILE)
    og, gla_s = _gla_dec(ug, state_gla[i].astype(F32).reshape(bs, GLA_HEADS * GLA_DK, GLA_DV),
                         gla_w, ts, DEC_TILE_SEQS)
    shiftx = jnp.pad(state_shift[i].astype(F32)[:, None, :], ((0, 0), (0, ts - 1), (0, 0)))
    orw, rwkv_s = _rwkv_step(ur, shiftx.reshape(bs * ts, RWKV_PROJ),
                             jnp.transpose(state_rwkv[i].astype(F32), (1, 2, 3, 0)), rwkv_params, ts)
    rwkv_s = jnp.transpose(rwkv_s, (3, 0, 1, 2))[None]
    shift_s = ur.reshape(bs, ts, RWKV_PROJ)[:, ts - 1]
    ys = _post(xs, og, orw, p_sample[i].reshape(bs * ts, PLE_DIM), post_w, TOKEN_TILE)

    gla_shape = (1, -1, GLA_HEADS, GLA_DK, GLA_DV)
    return (yp.reshape(bp, tp, D_MODEL).astype(x_prompt.dtype),
            ys.reshape(bs, ts, D_MODEL).astype(x_sample.dtype),
            gla_p.reshape(gla_shape).astype(state_gla.dtype), rwkv_p.astype(state_rwkv.dtype),
            shift_p[None].astype(state_shift.dtype),
            gla_s.reshape(gla_shape).astype(state_gla.dtype), rwkv_s.astype(state_rwkv.dtype),
            shift_s[None].astype(state_shift.dtype))
```
